```python
import jax
import jax.numpy as jnp
from jax import lax
import numpy as np

D_MODEL = 2048
BATCH = 16
SEQ = 2048
DEPTH = 4

CHUNK = 64
N_MEM = 256
RMS_EPS = 1e-6
MAX_STREAM_CHUNKS = 1024

GMLP_BLOCK = 128
A_WIDTH = D_MODEL // 2
A_GROUPS = 8
A_GROUP_DIM = A_WIDTH // A_GROUPS

B_WIDTH = D_MODEL - A_WIDTH
B_GROUPS = 8
CONV_WIDTH = 3
AB_IN = 2 * A_WIDTH + 3 * B_WIDTH

C_HEADS = D_MODEL // 128
C_NOPE = 128
C_ROPE = 64
C_V = 128
C_Q_RANK = 512
C_KV_RANK = 256
C_IN = C_Q_RANK + C_KV_RANK + C_ROPE
ROPE_THETA = 10000.0
Q_BLOCK = 128

MEM_HEADS = 4
MEM_HEAD_DIM = D_MODEL // MEM_HEADS

D_FF = 4 * D_MODEL

N_EVEN = (DEPTH + 1) // 2
N_ODD = DEPTH // 2

kernel_name = 'hybrid_gmlp_shortconv_mla_memxattn_trunk'


def rmsnorm(x, g):
    xf = x.astype(jnp.float32)
    y = xf * lax.rsqrt(jnp.mean(xf * xf, axis=-1, keepdims=True) + RMS_EPS)
    return (y * g.astype(jnp.float32)).astype(x.dtype)


def gmlp_spatial_gate(u, v, v_norm_g, w_s, b_s):
    bsz, seq, _ = u.shape
    nb = seq // GMLP_BLOCK
    vg = v.reshape(bsz, seq, A_GROUPS, A_GROUP_DIM)
    vg = rmsnorm(vg, v_norm_g.reshape(A_GROUPS, A_GROUP_DIM))
    vg = vg.reshape(bsz, nb, GMLP_BLOCK, A_GROUPS, A_GROUP_DIM)
    cid = jnp.arange(GMLP_BLOCK) // CHUNK
    mask = (cid[None, :] <= cid[:, None]).astype(w_s.dtype)
    w = w_s * mask[None]
    mixed = jnp.einsum('gij,bnjgc->bnigc', w, vg) + b_s.T[None, None, :, :, None]
    return u * mixed.reshape(bsz, seq, A_WIDTH)


def gated_short_conv(bg, cg, h, conv_w):
    z = cg * h
    seq = z.shape[1]
    zp = jnp.pad(z, ((0, 0), (CONV_WIDTH - 1, 0), (0, 0)))
    conv = sum(conv_w[k] * zp[:, k:k + seq] for k in range(CONV_WIDTH))
    return bg * conv


def mixer_gmlp_conv(xn, w_in, v_norm_g, w_s, b_s, conv_w, w_out):
    z = xn @ w_in
    u, v, bg, cg, h = jnp.split(
        z, [A_WIDTH, 2 * A_WIDTH, 2 * A_WIDTH + B_WIDTH, 2 * A_WIDTH + 2 * B_WIDTH], axis=-1)
    y_a = gmlp_spatial_gate(jax.nn.gelu(u), jax.nn.gelu(v), v_norm_g, w_s, b_s)
    y_b = gated_short_conv(bg, cg, h, conv_w)
    return jnp.concatenate([y_a, y_b], axis=-1) @ w_out


def rope_tables(positions):
    inv = ROPE_THETA ** (-jnp.arange(0, C_ROPE, 2, dtype=jnp.float32) / C_ROPE)
    ang = positions.astype(jnp.float32)[..., None] * inv
    return jnp.cos(ang), jnp.sin(ang)


def apply_rope(x, cos, sin):
    half = x.shape[-1] // 2
    x1 = x[..., :half].astype(jnp.float32)
    x2 = x[..., half:].astype(jnp.float32)
    return jnp.concatenate([x1 * cos - x2 * sin, x2 * cos + x1 * sin], axis=-1).astype(x.dtype)


def mixer_mla(xn, positions, w_in, q_norm_g, kv_norm_g, w_uq, w_ukv, w_out):
    bsz, seq, _ = xn.shape
    z = xn @ w_in
    c_q, c_kv, k_rope = jnp.split(z, [C_Q_RANK, C_Q_RANK + C_KV_RANK], axis=-1)
    q = (rmsnorm(c_q, q_norm_g) @ w_uq).reshape(bsz, seq, C_HEADS, C_NOPE + C_ROPE)
    kv = (rmsnorm(c_kv, kv_norm_g) @ w_ukv).reshape(bsz, seq, C_HEADS, C_NOPE + C_V)
    cos, sin = rope_tables(positions)
    q_nope = q[..., :C_NOPE]
    q_rope = apply_rope(q[..., C_NOPE:], cos[:, :, None], sin[:, :, None])
    k_nope = kv[..., :C_NOPE]
    v = kv[..., C_NOPE:]
    k_rope = apply_rope(k_rope, cos, sin)
    scale = (C_NOPE + C_ROPE) ** -0.5
    chunk_id = jnp.arange(seq) // CHUNK
    outs = []
    for qb in range(seq // Q_BLOCK):
        q0, q1 = qb * Q_BLOCK, (qb + 1) * Q_BLOCK
        s = (jnp.einsum('bqhd,bkhd->bhqk', q_nope[:, q0:q1], k_nope[:, :q1])
             + jnp.einsum('bqhr,bkr->bhqk', q_rope[:, q0:q1], k_rope[:, :q1]))
        s = s.astype(jnp.float32) * scale
        mask = chunk_id[None, :q1] <= chunk_id[q0:q1, None]
        s = jnp.where(mask[None, None], s, -jnp.inf)
        p = jax.nn.softmax(s, axis=-1).astype(v.dtype)
        outs.append(jnp.einsum('bhqk,bkhd->bqhd', p, v[:, :q1]))
    o = jnp.concatenate(outs, axis=1).reshape(bsz, seq, C_HEADS * C_V)
    return o @ w_out


def mem_cross_attention(xn, memn, wq, wk, wv, wo):
    bsz, seq, _ = xn.shape
    n_mem = memn.shape[1]
    q = (xn @ wq).reshape(bsz, seq, MEM_HEADS, MEM_HEAD_DIM)
    k = (memn @ wk).reshape(bsz, n_mem, MEM_HEADS, MEM_HEAD_DIM)
    v = (memn @ wv).reshape(bsz, n_mem, MEM_HEADS, MEM_HEAD_DIM)
    s = jnp.einsum('bqhd,bmhd->bhqm', q, k).astype(jnp.float32) * (MEM_HEAD_DIM ** -0.5)
    p = jax.nn.softmax(s, axis=-1).astype(v.dtype)
    o = jnp.einsum('bhqm,bmhd->bqhd', p, v).reshape(bsz, seq, D_MODEL)
    return o @ wo


def squared_relu_mlp(xn, w1, w2):
    h = jax.nn.relu(xn @ w1)
    return (h * h) @ w2


def _normal(key, shape, scale):
    return jax.random.normal(key, shape, jnp.float32) * scale


def _gain(key, shape):
    return 1.0 + 0.02 * jax.random.normal(key, shape, jnp.float32)


def _fwd_setup_inputs(seed: int = 0) -> dict:
    key = jax.random.key(seed)
    ks = jax.random.split(key, 26)
    d = D_MODEL
    offset = jax.random.randint(ks[2], (BATCH, 1), 0, MAX_STREAM_CHUNKS, dtype=jnp.int32) * CHUNK
    positions = offset + jnp.arange(SEQ, dtype=jnp.int32)[None, :]
    return {
        'x': _normal(ks[0], (BATCH, SEQ, d), 1.0),
        'mem': _normal(ks[1], (BATCH, N_MEM, d), 1.0),
        'positions': positions,
        'norm_mix_g': _gain(ks[3], (DEPTH, d)),
        'norm_mem_q_g': _gain(ks[4], (DEPTH, d)),
        'norm_mem_kv_g': _gain(ks[5], (DEPTH, d)),
        'norm_ffn_g': _gain(ks[6], (DEPTH, d)),
        'final_norm_g': _gain(ks[7], (d,)),
        'ab_w_in': _normal(ks[8], (N_EVEN, d, AB_IN), d ** -0.5),
        'a_v_norm_g': _gain(ks[9], (N_EVEN, A_WIDTH)),
        'a_w_s': _normal(ks[10], (N_EVEN, A_GROUPS, GMLP_BLOCK, GMLP_BLOCK), GMLP_BLOCK ** -0.5),
        'a_b_s': _gain(ks[11], (N_EVEN, A_GROUPS, GMLP_BLOCK)),
        'b_conv_w': _normal(ks[12], (N_EVEN, CONV_WIDTH, B_WIDTH), CONV_WIDTH ** -0.5),
        'ab_w_out': _normal(ks[13], (N_EVEN, A_WIDTH + B_WIDTH, d), (A_WIDTH + B_WIDTH) ** -0.5),
        'c_w_in': _normal(ks[14], (N_ODD, d, C_IN), d ** -0.5),
        'c_q_norm_g': _gain(ks[15], (N_ODD, C_Q_RANK)),
        'c_kv_norm_g': _gain(ks[16], (N_ODD, C_KV_RANK)),
        'c_w_uq': _normal(ks[17], (N_ODD, C_Q_RANK, C_HEADS * (C_NOPE + C_ROPE)), C_Q_RANK ** -0.5),
        'c_w_ukv': _normal(ks[18], (N_ODD, C_KV_RANK, C_HEADS * (C_NOPE + C_V)), C_KV_RANK ** -0.5),
        'c_w_out': _normal(ks[19], (N_ODD, C_HEADS * C_V, d), (C_HEADS * C_V) ** -0.5),
        'm_wq': _normal(ks[20], (DEPTH, d, d), d ** -0.5),
        'm_wk': _normal(ks[21], (DEPTH, d, d), d ** -0.5),
        'm_wv': _normal(ks[22], (DEPTH, d, d), d ** -0.5),
        'm_wo': _normal(ks[23], (DEPTH, d, d), d ** -0.5),
        'f_w1': _normal(ks[24], (DEPTH, d, D_FF), d ** -0.5),
        'f_w2': _normal(ks[25], (DEPTH, D_FF, d), D_FF ** -0.5),
    }


def _fwd_reference(x, mem, positions, norm_mix_g, norm_mem_q_g, norm_mem_kv_g, norm_ffn_g,
              final_norm_g, ab_w_in, a_v_norm_g, a_w_s, a_b_s, b_conv_w, ab_w_out,
              c_w_in, c_q_norm_g, c_kv_norm_g, c_w_uq, c_w_ukv, c_w_out,
              m_wq, m_wk, m_wv, m_wo, f_w1, f_w2):
    for layer in range(DEPTH):
        xn = rmsnorm(x, norm_mix_g[layer])
        if layer % 2 == 0:
            e = layer // 2
            x = x + mixer_gmlp_conv(xn, ab_w_in[e], a_v_norm_g[e], a_w_s[e], a_b_s[e],
                                    b_conv_w[e], ab_w_out[e])
        else:
            o = layer // 2
            x = x + mixer_mla(xn, positions, c_w_in[o], c_q_norm_g[o], c_kv_norm_g[o],
                              c_w_uq[o], c_w_ukv[o], c_w_out[o])
        x = x + mem_cross_attention(rmsnorm(x, norm_mem_q_g[layer]),
                                    rmsnorm(mem, norm_mem_kv_g[layer]),
                                    m_wq[layer], m_wk[layer], m_wv[layer], m_wo[layer])
        x = x + squared_relu_mlp(rmsnorm(x, norm_ffn_g[layer]), f_w1[layer], f_w2[layer])
    return rmsnorm(x, final_norm_g)


import jax as _jax
import jax.numpy as _jnp

TWIN_FORMAT = 'train_step'
FWD_PARAMS = ['x', 'mem', 'positions', 'norm_mix_g', 'norm_mem_q_g', 'norm_mem_kv_g', 'norm_ffn_g', 'final_norm_g', 'ab_w_in', 'a_v_norm_g', 'a_w_s', 'a_b_s', 'b_conv_w', 'ab_w_out', 'c_w_in', 'c_q_norm_g', 'c_kv_norm_g', 'c_w_uq', 'c_w_ukv', 'c_w_out', 'm_wq', 'm_wk', 'm_wv', 'm_wo', 'f_w1', 'f_w2']
TWIN_WEIGHTS = ['norm_mix_g', 'norm_mem_q_g', 'norm_mem_kv_g', 'norm_ffn_g', 'final_norm_g', 'ab_w_in', 'a_v_norm_g', 'a_w_s', 'a_b_s', 'b_conv_w', 'ab_w_out', 'c_w_in', 'c_q_norm_g', 'c_kv_norm_g', 'c_w_uq', 'c_w_ukv', 'c_w_out', 'm_wq', 'm_wk', 'm_wv', 'm_wo', 'f_w1', 'f_w2']
TWIN_DIFF_INPUT = 'x'
TWIN_INPUTS = ['x', 'mem', 'positions', 'norm_mix_g', 'norm_mem_q_g', 'norm_mem_kv_g', 'norm_ffn_g', 'final_norm_g', 'ab_w_in', 'a_v_norm_g', 'a_w_s', 'a_b_s', 'b_conv_w', 'ab_w_out', 'c_w_in', 'c_q_norm_g', 'c_kv_norm_g', 'c_w_uq', 'c_w_ukv', 'c_w_out', 'm_wq', 'm_wk', 'm_wv', 'm_wo', 'f_w1', 'f_w2', 'loss_target', 'm_norm_mix_g', 'm_norm_mem_q_g', 'm_norm_mem_kv_g', 'm_norm_ffn_g', 'm_final_norm_g', 'm_ab_w_in', 'm_a_v_norm_g', 'm_a_w_s', 'm_a_b_s', 'm_b_conv_w', 'm_ab_w_out', 'm_c_w_in', 'm_c_q_norm_g', 'm_c_kv_norm_g', 'm_c_w_uq', 'm_c_w_ukv', 'm_c_w_out', 'm_m_wq', 'm_m_wk', 'm_m_wv', 'm_m_wo', 'm_f_w1', 'm_f_w2', 'v_norm_mix_g', 'v_norm_mem_q_g', 'v_norm_mem_kv_g', 'v_norm_ffn_g', 'v_final_norm_g', 'v_ab_w_in', 'v_a_v_norm_g', 'v_a_w_s', 'v_a_b_s', 'v_b_conv_w', 'v_ab_w_out', 'v_c_w_in', 'v_c_q_norm_g', 'v_c_kv_norm_g', 'v_c_w_uq', 'v_c_w_ukv', 'v_c_w_out', 'v_m_wq', 'v_m_wk', 'v_m_wv', 'v_m_wo', 'v_f_w1', 'v_f_w2']
TWIN_OUTPUTS = ['loss', 'grad_x', 'grad_norm_mix_g', 'grad_norm_mem_q_g', 'grad_norm_mem_kv_g', 'grad_norm_ffn_g', 'grad_final_norm_g', 'grad_ab_w_in', 'grad_a_v_norm_g', 'grad_a_w_s', 'grad_a_b_s', 'grad_b_conv_w', 'grad_ab_w_out', 'grad_c_w_in', 'grad_c_q_norm_g', 'grad_c_kv_norm_g', 'grad_c_w_uq', 'grad_c_w_ukv', 'grad_c_w_out', 'grad_m_wq', 'grad_m_wk', 'grad_m_wv', 'grad_m_wo', 'grad_f_w1', 'grad_f_w2', 'delta_norm_mix_g', 'delta_norm_mem_q_g', 'delta_norm_mem_kv_g', 'delta_norm_ffn_g', 'delta_final_norm_g', 'delta_ab_w_in', 'delta_a_v_norm_g', 'delta_a_w_s', 'delta_a_b_s', 'delta_b_conv_w', 'delta_ab_w_out', 'delta_c_w_in', 'delta_c_q_norm_g', 'delta_c_kv_norm_g', 'delta_c_w_uq', 'delta_c_w_ukv', 'delta_c_w_out', 'delta_m_wq', 'delta_m_wk', 'delta_m_wv', 'delta_m_wo', 'delta_f_w1', 'delta_f_w2', 'new_m_norm_mix_g', 'new_m_norm_mem_q_g', 'new_m_norm_mem_kv_g', 'new_m_norm_ffn_g', 'new_m_final_norm_g', 'new_m_ab_w_in', 'new_m_a_v_norm_g', 'new_m_a_w_s', 'new_m_a_b_s', 'new_m_b_conv_w', 'new_m_ab_w_out', 'new_m_c_w_in', 'new_m_c_q_norm_g', 'new_m_c_kv_norm_g', 'new_m_c_w_uq', 'new_m_c_w_ukv', 'new_m_c_w_out', 'new_m_m_wq', 'new_m_m_wk', 'new_m_m_wv', 'new_m_m_wo', 'new_m_f_w1', 'new_m_f_w2', 'new_v_norm_mix_g', 'new_v_norm_mem_q_g', 'new_v_norm_mem_kv_g', 'new_v_norm_ffn_g', 'new_v_final_norm_g', 'new_v_ab_w_in', 'new_v_a_v_norm_g', 'new_v_a_w_s', 'new_v_a_b_s', 'new_v_b_conv_w', 'new_v_ab_w_out', 'new_v_c_w_in', 'new_v_c_q_norm_g', 'new_v_c_kv_norm_g', 'new_v_c_w_uq', 'new_v_c_w_ukv', 'new_v_c_w_out', 'new_v_m_wq', 'new_v_m_wk', 'new_v_m_wv', 'new_v_m_wo', 'new_v_f_w1', 'new_v_f_w2']
TWIN_LEAF_KINDS = {'loss': 'loss', 'grad_x': 'grad_x', 'grad_norm_mix_g': 'grad_w', 'grad_norm_mem_q_g': 'grad_w', 'grad_norm_mem_kv_g': 'grad_w', 'grad_norm_ffn_g': 'grad_w', 'grad_final_norm_g': 'grad_w', 'grad_ab_w_in': 'grad_w', 'grad_a_v_norm_g': 'grad_w', 'grad_a_w_s': 'grad_w', 'grad_a_b_s': 'grad_w', 'grad_b_conv_w': 'grad_w', 'grad_ab_w_out': 'grad_w', 'grad_c_w_in': 'grad_w', 'grad_c_q_norm_g': 'grad_w', 'grad_c_kv_norm_g': 'grad_w', 'grad_c_w_uq': 'grad_w', 'grad_c_w_ukv': 'grad_w', 'grad_c_w_out': 'grad_w', 'grad_m_wq': 'grad_w', 'grad_m_wk': 'grad_w', 'grad_m_wv': 'grad_w', 'grad_m_wo': 'grad_w', 'grad_f_w1': 'grad_w', 'grad_f_w2': 'grad_w', 'delta_norm_mix_g': 'delta_w', 'delta_norm_mem_q_g': 'delta_w', 'delta_norm_mem_kv_g': 'delta_w', 'delta_norm_ffn_g': 'delta_w', 'delta_final_norm_g': 'delta_w', 'delta_ab_w_in': 'delta_w', 'delta_a_v_norm_g': 'delta_w', 'delta_a_w_s': 'delta_w', 'delta_a_b_s': 'delta_w', 'delta_b_conv_w': 'delta_w', 'delta_ab_w_out': 'delta_w', 'delta_c_w_in': 'delta_w', 'delta_c_q_norm_g': 'delta_w', 'delta_c_kv_norm_g': 'delta_w', 'delta_c_w_uq': 'delta_w', 'delta_c_w_ukv': 'delta_w', 'delta_c_w_out': 'delta_w', 'delta_m_wq': 'delta_w', 'delta_m_wk': 'delta_w', 'delta_m_wv': 'delta_w', 'delta_m_wo': 'delta_w', 'delta_f_w1': 'delta_w', 'delta_f_w2': 'delta_w', 'new_m_norm_mix_g': 'new_m', 'new_m_norm_mem_q_g': 'new_m', 'new_m_norm_mem_kv_g': 'new_m', 'new_m_norm_ffn_g': 'new_m', 'new_m_final_norm_g': 'new_m', 'new_m_ab_w_in': 'new_m', 'new_m_a_v_norm_g': 'new_m', 'new_m_a_w_s': 'new_m', 'new_m_a_b_s': 'new_m', 'new_m_b_conv_w': 'new_m', 'new_m_ab_w_out': 'new_m', 'new_m_c_w_in': 'new_m', 'new_m_c_q_norm_g': 'new_m', 'new_m_c_kv_norm_g': 'new_m', 'new_m_c_w_uq': 'new_m', 'new_m_c_w_ukv': 'new_m', 'new_m_c_w_out': 'new_m', 'new_m_m_wq': 'new_m', 'new_m_m_wk': 'new_m', 'new_m_m_wv': 'new_m', 'new_m_m_wo': 'new_m', 'new_m_f_w1': 'new_m', 'new_m_f_w2': 'new_m', 'new_v_norm_mix_g': 'new_v', 'new_v_norm_mem_q_g': 'new_v', 'new_v_norm_mem_kv_g': 'new_v', 'new_v_norm_ffn_g': 'new_v', 'new_v_final_norm_g': 'new_v', 'new_v_ab_w_in': 'new_v', 'new_v_a_v_norm_g': 'new_v', 'new_v_a_w_s': 'new_v', 'new_v_a_b_s': 'new_v', 'new_v_b_conv_w': 'new_v', 'new_v_ab_w_out': 'new_v', 'new_v_c_w_in': 'new_v', 'new_v_c_q_norm_g': 'new_v', 'new_v_c_kv_norm_g': 'new_v', 'new_v_c_w_uq': 'new_v', 'new_v_c_w_ukv': 'new_v', 'new_v_c_w_out': 'new_v', 'new_v_m_wq': 'new_v', 'new_v_m_wk': 'new_v', 'new_v_m_wv': 'new_v', 'new_v_m_wo': 'new_v', 'new_v_f_w1': 'new_v', 'new_v_f_w2': 'new_v'}


def _forward(args):
    return _fwd_reference(*[args[k] for k in FWD_PARAMS])


def _output_shape():
    out = _jax.eval_shape(lambda: _forward(_fwd_setup_inputs(0)))
    return out.shape, out.dtype

N_MICROBATCH = 1
ADAM_LR = 0.001
ADAM_B1 = 0.9
ADAM_B2 = 0.999
ADAM_EPS = 1e-08
ADAM_WD = 0.01
ADAM_STEP = 10
PER_EXAMPLE_BATCH_AXIS = {'x': 0, 'mem': 0, 'positions': 0, 'loss_target': 0}
SHARED_INPUTS = []
_WEIGHT_DTYPES = {'norm_mix_g': _jnp.float32, 'norm_mem_q_g': _jnp.float32, 'norm_mem_kv_g': _jnp.float32, 'norm_ffn_g': _jnp.float32, 'final_norm_g': _jnp.float32, 'ab_w_in': _jnp.float32, 'a_v_norm_g': _jnp.float32, 'a_w_s': _jnp.float32, 'a_b_s': _jnp.float32, 'b_conv_w': _jnp.float32, 'ab_w_out': _jnp.float32, 'c_w_in': _jnp.float32, 'c_q_norm_g': _jnp.float32, 'c_kv_norm_g': _jnp.float32, 'c_w_uq': _jnp.float32, 'c_w_ukv': _jnp.float32, 'c_w_out': _jnp.float32, 'm_wq': _jnp.float32, 'm_wk': _jnp.float32, 'm_wv': _jnp.float32, 'm_wo': _jnp.float32, 'f_w1': _jnp.float32, 'f_w2': _jnp.float32}
MOMENT_SCALE = {'norm_mix_g': 6.992332e-02, 'norm_mem_q_g': 5.835702e-03, 'norm_mem_kv_g': 9.017964e-03, 'norm_ffn_g': 6.172139e-02, 'final_norm_g': 1.657009e+01, 'ab_w_in': 5.827935e-02, 'a_v_norm_g': 3.714851e-02, 'a_w_s': 3.630264e-02, 'a_b_s': 4.221085e-02, 'b_conv_w': 6.516249e-02, 'ab_w_out': 6.196578e-02, 'c_w_in': 3.756946e-02, 'c_q_norm_g': 1.271475e-02, 'c_kv_norm_g': 6.463330e-02, 'c_w_uq': 4.925365e-03, 'c_w_ukv': 1.785153e-02, 'c_w_out': 2.654143e-02, 'm_wq': 5.795551e-03, 'm_wk': 5.806618e-03, 'm_wv': 6.359925e-03, 'm_wo': 6.393318e-03, 'f_w1': 3.149750e-02, 'f_w2': 6.395889e-02}


def _to_microbatches(a, axis):
    t = _jnp.moveaxis(a, axis, 0)
    t = t.reshape((N_MICROBATCH, t.shape[0] // N_MICROBATCH) + t.shape[1:])
    return _jnp.moveaxis(t, 1, axis + 1)


def setup_inputs(seed: int = 0) -> dict:
    inp = _fwd_setup_inputs(seed)
    key = _jax.random.fold_in(_jax.random.key(seed), 7919)
    shape, _ = _output_shape()
    out = dict(inp)
    out["loss_target"] = _jax.random.normal(_jax.random.fold_in(key, 0), shape, _jnp.float32)
    for i, name in enumerate(TWIN_WEIGHTS):
        w = inp[name].astype(_jnp.float32)
        if MOMENT_SCALE is None:
            s = _jnp.sqrt(_jnp.mean(_jnp.square(w)) + 1e-30)
        else:
            s = MOMENT_SCALE[name]
        km, kv = _jax.random.split(_jax.random.fold_in(key, i + 1))
        out[name] = w
        out["m_" + name] = s * _jax.random.normal(km, w.shape, _jnp.float32)
        out["v_" + name] = (s * s) * _jax.random.uniform(kv, w.shape, _jnp.float32, 0.5, 1.5)
    if N_MICROBATCH > 1:
        for name, axis in PER_EXAMPLE_BATCH_AXIS.items():
            out[name] = _to_microbatches(out[name], axis)
    return {'x': out['x'], 'mem': out['mem'], 'positions': out['positions'], 'norm_mix_g': out['norm_mix_g'], 'norm_mem_q_g': out['norm_mem_q_g'], 'norm_mem_kv_g': out['norm_mem_kv_g'], 'norm_ffn_g': out['norm_ffn_g'], 'final_norm_g': out['final_norm_g'], 'ab_w_in': out['ab_w_in'], 'a_v_norm_g': out['a_v_norm_g'], 'a_w_s': out['a_w_s'], 'a_b_s': out['a_b_s'], 'b_conv_w': out['b_conv_w'], 'ab_w_out': out['ab_w_out'], 'c_w_in': out['c_w_in'], 'c_q_norm_g': out['c_q_norm_g'], 'c_kv_norm_g': out['c_kv_norm_g'], 'c_w_uq': out['c_w_uq'], 'c_w_ukv': out['c_w_ukv'], 'c_w_out': out['c_w_out'], 'm_wq': out['m_wq'], 'm_wk': out['m_wk'], 'm_wv': out['m_wv'], 'm_wo': out['m_wo'], 'f_w1': out['f_w1'], 'f_w2': out['f_w2'], 'loss_target': out['loss_target'], 'm_norm_mix_g': out['m_norm_mix_g'], 'm_norm_mem_q_g': out['m_norm_mem_q_g'], 'm_norm_mem_kv_g': out['m_norm_mem_kv_g'], 'm_norm_ffn_g': out['m_norm_ffn_g'], 'm_final_norm_g': out['m_final_norm_g'], 'm_ab_w_in': out['m_ab_w_in'], 'm_a_v_norm_g': out['m_a_v_norm_g'], 'm_a_w_s': out['m_a_w_s'], 'm_a_b_s': out['m_a_b_s'], 'm_b_conv_w': out['m_b_conv_w'], 'm_ab_w_out': out['m_ab_w_out'], 'm_c_w_in': out['m_c_w_in'], 'm_c_q_norm_g': out['m_c_q_norm_g'], 'm_c_kv_norm_g': out['m_c_kv_norm_g'], 'm_c_w_uq': out['m_c_w_uq'], 'm_c_w_ukv': out['m_c_w_ukv'], 'm_c_w_out': out['m_c_w_out'], 'm_m_wq': out['m_m_wq'], 'm_m_wk': out['m_m_wk'], 'm_m_wv': out['m_m_wv'], 'm_m_wo': out['m_m_wo'], 'm_f_w1': out['m_f_w1'], 'm_f_w2': out['m_f_w2'], 'v_norm_mix_g': out['v_norm_mix_g'], 'v_norm_mem_q_g': out['v_norm_mem_q_g'], 'v_norm_mem_kv_g': out['v_norm_mem_kv_g'], 'v_norm_ffn_g': out['v_norm_ffn_g'], 'v_final_norm_g': out['v_final_norm_g'], 'v_ab_w_in': out['v_ab_w_in'], 'v_a_v_norm_g': out['v_a_v_norm_g'], 'v_a_w_s': out['v_a_w_s'], 'v_a_b_s': out['v_a_b_s'], 'v_b_conv_w': out['v_b_conv_w'], 'v_ab_w_out': out['v_ab_w_out'], 'v_c_w_in': out['v_c_w_in'], 'v_c_q_norm_g': out['v_c_q_norm_g'], 'v_c_kv_norm_g': out['v_c_kv_norm_g'], 'v_c_w_uq': out['v_c_w_uq'], 'v_c_w_ukv': out['v_c_w_ukv'], 'v_c_w_out': out['v_c_w_out'], 'v_m_wq': out['v_m_wq'], 'v_m_wk': out['v_m_wk'], 'v_m_wv': out['v_m_wv'], 'v_m_wo': out['v_m_wo'], 'v_f_w1': out['v_f_w1'], 'v_f_w2': out['v_f_w2']}


def _loss(weights, diff, rest, loss_target):
    with _jax.named_scope("forward"):
        args = {**rest, TWIN_DIFF_INPUT: diff, **{k: w.astype(_WEIGHT_DTYPES[k]) for k, w in weights.items()}}
        y = _forward(args)
    with _jax.named_scope("loss_head"):
        err = _jnp.square(y.astype(_jnp.float32) - loss_target)
        return 0.5 * _jnp.sum(_jnp.mean(err, axis=-1)) if err.ndim else 0.5 * err


def _adamw(w, g, m, v):
    m = ADAM_B1 * m + (1.0 - ADAM_B1) * g
    v = ADAM_B2 * v + (1.0 - ADAM_B2) * _jnp.square(g)
    m_hat = m / (1.0 - ADAM_B1 ** ADAM_STEP)
    v_hat = v / (1.0 - ADAM_B2 ** ADAM_STEP)
    delta = -ADAM_LR * (m_hat / (_jnp.sqrt(v_hat) + ADAM_EPS) + ADAM_WD * w)
    return delta, m, v


def reference(x, mem, positions, norm_mix_g, norm_mem_q_g, norm_mem_kv_g, norm_ffn_g, final_norm_g, ab_w_in, a_v_norm_g, a_w_s, a_b_s, b_conv_w, ab_w_out, c_w_in, c_q_norm_g, c_kv_norm_g, c_w_uq, c_w_ukv, c_w_out, m_wq, m_wk, m_wv, m_wo, f_w1, f_w2, loss_target, m_norm_mix_g, m_norm_mem_q_g, m_norm_mem_kv_g, m_norm_ffn_g, m_final_norm_g, m_ab_w_in, m_a_v_norm_g, m_a_w_s, m_a_b_s, m_b_conv_w, m_ab_w_out, m_c_w_in, m_c_q_norm_g, m_c_kv_norm_g, m_c_w_uq, m_c_w_ukv, m_c_w_out, m_m_wq, m_m_wk, m_m_wv, m_m_wo, m_f_w1, m_f_w2, v_norm_mix_g, v_norm_mem_q_g, v_norm_mem_kv_g, v_norm_ffn_g, v_final_norm_g, v_ab_w_in, v_a_v_norm_g, v_a_w_s, v_a_b_s, v_b_conv_w, v_ab_w_out, v_c_w_in, v_c_q_norm_g, v_c_kv_norm_g, v_c_w_uq, v_c_w_ukv, v_c_w_out, v_m_wq, v_m_wk, v_m_wv, v_m_wo, v_f_w1, v_f_w2):
    given = dict(x=x, mem=mem, positions=positions, norm_mix_g=norm_mix_g, norm_mem_q_g=norm_mem_q_g, norm_mem_kv_g=norm_mem_kv_g, norm_ffn_g=norm_ffn_g, final_norm_g=final_norm_g, ab_w_in=ab_w_in, a_v_norm_g=a_v_norm_g, a_w_s=a_w_s, a_b_s=a_b_s, b_conv_w=b_conv_w, ab_w_out=ab_w_out, c_w_in=c_w_in, c_q_norm_g=c_q_norm_g, c_kv_norm_g=c_kv_norm_g, c_w_uq=c_w_uq, c_w_ukv=c_w_ukv, c_w_out=c_w_out, m_wq=m_wq, m_wk=m_wk, m_wv=m_wv, m_wo=m_wo, f_w1=f_w1, f_w2=f_w2, loss_target=loss_target, m_norm_mix_g=m_norm_mix_g, m_norm_mem_q_g=m_norm_mem_q_g, m_norm_mem_kv_g=m_norm_mem_kv_g, m_norm_ffn_g=m_norm_ffn_g, m_final_norm_g=m_final_norm_g, m_ab_w_in=m_ab_w_in, m_a_v_norm_g=m_a_v_norm_g, m_a_w_s=m_a_w_s, m_a_b_s=m_a_b_s, m_b_conv_w=m_b_conv_w, m_ab_w_out=m_ab_w_out, m_c_w_in=m_c_w_in, m_c_q_norm_g=m_c_q_norm_g, m_c_kv_norm_g=m_c_kv_norm_g, m_c_w_uq=m_c_w_uq, m_c_w_ukv=m_c_w_ukv, m_c_w_out=m_c_w_out, m_m_wq=m_m_wq, m_m_wk=m_m_wk, m_m_wv=m_m_wv, m_m_wo=m_m_wo, m_f_w1=m_f_w1, m_f_w2=m_f_w2, v_norm_mix_g=v_norm_mix_g, v_norm_mem_q_g=v_norm_mem_q_g, v_norm_mem_kv_g=v_norm_mem_kv_g, v_norm_ffn_g=v_norm_ffn_g, v_final_norm_g=v_final_norm_g, v_ab_w_in=v_ab_w_in, v_a_v_norm_g=v_a_v_norm_g, v_a_w_s=v_a_w_s, v_a_b_s=v_a_b_s, v_b_conv_w=v_b_conv_w, v_ab_w_out=v_ab_w_out, v_c_w_in=v_c_w_in, v_c_q_norm_g=v_c_q_norm_g, v_c_kv_norm_g=v_c_kv_norm_g, v_c_w_uq=v_c_w_uq, v_c_w_ukv=v_c_w_ukv, v_c_w_out=v_c_w_out, v_m_wq=v_m_wq, v_m_wk=v_m_wk, v_m_wv=v_m_wv, v_m_wo=v_m_wo, v_f_w1=v_f_w1, v_f_w2=v_f_w2)
    weights = {n: given[n] for n in TWIN_WEIGHTS}
    shared = {n: given[n] for n in SHARED_INPUTS}
    per_example = {n: given[n] for n in ['x', 'mem', 'positions']}
    grad_fn = _jax.value_and_grad(_loss, argnums=(0, 1))

    def one_microbatch(ex, loss_target):
        ex = dict(ex)
        diff = ex.pop(TWIN_DIFF_INPUT)
        return grad_fn(weights, diff, {**shared, **ex}, loss_target)

    if N_MICROBATCH == 1:
        loss, (grad_w, grad_x) = one_microbatch(per_example, given["loss_target"])
    else:
        def body(carry, xs):
            loss_sum, grad_sum = carry
            l_k, (gw_k, gx_k) = one_microbatch(xs[0], xs[1])
            with _jax.named_scope("update"):
                return (loss_sum + l_k, _jax.tree.map(_jnp.add, grad_sum, gw_k)), gx_k

        init = (_jnp.zeros((), _jnp.float32), _jax.tree.map(_jnp.zeros_like, weights))
        (loss, grad_w), grad_x = _jax.lax.scan(body, init, (per_example, given["loss_target"]))
    with _jax.named_scope("update"):
        delta_w, new_m, new_v = {}, {}, {}
        for n in TWIN_WEIGHTS:
            delta_w[n], new_m[n], new_v[n] = _adamw(weights[n], grad_w[n], given["m_" + n], given["v_" + n])
    return (loss, grad_x, *[grad_w[n] for n in TWIN_WEIGHTS], *[delta_w[n] for n in TWIN_WEIGHTS],
            *[new_m[n] for n in TWIN_WEIGHTS], *[new_v[n] for n in TWIN_WEIGHTS])
```

```python
import functools
import math

import jax
import jax.numpy as jnp
from jax import lax
from jax.experimental import pallas as pl
from jax.experimental.pallas import tpu as pltpu

F32 = jnp.float32
BF16 = jnp.bfloat16
MESH = pl.DeviceIdType.MESH
N_DEV = 8
LANES = 128

RMS_EPS = 1e-6
CHUNK = 64
GMLP_BLOCK = 128
C_NOPE = 128
C_ROPE = 64
C_V = 128
ROPE_THETA = 10000.0
MEM_HEADS = 4
ADAM_LR = 0.001
ADAM_B1 = 0.9
ADAM_B2 = 0.999
ADAM_EPS = 1e-08
ADAM_WD = 0.01
ADAM_STEP = 10
NEG_BIG = -1e30
VMEM_LIMIT = 56 * 1024 * 1024


def _cp(*sem):
    return pltpu.CompilerParams(dimension_semantics=sem, vmem_limit_bytes=VMEM_LIMIT)


def _tile(n, pref):
    t = min(n, pref)
    while n % t and t > LANES:
        t //= 2
    return n if n % t else t


def _mm(a, b, *, ta=False, tb=False, out_dtypes=(BF16,), extras=(), epilogue=None, tm=1024, tn=1024, tk=512, name):
    m, k = (a.shape[1], a.shape[0]) if ta else a.shape
    n = b.shape[0] if tb else b.shape[1]
    assert (b.shape[1] if tb else b.shape[0]) == k, (a.shape, b.shape, ta, tb)
    tm, tn, tk = _tile(m, tm), _tile(n, tn), _tile(k, tk)
    nk = k // tk
    ne, no = len(extras), len(out_dtypes)
    dims = (((0 if ta else 1,), (1 if tb else 0,)), ((), ()))

    def body(*refs):
        a_ref, b_ref = refs[0], refs[1]
        e_refs = refs[2:2 + ne]
        o_refs = refs[2 + ne:2 + ne + no]
        acc_ref = refs[-1]
        kk = pl.program_id(2)

        @pl.when(kk == 0)
        def _():
            acc_ref[...] = jnp.zeros_like(acc_ref)

        acc_ref[...] += lax.dot_general(a_ref[...].astype(BF16), b_ref[...].astype(BF16), dims, preferred_element_type=F32)

        @pl.when(kk == nk - 1)
        def _():
            acc = acc_ref[...]
            outs = (acc,) if epilogue is None else epilogue(acc, *[e[...] for e in e_refs])
            for o, val in zip(o_refs, outs):
                o[...] = val.astype(o.dtype)

    a_spec = pl.BlockSpec((tk, tm), lambda i, j, kk: (kk, i)) if ta else pl.BlockSpec((tm, tk), lambda i, j, kk: (i, kk))
    b_spec = pl.BlockSpec((tn, tk), lambda i, j, kk: (j, kk)) if tb else pl.BlockSpec((tk, tn), lambda i, j, kk: (kk, j))
    e_specs = []
    for arr, kind in extras:
        if kind == "mn":
            e_specs.append(pl.BlockSpec((tm, tn), lambda i, j, kk: (i, j)))
        else:
            e_specs.append(pl.BlockSpec((tm, arr.shape[1]), lambda i, j, kk: (i, 0)))
    outs = pl.pallas_call(
        body,
        out_shape=tuple(jax.ShapeDtypeStruct((m, n), d) for d in out_dtypes),
        grid=(m // tm, n // tn, nk),
        in_specs=[a_spec, b_spec] + e_specs,
        out_specs=tuple(pl.BlockSpec((tm, tn), lambda i, j, kk: (i, j)) for _ in out_dtypes),
        scratch_shapes=[pltpu.VMEM((tm, tn), F32)],
        compiler_params=_cp("parallel", "parallel", "arbitrary"),
        name=name,
    )(a, b, *[arr for arr, _ in extras])
    return outs[0] if no == 1 else outs


def _epi_add(acc, r):
    return (acc + r,)


def _epi_relu2(acc):
    h = jnp.maximum(acc, 0.0)
    return (acc, h * h)


def _epi_drelu2(acc, h):
    return (acc * (2.0 * jnp.maximum(h.astype(F32), 0.0)),)


def _rope(x, c, s1, s2):
    return x * c + pltpu.roll(x, LANES - C_ROPE // 2, 1) * s1 + pltpu.roll(x, C_ROPE // 2, 1) * s2


def _rope_t(dy, c, s1, s2):
    return dy * c + pltpu.roll(dy * s1, C_ROPE // 2, 1) + pltpu.roll(dy * s2, LANES - C_ROPE // 2, 1)


def _epi_rope_heads(acc, c, s1, s2):
    parts = []
    for h0 in range(0, acc.shape[1], 2 * LANES):
        parts.append(acc[:, h0:h0 + LANES])
        parts.append(_rope(acc[:, h0 + LANES:h0 + 2 * LANES], c, s1, s2))
    return (jnp.concatenate(parts, axis=1),)


def _rmsnorm_fwd(x, g, *, width=None, col=0, name):
    t = x.shape[0]
    w = x.shape[1] if width is None else width
    tm = _tile(t, 512)

    def body(x_ref, g_ref, o_ref):
        xv = x_ref[...].astype(F32)
        r = lax.rsqrt(jnp.mean(xv * xv, axis=-1, keepdims=True) + RMS_EPS)
        o_ref[...] = (xv * r * g_ref[...]).astype(o_ref.dtype)

    return pl.pallas_call(
        body,
        out_shape=jax.ShapeDtypeStruct((t, w), BF16),
        grid=(t // tm,),
        in_specs=[pl.BlockSpec((tm, w), lambda i: (i, col)), pl.BlockSpec((1, w), lambda i: (0, 0))],
        out_specs=pl.BlockSpec((tm, w), lambda i: (i, 0)),
        compiler_params=_cp("parallel"),
        name=name,
    )(x, g.reshape(1, w))


def _rmsnorm_bwd(x, g, dy, res=None, *, width=None, col=0, name):
    t = x.shape[0]
    w = x.shape[1] if width is None else width
    tm = _tile(t, 256)
    has_res = res is not None

    def body(*refs):
        x_ref, g_ref, dy_ref = refs[:3]
        res_ref = refs[3] if has_res else None
        dx_ref, dg_ref = refs[-2], refs[-1]
        xv = x_ref[...].astype(F32)
        r = lax.rsqrt(jnp.mean(xv * xv, axis=-1, keepdims=True) + RMS_EPS)
        xh = xv * r
        dyv = dy_ref[...].astype(F32)
        dxh = dyv * g_ref[...]
        dx = r * (dxh - xh * jnp.mean(dxh * xh, axis=-1, keepdims=True))
        if has_res:
            dx = dx + res_ref[...]
        dx_ref[...] = dx

        @pl.when(pl.program_id(0) == 0)
        def _():
            dg_ref[...] = jnp.zeros_like(dg_ref)

        dg_ref[...] += jnp.sum(dyv * xh, axis=0, keepdims=True)

    in_specs = [pl.BlockSpec((tm, w), lambda i: (i, col)), pl.BlockSpec((1, w), lambda i: (0, 0)), pl.BlockSpec((tm, w), lambda i: (i, 0))]
    args = [x, g.reshape(1, w), dy]
    if has_res:
        in_specs.append(pl.BlockSpec((tm, w), lambda i: (i, 0)))
        args.append(res)
    return pl.pallas_call(
        body,
        out_shape=(jax.ShapeDtypeStruct((t, w), F32), jax.ShapeDtypeStruct((1, w), F32)),
        grid=(t // tm,),
        in_specs=in_specs,
        out_specs=(pl.BlockSpec((tm, w), lambda i: (i, 0)), pl.BlockSpec((1, w), lambda i: (0, 0))),
        compiler_params=_cp("arbitrary"),
        name=name,
    )(*args)


def _final_loss(x, g, target, *, name):
    t, d = x.shape
    tm = _tile(t, 256)

    def body(x_ref, g_ref, t_ref, loss_ref, dx_ref, dg_ref):
        xv = x_ref[...]
        r = lax.rsqrt(jnp.mean(xv * xv, axis=-1, keepdims=True) + RMS_EPS)
        xh = xv * r
        gv = g_ref[...]
        err = xh * gv - t_ref[...]
        dyv = err * (1.0 / d)
        dxh = dyv * gv
        dx_ref[...] = r * (dxh - xh * jnp.mean(dxh * xh, axis=-1, keepdims=True))

        @pl.when(pl.program_id(0) == 0)
        def _():
            dg_ref[...] = jnp.zeros_like(dg_ref)
            loss_ref[...] = jnp.zeros_like(loss_ref)

        dg_ref[...] += jnp.sum(dyv * xh, axis=0, keepdims=True)
        row = jnp.sum(err * err, axis=-1, keepdims=True) * (0.5 / d)
        loss_ref[...] += jnp.sum(row, axis=0, keepdims=True)

    return pl.pallas_call(
        body,
        out_shape=(jax.ShapeDtypeStruct((1, 1), F32), jax.ShapeDtypeStruct((t, d), F32), jax.ShapeDtypeStruct((1, d), F32)),
        grid=(t // tm,),
        in_specs=[pl.BlockSpec((tm, d), lambda i: (i, 0)), pl.BlockSpec((1, d), lambda i: (0, 0)), pl.BlockSpec((tm, d), lambda i: (i, 0))],
        out_specs=(pl.BlockSpec((1, 1), lambda i: (0, 0)), pl.BlockSpec((tm, d), lambda i: (i, 0)), pl.BlockSpec((1, d), lambda i: (0, 0))),
        compiler_params=_cp("arbitrary"),
        name=name,
    )(x, g.reshape(1, d), target)


_GELU_C = math.sqrt(2.0 / math.pi)
_GELU_A = 0.044715


def _gelu(x):
    t = jnp.tanh(_GELU_C * (x + _GELU_A * x * x * x))
    return 0.5 * x * (1.0 + t), t


def _dgelu(x, t):
    return 0.5 * (1.0 + t) + 0.5 * x * (1.0 - t * t) * (_GELU_C * (1.0 + 3.0 * _GELU_A * x * x))


def _spatial_mask():
    ci = lax.broadcasted_iota(jnp.int32, (GMLP_BLOCK, GMLP_BLOCK), 0) // CHUNK
    cj = lax.broadcasted_iota(jnp.int32, (GMLP_BLOCK, GMLP_BLOCK), 1) // CHUNK
    return (cj <= ci).astype(F32)


def _shift_down(x, k):
    rows = lax.broadcasted_iota(jnp.int32, x.shape, 0)
    return jnp.where(rows >= k, pltpu.roll(x, k, 0), 0.0)


def _shift_up(x, k):
    n = x.shape[0]
    rows = lax.broadcasted_iota(jnp.int32, x.shape, 0)
    return jnp.where(rows < n - k, pltpu.roll(x, n - k, 0), 0.0)


def _gmlp_conv_fwd(z, ws, bs, gn, cw, *, seq, name):
    t = z.shape[0]
    g_n = ws.shape[0]
    nb = seq // GMLP_BLOCK
    L = LANES

    def body(z_ref, ws_ref, bs_ref, gn_ref, cw_ref, y_ref):
        u = z_ref[:, 0:L]
        v = z_ref[:, L:2 * L]
        gu, _ = _gelu(u)
        gv, _ = _gelu(v)
        r = lax.rsqrt(jnp.mean(gv * gv, axis=-1, keepdims=True) + RMS_EPS)
        vg = (gv * r * gn_ref[0]).astype(BF16)
        wm = (ws_ref[0] * _spatial_mask()).astype(BF16)
        bias = bs_ref[0]
        for n in range(nb):
            sl = slice(n * GMLP_BLOCK, (n + 1) * GMLP_BLOCK)
            mixed = jnp.dot(wm, vg[sl], preferred_element_type=F32) + bias
            y_ref[sl, 0:L] = (gu[sl] * mixed).astype(y_ref.dtype)
        bg = z_ref[:, 2 * L:3 * L]
        zc = z_ref[:, 3 * L:4 * L] * z_ref[:, 4 * L:5 * L]
        w0, w1, w2 = cw_ref[0, 0:1, :], cw_ref[0, 1:2, :], cw_ref[0, 2:3, :]
        conv = w0 * _shift_down(zc, 2) + w1 * _shift_down(zc, 1) + w2 * zc
        y_ref[:, L:2 * L] = (bg * conv).astype(y_ref.dtype)

    return pl.pallas_call(
        body,
        out_shape=jax.ShapeDtypeStruct((t, 2 * L * g_n), BF16),
        grid=(g_n, t // seq),
        in_specs=[
            pl.BlockSpec((seq, 5 * L), lambda g, b: (b, g)),
            pl.BlockSpec((1, GMLP_BLOCK, GMLP_BLOCK), lambda g, b: (g, 0, 0)),
            pl.BlockSpec((1, GMLP_BLOCK, 1), lambda g, b: (g, 0, 0)),
            pl.BlockSpec((1, 1, L), lambda g, b: (g, 0, 0)),
            pl.BlockSpec((1, 3, L), lambda g, b: (g, 0, 0)),
        ],
        out_specs=pl.BlockSpec((seq, 2 * L), lambda g, b: (b, g)),
        compiler_params=_cp("parallel", "parallel"),
        name=name,
    )(z, ws, bs, gn, cw)


def _gmlp_conv_bwd(z, dy, ws, bs, gn, cw, *, seq, name):
    t = z.shape[0]
    g_n = ws.shape[0]
    nb = seq // GMLP_BLOCK
    L = LANES
    dims_nt = (((1,), (1,)), ((), ()))
    dims_tn = (((0,), (0,)), ((), ()))

    def body(z_ref, dy_ref, ws_ref, bs_ref, gn_ref, cw_ref, dz_ref, dws_ref, dbs_ref, dgn_ref, dcw_ref, dvg_ref):
        @pl.when(pl.program_id(1) == 0)
        def _():
            dws_ref[...] = jnp.zeros_like(dws_ref)
            dbs_ref[...] = jnp.zeros_like(dbs_ref)
            dgn_ref[...] = jnp.zeros_like(dgn_ref)
            dcw_ref[...] = jnp.zeros_like(dcw_ref)

        u = z_ref[:, 0:L]
        v = z_ref[:, L:2 * L]
        gu, tu = _gelu(u)
        gv, tv = _gelu(v)
        r = lax.rsqrt(jnp.mean(gv * gv, axis=-1, keepdims=True) + RMS_EPS)
        gvh = gv * r
        gnv = gn_ref[0]
        vg = (gvh * gnv).astype(BF16)
        mask = _spatial_mask()
        wm = (ws_ref[0] * mask).astype(BF16)
        bias = bs_ref[0]
        dya = dy_ref[:, 0:L].astype(F32)
        dws = jnp.zeros((GMLP_BLOCK, GMLP_BLOCK), F32)
        dbs = jnp.zeros((GMLP_BLOCK, 1), F32)
        for n in range(nb):
            sl = slice(n * GMLP_BLOCK, (n + 1) * GMLP_BLOCK)
            mixed = jnp.dot(wm, vg[sl], preferred_element_type=F32) + bias
            dmixed = dya[sl] * gu[sl]
            dgu = dya[sl] * mixed
            dz_ref[sl, 0:L] = (dgu * _dgelu(u[sl], tu[sl])).astype(dz_ref.dtype)
            dmb = dmixed.astype(BF16)
            dws = dws + lax.dot_general(dmb, vg[sl], dims_nt, preferred_element_type=F32)
            dbs = dbs + jnp.sum(dmixed, axis=1, keepdims=True)
            dvg_ref[sl, :] = lax.dot_general(wm, dmb, dims_tn, preferred_element_type=F32)
        dws_ref[0] += dws * mask
        dbs_ref[0] += dbs
        dvg = dvg_ref[...]
        dgn_ref[0] += jnp.sum(dvg * gvh, axis=0, keepdims=True)
        dvh = dvg * gnv
        dgv = r * (dvh - gvh * jnp.mean(dvh * gvh, axis=-1, keepdims=True))
        dz_ref[:, L:2 * L] = (dgv * _dgelu(v, tv)).astype(dz_ref.dtype)

        bg = z_ref[:, 2 * L:3 * L]
        cg = z_ref[:, 3 * L:4 * L]
        hh = z_ref[:, 4 * L:5 * L]
        zc = cg * hh
        w0, w1, w2 = cw_ref[0, 0:1, :], cw_ref[0, 1:2, :], cw_ref[0, 2:3, :]
        z1 = _shift_down(zc, 1)
        z2 = _shift_down(zc, 2)
        conv = w0 * z2 + w1 * z1 + w2 * zc
        dyb = dy_ref[:, L:2 * L].astype(F32)
        dconv = dyb * bg
        dz_ref[:, 2 * L:3 * L] = (dyb * conv).astype(dz_ref.dtype)
        dzc = w2 * dconv + w1 * _shift_up(dconv, 1) + w0 * _shift_up(dconv, 2)
        dz_ref[:, 3 * L:4 * L] = (dzc * hh).astype(dz_ref.dtype)
        dz_ref[:, 4 * L:5 * L] = (dzc * cg).astype(dz_ref.dtype)
        dcw_ref[0, 0:1, :] += jnp.sum(dconv * z2, axis=0, keepdims=True)
        dcw_ref[0, 1:2, :] += jnp.sum(dconv * z1, axis=0, keepdims=True)
        dcw_ref[0, 2:3, :] += jnp.sum(dconv * zc, axis=0, keepdims=True)

    return pl.pallas_call(
        body,
        out_shape=(
            jax.ShapeDtypeStruct((t, 5 * L * g_n), BF16),
            jax.ShapeDtypeStruct(ws.shape, F32),
            jax.ShapeDtypeStruct(bs.shape, F32),
            jax.ShapeDtypeStruct(gn.shape, F32),
            jax.ShapeDtypeStruct(cw.shape, F32),
        ),
        grid=(g_n, t // seq),
        in_specs=[
            pl.BlockSpec((seq, 5 * L), lambda g, b: (b, g)),
            pl.BlockSpec((seq, 2 * L), lambda g, b: (b, g)),
            pl.BlockSpec((1, GMLP_BLOCK, GMLP_BLOCK), lambda g, b: (g, 0, 0)),
            pl.BlockSpec((1, GMLP_BLOCK, 1), lambda g, b: (g, 0, 0)),
            pl.BlockSpec((1, 1, L), lambda g, b: (g, 0, 0)),
            pl.BlockSpec((1, 3, L), lambda g, b: (g, 0, 0)),
        ],
        out_specs=(
            pl.BlockSpec((seq, 5 * L), lambda g, b: (b, g)),
            pl.BlockSpec((1, GMLP_BLOCK, GMLP_BLOCK), lambda g, b: (g, 0, 0)),
            pl.BlockSpec((1, GMLP_BLOCK, 1), lambda g, b: (g, 0, 0)),
            pl.BlockSpec((1, 1, L), lambda g, b: (g, 0, 0)),
            pl.BlockSpec((1, 3, L), lambda g, b: (g, 0, 0)),
        ),
        scratch_shapes=[pltpu.VMEM((seq, L), F32)],
        compiler_params=_cp("parallel", "arbitrary"),
        name=name,
    )(z, dy, ws, bs, gn, cw)


def _rope_rows(x, c, s1, s2, *, transpose, out_dtype, name):
    t, w = x.shape
    tm = _tile(t, 512)
    only = w == LANES
    fn = _rope_t if transpose else _rope

    def body(x_ref, c_ref, s1_ref, s2_ref, o_ref):
        cv, s1v, s2v = c_ref[...], s1_ref[...], s2_ref[...]
        if only:
            o_ref[...] = fn(x_ref[...].astype(F32), cv, s1v, s2v).astype(o_ref.dtype)
        else:
            for h0 in range(0, w, 2 * LANES):
                o_ref[:, h0:h0 + LANES] = x_ref[:, h0:h0 + LANES].astype(o_ref.dtype)
                o_ref[:, h0 + LANES:h0 + 2 * LANES] = fn(x_ref[:, h0 + LANES:h0 + 2 * LANES].astype(F32), cv, s1v, s2v).astype(o_ref.dtype)

    tab = pl.BlockSpec((tm, LANES), lambda i: (i, 0))
    return pl.pallas_call(
        body,
        out_shape=jax.ShapeDtypeStruct((t, w), out_dtype),
        grid=(t // tm,),
        in_specs=[pl.BlockSpec((tm, w), lambda i: (i, 0)), tab, tab, tab],
        out_specs=pl.BlockSpec((tm, w), lambda i: (i, 0)),
        compiler_params=_cp("parallel"),
        name=name,
    )(x, c, s1, s2)


_NT = (((1,), (1,)), ((), ()))
_TN = (((0,), (0,)), ((), ()))


def _mla_scores(q, kn, kr, row0, scale):
    s = lax.dot_general(q[:, :LANES], kn, _NT, preferred_element_type=F32)
    s = s + lax.dot_general(q[:, LANES:], kr, _NT, preferred_element_type=F32)
    s = s * scale
    qc = (row0 + lax.broadcasted_iota(jnp.int32, s.shape, 0)) // CHUNK
    kc = lax.broadcasted_iota(jnp.int32, s.shape, 1) // CHUNK
    s = jnp.where(kc <= qc, s, NEG_BIG)
    e = jnp.exp(s - jnp.max(s, axis=-1, keepdims=True))
    return e / jnp.sum(e, axis=-1, keepdims=True)


def _mla_fwd(q, kv, kr, *, seq, heads, name):
    t = q.shape[0]
    tq = _tile(seq, 256)
    nq = seq // tq
    scale = (C_NOPE + C_ROPE) ** -0.5

    def body(q_ref, kv_ref, kr_ref, o_ref):
        p = _mla_scores(q_ref[...], kv_ref[:, :LANES], kr_ref[...], pl.program_id(2) * tq, scale)
        o_ref[...] = jnp.dot(p.astype(BF16), kv_ref[:, LANES:], preferred_element_type=F32).astype(o_ref.dtype)

    return pl.pallas_call(
        body,
        out_shape=jax.ShapeDtypeStruct((t, heads * C_V), BF16),
        grid=(t // seq, heads, nq),
        in_specs=[
            pl.BlockSpec((tq, 2 * LANES), lambda b, h, i: (b * nq + i, h)),
            pl.BlockSpec((seq, 2 * LANES), lambda b, h, i: (b, h)),
            pl.BlockSpec((seq, LANES), lambda b, h, i: (b, 0)),
        ],
        out_specs=pl.BlockSpec((tq, C_V), lambda b, h, i: (b * nq + i, h)),
        compiler_params=_cp("parallel", "parallel", "arbitrary"),
        name=name,
    )(q, kv, kr)


def _mla_bwd(q, kv, kr, do, *, seq, heads, name):
    t = q.shape[0]
    tq = _tile(seq, 256)
    nq = seq // tq
    scale = (C_NOPE + C_ROPE) ** -0.5

    def body(q_ref, kv_ref, kr_ref, do_ref, dq_ref, dkv_ref, dkr_ref, acc_ref):
        h, i = pl.program_id(1), pl.program_id(2)

        @pl.when(i == 0)
        def _():
            acc_ref[...] = jnp.zeros_like(acc_ref)

        @pl.when((i == 0) & (h == 0))
        def _():
            dkr_ref[...] = jnp.zeros_like(dkr_ref)

        qv = q_ref[...]
        kn = kv_ref[:, :LANES]
        vv = kv_ref[:, LANES:]
        krv = kr_ref[...]
        dov = do_ref[...]
        p = _mla_scores(qv, kn, krv, i * tq, scale)
        dp = lax.dot_general(dov, vv, _NT, preferred_element_type=F32)
        ds = (p * (dp - jnp.sum(dp * p, axis=-1, keepdims=True)) * scale).astype(BF16)
        dq_ref[:, :LANES] = jnp.dot(ds, kn, preferred_element_type=F32)
        dq_ref[:, LANES:] = jnp.dot(ds, krv, preferred_element_type=F32)
        acc_ref[:, :LANES] += lax.dot_general(ds, qv[:, :LANES], _TN, preferred_element_type=F32)
        acc_ref[:, LANES:] += lax.dot_general(p.astype(BF16), dov, _TN, preferred_element_type=F32)
        dkr_ref[...] += lax.dot_general(ds, qv[:, LANES:], _TN, preferred_element_type=F32)

        @pl.when(i == nq - 1)
        def _():
            dkv_ref[...] = acc_ref[...].astype(dkv_ref.dtype)

    return pl.pallas_call(
        body,
        out_shape=(
            jax.ShapeDtypeStruct((t, heads * 2 * LANES), F32),
            jax.ShapeDtypeStruct((t, heads * 2 * LANES), BF16),
            jax.ShapeDtypeStruct((t, LANES), F32),
        ),
        grid=(t // seq, heads, nq),
        in_specs=[
            pl.BlockSpec((tq, 2 * LANES), lambda b, h, i: (b * nq + i, h)),
            pl.BlockSpec((seq, 2 * LANES), lambda b, h, i: (b, h)),
            pl.BlockSpec((seq, LANES), lambda b, h, i: (b, 0)),
            pl.BlockSpec((tq, C_V), lambda b, h, i: (b * nq + i, h)),
        ],
        out_specs=(
            pl.BlockSpec((tq, 2 * LANES), lambda b, h, i: (b * nq + i, h)),
            pl.BlockSpec((seq, 2 * LANES), lambda b, h, i: (b, h)),
            pl.BlockSpec((seq, LANES), lambda b, h, i: (b, 0)),
        ),
        scratch_shapes=[pltpu.VMEM((seq, 2 * LANES), F32)],
        compiler_params=_cp("parallel", "arbitrary", "arbitrary"),
        name=name,
    )(q, kv, kr, do)


def _softmax_rows(s):
    e = jnp.exp(s - jnp.max(s, axis=-1, keepdims=True))
    return e / jnp.sum(e, axis=-1, keepdims=True)


def _mem_fwd(q, k, v, *, seq, n_mem, name):
    t, d = q.shape
    hd = d // MEM_HEADS
    tq = _tile(seq, 512)
    nq = seq // tq
    scale = hd ** -0.5

    def body(q_ref, k_ref, v_ref, o_ref):
        s = lax.dot_general(q_ref[...], k_ref[...], _NT, preferred_element_type=F32) * scale
        p = _softmax_rows(s)
        o_ref[...] = jnp.dot(p.astype(BF16), v_ref[...], preferred_element_type=F32).astype(o_ref.dtype)

    return pl.pallas_call(
        body,
        out_shape=jax.ShapeDtypeStruct((t, d), BF16),
        grid=(t // seq, MEM_HEADS, nq),
        in_specs=[
            pl.BlockSpec((tq, hd), lambda b, h, i: (b * nq + i, h)),
            pl.BlockSpec((n_mem, hd), lambda b, h, i: (b, h)),
            pl.BlockSpec((n_mem, hd), lambda b, h, i: (b, h)),
        ],
        out_specs=pl.BlockSpec((tq, hd), lambda b, h, i: (b * nq + i, h)),
        compiler_params=_cp("parallel", "parallel", "arbitrary"),
        name=name,
    )(q, k, v)


def _mem_bwd(q, k, v, do, *, seq, n_mem, name):
    t, d = q.shape
    hd = d // MEM_HEADS
    tq = _tile(seq, 512)
    nq = seq // tq
    scale = hd ** -0.5

    def body(q_ref, k_ref, v_ref, do_ref, dq_ref, dk_ref, dv_ref):
        @pl.when(pl.program_id(2) == 0)
        def _():
            dk_ref[...] = jnp.zeros_like(dk_ref)
            dv_ref[...] = jnp.zeros_like(dv_ref)

        qv, kv_, vv, dov = q_ref[...], k_ref[...], v_ref[...], do_ref[...]
        p = _softmax_rows(lax.dot_general(qv, kv_, _NT, preferred_element_type=F32) * scale)
        dp = lax.dot_general(dov, vv, _NT, preferred_element_type=F32)
        ds = (p * (dp - jnp.sum(dp * p, axis=-1, keepdims=True)) * scale).astype(BF16)
        dq_ref[...] = jnp.dot(ds, kv_, preferred_element_type=F32).astype(dq_ref.dtype)
        dk_ref[...] += lax.dot_general(ds, qv, _TN, preferred_element_type=F32)
        dv_ref[...] += lax.dot_general(p.astype(BF16), dov, _TN, preferred_element_type=F32)

    return pl.pallas_call(
        body,
        out_shape=(
            jax.ShapeDtypeStruct((t, d), BF16),
            jax.ShapeDtypeStruct(k.shape, F32),
            jax.ShapeDtypeStruct(v.shape, F32),
        ),
        grid=(t // seq, MEM_HEADS, nq),
        in_specs=[
            pl.BlockSpec((tq, hd), lambda b, h, i: (b * nq + i, h)),
            pl.BlockSpec((n_mem, hd), lambda b, h, i: (b, h)),
            pl.BlockSpec((n_mem, hd), lambda b, h, i: (b, h)),
            pl.BlockSpec((tq, hd), lambda b, h, i: (b * nq + i, h)),
        ],
        out_specs=(
            pl.BlockSpec((tq, hd), lambda b, h, i: (b * nq + i, h)),
            pl.BlockSpec((n_mem, hd), lambda b, h, i: (b, h)),
            pl.BlockSpec((n_mem, hd), lambda b, h, i: (b, h)),
        ),
        compiler_params=_cp("parallel", "parallel", "arbitrary"),
        name=name,
    )(q, k, v, do)


def _me():
    return lax.axis_index("x"), lax.axis_index("y"), lax.axis_index("c")


def _dev_index(px, py, pc):
    return 4 * px + 2 * py + pc


def _chunk(ref, idx, axis, size):
    start = pl.multiple_of(idx * size, size)
    return ref.at[pl.ds(start, size), :] if axis == 0 else ref.at[:, pl.ds(start, size)]


def _all_gather_small(v):
    m, n = v.shape

    def body(x_ref, out_ref, send_sems, recv_sems, local_sem):
        x, y, c = _me()
        me, sibling = (x, y, c), (x, y, 1 - c)
        chips = [(1 - x, y), (x, 1 - y), (1 - x, 1 - y)]

        def rows(px, py, pc):
            return out_ref.at[_dev_index(px, py, pc)]

        def copy(k, block, to, src=None):
            return pltpu.make_async_remote_copy(
                src_ref=rows(*block) if src is None else src, dst_ref=rows(*block),
                send_sem=send_sems.at[k], recv_sem=recv_sems.at[k], device_id=to, device_id_type=MESH)

        mine = pltpu.make_async_copy(x_ref, rows(*me), local_sem)
        mine.start()
        first = [copy(0, me, sibling, src=x_ref)]
        first += [copy(1 + j, me, (*chip, c), src=x_ref) for j, chip in enumerate(chips)]
        for cp in first:
            cp.start()
        passed = [copy(4 + j, (*chip, c), sibling) for j, chip in enumerate(chips)]
        for j, chip in enumerate(chips):
            copy(1 + j, (*chip, c), me).wait_recv()
            passed[j].start()
        copy(0, sibling, me).wait_recv()
        for j, chip in enumerate(chips):
            copy(4 + j, (*chip, 1 - c), me).wait_recv()
        for cp in first + passed:
            cp.wait_send()
        mine.wait()

    return pl.pallas_call(
        body,
        out_shape=jax.ShapeDtypeStruct((N_DEV, m, n), v.dtype),
        in_specs=[pl.BlockSpec(memory_space=pltpu.VMEM)],
        out_specs=pl.BlockSpec(memory_space=pltpu.VMEM),
        scratch_shapes=[pltpu.SemaphoreType.DMA((7,)), pltpu.SemaphoreType.DMA((7,)), pltpu.SemaphoreType.DMA],
        name="all_gather_small",
    )(v)


def _all_gather_weights(shards, axes, *, name):
    n = len(shards)
    sizes = [s.shape[a] for s, a in zip(shards, axes)]

    def body(*refs):
        ins, outs = refs[:n], refs[n:2 * n]
        send_sems, recv_sems, local_sems = refs[2 * n:]
        x, y, c = _me()
        me, sibling = (x, y, c), (x, y, 1 - c)
        chips = [(1 - x, y), (x, 1 - y), (1 - x, 1 - y)]

        def rows(t, dev):
            return _chunk(outs[t], _dev_index(*dev), axes[t], sizes[t])

        def copy(t, k, block, to, src=None):
            return pltpu.make_async_remote_copy(
                src_ref=rows(t, block) if src is None else src, dst_ref=rows(t, block),
                send_sem=send_sems.at[t, k], recv_sem=recv_sems.at[t, k], device_id=to, device_id_type=MESH)

        mine, first, passed = [], [], []
        for t in range(n):
            cp = pltpu.make_async_copy(ins[t], rows(t, me), local_sems.at[t])
            cp.start()
            mine.append(cp)
            f = [copy(t, 0, me, sibling, src=ins[t])]
            f += [copy(t, 1 + j, me, (*chip, c), src=ins[t]) for j, chip in enumerate(chips)]
            for cp in f:
                cp.start()
            first.append(f)
        for t in range(n):
            p = [copy(t, 4 + j, (*chip, c), sibling) for j, chip in enumerate(chips)]
            for j, chip in enumerate(chips):
                copy(t, 1 + j, (*chip, c), me).wait_recv()
                p[j].start()
            passed.append(p)
        for t in range(n):
            copy(t, 0, sibling, me).wait_recv()
            for j, chip in enumerate(chips):
                copy(t, 4 + j, (*chip, 1 - c), me).wait_recv()
            for cp in first[t] + passed[t]:
                cp.wait_send()
            mine[t].wait()

    out_shape = []
    for s, a in zip(shards, axes):
        full = list(s.shape)
        full[a] *= N_DEV
        out_shape.append(jax.ShapeDtypeStruct(tuple(full), s.dtype))
    any_spec = pl.BlockSpec(memory_space=pl.ANY)
    return pl.pallas_call(
        body,
        out_shape=tuple(out_shape),
        in_specs=[any_spec] * n,
        out_specs=tuple([any_spec] * n),
        scratch_shapes=[pltpu.SemaphoreType.DMA((n, 7)), pltpu.SemaphoreType.DMA((n, 7)), pltpu.SemaphoreType.DMA((n,))],
        name=name,
    )(*shards)


def _scatter_grads(grads, axes, *, name):
    n = len(grads)
    sizes = [g.shape[a] // N_DEV for g, a in zip(grads, axes)]

    def body(*refs):
        ins, outs = refs[:n], refs[n:2 * n]
        send_sems, recv_sems, local_sems = refs[2 * n:]
        x, y, c = _me()
        me = (x, y, c)
        my_idx = _dev_index(*me)
        peers = []
        for k in range(1, N_DEV):
            fx, fy, fc = (k >> 2) & 1, (k >> 1) & 1, k & 1
            peers.append((1 - x if fx else x, 1 - y if fy else y, 1 - c if fc else c))

        def copy(t, k, peer):
            return pltpu.make_async_remote_copy(
                src_ref=_chunk(ins[t], _dev_index(*peer), axes[t], sizes[t]), dst_ref=outs[t].at[my_idx],
                send_sem=send_sems.at[t, k], recv_sem=recv_sems.at[t, k], device_id=peer, device_id_type=MESH)

        def landed(t, k, peer):
            return pltpu.make_async_remote_copy(
                src_ref=_chunk(ins[t], my_idx, axes[t], sizes[t]), dst_ref=outs[t].at[_dev_index(*peer)],
                send_sem=send_sems.at[t, k], recv_sem=recv_sems.at[t, k], device_id=peer, device_id_type=MESH)

        mine, sent = [], []
        for t in range(n):
            cp = pltpu.make_async_copy(_chunk(ins[t], my_idx, axes[t], sizes[t]), outs[t].at[my_idx], local_sems.at[t])
            cp.start()
            mine.append(cp)
            for k, peer in enumerate(peers):
                cp = copy(t, k, peer)
                cp.start()
                sent.append(cp)
        for t in range(n):
            for k, peer in enumerate(peers):
                landed(t, k, peer).wait_recv()
            mine[t].wait()
        for cp in sent:
            cp.wait_send()

    out_shape = []
    for g, a, sz in zip(grads, axes, sizes):
        sh = list(g.shape)
        sh[a] = sz
        out_shape.append(jax.ShapeDtypeStruct((N_DEV, *sh), g.dtype))
    any_spec = pl.BlockSpec(memory_space=pl.ANY)
    return pl.pallas_call(
        body,
        out_shape=tuple(out_shape),
        in_specs=[any_spec] * n,
        out_specs=tuple([any_spec] * n),
        scratch_shapes=[pltpu.SemaphoreType.DMA((n, 7)), pltpu.SemaphoreType.DMA((n, 7)), pltpu.SemaphoreType.DMA((n,))],
        name=name,
    )(*grads)


def _adam_math(w, g, m, v):
    m = ADAM_B1 * m + (1.0 - ADAM_B1) * g
    v = ADAM_B2 * v + (1.0 - ADAM_B2) * (g * g)
    m_hat = m / (1.0 - ADAM_B1 ** ADAM_STEP)
    v_hat = v / (1.0 - ADAM_B2 ** ADAM_STEP)
    delta = -ADAM_LR * (m_hat / (jnp.sqrt(v_hat) + ADAM_EPS) + ADAM_WD * w)
    return delta, m, v


def _adamw(w, m, v, parts, *, name):
    n_l, r, c = w.shape
    tr = _tile(r, 64)
    nr = r // tr

    def body(*refs):
        w_ref, m_ref, v_ref = refs[:3]
        p_refs = refs[3:3 + n_l]
        g_ref, d_ref, nm_ref, nv_ref = refs[3 + n_l:]
        for l in range(n_l):
            @pl.when(pl.program_id(0) == l)
            def _(l=l):
                g = p_refs[l][0].astype(F32)
                for s in range(1, N_DEV):
                    g = g + p_refs[l][s].astype(F32)
                delta, nm, nv = _adam_math(w_ref[...], g, m_ref[...], v_ref[...])
                g_ref[...] = g
                d_ref[...] = delta
                nm_ref[...] = nm
                nv_ref[...] = nv

    wspec = pl.BlockSpec((None, tr, c), lambda l, i: (l, i, 0))
    pspecs = [pl.BlockSpec((N_DEV, tr, c), functools.partial(lambda l, i, ll: (0, jnp.where(l == ll, i, 0), 0), ll=ll)) for ll in range(n_l)]
    sds = jax.ShapeDtypeStruct(w.shape, F32)
    return pl.pallas_call(
        body,
        out_shape=(sds, sds, sds, sds),
        grid=(n_l, nr),
        in_specs=[wspec, wspec, wspec] + pspecs,
        out_specs=(wspec, wspec, wspec, wspec),
        compiler_params=_cp("arbitrary", "arbitrary"),
        name=name,
    )(w, m, v, *parts)


def _adamw_small(w, m, v, parts, *, name):
    r, c = w.shape
    tr = _tile(r, 512)

    def body(w_ref, m_ref, v_ref, p_ref, g_ref, d_ref, nm_ref, nv_ref):
        g = p_ref[0]
        for s in range(1, N_DEV):
            g = g + p_ref[s]
        delta, nm, nv = _adam_math(w_ref[...], g, m_ref[...], v_ref[...])
        g_ref[...] = g
        d_ref[...] = delta
        nm_ref[...] = nm
        nv_ref[...] = nv

    spec = pl.BlockSpec((tr, c), lambda i: (i, 0))
    sds = jax.ShapeDtypeStruct(w.shape, F32)
    return pl.pallas_call(
        body,
        out_shape=(sds, sds, sds, sds),
        grid=(r // tr,),
        in_specs=[spec, spec, spec, pl.BlockSpec((N_DEV, tr, c), lambda i: (0, i, 0))],
        out_specs=(spec, spec, spec, spec),
        compiler_params=_cp("parallel"),
        name=name,
    )(w, m, v, parts)


def _pack(arrs):
    flat = jnp.concatenate([a.reshape(-1).astype(F32) for a in arrs])
    pad = (-flat.shape[0]) % (8 * LANES)
    return jnp.pad(flat, (0, pad)).reshape(-1, LANES)


def _unpack(packed, shapes):
    flat = packed.reshape(-1)
    out, off = [], 0
    for sh in shapes:
        n = math.prod(sh)
        out.append(flat[off:off + n].reshape(sh))
        off += n
    return out


def _unpack_gathered(packed, shapes, axis):
    flat = packed.reshape(N_DEV, -1)
    out, off = [], 0
    for sh, ax in zip(shapes, axis):
        n = math.prod(sh)
        blocks = flat[:, off:off + n].reshape((N_DEV, *sh))
        out.append(jnp.concatenate([blocks[d] for d in range(N_DEV)], axis=ax))
        off += n
    return out


def _my_block(full, axis, size):
    idx = _dev_index(*_me())
    return lax.dynamic_slice_in_dim(full, idx * size, size, axis)


def kernel(x, mem, positions, norm_mix_g, norm_mem_q_g, norm_mem_kv_g, norm_ffn_g, final_norm_g, ab_w_in, a_v_norm_g, a_w_s, a_b_s, b_conv_w, ab_w_out, c_w_in, c_q_norm_g, c_kv_norm_g, c_w_uq, c_w_ukv, c_w_out, m_wq, m_wk, m_wv, m_wo, f_w1, f_w2, loss_target, m_norm_mix_g, m_norm_mem_q_g, m_norm_mem_kv_g, m_norm_ffn_g, m_final_norm_g, m_ab_w_in, m_a_v_norm_g, m_a_w_s, m_a_b_s, m_b_conv_w, m_ab_w_out, m_c_w_in, m_c_q_norm_g, m_c_kv_norm_g, m_c_w_uq, m_c_w_ukv, m_c_w_out, m_m_wq, m_m_wk, m_m_wv, m_m_wo, m_f_w1, m_f_w2, v_norm_mix_g, v_norm_mem_q_g, v_norm_mem_kv_g, v_norm_ffn_g, v_final_norm_g, v_ab_w_in, v_a_v_norm_g, v_a_w_s, v_a_b_s, v_b_conv_w, v_ab_w_out, v_c_w_in, v_c_q_norm_g, v_c_kv_norm_g, v_c_w_uq, v_c_w_ukv, v_c_w_out, v_m_wq, v_m_wk, v_m_wv, v_m_wo, v_f_w1, v_f_w2):
    bl, seq, d = x.shape
    t = bl * seq
    n_mem = mem.shape[1]
    depth = norm_mix_g.shape[0]
    n_even, n_odd = ab_w_in.shape[0], c_w_in.shape[0]
    a_width = a_v_norm_g.shape[1]
    groups = a_w_s.shape[1]
    heads = d // C_NOPE
    q_rank = c_q_norm_g.shape[1] * N_DEV
    kv_rank = c_kv_norm_g.shape[1] * N_DEV

    big = {
        "ab_w_in": (ab_w_in, m_ab_w_in, v_ab_w_in, 1), "ab_w_out": (ab_w_out, m_ab_w_out, v_ab_w_out, 0),
        "c_w_in": (c_w_in, m_c_w_in, v_c_w_in, 0), "c_w_uq": (c_w_uq, m_c_w_uq, v_c_w_uq, 1),
        "c_w_ukv": (c_w_ukv, m_c_w_ukv, v_c_w_ukv, 1), "c_w_out": (c_w_out, m_c_w_out, v_c_w_out, 0),
        "m_wq": (m_wq, m_m_wq, v_m_wq, 0), "m_wk": (m_wk, m_m_wk, v_m_wk, 0), "m_wv": (m_wv, m_m_wv, v_m_wv, 0),
        "m_wo": (m_wo, m_m_wo, v_m_wo, 0), "f_w1": (f_w1, m_f_w1, v_f_w1, 1), "f_w2": (f_w2, m_f_w2, v_f_w2, 0),
    }
    even_names = ["ab_w_in", "ab_w_out"]
    odd_names = ["c_w_in", "c_w_uq", "c_w_ukv", "c_w_out"]
    all_names = ["m_wq", "m_wk", "m_wv", "m_wo", "f_w1", "f_w2"]

    def layer_names(layer):
        return (even_names if layer % 2 == 0 else odd_names) + all_names

    def sub(name, layer):
        return layer // 2 if name in even_names or name in odd_names else layer

    split_small = [(c_q_norm_g, m_c_q_norm_g, v_c_q_norm_g, 1), (c_kv_norm_g, m_c_kv_norm_g, v_c_kv_norm_g, 1), (b_conv_w, m_b_conv_w, v_b_conv_w, 2)]
    shard_shapes = [s[0].shape for s in split_small] * 3
    shard_axes = [s[3] for s in split_small] * 3
    gathered = _all_gather_small(_pack([s[0] for s in split_small] + [s[1] for s in split_small] + [s[2] for s in split_small]))
    whole = _unpack_gathered(gathered, shard_shapes, shard_axes)
    q_norm_full, kv_norm_full, conv_full = whole[0:3]
    small_w = [norm_mix_g, norm_mem_q_g, norm_mem_kv_g, norm_ffn_g, final_norm_g, a_v_norm_g, a_w_s, a_b_s] + whole[0:3]
    small_m = [m_norm_mix_g, m_norm_mem_q_g, m_norm_mem_kv_g, m_norm_ffn_g, m_final_norm_g, m_a_v_norm_g, m_a_w_s, m_a_b_s] + whole[3:6]
    small_v = [v_norm_mix_g, v_norm_mem_q_g, v_norm_mem_kv_g, v_norm_ffn_g, v_final_norm_g, v_a_v_norm_g, v_a_w_s, v_a_b_s] + whole[6:9]

    inv = ROPE_THETA ** (-jnp.arange(0, C_ROPE, 2, dtype=F32) / C_ROPE)
    ang = positions.astype(F32).reshape(t, 1) * inv
    cos, sin, zero = jnp.cos(ang), jnp.sin(ang), jnp.zeros((t, C_ROPE // 2), F32)
    rc = jnp.concatenate([cos, cos, zero, zero], axis=1)
    rs1 = jnp.concatenate([-sin, zero, zero, zero], axis=1)
    rs2 = jnp.concatenate([zero, sin, zero, zero], axis=1)

    wts = []
    for layer in range(depth):
        names = layer_names(layer)
        shards = [big[nm][0][sub(nm, layer)].astype(BF16) for nm in names]
        full = _all_gather_weights(shards, [big[nm][3] for nm in names], name=f"all_gather_l{layer}")
        wts.append(dict(zip(names, full)))

    def group_cols(w_in):
        return w_in.reshape(d, 5, groups, LANES).transpose(0, 2, 1, 3).reshape(d, 5 * a_width)

    def ungroup_cols(g_in):
        return g_in.reshape(d, groups, 5, LANES).transpose(0, 2, 1, 3).reshape(d, 5 * a_width)

    def group_rows(w_out):
        return w_out.reshape(2, groups, LANES, d).transpose(1, 0, 2, 3).reshape(2 * a_width, d)

    def ungroup_rows(g_out):
        return g_out.reshape(groups, 2, LANES, d).transpose(1, 0, 2, 3).reshape(2 * a_width, d)

    def pad_heads(w_uq):
        w3 = w_uq.reshape(q_rank, heads, C_NOPE + C_ROPE)
        return jnp.pad(w3, ((0, 0), (0, 0), (0, LANES - C_ROPE))).reshape(q_rank, heads * 2 * LANES)

    def unpad_heads(g_uq):
        return g_uq.reshape(q_rank, heads, 2 * LANES)[:, :, :C_NOPE + C_ROPE].reshape(q_rank, heads * (C_NOPE + C_ROPE))

    x2 = x.reshape(t, d)
    mem2 = mem.reshape(bl * n_mem, d)
    saved = []

    for layer in range(depth):
        w = wts[layer]
        sv = {"x0": x2}
        xn = _rmsnorm_fwd(x2, norm_mix_g[layer], name="norm_mix")
        sv["xn"] = xn
        if layer % 2 == 0:
            e = layer // 2
            w_in_g = group_cols(w["ab_w_in"])
            w_out_g = group_rows(w["ab_w_out"])
            z = _mm(xn, w_in_g, out_dtypes=(F32,), name="ab_in")
            ws_e = a_w_s[e]
            bs_e = a_b_s[e].reshape(groups, GMLP_BLOCK, 1)
            gn_e = a_v_norm_g[e].reshape(groups, 1, LANES)
            cw_e = conv_full[e].reshape(3, groups, LANES).transpose(1, 0, 2)
            ycat = _gmlp_conv_fwd(z, ws_e, bs_e, gn_e, cw_e, seq=seq, name="gmlp_conv_fwd")
            x2 = _mm(ycat, w_out_g, out_dtypes=(F32,), extras=[(x2, "mn")], epilogue=_epi_add, name="ab_out")
            sv.update(z=z, ycat=ycat, w_in_g=w_in_g, w_out_g=w_out_g, ws=ws_e, bs=bs_e, gn=gn_e, cw=cw_e)
        else:
            o = layer // 2
            w_uq_p = pad_heads(w["c_w_uq"])
            z = _mm(xn, w["c_w_in"], out_dtypes=(F32,), name="c_in")
            qn = _rmsnorm_fwd(z, q_norm_full[o], width=q_rank, col=0, name="norm_cq")
            kvn = _rmsnorm_fwd(z, kv_norm_full[o], width=kv_rank, col=q_rank // kv_rank, name="norm_ckv")
            qh = _mm(qn, w_uq_p, extras=[(rc, "m"), (rs1, "m"), (rs2, "m")], epilogue=_epi_rope_heads, tn=1024, name="c_uq")
            kvh = _mm(kvn, w["c_w_ukv"], name="c_ukv")
            k_rope = jnp.pad(z[:, q_rank + kv_rank:], ((0, 0), (0, LANES - C_ROPE)))
            kr = _rope_rows(k_rope, rc, rs1, rs2, transpose=False, out_dtype=BF16, name="rope_k")
            att = _mla_fwd(qh, kvh, kr, seq=seq, heads=heads, name="mla_fwd")
            x2 = _mm(att, w["c_w_out"], out_dtypes=(F32,), extras=[(x2, "mn")], epilogue=_epi_add, name="c_out")
            sv.update(z=z, qn=qn, kvn=kvn, qh=qh, kvh=kvh, kr=kr, att=att, w_uq_p=w_uq_p)
        sv["x1"] = x2
        xq = _rmsnorm_fwd(x2, norm_mem_q_g[layer], name="norm_mem_q")
        memn = _rmsnorm_fwd(mem2, norm_mem_kv_g[layer], name="norm_mem_kv")
        mq = _mm(xq, w["m_wq"], name="m_q")
        mk = _mm(memn, w["m_wk"], name="m_k")
        mv = _mm(memn, w["m_wv"], name="m_v")
        mo = _mem_fwd(mq, mk, mv, seq=seq, n_mem=n_mem, name="mem_fwd")
        x2 = _mm(mo, w["m_wo"], out_dtypes=(F32,), extras=[(x2, "mn")], epilogue=_epi_add, name="m_o")
        sv.update(xq=xq, memn=memn, mq=mq, mk=mk, mv=mv, mo=mo)
        sv["x2"] = x2
        xf = _rmsnorm_fwd(x2, norm_ffn_g[layer], name="norm_ffn")
        hpre, hact = _mm(xf, w["f_w1"], out_dtypes=(BF16, BF16), epilogue=_epi_relu2, name="f_1")
        x2 = _mm(hact, w["f_w2"], out_dtypes=(F32,), extras=[(x2, "mn")], epilogue=_epi_add, name="f_2")
        sv.update(xf=xf, hpre=hpre, hact=hact)
        saved.append(sv)

    loss_part, dx, dg_final = _final_loss(x2, final_norm_g, loss_target.reshape(t, d), name="final_loss")
    loss = lax.psum(loss_part[0, 0], ("x", "y", "c"))

    dg_mix, dg_mq, dg_mkv, dg_ffn = [None] * depth, [None] * depth, [None] * depth, [None] * depth
    d_av, d_ws, d_bs = [None] * n_even, [None] * n_even, [None] * n_even
    d_conv, d_qn, d_kvn = [None] * n_even, [None] * n_odd, [None] * n_odd
    parts = {nm: [None] * big[nm][0].shape[0] for nm in big}

    for layer in reversed(range(depth)):
        w, sv = wts[layer], saved[layer]
        grads = {}
        dh = _mm(dx, w["f_w2"], tb=True, extras=[(sv["hpre"], "mn")], epilogue=_epi_drelu2, name="f_2_dx")
        grads["f_w2"] = _mm(sv["hact"], dx, ta=True, name="f_2_dw")
        grads["f_w1"] = _mm(sv["xf"], dh, ta=True, name="f_1_dw")
        dxf = _mm(dh, w["f_w1"], tb=True, name="f_1_dx")
        dx, dg_ffn[layer] = _rmsnorm_bwd(sv["x2"], norm_ffn_g[layer], dxf, dx, name="norm_ffn_bwd")
        dmo = _mm(dx, w["m_wo"], tb=True, name="m_o_dx")
        grads["m_wo"] = _mm(sv["mo"], dx, ta=True, name="m_o_dw")
        dmq, dmk, dmv = _mem_bwd(sv["mq"], sv["mk"], sv["mv"], dmo, seq=seq, n_mem=n_mem, name="mem_bwd")
        grads["m_wq"] = _mm(sv["xq"], dmq, ta=True, name="m_q_dw")
        grads["m_wk"] = _mm(sv["memn"], dmk, ta=True, name="m_k_dw")
        grads["m_wv"] = _mm(sv["memn"], dmv, ta=True, name="m_v_dw")
        dxq = _mm(dmq, w["m_wq"], tb=True, name="m_q_dx")
        dmemn_k = _mm(dmk, w["m_wk"], tb=True, out_dtypes=(F32,), name="m_k_dx")
        dmemn = _mm(dmv, w["m_wv"], tb=True, out_dtypes=(F32,), extras=[(dmemn_k, "mn")], epilogue=_epi_add, name="m_v_dx")
        _, dg_mkv[layer] = _rmsnorm_bwd(mem2, norm_mem_kv_g[layer], dmemn, name="norm_mem_kv_bwd")
        dx, dg_mq[layer] = _rmsnorm_bwd(sv["x1"], norm_mem_q_g[layer], dxq, dx, name="norm_mem_q_bwd")
        if layer % 2 == 0:
            e = layer // 2
            dycat = _mm(dx, sv["w_out_g"], tb=True, name="ab_out_dx")
            grads["ab_w_out"] = ungroup_rows(_mm(sv["ycat"], dx, ta=True, name="ab_out_dw"))
            dz, dws, dbs, dgn, dcw = _gmlp_conv_bwd(sv["z"], dycat, sv["ws"], sv["bs"], sv["gn"], sv["cw"], seq=seq, name="gmlp_conv_bwd")
            d_ws[e], d_bs[e], d_av[e] = dws, dbs.reshape(groups, GMLP_BLOCK), dgn.reshape(a_width)
            d_conv[e] = dcw.transpose(1, 0, 2).reshape(3, a_width)
            grads["ab_w_in"] = ungroup_cols(_mm(sv["xn"], dz, ta=True, name="ab_in_dw"))
            dxn = _mm(dz, sv["w_in_g"], tb=True, name="ab_in_dx")
        else:
            o = layer // 2
            datt = _mm(dx, w["c_w_out"], tb=True, name="c_out_dx")
            grads["c_w_out"] = _mm(sv["att"], dx, ta=True, name="c_out_dw")
            dqh, dkvh, dkr = _mla_bwd(sv["qh"], sv["kvh"], sv["kr"], datt, seq=seq, heads=heads, name="mla_bwd")
            dq_pre = _rope_rows(dqh, rc, rs1, rs2, transpose=True, out_dtype=BF16, name="rope_q_bwd")
            dk_rope = _rope_rows(dkr, rc, rs1, rs2, transpose=True, out_dtype=F32, name="rope_k_bwd")
            grads["c_w_uq"] = unpad_heads(_mm(sv["qn"], dq_pre, ta=True, name="c_uq_dw"))
            grads["c_w_ukv"] = _mm(sv["kvn"], dkvh, ta=True, name="c_ukv_dw")
            dqn = _mm(dq_pre, sv["w_uq_p"], tb=True, name="c_uq_dx")
            dkvn = _mm(dkvh, w["c_w_ukv"], tb=True, name="c_ukv_dx")
            dcq, d_qn[o] = _rmsnorm_bwd(sv["z"], q_norm_full[o], dqn, width=q_rank, col=0, name="norm_cq_bwd")
            dckv, d_kvn[o] = _rmsnorm_bwd(sv["z"], kv_norm_full[o], dkvn, width=kv_rank, col=q_rank // kv_rank, name="norm_ckv_bwd")
            dz = jnp.concatenate([dcq, dckv, dk_rope[:, :C_ROPE]], axis=1)
            grads["c_w_in"] = _mm(sv["xn"], dz, ta=True, name="c_in_dw")
            dxn = _mm(dz, w["c_w_in"], tb=True, name="c_in_dx")
        dx, dg_mix[layer] = _rmsnorm_bwd(sv["x0"], norm_mix_g[layer], dxn, dx, name="norm_mix_bwd")
        names = layer_names(layer)
        recv = _scatter_grads([grads[nm] for nm in names], [big[nm][3] for nm in names], name=f"scatter_grads_l{layer}")
        for nm, r in zip(names, recv):
            parts[nm][sub(nm, layer)] = r

    grad_x = dx.reshape(bl, seq, d)

    big_out = {}
    for nm, (wv, mv_, vv, _) in big.items():
        big_out[nm] = _adamw(wv, mv_, vv, parts[nm], name=f"adamw_{nm}")

    small_g = [jnp.concatenate(dg_mix), jnp.concatenate(dg_mq), jnp.concatenate(dg_mkv), jnp.concatenate(dg_ffn), dg_final.reshape(d),
               jnp.stack(d_av), jnp.stack(d_ws), jnp.stack(d_bs), jnp.concatenate(d_qn), jnp.concatenate(d_kvn), jnp.stack(d_conv)]
    small_shapes = [a.shape for a in small_w]
    g_all = _all_gather_small(_pack(small_g))
    sg, sd, sm, svv = _adamw_small(_pack(small_w), _pack(small_m), _pack(small_v), g_all, name="adamw_small")
    small_names = ["norm_mix_g", "norm_mem_q_g", "norm_mem_kv_g", "norm_ffn_g", "final_norm_g", "a_v_norm_g", "a_w_s", "a_b_s", "c_q_norm_g", "c_kv_norm_g", "b_conv_w"]
    small_out = {}
    unpacked = [_unpack(p, small_shapes) for p in (sg, sd, sm, svv)]
    for i, nm in enumerate(small_names):
        vals = [u[i] for u in unpacked]
        if i >= 8:
            full_w, _, _, ax = split_small[i - 8]
            vals = [_my_block(val, ax, full_w.shape[ax]) for val in vals]
        small_out[nm] = vals

    order = ["norm_mix_g", "norm_mem_q_g", "norm_mem_kv_g", "norm_ffn_g", "final_norm_g", "ab_w_in", "a_v_norm_g", "a_w_s", "a_b_s", "b_conv_w", "ab_w_out",
             "c_w_in", "c_q_norm_g", "c_kv_norm_g", "c_w_uq", "c_w_ukv", "c_w_out", "m_wq", "m_wk", "m_wv", "m_wo", "f_w1", "f_w2"]
    res = [loss, grad_x]
    for kind in range(4):
        for nm in order:
            res.append(big_out[nm][kind] if nm in big_out else small_out[nm][kind])
    return tuple(res)
```

```python
import functools
import math

import jax
import jax.numpy as jnp
from jax import lax
from jax.experimental import pallas as pl
from jax.experimental.pallas import tpu as pltpu

F32 = jnp.float32
BF16 = jnp.bfloat16
MESH = pl.DeviceIdType.MESH
N_DEV = 8
LANES = 128

RMS_EPS = 1e-6
CHUNK = 64
GMLP_BLOCK = 128
C_NOPE = 128
C_ROPE = 64
C_V = 128
ROPE_THETA = 10000.0
MEM_HEADS = 4
ADAM_LR = 0.001
ADAM_B1 = 0.9
ADAM_B2 = 0.999
ADAM_EPS = 1e-08
ADAM_WD = 0.01
ADAM_STEP = 10
NEG_BIG = -1e30
VMEM_LIMIT = 56 * 1024 * 1024


def _cp(*sem):
    return pltpu.CompilerParams(dimension_semantics=sem, vmem_limit_bytes=VMEM_LIMIT)


def _tile(n, pref):
    t = min(n, pref)
    while n % t and t > LANES:
        t //= 2
    return n if n % t else t


def _mm(a, b, *, ta=False, tb=False, out_dtypes=(BF16,), extras=(), epilogue=None, tm=1024, tn=1024, tk=2048, name):
    m, k = (a.shape[1], a.shape[0]) if ta else a.shape
    n = b.shape[0] if tb else b.shape[1]
    assert (b.shape[1] if tb else b.shape[0]) == k, (a.shape, b.shape, ta, tb)
    tm, tn, tk = _tile(m, tm), _tile(n, tn), _tile(k, tk)
    nk = k // tk
    ne, no = len(extras), len(out_dtypes)
    dims = (((0 if ta else 1,), (1 if tb else 0,)), ((), ()))

    def body(*refs):
        a_ref, b_ref = refs[0], refs[1]
        e_refs = refs[2:2 + ne]
        o_refs = refs[2 + ne:2 + ne + no]

        def finish(acc):
            outs = (acc,) if epilogue is None else epilogue(acc, *[e[...] for e in e_refs])
            for o, val in zip(o_refs, outs):
                o[...] = val.astype(o.dtype)

        prod = lax.dot_general(a_ref[...].astype(BF16), b_ref[...].astype(BF16), dims, preferred_element_type=F32)
        if nk == 1:
            finish(prod)
        else:
            acc_ref = refs[-1]
            kk = pl.program_id(2)

            @pl.when(kk == 0)
            def _():
                acc_ref[...] = prod

            @pl.when((kk > 0) & (kk < nk - 1))
            def _():
                acc_ref[...] += prod

            @pl.when(kk == nk - 1)
            def _():
                finish(acc_ref[...] + prod)

    a_spec = pl.BlockSpec((tk, tm), lambda i, j, kk: (kk, i)) if ta else pl.BlockSpec((tm, tk), lambda i, j, kk: (i, kk))
    b_spec = pl.BlockSpec((tn, tk), lambda i, j, kk: (j, kk)) if tb else pl.BlockSpec((tk, tn), lambda i, j, kk: (kk, j))
    e_specs = []
    for arr, kind in extras:
        if kind == "mn":
            e_specs.append(pl.BlockSpec((tm, tn), lambda i, j, kk: (i, j)))
        else:
            e_specs.append(pl.BlockSpec((tm, arr.shape[1]), lambda i, j, kk: (i, 0)))
    outs = pl.pallas_call(
        body,
        out_shape=tuple(jax.ShapeDtypeStruct((m, n), d) for d in out_dtypes),
        grid=(m // tm, n // tn, nk),
        in_specs=[a_spec, b_spec] + e_specs,
        out_specs=tuple(pl.BlockSpec((tm, tn), lambda i, j, kk: (i, j)) for _ in out_dtypes),
        scratch_shapes=[pltpu.VMEM((tm, tn), F32)] if nk > 1 else [],
        compiler_params=_cp("parallel", "parallel", "arbitrary"),
        name=name,
    )(a, b, *[arr for arr, _ in extras])
    return outs[0] if no == 1 else outs


def _epi_add(acc, r):
    return (acc + r,)


def _epi_relu2(acc):
    h = jnp.maximum(acc, 0.0)
    return (acc, h * h)


def _epi_drelu2(acc, h):
    return (acc * (2.0 * jnp.maximum(h.astype(F32), 0.0)),)


def _rope(x, c, s1, s2):
    return x * c + pltpu.roll(x, LANES - C_ROPE // 2, 1) * s1 + pltpu.roll(x, C_ROPE // 2, 1) * s2


def _rope_t(dy, c, s1, s2):
    return dy * c + pltpu.roll(dy * s1, C_ROPE // 2, 1) + pltpu.roll(dy * s2, LANES - C_ROPE // 2, 1)


def _epi_rope_heads(acc, c, s1, s2):
    parts = []
    for h0 in range(0, acc.shape[1], 2 * LANES):
        parts.append(acc[:, h0:h0 + LANES])
        parts.append(_rope(acc[:, h0 + LANES:h0 + 2 * LANES], c, s1, s2))
    return (jnp.concatenate(parts, axis=1),)


def _rmsnorm_fwd(x, g, *, width=None, col=0, name):
    t = x.shape[0]
    w = x.shape[1] if width is None else width
    tm = _tile(t, 512)

    def body(x_ref, g_ref, o_ref):
        xv = x_ref[...].astype(F32)
        r = lax.rsqrt(jnp.mean(xv * xv, axis=-1, keepdims=True) + RMS_EPS)
        o_ref[...] = (xv * r * g_ref[...]).astype(o_ref.dtype)

    return pl.pallas_call(
        body,
        out_shape=jax.ShapeDtypeStruct((t, w), BF16),
        grid=(t // tm,),
        in_specs=[pl.BlockSpec((tm, w), lambda i: (i, col)), pl.BlockSpec((1, w), lambda i: (0, 0))],
        out_specs=pl.BlockSpec((tm, w), lambda i: (i, 0)),
        compiler_params=_cp("parallel"),
        name=name,
    )(x, g.reshape(1, w))


def _rmsnorm_bwd(x, g, dy, res=None, *, width=None, col=0, name):
    t = x.shape[0]
    w = x.shape[1] if width is None else width
    tm = _tile(t, 256)
    has_res = res is not None

    def body(*refs):
        x_ref, g_ref, dy_ref = refs[:3]
        res_ref = refs[3] if has_res else None
        dx_ref, dg_ref = refs[-2], refs[-1]
        xv = x_ref[...].astype(F32)
        r = lax.rsqrt(jnp.mean(xv * xv, axis=-1, keepdims=True) + RMS_EPS)
        xh = xv * r
        dyv = dy_ref[...].astype(F32)
        dxh = dyv * g_ref[...]
        dx = r * (dxh - xh * jnp.mean(dxh * xh, axis=-1, keepdims=True))
        if has_res:
            dx = dx + res_ref[...]
        dx_ref[...] = dx

        @pl.when(pl.program_id(0) == 0)
        def _():
            dg_ref[...] = jnp.zeros_like(dg_ref)

        dg_ref[...] += jnp.sum(dyv * xh, axis=0, keepdims=True)

    in_specs = [pl.BlockSpec((tm, w), lambda i: (i, col)), pl.BlockSpec((1, w), lambda i: (0, 0)), pl.BlockSpec((tm, w), lambda i: (i, 0))]
    args = [x, g.reshape(1, w), dy]
    if has_res:
        in_specs.append(pl.BlockSpec((tm, w), lambda i: (i, 0)))
        args.append(res)
    return pl.pallas_call(
        body,
        out_shape=(jax.ShapeDtypeStruct((t, w), F32), jax.ShapeDtypeStruct((1, w), F32)),
        grid=(t // tm,),
        in_specs=in_specs,
        out_specs=(pl.BlockSpec((tm, w), lambda i: (i, 0)), pl.BlockSpec((1, w), lambda i: (0, 0))),
        compiler_params=_cp("arbitrary"),
        name=name,
    )(*args)


def _final_loss(x, g, target, *, name):
    t, d = x.shape
    tm = _tile(t, 256)

    def body(x_ref, g_ref, t_ref, loss_ref, dx_ref, dg_ref):
        xv = x_ref[...]
        r = lax.rsqrt(jnp.mean(xv * xv, axis=-1, keepdims=True) + RMS_EPS)
        xh = xv * r
        gv = g_ref[...]
        err = xh * gv - t_ref[...]
        dyv = err * (1.0 / d)
        dxh = dyv * gv
        dx_ref[...] = r * (dxh - xh * jnp.mean(dxh * xh, axis=-1, keepdims=True))

        @pl.when(pl.program_id(0) == 0)
        def _():
            dg_ref[...] = jnp.zeros_like(dg_ref)
            loss_ref[...] = jnp.zeros_like(loss_ref)

        dg_ref[...] += jnp.sum(dyv * xh, axis=0, keepdims=True)
        row = jnp.sum(err * err, axis=-1, keepdims=True) * (0.5 / d)
        loss_ref[...] += jnp.sum(row, axis=0, keepdims=True)

    return pl.pallas_call(
        body,
        out_shape=(jax.ShapeDtypeStruct((1, 1), F32), jax.ShapeDtypeStruct((t, d), F32), jax.ShapeDtypeStruct((1, d), F32)),
        grid=(t // tm,),
        in_specs=[pl.BlockSpec((tm, d), lambda i: (i, 0)), pl.BlockSpec((1, d), lambda i: (0, 0)), pl.BlockSpec((tm, d), lambda i: (i, 0))],
        out_specs=(pl.BlockSpec((1, 1), lambda i: (0, 0)), pl.BlockSpec((tm, d), lambda i: (i, 0)), pl.BlockSpec((1, d), lambda i: (0, 0))),
        compiler_params=_cp("arbitrary"),
        name=name,
    )(x, g.reshape(1, d), target)


_GELU_C = math.sqrt(2.0 / math.pi)
_GELU_A = 0.044715


def _gelu(x):
    t = jnp.tanh(_GELU_C * (x + _GELU_A * x * x * x))
    return 0.5 * x * (1.0 + t), t


def _dgelu(x, t):
    return 0.5 * (1.0 + t) + 0.5 * x * (1.0 - t * t) * (_GELU_C * (1.0 + 3.0 * _GELU_A * x * x))


def _spatial_mask():
    ci = lax.broadcasted_iota(jnp.int32, (GMLP_BLOCK, GMLP_BLOCK), 0) // CHUNK
    cj = lax.broadcasted_iota(jnp.int32, (GMLP_BLOCK, GMLP_BLOCK), 1) // CHUNK
    return (cj <= ci).astype(F32)


def _shift_down(x, k):
    rows = lax.broadcasted_iota(jnp.int32, x.shape, 0)
    return jnp.where(rows >= k, pltpu.roll(x, k, 0), 0.0)


def _shift_up(x, k):
    n = x.shape[0]
    rows = lax.broadcasted_iota(jnp.int32, x.shape, 0)
    return jnp.where(rows < n - k, pltpu.roll(x, n - k, 0), 0.0)


def _gmlp_conv_fwd(z, ws, bs, gn, cw, *, seq, name):
    t = z.shape[0]
    g_n = ws.shape[0]
    nb = seq // GMLP_BLOCK
    L = LANES

    def body(z_ref, ws_ref, bs_ref, gn_ref, cw_ref, y_ref):
        u = z_ref[:, 0:L]
        v = z_ref[:, L:2 * L]
        gu, _ = _gelu(u)
        gv, _ = _gelu(v)
        r = lax.rsqrt(jnp.mean(gv * gv, axis=-1, keepdims=True) + RMS_EPS)
        vg = (gv * r * gn_ref[0]).astype(BF16)
        wm = (ws_ref[0] * _spatial_mask()).astype(BF16)
        bias = bs_ref[0]
        for n in range(nb):
            sl = slice(n * GMLP_BLOCK, (n + 1) * GMLP_BLOCK)
            mixed = jnp.dot(wm, vg[sl], preferred_element_type=F32) + bias
            y_ref[sl, 0:L] = (gu[sl] * mixed).astype(y_ref.dtype)
        bg = z_ref[:, 2 * L:3 * L]
        zc = z_ref[:, 3 * L:4 * L] * z_ref[:, 4 * L:5 * L]
        w0, w1, w2 = cw_ref[0, 0:1, :], cw_ref[0, 1:2, :], cw_ref[0, 2:3, :]
        conv = w0 * _shift_down(zc, 2) + w1 * _shift_down(zc, 1) + w2 * zc
        y_ref[:, L:2 * L] = (bg * conv).astype(y_ref.dtype)

    return pl.pallas_call(
        body,
        out_shape=jax.ShapeDtypeStruct((t, 2 * L * g_n), BF16),
        grid=(g_n, t // seq),
        in_specs=[
            pl.BlockSpec((seq, 5 * L), lambda g, b: (b, g)),
            pl.BlockSpec((1, GMLP_BLOCK, GMLP_BLOCK), lambda g, b: (g, 0, 0)),
            pl.BlockSpec((1, GMLP_BLOCK, 1), lambda g, b: (g, 0, 0)),
            pl.BlockSpec((1, 1, L), lambda g, b: (g, 0, 0)),
            pl.BlockSpec((1, 3, L), lambda g, b: (g, 0, 0)),
        ],
        out_specs=pl.BlockSpec((seq, 2 * L), lambda g, b: (b, g)),
        compiler_params=_cp("parallel", "parallel"),
        name=name,
    )(z, ws, bs, gn, cw)


def _gmlp_conv_bwd(z, dy, ws, bs, gn, cw, *, seq, name):
    t = z.shape[0]
    g_n = ws.shape[0]
    nb = seq // GMLP_BLOCK
    L = LANES
    dims_nt = (((1,), (1,)), ((), ()))
    dims_tn = (((0,), (0,)), ((), ()))

    def body(z_ref, dy_ref, ws_ref, bs_ref, gn_ref, cw_ref, dz_ref, dws_ref, dbs_ref, dgn_ref, dcw_ref, dvg_ref):
        @pl.when(pl.program_id(1) == 0)
        def _():
            dws_ref[...] = jnp.zeros_like(dws_ref)
            dbs_ref[...] = jnp.zeros_like(dbs_ref)
            dgn_ref[...] = jnp.zeros_like(dgn_ref)
            dcw_ref[...] = jnp.zeros_like(dcw_ref)

        u = z_ref[:, 0:L]
        v = z_ref[:, L:2 * L]
        gu, tu = _gelu(u)
        gv, tv = _gelu(v)
        r = lax.rsqrt(jnp.mean(gv * gv, axis=-1, keepdims=True) + RMS_EPS)
        gvh = gv * r
        gnv = gn_ref[0]
        vg = (gvh * gnv).astype(BF16)
        mask = _spatial_mask()
        wm = (ws_ref[0] * mask).astype(BF16)
        bias = bs_ref[0]
        dya = dy_ref[:, 0:L].astype(F32)
        dws = jnp.zeros((GMLP_BLOCK, GMLP_BLOCK), F32)
        dbs = jnp.zeros((GMLP_BLOCK, 1), F32)
        for n in range(nb):
            sl = slice(n * GMLP_BLOCK, (n + 1) * GMLP_BLOCK)
            mixed = jnp.dot(wm, vg[sl], preferred_element_type=F32) + bias
            dmixed = dya[sl] * gu[sl]
            dgu = dya[sl] * mixed
            dz_ref[sl, 0:L] = (dgu * _dgelu(u[sl], tu[sl])).astype(dz_ref.dtype)
            dmb = dmixed.astype(BF16)
            dws = dws + lax.dot_general(dmb, vg[sl], dims_nt, preferred_element_type=F32)
            dbs = dbs + jnp.sum(dmixed, axis=1, keepdims=True)
            dvg_ref[sl, :] = lax.dot_general(wm, dmb, dims_tn, preferred_element_type=F32)
        dws_ref[0] += dws * mask
        dbs_ref[0] += dbs
        dvg = dvg_ref[...]
        dgn_ref[0] += jnp.sum(dvg * gvh, axis=0, keepdims=True)
        dvh = dvg * gnv
        dgv = r * (dvh - gvh * jnp.mean(dvh * gvh, axis=-1, keepdims=True))
        dz_ref[:, L:2 * L] = (dgv * _dgelu(v, tv)).astype(dz_ref.dtype)

        bg = z_ref[:, 2 * L:3 * L]
        cg = z_ref[:, 3 * L:4 * L]
        hh = z_ref[:, 4 * L:5 * L]
        zc = cg * hh
        w0, w1, w2 = cw_ref[0, 0:1, :], cw_ref[0, 1:2, :], cw_ref[0, 2:3, :]
        z1 = _shift_down(zc, 1)
        z2 = _shift_down(zc, 2)
        conv = w0 * z2 + w1 * z1 + w2 * zc
        dyb = dy_ref[:, L:2 * L].astype(F32)
        dconv = dyb * bg
        dz_ref[:, 2 * L:3 * L] = (dyb * conv).astype(dz_ref.dtype)
        dzc = w2 * dconv + w1 * _shift_up(dconv, 1) + w0 * _shift_up(dconv, 2)
        dz_ref[:, 3 * L:4 * L] = (dzc * hh).astype(dz_ref.dtype)
        dz_ref[:, 4 * L:5 * L] = (dzc * cg).astype(dz_ref.dtype)
        dcw_ref[0, 0:1, :] += jnp.sum(dconv * z2, axis=0, keepdims=True)
        dcw_ref[0, 1:2, :] += jnp.sum(dconv * z1, axis=0, keepdims=True)
        dcw_ref[0, 2:3, :] += jnp.sum(dconv * zc, axis=0, keepdims=True)

    return pl.pallas_call(
        body,
        out_shape=(
            jax.ShapeDtypeStruct((t, 5 * L * g_n), BF16),
            jax.ShapeDtypeStruct(ws.shape, F32),
            jax.ShapeDtypeStruct(bs.shape, F32),
            jax.ShapeDtypeStruct(gn.shape, F32),
            jax.ShapeDtypeStruct(cw.shape, F32),
        ),
        grid=(g_n, t // seq),
        in_specs=[
            pl.BlockSpec((seq, 5 * L), lambda g, b: (b, g)),
            pl.BlockSpec((seq, 2 * L), lambda g, b: (b, g)),
            pl.BlockSpec((1, GMLP_BLOCK, GMLP_BLOCK), lambda g, b: (g, 0, 0)),
            pl.BlockSpec((1, GMLP_BLOCK, 1), lambda g, b: (g, 0, 0)),
            pl.BlockSpec((1, 1, L), lambda g, b: (g, 0, 0)),
            pl.BlockSpec((1, 3, L), lambda g, b: (g, 0, 0)),
        ],
        out_specs=(
            pl.BlockSpec((seq, 5 * L), lambda g, b: (b, g)),
            pl.BlockSpec((1, GMLP_BLOCK, GMLP_BLOCK), lambda g, b: (g, 0, 0)),
            pl.BlockSpec((1, GMLP_BLOCK, 1), lambda g, b: (g, 0, 0)),
            pl.BlockSpec((1, 1, L), lambda g, b: (g, 0, 0)),
            pl.BlockSpec((1, 3, L), lambda g, b: (g, 0, 0)),
        ),
        scratch_shapes=[pltpu.VMEM((seq, L), F32)],
        compiler_params=_cp("parallel", "arbitrary"),
        name=name,
    )(z, dy, ws, bs, gn, cw)


def _rope_rows(x, c, s1, s2, *, transpose, out_dtype, name):
    t, w = x.shape
    tm = _tile(t, 512)
    only = w == LANES
    fn = _rope_t if transpose else _rope

    def body(x_ref, c_ref, s1_ref, s2_ref, o_ref):
        cv, s1v, s2v = c_ref[...], s1_ref[...], s2_ref[...]
        if only:
            o_ref[...] = fn(x_ref[...].astype(F32), cv, s1v, s2v).astype(o_ref.dtype)
        else:
            for h0 in range(0, w, 2 * LANES):
                o_ref[:, h0:h0 + LANES] = x_ref[:, h0:h0 + LANES].astype(o_ref.dtype)
                o_ref[:, h0 + LANES:h0 + 2 * LANES] = fn(x_ref[:, h0 + LANES:h0 + 2 * LANES].astype(F32), cv, s1v, s2v).astype(o_ref.dtype)

    tab = pl.BlockSpec((tm, LANES), lambda i: (i, 0))
    return pl.pallas_call(
        body,
        out_shape=jax.ShapeDtypeStruct((t, w), out_dtype),
        grid=(t // tm,),
        in_specs=[pl.BlockSpec((tm, w), lambda i: (i, 0)), tab, tab, tab],
        out_specs=pl.BlockSpec((tm, w), lambda i: (i, 0)),
        compiler_params=_cp("parallel"),
        name=name,
    )(x, c, s1, s2)


_NT = (((1,), (1,)), ((), ()))
_TN = (((0,), (0,)), ((), ()))


def _chunk_mask(tq):
    qc = lax.broadcasted_iota(jnp.int32, (tq, tq), 0) // CHUNK
    kc = lax.broadcasted_iota(jnp.int32, (tq, tq), 1) // CHUNK
    return kc <= qc


MLA_BLOCK = 512


def _mla_fwd(q, kv, kr, *, seq, heads, name):
    t = q.shape[0]
    tq = _tile(seq, MLA_BLOCK)
    nq = seq // tq
    scale = (C_NOPE + C_ROPE) ** -0.5

    def body(q_ref, kv_ref, kr_ref, o_ref, lse_ref, kcat_ref):
        i = pl.program_id(2)

        @pl.when(i == 0)
        def _():
            kcat_ref[:, :LANES] = kv_ref[:, :LANES]
            kcat_ref[:, LANES:] = kr_ref[...]

        qv = q_ref[...]

        def step(j, carry, diagonal):
            m_run, l_run, acc = carry
            rows = pl.ds(pl.multiple_of(j * tq, tq), tq)
            s = lax.dot_general(qv, kcat_ref[rows, :], _NT, preferred_element_type=F32) * scale
            if diagonal:
                s = jnp.where(_chunk_mask(tq), s, NEG_BIG)
            m_new = jnp.maximum(m_run, jnp.max(s, axis=-1, keepdims=True))
            alpha = jnp.exp(m_run - m_new)
            p = jnp.exp(s - m_new)
            l_new = alpha * l_run + jnp.sum(p, axis=-1, keepdims=True)
            acc = alpha * acc + jnp.dot(p.astype(BF16), kv_ref[rows, LANES:], preferred_element_type=F32)
            return m_new, l_new, acc

        init = (jnp.full((tq, 1), NEG_BIG, F32), jnp.zeros((tq, 1), F32), jnp.zeros((tq, C_V), F32))
        carry = lax.fori_loop(0, i, lambda j, c: step(j, c, False), init)
        m_run, l_run, acc = step(i, carry, True)
        o_ref[...] = (acc / l_run).astype(o_ref.dtype)
        lse_ref[...] = m_run + jnp.log(l_run)

    return pl.pallas_call(
        body,
        out_shape=(jax.ShapeDtypeStruct((t, heads * C_V), BF16), jax.ShapeDtypeStruct((heads, t, 1), F32)),
        grid=(t // seq, heads, nq),
        in_specs=[
            pl.BlockSpec((tq, 2 * LANES), lambda b, h, i: (b * nq + i, h)),
            pl.BlockSpec((seq, 2 * LANES), lambda b, h, i: (b, h)),
            pl.BlockSpec((seq, LANES), lambda b, h, i: (b, 0)),
        ],
        out_specs=(
            pl.BlockSpec((tq, C_V), lambda b, h, i: (b * nq + i, h)),
            pl.BlockSpec((None, tq, 1), lambda b, h, i: (h, b * nq + i, 0)),
        ),
        scratch_shapes=[pltpu.VMEM((seq, 2 * LANES), BF16)],
        compiler_params=_cp("parallel", "parallel", "arbitrary"),
        name=name,
    )(q, kv, kr)


def _mla_bwd(q, kv, kr, do, o, lse, *, seq, heads, name):
    t = q.shape[0]
    tq = _tile(seq, MLA_BLOCK)
    nq = seq // tq
    scale = (C_NOPE + C_ROPE) ** -0.5

    def body(q_ref, kv_ref, kr_ref, do_ref, o_ref, lse_ref, dq_ref, dkv_ref, dkr_ref, kcat_ref, dk_ref, dv_ref):
        h, i = pl.program_id(1), pl.program_id(2)

        @pl.when(i == 0)
        def _():
            kcat_ref[:, :LANES] = kv_ref[:, :LANES]
            kcat_ref[:, LANES:] = kr_ref[...]
            dk_ref[...] = jnp.zeros_like(dk_ref)
            dv_ref[...] = jnp.zeros_like(dv_ref)

        @pl.when((i == 0) & (h == 0))
        def _():
            dkr_ref[...] = jnp.zeros_like(dkr_ref)

        qv = q_ref[...]
        dov = do_ref[...]
        lse_v = lse_ref[...]
        delta = jnp.sum(dov.astype(F32) * o_ref[...].astype(F32), axis=-1, keepdims=True)

        def step(j, dq, diagonal):
            rows = pl.ds(pl.multiple_of(j * tq, tq), tq)
            kj = kcat_ref[rows, :]
            s = lax.dot_general(qv, kj, _NT, preferred_element_type=F32) * scale
            if diagonal:
                s = jnp.where(_chunk_mask(tq), s, NEG_BIG)
            p = jnp.exp(s - lse_v)
            dp = lax.dot_general(dov, kv_ref[rows, LANES:], _NT, preferred_element_type=F32)
            ds = (p * (dp - delta) * scale).astype(BF16)
            dk_ref[rows, :] += lax.dot_general(ds, qv, _TN, preferred_element_type=F32)
            dv_ref[rows, :] += lax.dot_general(p.astype(BF16), dov, _TN, preferred_element_type=F32)
            return dq + jnp.dot(ds, kj, preferred_element_type=F32)

        dq = lax.fori_loop(0, i, lambda j, c: step(j, c, False), jnp.zeros((tq, 2 * LANES), F32))
        dq_ref[...] = step(i, dq, True)

        @pl.when(i == nq - 1)
        def _():
            dkv_ref[:, :LANES] = dk_ref[:, :LANES].astype(dkv_ref.dtype)
            dkv_ref[:, LANES:] = dv_ref[...].astype(dkv_ref.dtype)
            dkr_ref[...] += dk_ref[:, LANES:]

    return pl.pallas_call(
        body,
        out_shape=(
            jax.ShapeDtypeStruct((t, heads * 2 * LANES), F32),
            jax.ShapeDtypeStruct((t, heads * 2 * LANES), BF16),
            jax.ShapeDtypeStruct((t, LANES), F32),
        ),
        grid=(t // seq, heads, nq),
        in_specs=[
            pl.BlockSpec((tq, 2 * LANES), lambda b, h, i: (b * nq + i, h)),
            pl.BlockSpec((seq, 2 * LANES), lambda b, h, i: (b, h)),
            pl.BlockSpec((seq, LANES), lambda b, h, i: (b, 0)),
            pl.BlockSpec((tq, C_V), lambda b, h, i: (b * nq + i, h)),
            pl.BlockSpec((tq, C_V), lambda b, h, i: (b * nq + i, h)),
            pl.BlockSpec((None, tq, 1), lambda b, h, i: (h, b * nq + i, 0)),
        ],
        out_specs=(
            pl.BlockSpec((tq, 2 * LANES), lambda b, h, i: (b * nq + i, h)),
            pl.BlockSpec((seq, 2 * LANES), lambda b, h, i: (b, h)),
            pl.BlockSpec((seq, LANES), lambda b, h, i: (b, 0)),
        ),
        scratch_shapes=[pltpu.VMEM((seq, 2 * LANES), BF16), pltpu.VMEM((seq, 2 * LANES), F32), pltpu.VMEM((seq, C_V), F32)],
        compiler_params=_cp("parallel", "arbitrary", "arbitrary"),
        name=name,
    )(q, kv, kr, do, o, lse)


def _softmax_rows(s):
    e = jnp.exp(s - jnp.max(s, axis=-1, keepdims=True))
    return e / jnp.sum(e, axis=-1, keepdims=True)


def _mem_fwd(q, k, v, *, seq, n_mem, name):
    t, d = q.shape
    hd = d // MEM_HEADS
    tq = _tile(seq, 512)
    nq = seq // tq
    scale = hd ** -0.5

    def body(q_ref, k_ref, v_ref, o_ref):
        s = lax.dot_general(q_ref[...], k_ref[...], _NT, preferred_element_type=F32) * scale
        p = _softmax_rows(s)
        o_ref[...] = jnp.dot(p.astype(BF16), v_ref[...], preferred_element_type=F32).astype(o_ref.dtype)

    return pl.pallas_call(
        body,
        out_shape=jax.ShapeDtypeStruct((t, d), BF16),
        grid=(t // seq, MEM_HEADS, nq),
        in_specs=[
            pl.BlockSpec((tq, hd), lambda b, h, i: (b * nq + i, h)),
            pl.BlockSpec((n_mem, hd), lambda b, h, i: (b, h)),
            pl.BlockSpec((n_mem, hd), lambda b, h, i: (b, h)),
        ],
        out_specs=pl.BlockSpec((tq, hd), lambda b, h, i: (b * nq + i, h)),
        compiler_params=_cp("parallel", "parallel", "arbitrary"),
        name=name,
    )(q, k, v)


def _mem_bwd(q, k, v, do, *, seq, n_mem, name):
    t, d = q.shape
    hd = d // MEM_HEADS
    tq = _tile(seq, 512)
    nq = seq // tq
    scale = hd ** -0.5

    def body(q_ref, k_ref, v_ref, do_ref, dq_ref, dk_ref, dv_ref):
        @pl.when(pl.program_id(2) == 0)
        def _():
            dk_ref[...] = jnp.zeros_like(dk_ref)
            dv_ref[...] = jnp.zeros_like(dv_ref)

        qv, kv_, vv, dov = q_ref[...], k_ref[...], v_ref[...], do_ref[...]
        p = _softmax_rows(lax.dot_general(qv, kv_, _NT, preferred_element_type=F32) * scale)
        dp = lax.dot_general(dov, vv, _NT, preferred_element_type=F32)
        ds = (p * (dp - jnp.sum(dp * p, axis=-1, keepdims=True)) * scale).astype(BF16)
        dq_ref[...] = jnp.dot(ds, kv_, preferred_element_type=F32).astype(dq_ref.dtype)
        dk_ref[...] += lax.dot_general(ds, qv, _TN, preferred_element_type=F32)
        dv_ref[...] += lax.dot_general(p.astype(BF16), dov, _TN, preferred_element_type=F32)

    return pl.pallas_call(
        body,
        out_shape=(
            jax.ShapeDtypeStruct((t, d), BF16),
            jax.ShapeDtypeStruct(k.shape, F32),
            jax.ShapeDtypeStruct(v.shape, F32),
        ),
        grid=(t // seq, MEM_HEADS, nq),
        in_specs=[
            pl.BlockSpec((tq, hd), lambda b, h, i: (b * nq + i, h)),
            pl.BlockSpec((n_mem, hd), lambda b, h, i: (b, h)),
            pl.BlockSpec((n_mem, hd), lambda b, h, i: (b, h)),
            pl.BlockSpec((tq, hd), lambda b, h, i: (b * nq + i, h)),
        ],
        out_specs=(
            pl.BlockSpec((tq, hd), lambda b, h, i: (b * nq + i, h)),
            pl.BlockSpec((n_mem, hd), lambda b, h, i: (b, h)),
            pl.BlockSpec((n_mem, hd), lambda b, h, i: (b, h)),
        ),
        compiler_params=_cp("parallel", "parallel", "arbitrary"),
        name=name,
    )(q, k, v, do)


def _me():
    return lax.axis_index("x"), lax.axis_index("y"), lax.axis_index("c")


def _dev_index(px, py, pc):
    return 4 * px + 2 * py + pc


def _chunk(ref, idx, axis, size):
    start = pl.multiple_of(idx * size, size)
    return ref.at[pl.ds(start, size), :] if axis == 0 else ref.at[:, pl.ds(start, size)]


def _all_gather_small(v):
    m, n = v.shape

    def body(x_ref, out_ref, send_sems, recv_sems, local_sem):
        x, y, c = _me()
        me, sibling = (x, y, c), (x, y, 1 - c)
        chips = [(1 - x, y), (x, 1 - y), (1 - x, 1 - y)]

        def rows(px, py, pc):
            return out_ref.at[_dev_index(px, py, pc)]

        def copy(k, block, to, src=None):
            return pltpu.make_async_remote_copy(
                src_ref=rows(*block) if src is None else src, dst_ref=rows(*block),
                send_sem=send_sems.at[k], recv_sem=recv_sems.at[k], device_id=to, device_id_type=MESH)

        mine = pltpu.make_async_copy(x_ref, rows(*me), local_sem)
        mine.start()
        first = [copy(0, me, sibling, src=x_ref)]
        first += [copy(1 + j, me, (*chip, c), src=x_ref) for j, chip in enumerate(chips)]
        for cp in first:
            cp.start()
        passed = [copy(4 + j, (*chip, c), sibling) for j, chip in enumerate(chips)]
        for j, chip in enumerate(chips):
            copy(1 + j, (*chip, c), me).wait_recv()
            passed[j].start()
        copy(0, sibling, me).wait_recv()
        for j, chip in enumerate(chips):
            copy(4 + j, (*chip, 1 - c), me).wait_recv()
        for cp in first + passed:
            cp.wait_send()
        mine.wait()

    return pl.pallas_call(
        body,
        out_shape=jax.ShapeDtypeStruct((N_DEV, m, n), v.dtype),
        in_specs=[pl.BlockSpec(memory_space=pltpu.VMEM)],
        out_specs=pl.BlockSpec(memory_space=pltpu.VMEM),
        scratch_shapes=[pltpu.SemaphoreType.DMA((7,)), pltpu.SemaphoreType.DMA((7,)), pltpu.SemaphoreType.DMA],
        name="all_gather_small",
    )(v)


def _all_gather_weights(shards, axes, *, name):
    n = len(shards)
    sizes = [s.shape[a] for s, a in zip(shards, axes)]

    def body(*refs):
        ins, outs = refs[:n], refs[n:2 * n]
        send_sems, recv_sems, local_sems = refs[2 * n:]
        x, y, c = _me()
        me, sibling = (x, y, c), (x, y, 1 - c)
        chips = [(1 - x, y), (x, 1 - y), (1 - x, 1 - y)]

        def rows(t, dev):
            return _chunk(outs[t], _dev_index(*dev), axes[t], sizes[t])

        def copy(t, k, block, to, src=None):
            return pltpu.make_async_remote_copy(
                src_ref=rows(t, block) if src is None else src, dst_ref=rows(t, block),
                send_sem=send_sems.at[t, k], recv_sem=recv_sems.at[t, k], device_id=to, device_id_type=MESH)

        mine, first, passed = [], [], []
        for t in range(n):
            cp = pltpu.make_async_copy(ins[t], rows(t, me), local_sems.at[t])
            cp.start()
            mine.append(cp)
            f = [copy(t, 0, me, sibling, src=ins[t])]
            f += [copy(t, 1 + j, me, (*chip, c), src=ins[t]) for j, chip in enumerate(chips)]
            for cp in f:
                cp.start()
            first.append(f)
        for t in range(n):
            p = [copy(t, 4 + j, (*chip, c), sibling) for j, chip in enumerate(chips)]
            for j, chip in enumerate(chips):
                copy(t, 1 + j, (*chip, c), me).wait_recv()
                p[j].start()
            passed.append(p)
        for t in range(n):
            copy(t, 0, sibling, me).wait_recv()
            for j, chip in enumerate(chips):
                copy(t, 4 + j, (*chip, 1 - c), me).wait_recv()
            for cp in first[t] + passed[t]:
                cp.wait_send()
            mine[t].wait()

    out_shape = []
    for s, a in zip(shards, axes):
        full = list(s.shape)
        full[a] *= N_DEV
        out_shape.append(jax.ShapeDtypeStruct(tuple(full), s.dtype))
    any_spec = pl.BlockSpec(memory_space=pl.ANY)
    return pl.pallas_call(
        body,
        out_shape=tuple(out_shape),
        in_specs=[any_spec] * n,
        out_specs=tuple([any_spec] * n),
        scratch_shapes=[pltpu.SemaphoreType.DMA((n, 7)), pltpu.SemaphoreType.DMA((n, 7)), pltpu.SemaphoreType.DMA((n,))],
        name=name,
    )(*shards)


def _scatter_d2d(grads, axes, *, name):
    n = len(grads)
    sizes = [g.shape[a] // N_DEV for g, a in zip(grads, axes)]

    def body(*refs):
        ins, owns, gots = refs[:n], refs[n:2 * n], refs[2 * n:3 * n]
        send_sems, recv_sems, local_sems = refs[3 * n:]
        x, y, c = _me()
        sibling = (x, y, 1 - c)
        started = []
        for t in range(n):
            for k in range(4):
                keep = pltpu.make_async_copy(_chunk(ins[t], 2 * k + c, axes[t], sizes[t]), _chunk(owns[t], k, axes[t], sizes[t]), local_sems.at[t, k])
                give = pltpu.make_async_remote_copy(
                    src_ref=_chunk(ins[t], 2 * k + 1 - c, axes[t], sizes[t]), dst_ref=_chunk(gots[t], k, axes[t], sizes[t]),
                    send_sem=send_sems.at[t, k], recv_sem=recv_sems.at[t, k], device_id=sibling, device_id_type=MESH)
                keep.start()
                give.start()
                started.append((keep, give))
        for keep, give in started:
            give.wait_recv()
            give.wait_send()
            keep.wait()

    out_shape = []
    for g, a in zip(grads, axes):
        sh = list(g.shape)
        sh[a] //= 2
        out_shape.append(jax.ShapeDtypeStruct(tuple(sh), g.dtype))
    any_spec = pl.BlockSpec(memory_space=pl.ANY)
    outs = pl.pallas_call(
        body,
        out_shape=tuple(out_shape + out_shape),
        in_specs=[any_spec] * n,
        out_specs=tuple([any_spec] * (2 * n)),
        scratch_shapes=[pltpu.SemaphoreType.DMA((n, 4)), pltpu.SemaphoreType.DMA((n, 4)), pltpu.SemaphoreType.DMA((n, 4))],
        name=name,
    )(*grads)
    return outs[:n], outs[n:]


def _scatter_ici(sums, axes, *, name):
    n = len(sums)
    sizes = [g.shape[a] // 4 for g, a in zip(sums, axes)]

    def body(*refs):
        ins, outs = refs[:n], refs[n:2 * n]
        send_sems, recv_sems, local_sems = refs[2 * n:]
        x, y, c = _me()
        my_chip = 2 * x + y
        peers = []
        for j in range(1, 4):
            px, py = (1 - x if (j >> 1) & 1 else x), (1 - y if j & 1 else y)
            peers.append((px, py))

        def copy(t, j, chip_src, chip_dst, to):
            return pltpu.make_async_remote_copy(
                src_ref=_chunk(ins[t], chip_src, axes[t], sizes[t]), dst_ref=_chunk(outs[t], chip_dst, axes[t], sizes[t]),
                send_sem=send_sems.at[t, j], recv_sem=recv_sems.at[t, j], device_id=to, device_id_type=MESH)

        mine, sent = [], []
        for t in range(n):
            cp = pltpu.make_async_copy(_chunk(ins[t], my_chip, axes[t], sizes[t]), _chunk(outs[t], my_chip, axes[t], sizes[t]), local_sems.at[t])
            cp.start()
            mine.append(cp)
            for j, (px, py) in enumerate(peers):
                cp = copy(t, j, 2 * px + py, my_chip, (px, py, c))
                cp.start()
                sent.append(cp)
        for t in range(n):
            for j, (px, py) in enumerate(peers):
                copy(t, j, my_chip, 2 * px + py, (px, py, c)).wait_recv()
            mine[t].wait()
        for cp in sent:
            cp.wait_send()

    any_spec = pl.BlockSpec(memory_space=pl.ANY)
    return pl.pallas_call(
        body,
        out_shape=tuple(jax.ShapeDtypeStruct(g.shape, g.dtype) for g in sums),
        in_specs=[any_spec] * n,
        out_specs=tuple([any_spec] * n),
        scratch_shapes=[pltpu.SemaphoreType.DMA((n, 3)), pltpu.SemaphoreType.DMA((n, 3)), pltpu.SemaphoreType.DMA((n,))],
        name=name,
    )(*sums)


def _add_pair(a, b, *, name):
    r, c = a.shape
    tr = _tile(r, 512)

    def body(a_ref, b_ref, o_ref):
        o_ref[...] = (a_ref[...].astype(F32) + b_ref[...].astype(F32)).astype(o_ref.dtype)

    spec = pl.BlockSpec((tr, c), lambda i: (i, 0))
    return pl.pallas_call(
        body, out_shape=jax.ShapeDtypeStruct(a.shape, a.dtype), grid=(r // tr,), in_specs=[spec, spec], out_specs=spec,
        compiler_params=_cp("parallel"), name=name,
    )(a, b)


def _adam_math(w, g, m, v):
    m = ADAM_B1 * m + (1.0 - ADAM_B1) * g
    v = ADAM_B2 * v + (1.0 - ADAM_B2) * (g * g)
    m_hat = m / (1.0 - ADAM_B1 ** ADAM_STEP)
    v_hat = v / (1.0 - ADAM_B2 ** ADAM_STEP)
    delta = -ADAM_LR * (m_hat / (jnp.sqrt(v_hat) + ADAM_EPS) + ADAM_WD * w)
    return delta, m, v


def _adamw(w, m, v, parts, axis, *, name):
    n_l, r, c = w.shape
    tr = _tile(r, 128)
    nr = r // tr
    n_chip = 4

    def body(*refs):
        w_ref, m_ref, v_ref = refs[:3]
        p_refs = refs[3:3 + n_chip * n_l]
        g_ref, d_ref, nm_ref, nv_ref = refs[3 + n_chip * n_l:]
        for l in range(n_l):
            @pl.when(pl.program_id(0) == l)
            def _(l=l):
                g = p_refs[n_chip * l][...].astype(F32)
                for k in range(1, n_chip):
                    g = g + p_refs[n_chip * l + k][...].astype(F32)
                delta, nm, nv = _adam_math(w_ref[...], g, m_ref[...], v_ref[...])
                g_ref[...] = g
                d_ref[...] = delta
                nm_ref[...] = nm
                nv_ref[...] = nv

    def part_spec(ll, k):
        if axis == 0:
            return pl.BlockSpec((tr, c), lambda l, i: (k * nr + jnp.where(l == ll, i, 0), 0))
        return pl.BlockSpec((tr, c), lambda l, i: (jnp.where(l == ll, i, 0), k))

    wspec = pl.BlockSpec((None, tr, c), lambda l, i: (l, i, 0))
    pspecs = [part_spec(ll, k) for ll in range(n_l) for k in range(n_chip)]
    sds = jax.ShapeDtypeStruct(w.shape, F32)
    return pl.pallas_call(
        body,
        out_shape=(sds, sds, sds, sds),
        grid=(n_l, nr),
        in_specs=[wspec, wspec, wspec] + pspecs,
        out_specs=(wspec, wspec, wspec, wspec),
        compiler_params=_cp("arbitrary", "arbitrary"),
        name=name,
    )(w, m, v, *[parts[ll] for ll in range(n_l) for _ in range(n_chip)])


def _adamw_small(w, m, v, parts, *, name):
    r, c = w.shape
    tr = _tile(r, 512)

    def body(w_ref, m_ref, v_ref, p_ref, g_ref, d_ref, nm_ref, nv_ref):
        g = p_ref[0]
        for s in range(1, N_DEV):
            g = g + p_ref[s]
        delta, nm, nv = _adam_math(w_ref[...], g, m_ref[...], v_ref[...])
        g_ref[...] = g
        d_ref[...] = delta
        nm_ref[...] = nm
        nv_ref[...] = nv

    spec = pl.BlockSpec((tr, c), lambda i: (i, 0))
    sds = jax.ShapeDtypeStruct(w.shape, F32)
    return pl.pallas_call(
        body,
        out_shape=(sds, sds, sds, sds),
        grid=(r // tr,),
        in_specs=[spec, spec, spec, pl.BlockSpec((N_DEV, tr, c), lambda i: (0, i, 0))],
        out_specs=(spec, spec, spec, spec),
        compiler_params=_cp("parallel"),
        name=name,
    )(w, m, v, parts)


def _pack(arrs):
    flat = jnp.concatenate([a.reshape(-1).astype(F32) for a in arrs])
    pad = (-flat.shape[0]) % (8 * LANES)
    return jnp.pad(flat, (0, pad)).reshape(-1, LANES)


def _unpack(packed, shapes):
    flat = packed.reshape(-1)
    out, off = [], 0
    for sh in shapes:
        n = math.prod(sh)
        out.append(flat[off:off + n].reshape(sh))
        off += n
    return out


def _unpack_gathered(packed, shapes, axis):
    flat = packed.reshape(N_DEV, -1)
    out, off = [], 0
    for sh, ax in zip(shapes, axis):
        n = math.prod(sh)
        blocks = flat[:, off:off + n].reshape((N_DEV, *sh))
        out.append(jnp.concatenate([blocks[d] for d in range(N_DEV)], axis=ax))
        off += n
    return out


def _my_block(full, axis, size):
    idx = _dev_index(*_me())
    return lax.dynamic_slice_in_dim(full, idx * size, size, axis)


def kernel(x, mem, positions, norm_mix_g, norm_mem_q_g, norm_mem_kv_g, norm_ffn_g, final_norm_g, ab_w_in, a_v_norm_g, a_w_s, a_b_s, b_conv_w, ab_w_out, c_w_in, c_q_norm_g, c_kv_norm_g, c_w_uq, c_w_ukv, c_w_out, m_wq, m_wk, m_wv, m_wo, f_w1, f_w2, loss_target, m_norm_mix_g, m_norm_mem_q_g, m_norm_mem_kv_g, m_norm_ffn_g, m_final_norm_g, m_ab_w_in, m_a_v_norm_g, m_a_w_s, m_a_b_s, m_b_conv_w, m_ab_w_out, m_c_w_in, m_c_q_norm_g, m_c_kv_norm_g, m_c_w_uq, m_c_w_ukv, m_c_w_out, m_m_wq, m_m_wk, m_m_wv, m_m_wo, m_f_w1, m_f_w2, v_norm_mix_g, v_norm_mem_q_g, v_norm_mem_kv_g, v_norm_ffn_g, v_final_norm_g, v_ab_w_in, v_a_v_norm_g, v_a_w_s, v_a_b_s, v_b_conv_w, v_ab_w_out, v_c_w_in, v_c_q_norm_g, v_c_kv_norm_g, v_c_w_uq, v_c_w_ukv, v_c_w_out, v_m_wq, v_m_wk, v_m_wv, v_m_wo, v_f_w1, v_f_w2):
    bl, seq, d = x.shape
    t = bl * seq
    n_mem = mem.shape[1]
    depth = norm_mix_g.shape[0]
    n_even, n_odd = ab_w_in.shape[0], c_w_in.shape[0]
    a_width = a_v_norm_g.shape[1]
    groups = a_w_s.shape[1]
    heads = d // C_NOPE
    q_rank = c_q_norm_g.shape[1] * N_DEV
    kv_rank = c_kv_norm_g.shape[1] * N_DEV

    big = {
        "ab_w_in": (ab_w_in, m_ab_w_in, v_ab_w_in, 1), "ab_w_out": (ab_w_out, m_ab_w_out, v_ab_w_out, 0),
        "c_w_in": (c_w_in, m_c_w_in, v_c_w_in, 0), "c_w_uq": (c_w_uq, m_c_w_uq, v_c_w_uq, 1),
        "c_w_ukv": (c_w_ukv, m_c_w_ukv, v_c_w_ukv, 1), "c_w_out": (c_w_out, m_c_w_out, v_c_w_out, 0),
        "m_wq": (m_wq, m_m_wq, v_m_wq, 0), "m_wk": (m_wk, m_m_wk, v_m_wk, 0), "m_wv": (m_wv, m_m_wv, v_m_wv, 0),
        "m_wo": (m_wo, m_m_wo, v_m_wo, 0), "f_w1": (f_w1, m_f_w1, v_f_w1, 1), "f_w2": (f_w2, m_f_w2, v_f_w2, 0),
    }
    even_names = ["ab_w_in", "ab_w_out"]
    odd_names = ["c_w_in", "c_w_uq", "c_w_ukv", "c_w_out"]
    all_names = ["m_wq", "m_wk", "m_wv", "m_wo", "f_w1", "f_w2"]

    def layer_names(layer):
        return (even_names if layer % 2 == 0 else odd_names) + all_names

    def sub(name, layer):
        return layer // 2 if name in even_names or name in odd_names else layer

    split_small = [(c_q_norm_g, m_c_q_norm_g, v_c_q_norm_g, 1), (c_kv_norm_g, m_c_kv_norm_g, v_c_kv_norm_g, 1), (b_conv_w, m_b_conv_w, v_b_conv_w, 2)]
    shard_shapes = [s[0].shape for s in split_small] * 3
    shard_axes = [s[3] for s in split_small] * 3
    gathered = _all_gather_small(_pack([s[0] for s in split_small] + [s[1] for s in split_small] + [s[2] for s in split_small]))
    whole = _unpack_gathered(gathered, shard_shapes, shard_axes)
    q_norm_full, kv_norm_full, conv_full = whole[0:3]
    small_w = [norm_mix_g, norm_mem_q_g, norm_mem_kv_g, norm_ffn_g, final_norm_g, a_v_norm_g, a_w_s, a_b_s] + whole[0:3]
    small_m = [m_norm_mix_g, m_norm_mem_q_g, m_norm_mem_kv_g, m_norm_ffn_g, m_final_norm_g, m_a_v_norm_g, m_a_w_s, m_a_b_s] + whole[3:6]
    small_v = [v_norm_mix_g, v_norm_mem_q_g, v_norm_mem_kv_g, v_norm_ffn_g, v_final_norm_g, v_a_v_norm_g, v_a_w_s, v_a_b_s] + whole[6:9]

    inv = ROPE_THETA ** (-jnp.arange(0, C_ROPE, 2, dtype=F32) / C_ROPE)
    ang = positions.astype(F32).reshape(t, 1) * inv
    cos, sin, zero = jnp.cos(ang), jnp.sin(ang), jnp.zeros((t, C_ROPE // 2), F32)
    rc = jnp.concatenate([cos, cos, zero, zero], axis=1)
    rs1 = jnp.concatenate([-sin, zero, zero, zero], axis=1)
    rs2 = jnp.concatenate([zero, sin, zero, zero], axis=1)

    wts = []
    for layer in range(depth):
        names = layer_names(layer)
        shards = [big[nm][0][sub(nm, layer)].astype(BF16) for nm in names]
        full = _all_gather_weights(shards, [big[nm][3] for nm in names], name=f"all_gather_l{layer}")
        wts.append(dict(zip(names, full)))

    def group_cols(w_in):
        return w_in.reshape(d, 5, groups, LANES).transpose(0, 2, 1, 3).reshape(d, 5 * a_width)

    def ungroup_cols(g_in):
        return g_in.reshape(d, groups, 5, LANES).transpose(0, 2, 1, 3).reshape(d, 5 * a_width)

    def group_rows(w_out):
        return w_out.reshape(2, groups, LANES, d).transpose(1, 0, 2, 3).reshape(2 * a_width, d)

    def ungroup_rows(g_out):
        return g_out.reshape(groups, 2, LANES, d).transpose(1, 0, 2, 3).reshape(2 * a_width, d)

    def pad_heads(w_uq):
        w3 = w_uq.reshape(q_rank, heads, C_NOPE + C_ROPE)
        return jnp.pad(w3, ((0, 0), (0, 0), (0, LANES - C_ROPE))).reshape(q_rank, heads * 2 * LANES)

    def unpad_heads(g_uq):
        return g_uq.reshape(q_rank, heads, 2 * LANES)[:, :, :C_NOPE + C_ROPE].reshape(q_rank, heads * (C_NOPE + C_ROPE))

    x2 = x.reshape(t, d)
    mem2 = mem.reshape(bl * n_mem, d)
    saved = []

    for layer in range(depth):
        w = wts[layer]
        sv = {"x0": x2}
        xn = _rmsnorm_fwd(x2, norm_mix_g[layer], name="norm_mix")
        sv["xn"] = xn
        if layer % 2 == 0:
            e = layer // 2
            w_in_g = group_cols(w["ab_w_in"])
            w_out_g = group_rows(w["ab_w_out"])
            z = _mm(xn, w_in_g, out_dtypes=(F32,), name="ab_in")
            ws_e = a_w_s[e]
            bs_e = a_b_s[e].reshape(groups, GMLP_BLOCK, 1)
            gn_e = a_v_norm_g[e].reshape(groups, 1, LANES)
            cw_e = conv_full[e].reshape(3, groups, LANES).transpose(1, 0, 2)
            ycat = _gmlp_conv_fwd(z, ws_e, bs_e, gn_e, cw_e, seq=seq, name="gmlp_conv_fwd")
            x2 = _mm(ycat, w_out_g, out_dtypes=(F32,), extras=[(x2, "mn")], epilogue=_epi_add, name="ab_out")
            sv.update(z=z, ycat=ycat, w_in_g=w_in_g, w_out_g=w_out_g, ws=ws_e, bs=bs_e, gn=gn_e, cw=cw_e)
        else:
            o = layer // 2
            w_uq_p = pad_heads(w["c_w_uq"])
            z = _mm(xn, w["c_w_in"], out_dtypes=(F32,), name="c_in")
            qn = _rmsnorm_fwd(z, q_norm_full[o], width=q_rank, col=0, name="norm_cq")
            kvn = _rmsnorm_fwd(z, kv_norm_full[o], width=kv_rank, col=q_rank // kv_rank, name="norm_ckv")
            qh = _mm(qn, w_uq_p, extras=[(rc, "m"), (rs1, "m"), (rs2, "m")], epilogue=_epi_rope_heads, tn=1024, name="c_uq")
            kvh = _mm(kvn, w["c_w_ukv"], name="c_ukv")
            k_rope = jnp.pad(z[:, q_rank + kv_rank:], ((0, 0), (0, LANES - C_ROPE)))
            kr = _rope_rows(k_rope, rc, rs1, rs2, transpose=False, out_dtype=BF16, name="rope_k")
            att, lse = _mla_fwd(qh, kvh, kr, seq=seq, heads=heads, name="mla_fwd")
            x2 = _mm(att, w["c_w_out"], out_dtypes=(F32,), extras=[(x2, "mn")], epilogue=_epi_add, name="c_out")
            sv.update(z=z, qn=qn, kvn=kvn, qh=qh, kvh=kvh, kr=kr, att=att, lse=lse, w_uq_p=w_uq_p)
        sv["x1"] = x2
        xq = _rmsnorm_fwd(x2, norm_mem_q_g[layer], name="norm_mem_q")
        memn = _rmsnorm_fwd(mem2, norm_mem_kv_g[layer], name="norm_mem_kv")
        mq = _mm(xq, w["m_wq"], name="m_q")
        mk = _mm(memn, w["m_wk"], name="m_k")
        mv = _mm(memn, w["m_wv"], name="m_v")
        mo = _mem_fwd(mq, mk, mv, seq=seq, n_mem=n_mem, name="mem_fwd")
        x2 = _mm(mo, w["m_wo"], out_dtypes=(F32,), extras=[(x2, "mn")], epilogue=_epi_add, name="m_o")
        sv.update(xq=xq, memn=memn, mq=mq, mk=mk, mv=mv, mo=mo)
        sv["x2"] = x2
        xf = _rmsnorm_fwd(x2, norm_ffn_g[layer], name="norm_ffn")
        hpre, hact = _mm(xf, w["f_w1"], out_dtypes=(BF16, BF16), epilogue=_epi_relu2, name="f_1")
        x2 = _mm(hact, w["f_w2"], out_dtypes=(F32,), extras=[(x2, "mn")], epilogue=_epi_add, name="f_2")
        sv.update(xf=xf, hpre=hpre, hact=hact)
        saved.append(sv)

    loss_part, dx, dg_final = _final_loss(x2, final_norm_g, loss_target.reshape(t, d), name="final_loss")
    loss = lax.psum(loss_part[0, 0], ("x", "y", "c"))

    dg_mix, dg_mq, dg_mkv, dg_ffn = [None] * depth, [None] * depth, [None] * depth, [None] * depth
    d_av, d_ws, d_bs = [None] * n_even, [None] * n_even, [None] * n_even
    d_conv, d_qn, d_kvn = [None] * n_even, [None] * n_odd, [None] * n_odd
    parts = {nm: [None] * big[nm][0].shape[0] for nm in big}

    for layer in reversed(range(depth)):
        w, sv = wts[layer], saved[layer]
        grads = {}
        dh = _mm(dx, w["f_w2"], tb=True, extras=[(sv["hpre"], "mn")], epilogue=_epi_drelu2, name="f_2_dx")
        grads["f_w2"] = _mm(sv["hact"], dx, ta=True, name="f_2_dw")
        grads["f_w1"] = _mm(sv["xf"], dh, ta=True, name="f_1_dw")
        dxf = _mm(dh, w["f_w1"], tb=True, name="f_1_dx")
        dx, dg_ffn[layer] = _rmsnorm_bwd(sv["x2"], norm_ffn_g[layer], dxf, dx, name="norm_ffn_bwd")
        dmo = _mm(dx, w["m_wo"], tb=True, name="m_o_dx")
        grads["m_wo"] = _mm(sv["mo"], dx, ta=True, name="m_o_dw")
        dmq, dmk, dmv = _mem_bwd(sv["mq"], sv["mk"], sv["mv"], dmo, seq=seq, n_mem=n_mem, name="mem_bwd")
        grads["m_wq"] = _mm(sv["xq"], dmq, ta=True, name="m_q_dw")
        grads["m_wk"] = _mm(sv["memn"], dmk, ta=True, name="m_k_dw")
        grads["m_wv"] = _mm(sv["memn"], dmv, ta=True, name="m_v_dw")
        dxq = _mm(dmq, w["m_wq"], tb=True, name="m_q_dx")
        dmemn_k = _mm(dmk, w["m_wk"], tb=True, out_dtypes=(F32,), name="m_k_dx")
        dmemn = _mm(dmv, w["m_wv"], tb=True, out_dtypes=(F32,), extras=[(dmemn_k, "mn")], epilogue=_epi_add, name="m_v_dx")
        _, dg_mkv[layer] = _rmsnorm_bwd(mem2, norm_mem_kv_g[layer], dmemn, name="norm_mem_kv_bwd")
        dx, dg_mq[layer] = _rmsnorm_bwd(sv["x1"], norm_mem_q_g[layer], dxq, dx, name="norm_mem_q_bwd")
        if layer % 2 == 0:
            e = layer // 2
            dycat = _mm(dx, sv["w_out_g"], tb=True, name="ab_out_dx")
            grads["ab_w_out"] = ungroup_rows(_mm(sv["ycat"], dx, ta=True, name="ab_out_dw"))
            dz, dws, dbs, dgn, dcw = _gmlp_conv_bwd(sv["z"], dycat, sv["ws"], sv["bs"], sv["gn"], sv["cw"], seq=seq, name="gmlp_conv_bwd")
            d_ws[e], d_bs[e], d_av[e] = dws, dbs.reshape(groups, GMLP_BLOCK), dgn.reshape(a_width)
            d_conv[e] = dcw.transpose(1, 0, 2).reshape(3, a_width)
            grads["ab_w_in"] = ungroup_cols(_mm(sv["xn"], dz, ta=True, name="ab_in_dw"))
            dxn = _mm(dz, sv["w_in_g"], tb=True, name="ab_in_dx")
        else:
            o = layer // 2
            datt = _mm(dx, w["c_w_out"], tb=True, name="c_out_dx")
            grads["c_w_out"] = _mm(sv["att"], dx, ta=True, name="c_out_dw")
            dqh, dkvh, dkr = _mla_bwd(sv["qh"], sv["kvh"], sv["kr"], datt, sv["att"], sv["lse"], seq=seq, heads=heads, name="mla_bwd")
            dq_pre = _rope_rows(dqh, rc, rs1, rs2, transpose=True, out_dtype=BF16, name="rope_q_bwd")
            dk_rope = _rope_rows(dkr, rc, rs1, rs2, transpose=True, out_dtype=F32, name="rope_k_bwd")
            grads["c_w_uq"] = unpad_heads(_mm(sv["qn"], dq_pre, ta=True, name="c_uq_dw"))
            grads["c_w_ukv"] = _mm(sv["kvn"], dkvh, ta=True, name="c_ukv_dw")
            dqn = _mm(dq_pre, sv["w_uq_p"], tb=True, name="c_uq_dx")
            dkvn = _mm(dkvh, w["c_w_ukv"], tb=True, name="c_ukv_dx")
            dcq, d_qn[o] = _rmsnorm_bwd(sv["z"], q_norm_full[o], dqn, width=q_rank, col=0, name="norm_cq_bwd")
            dckv, d_kvn[o] = _rmsnorm_bwd(sv["z"], kv_norm_full[o], dkvn, width=kv_rank, col=q_rank // kv_rank, name="norm_ckv_bwd")
            dz = jnp.concatenate([dcq, dckv, dk_rope[:, :C_ROPE]], axis=1)
            grads["c_w_in"] = _mm(sv["xn"], dz, ta=True, name="c_in_dw")
            dxn = _mm(dz, w["c_w_in"], tb=True, name="c_in_dx")
        dx, dg_mix[layer] = _rmsnorm_bwd(sv["x0"], norm_mix_g[layer], dxn, dx, name="norm_mix_bwd")
        names = layer_names(layer)
        axes = [big[nm][3] for nm in names]
        own, got = _scatter_d2d([grads[nm] for nm in names], axes, name=f"scatter_d2d_l{layer}")
        sums = [_add_pair(a, b, name="add_pair") for a, b in zip(own, got)]
        recv = _scatter_ici(sums, axes, name=f"scatter_ici_l{layer}")
        for nm, r in zip(names, recv):
            parts[nm][sub(nm, layer)] = r

    grad_x = dx.reshape(bl, seq, d)

    big_out = {}
    for nm, (wv, mv_, vv, ax) in big.items():
        big_out[nm] = _adamw(wv, mv_, vv, parts[nm], ax, name=f"adamw_{nm}")

    small_g = [jnp.concatenate(dg_mix), jnp.concatenate(dg_mq), jnp.concatenate(dg_mkv), jnp.concatenate(dg_ffn), dg_final.reshape(d),
               jnp.stack(d_av), jnp.stack(d_ws), jnp.stack(d_bs), jnp.concatenate(d_qn), jnp.concatenate(d_kvn), jnp.stack(d_conv)]
    small_shapes = [a.shape for a in small_w]
    g_all = _all_gather_small(_pack(small_g))
    sg, sd, sm, svv = _adamw_small(_pack(small_w), _pack(small_m), _pack(small_v), g_all, name="adamw_small")
    small_names = ["norm_mix_g", "norm_mem_q_g", "norm_mem_kv_g", "norm_ffn_g", "final_norm_g", "a_v_norm_g", "a_w_s", "a_b_s", "c_q_norm_g", "c_kv_norm_g", "b_conv_w"]
    small_out = {}
    unpacked = [_unpack(p, small_shapes) for p in (sg, sd, sm, svv)]
    for i, nm in enumerate(small_names):
        vals = [u[i] for u in unpacked]
        if i >= 8:
            full_w, _, _, ax = split_small[i - 8]
            vals = [_my_block(val, ax, full_w.shape[ax]) for val in vals]
        small_out[nm] = vals

    order = ["norm_mix_g", "norm_mem_q_g", "norm_mem_kv_g", "norm_ffn_g", "final_norm_g", "ab_w_in", "a_v_norm_g", "a_w_s", "a_b_s", "b_conv_w", "ab_w_out",
             "c_w_in", "c_q_norm_g", "c_kv_norm_g", "c_w_uq", "c_w_ukv", "c_w_out", "m_wq", "m_wk", "m_wv", "m_wo", "f_w1", "f_w2"]
    res = [loss, grad_x]
    for kind in range(4):
        for nm in order:
            res.append(big_out[nm][kind] if nm in big_out else small_out[nm][kind])
    return tuple(res)
```

```python
import functools
import math

import jax
import jax.numpy as jnp
from jax import lax
from jax.experimental import pallas as pl
from jax.experimental.pallas import tpu as pltpu

F32 = jnp.float32
BF16 = jnp.bfloat16
MESH = pl.DeviceIdType.MESH
N_DEV = 8
LANES = 128

RMS_EPS = 1e-6
CHUNK = 64
GMLP_BLOCK = 128
C_NOPE = 128
C_ROPE = 64
C_V = 128
ROPE_THETA = 10000.0
MEM_HEADS = 4
ADAM_LR = 0.001
ADAM_B1 = 0.9
ADAM_B2 = 0.999
ADAM_EPS = 1e-08
ADAM_WD = 0.01
ADAM_STEP = 10
NEG_BIG = -1e30
VMEM_LIMIT = 56 * 1024 * 1024


def _cp(*sem):
    return pltpu.CompilerParams(dimension_semantics=sem, vmem_limit_bytes=VMEM_LIMIT)


def _tile(n, pref):
    t = min(n, pref)
    while n % t and t > LANES:
        t //= 2
    return n if n % t else t


def _mm(a, b, *, ta=False, tb=False, out_dtypes=(BF16,), extras=(), epilogue=None, tm=1024, tn=1024, tk=2048, name):
    m, k = (a.shape[1], a.shape[0]) if ta else a.shape
    n = b.shape[0] if tb else b.shape[1]
    assert (b.shape[1] if tb else b.shape[0]) == k, (a.shape, b.shape, ta, tb)
    tm, tn, tk = _tile(m, tm), _tile(n, tn), _tile(k, tk)
    nk = k // tk
    ne, no = len(extras), len(out_dtypes)
    dims = (((0 if ta else 1,), (1 if tb else 0,)), ((), ()))

    def body(*refs):
        a_ref, b_ref = refs[0], refs[1]
        e_refs = refs[2:2 + ne]
        o_refs = refs[2 + ne:2 + ne + no]

        def finish(acc):
            outs = (acc,) if epilogue is None else epilogue(acc, *[e[...] for e in e_refs])
            for o, val in zip(o_refs, outs):
                o[...] = val.astype(o.dtype)

        prod = lax.dot_general(a_ref[...].astype(BF16), b_ref[...].astype(BF16), dims, preferred_element_type=F32)
        if nk == 1:
            finish(prod)
        else:
            acc_ref = refs[-1]
            kk = pl.program_id(2)

            @pl.when(kk == 0)
            def _():
                acc_ref[...] = prod

            @pl.when((kk > 0) & (kk < nk - 1))
            def _():
                acc_ref[...] += prod

            @pl.when(kk == nk - 1)
            def _():
                finish(acc_ref[...] + prod)

    a_spec = pl.BlockSpec((tk, tm), lambda i, j, kk: (kk, i)) if ta else pl.BlockSpec((tm, tk), lambda i, j, kk: (i, kk))
    b_spec = pl.BlockSpec((tn, tk), lambda i, j, kk: (j, kk)) if tb else pl.BlockSpec((tk, tn), lambda i, j, kk: (kk, j))
    e_specs = []
    for arr, kind in extras:
        if kind == "mn":
            e_specs.append(pl.BlockSpec((tm, tn), lambda i, j, kk: (i, j)))
        else:
            e_specs.append(pl.BlockSpec((tm, arr.shape[1]), lambda i, j, kk: (i, 0)))
    outs = pl.pallas_call(
        body,
        out_shape=tuple(jax.ShapeDtypeStruct((m, n), d) for d in out_dtypes),
        grid=(m // tm, n // tn, nk),
        in_specs=[a_spec, b_spec] + e_specs,
        out_specs=tuple(pl.BlockSpec((tm, tn), lambda i, j, kk: (i, j)) for _ in out_dtypes),
        scratch_shapes=[pltpu.VMEM((tm, tn), F32)] if nk > 1 else [],
        compiler_params=_cp("parallel", "parallel", "arbitrary"),
        name=name,
    )(a, b, *[arr for arr, _ in extras])
    return outs[0] if no == 1 else outs


def _epi_add(acc, r):
    return (acc + r,)


def _epi_relu2(acc):
    h = jnp.maximum(acc, 0.0)
    return (acc, h * h)


def _epi_drelu2(acc, h):
    return (acc * (2.0 * jnp.maximum(h.astype(F32), 0.0)),)


def _rope(x, c, s1, s2):
    return x * c + pltpu.roll(x, LANES - C_ROPE // 2, 1) * s1 + pltpu.roll(x, C_ROPE // 2, 1) * s2


def _rope_t(dy, c, s1, s2):
    return dy * c + pltpu.roll(dy * s1, C_ROPE // 2, 1) + pltpu.roll(dy * s2, LANES - C_ROPE // 2, 1)


def _epi_rope_heads(acc, c, s1, s2):
    parts = []
    for h0 in range(0, acc.shape[1], 2 * LANES):
        parts.append(acc[:, h0:h0 + LANES])
        parts.append(_rope(acc[:, h0 + LANES:h0 + 2 * LANES], c, s1, s2))
    return (jnp.concatenate(parts, axis=1),)


def _rmsnorm_fwd(x, g, *, width=None, col=0, name):
    t = x.shape[0]
    w = x.shape[1] if width is None else width
    tm = _tile(t, 512)

    def body(x_ref, g_ref, o_ref):
        xv = x_ref[...].astype(F32)
        r = lax.rsqrt(jnp.mean(xv * xv, axis=-1, keepdims=True) + RMS_EPS)
        o_ref[...] = (xv * r * g_ref[...]).astype(o_ref.dtype)

    return pl.pallas_call(
        body,
        out_shape=jax.ShapeDtypeStruct((t, w), BF16),
        grid=(t // tm,),
        in_specs=[pl.BlockSpec((tm, w), lambda i: (i, col)), pl.BlockSpec((1, w), lambda i: (0, 0))],
        out_specs=pl.BlockSpec((tm, w), lambda i: (i, 0)),
        compiler_params=_cp("parallel"),
        name=name,
    )(x, g.reshape(1, w))


def _rmsnorm_bwd(x, g, dy, res=None, *, width=None, col=0, name):
    t = x.shape[0]
    w = x.shape[1] if width is None else width
    tm = _tile(t, 256)
    has_res = res is not None

    def body(*refs):
        x_ref, g_ref, dy_ref = refs[:3]
        res_ref = refs[3] if has_res else None
        dx_ref, dg_ref = refs[-2], refs[-1]
        xv = x_ref[...].astype(F32)
        r = lax.rsqrt(jnp.mean(xv * xv, axis=-1, keepdims=True) + RMS_EPS)
        xh = xv * r
        dyv = dy_ref[...].astype(F32)
        dxh = dyv * g_ref[...]
        dx = r * (dxh - xh * jnp.mean(dxh * xh, axis=-1, keepdims=True))
        if has_res:
            dx = dx + res_ref[...]
        dx_ref[...] = dx

        @pl.when(pl.program_id(0) == 0)
        def _():
            dg_ref[...] = jnp.zeros_like(dg_ref)

        dg_ref[...] += jnp.sum(dyv * xh, axis=0, keepdims=True)

    in_specs = [pl.BlockSpec((tm, w), lambda i: (i, col)), pl.BlockSpec((1, w), lambda i: (0, 0)), pl.BlockSpec((tm, w), lambda i: (i, 0))]
    args = [x, g.reshape(1, w), dy]
    if has_res:
        in_specs.append(pl.BlockSpec((tm, w), lambda i: (i, 0)))
        args.append(res)
    return pl.pallas_call(
        body,
        out_shape=(jax.ShapeDtypeStruct((t, w), F32), jax.ShapeDtypeStruct((1, w), F32)),
        grid=(t // tm,),
        in_specs=in_specs,
        out_specs=(pl.BlockSpec((tm, w), lambda i: (i, 0)), pl.BlockSpec((1, w), lambda i: (0, 0))),
        compiler_params=_cp("arbitrary"),
        name=name,
    )(*args)


def _final_loss(x, g, target, *, name):
    t, d = x.shape
    tm = _tile(t, 256)

    def body(x_ref, g_ref, t_ref, loss_ref, dx_ref, dg_ref):
        xv = x_ref[...]
        r = lax.rsqrt(jnp.mean(xv * xv, axis=-1, keepdims=True) + RMS_EPS)
        xh = xv * r
        gv = g_ref[...]
        err = xh * gv - t_ref[...]
        dyv = err * (1.0 / d)
        dxh = dyv * gv
        dx_ref[...] = r * (dxh - xh * jnp.mean(dxh * xh, axis=-1, keepdims=True))

        @pl.when(pl.program_id(0) == 0)
        def _():
            dg_ref[...] = jnp.zeros_like(dg_ref)
            loss_ref[...] = jnp.zeros_like(loss_ref)

        dg_ref[...] += jnp.sum(dyv * xh, axis=0, keepdims=True)
        row = jnp.sum(err * err, axis=-1, keepdims=True) * (0.5 / d)
        loss_ref[...] += jnp.sum(row, axis=0, keepdims=True)

    return pl.pallas_call(
        body,
        out_shape=(jax.ShapeDtypeStruct((1, 1), F32), jax.ShapeDtypeStruct((t, d), F32), jax.ShapeDtypeStruct((1, d), F32)),
        grid=(t // tm,),
        in_specs=[pl.BlockSpec((tm, d), lambda i: (i, 0)), pl.BlockSpec((1, d), lambda i: (0, 0)), pl.BlockSpec((tm, d), lambda i: (i, 0))],
        out_specs=(pl.BlockSpec((1, 1), lambda i: (0, 0)), pl.BlockSpec((tm, d), lambda i: (i, 0)), pl.BlockSpec((1, d), lambda i: (0, 0))),
        compiler_params=_cp("arbitrary"),
        name=name,
    )(x, g.reshape(1, d), target)


_GELU_C = math.sqrt(2.0 / math.pi)
_GELU_A = 0.044715


def _gelu(x):
    t = jnp.tanh(_GELU_C * (x + _GELU_A * x * x * x))
    return 0.5 * x * (1.0 + t), t


def _dgelu(x, t):
    return 0.5 * (1.0 + t) + 0.5 * x * (1.0 - t * t) * (_GELU_C * (1.0 + 3.0 * _GELU_A * x * x))


def _spatial_mask():
    ci = lax.broadcasted_iota(jnp.int32, (GMLP_BLOCK, GMLP_BLOCK), 0) // CHUNK
    cj = lax.broadcasted_iota(jnp.int32, (GMLP_BLOCK, GMLP_BLOCK), 1) // CHUNK
    return (cj <= ci).astype(F32)


def _shift_down(x, k):
    rows = lax.broadcasted_iota(jnp.int32, x.shape, 0)
    return jnp.where(rows >= k, pltpu.roll(x, k, 0), 0.0)


def _shift_up(x, k):
    n = x.shape[0]
    rows = lax.broadcasted_iota(jnp.int32, x.shape, 0)
    return jnp.where(rows < n - k, pltpu.roll(x, n - k, 0), 0.0)


def _gmlp_conv_fwd(z, ws, bs, gn, cw, *, seq, name):
    t = z.shape[0]
    g_n = ws.shape[0]
    nb = seq // GMLP_BLOCK
    L = LANES

    def body(z_ref, ws_ref, bs_ref, gn_ref, cw_ref, y_ref):
        u = z_ref[:, 0:L]
        v = z_ref[:, L:2 * L]
        gu, _ = _gelu(u)
        gv, _ = _gelu(v)
        r = lax.rsqrt(jnp.mean(gv * gv, axis=-1, keepdims=True) + RMS_EPS)
        vg = (gv * r * gn_ref[0]).astype(BF16)
        wm = (ws_ref[0] * _spatial_mask()).astype(BF16)
        bias = bs_ref[0]
        for n in range(nb):
            sl = slice(n * GMLP_BLOCK, (n + 1) * GMLP_BLOCK)
            mixed = jnp.dot(wm, vg[sl], preferred_element_type=F32) + bias
            y_ref[sl, 0:L] = (gu[sl] * mixed).astype(y_ref.dtype)
        bg = z_ref[:, 2 * L:3 * L]
        zc = z_ref[:, 3 * L:4 * L] * z_ref[:, 4 * L:5 * L]
        w0, w1, w2 = cw_ref[0, 0:1, :], cw_ref[0, 1:2, :], cw_ref[0, 2:3, :]
        conv = w0 * _shift_down(zc, 2) + w1 * _shift_down(zc, 1) + w2 * zc
        y_ref[:, L:2 * L] = (bg * conv).astype(y_ref.dtype)

    return pl.pallas_call(
        body,
        out_shape=jax.ShapeDtypeStruct((t, 2 * L * g_n), BF16),
        grid=(g_n, t // seq),
        in_specs=[
            pl.BlockSpec((seq, 5 * L), lambda g, b: (b, g)),
            pl.BlockSpec((1, GMLP_BLOCK, GMLP_BLOCK), lambda g, b: (g, 0, 0)),
            pl.BlockSpec((1, GMLP_BLOCK, 1), lambda g, b: (g, 0, 0)),
            pl.BlockSpec((1, 1, L), lambda g, b: (g, 0, 0)),
            pl.BlockSpec((1, 3, L), lambda g, b: (g, 0, 0)),
        ],
        out_specs=pl.BlockSpec((seq, 2 * L), lambda g, b: (b, g)),
        compiler_params=_cp("parallel", "parallel"),
        name=name,
    )(z, ws, bs, gn, cw)


def _gmlp_conv_bwd(z, dy, ws, bs, gn, cw, *, seq, name):
    t = z.shape[0]
    g_n = ws.shape[0]
    nb = seq // GMLP_BLOCK
    L = LANES
    dims_nt = (((1,), (1,)), ((), ()))
    dims_tn = (((0,), (0,)), ((), ()))

    def body(z_ref, dy_ref, ws_ref, bs_ref, gn_ref, cw_ref, dz_ref, dws_ref, dbs_ref, dgn_ref, dcw_ref, dvg_ref):
        @pl.when(pl.program_id(1) == 0)
        def _():
            dws_ref[...] = jnp.zeros_like(dws_ref)
            dbs_ref[...] = jnp.zeros_like(dbs_ref)
            dgn_ref[...] = jnp.zeros_like(dgn_ref)
            dcw_ref[...] = jnp.zeros_like(dcw_ref)

        u = z_ref[:, 0:L]
        v = z_ref[:, L:2 * L]
        gu, tu = _gelu(u)
        gv, tv = _gelu(v)
        r = lax.rsqrt(jnp.mean(gv * gv, axis=-1, keepdims=True) + RMS_EPS)
        gvh = gv * r
        gnv = gn_ref[0]
        vg = (gvh * gnv).astype(BF16)
        mask = _spatial_mask()
        wm = (ws_ref[0] * mask).astype(BF16)
        bias = bs_ref[0]
        dya = dy_ref[:, 0:L].astype(F32)
        dws = jnp.zeros((GMLP_BLOCK, GMLP_BLOCK), F32)
        dbs = jnp.zeros((GMLP_BLOCK, 1), F32)
        for n in range(nb):
            sl = slice(n * GMLP_BLOCK, (n + 1) * GMLP_BLOCK)
            mixed = jnp.dot(wm, vg[sl], preferred_element_type=F32) + bias
            dmixed = dya[sl] * gu[sl]
            dgu = dya[sl] * mixed
            dz_ref[sl, 0:L] = (dgu * _dgelu(u[sl], tu[sl])).astype(dz_ref.dtype)
            dmb = dmixed.astype(BF16)
            dws = dws + lax.dot_general(dmb, vg[sl], dims_nt, preferred_element_type=F32)
            dbs = dbs + jnp.sum(dmixed, axis=1, keepdims=True)
            dvg_ref[sl, :] = lax.dot_general(wm, dmb, dims_tn, preferred_element_type=F32)
        dws_ref[0] += dws * mask
        dbs_ref[0] += dbs
        dvg = dvg_ref[...]
        dgn_ref[0] += jnp.sum(dvg * gvh, axis=0, keepdims=True)
        dvh = dvg * gnv
        dgv = r * (dvh - gvh * jnp.mean(dvh * gvh, axis=-1, keepdims=True))
        dz_ref[:, L:2 * L] = (dgv * _dgelu(v, tv)).astype(dz_ref.dtype)

        bg = z_ref[:, 2 * L:3 * L]
        cg = z_ref[:, 3 * L:4 * L]
        hh = z_ref[:, 4 * L:5 * L]
        zc = cg * hh
        w0, w1, w2 = cw_ref[0, 0:1, :], cw_ref[0, 1:2, :], cw_ref[0, 2:3, :]
        z1 = _shift_down(zc, 1)
        z2 = _shift_down(zc, 2)
        conv = w0 * z2 + w1 * z1 + w2 * zc
        dyb = dy_ref[:, L:2 * L].astype(F32)
        dconv = dyb * bg
        dz_ref[:, 2 * L:3 * L] = (dyb * conv).astype(dz_ref.dtype)
        dzc = w2 * dconv + w1 * _shift_up(dconv, 1) + w0 * _shift_up(dconv, 2)
        dz_ref[:, 3 * L:4 * L] = (dzc * hh).astype(dz_ref.dtype)
        dz_ref[:, 4 * L:5 * L] = (dzc * cg).astype(dz_ref.dtype)
        dcw_ref[0, 0:1, :] += jnp.sum(dconv * z2, axis=0, keepdims=True)
        dcw_ref[0, 1:2, :] += jnp.sum(dconv * z1, axis=0, keepdims=True)
        dcw_ref[0, 2:3, :] += jnp.sum(dconv * zc, axis=0, keepdims=True)

    return pl.pallas_call(
        body,
        out_shape=(
            jax.ShapeDtypeStruct((t, 5 * L * g_n), BF16),
            jax.ShapeDtypeStruct(ws.shape, F32),
            jax.ShapeDtypeStruct(bs.shape, F32),
            jax.ShapeDtypeStruct(gn.shape, F32),
            jax.ShapeDtypeStruct(cw.shape, F32),
        ),
        grid=(g_n, t // seq),
        in_specs=[
            pl.BlockSpec((seq, 5 * L), lambda g, b: (b, g)),
            pl.BlockSpec((seq, 2 * L), lambda g, b: (b, g)),
            pl.BlockSpec((1, GMLP_BLOCK, GMLP_BLOCK), lambda g, b: (g, 0, 0)),
            pl.BlockSpec((1, GMLP_BLOCK, 1), lambda g, b: (g, 0, 0)),
            pl.BlockSpec((1, 1, L), lambda g, b: (g, 0, 0)),
            pl.BlockSpec((1, 3, L), lambda g, b: (g, 0, 0)),
        ],
        out_specs=(
            pl.BlockSpec((seq, 5 * L), lambda g, b: (b, g)),
            pl.BlockSpec((1, GMLP_BLOCK, GMLP_BLOCK), lambda g, b: (g, 0, 0)),
            pl.BlockSpec((1, GMLP_BLOCK, 1), lambda g, b: (g, 0, 0)),
            pl.BlockSpec((1, 1, L), lambda g, b: (g, 0, 0)),
            pl.BlockSpec((1, 3, L), lambda g, b: (g, 0, 0)),
        ),
        scratch_shapes=[pltpu.VMEM((seq, L), F32)],
        compiler_params=_cp("parallel", "arbitrary"),
        name=name,
    )(z, dy, ws, bs, gn, cw)


def _rope_rows(x, c, s1, s2, *, transpose, out_dtype, name):
    t, w = x.shape
    tm = _tile(t, 512)
    only = w == LANES
    fn = _rope_t if transpose else _rope

    def body(x_ref, c_ref, s1_ref, s2_ref, o_ref):
        cv, s1v, s2v = c_ref[...], s1_ref[...], s2_ref[...]
        if only:
            o_ref[...] = fn(x_ref[...].astype(F32), cv, s1v, s2v).astype(o_ref.dtype)
        else:
            for h0 in range(0, w, 2 * LANES):
                o_ref[:, h0:h0 + LANES] = x_ref[:, h0:h0 + LANES].astype(o_ref.dtype)
                o_ref[:, h0 + LANES:h0 + 2 * LANES] = fn(x_ref[:, h0 + LANES:h0 + 2 * LANES].astype(F32), cv, s1v, s2v).astype(o_ref.dtype)

    tab = pl.BlockSpec((tm, LANES), lambda i: (i, 0))
    return pl.pallas_call(
        body,
        out_shape=jax.ShapeDtypeStruct((t, w), out_dtype),
        grid=(t // tm,),
        in_specs=[pl.BlockSpec((tm, w), lambda i: (i, 0)), tab, tab, tab],
        out_specs=pl.BlockSpec((tm, w), lambda i: (i, 0)),
        compiler_params=_cp("parallel"),
        name=name,
    )(x, c, s1, s2)


_NT = (((1,), (1,)), ((), ()))
_TN = (((0,), (0,)), ((), ()))


def _chunk_mask(tq):
    qc = lax.broadcasted_iota(jnp.int32, (tq, tq), 0) // CHUNK
    kc = lax.broadcasted_iota(jnp.int32, (tq, tq), 1) // CHUNK
    return kc <= qc


MLA_BLOCK = 512


def _mla_fwd(q, kv, kr, *, seq, heads, name):
    t = q.shape[0]
    tq = _tile(seq, MLA_BLOCK)
    nq = seq // tq
    scale = (C_NOPE + C_ROPE) ** -0.5

    def body(q_ref, kv_ref, kr_ref, o_ref, lse_ref, kcat_ref):
        i = pl.program_id(2)

        @pl.when(i == 0)
        def _():
            kcat_ref[:, :LANES] = kv_ref[:, :LANES]
            kcat_ref[:, LANES:] = kr_ref[...]

        qv = q_ref[...]

        def step(j, carry, diagonal):
            m_run, l_run, acc = carry
            rows = pl.ds(pl.multiple_of(j * tq, tq), tq)
            s = lax.dot_general(qv, kcat_ref[rows, :], _NT, preferred_element_type=F32) * scale
            if diagonal:
                s = jnp.where(_chunk_mask(tq), s, NEG_BIG)
            m_new = jnp.maximum(m_run, jnp.max(s, axis=-1, keepdims=True))
            alpha = jnp.exp(m_run - m_new)
            p = jnp.exp(s - m_new)
            l_new = alpha * l_run + jnp.sum(p, axis=-1, keepdims=True)
            acc = alpha * acc + jnp.dot(p.astype(BF16), kv_ref[rows, LANES:], preferred_element_type=F32)
            return m_new, l_new, acc

        init = (jnp.full((tq, 1), NEG_BIG, F32), jnp.zeros((tq, 1), F32), jnp.zeros((tq, C_V), F32))
        carry = lax.fori_loop(0, i, lambda j, c: step(j, c, False), init)
        m_run, l_run, acc = step(i, carry, True)
        o_ref[...] = (acc / l_run).astype(o_ref.dtype)
        lse_ref[...] = m_run + jnp.log(l_run)

    return pl.pallas_call(
        body,
        out_shape=(jax.ShapeDtypeStruct((t, heads * C_V), BF16), jax.ShapeDtypeStruct((heads, t, 1), F32)),
        grid=(t // seq, heads, nq),
        in_specs=[
            pl.BlockSpec((tq, 2 * LANES), lambda b, h, i: (b * nq + i, h)),
            pl.BlockSpec((seq, 2 * LANES), lambda b, h, i: (b, h)),
            pl.BlockSpec((seq, LANES), lambda b, h, i: (b, 0)),
        ],
        out_specs=(
            pl.BlockSpec((tq, C_V), lambda b, h, i: (b * nq + i, h)),
            pl.BlockSpec((None, tq, 1), lambda b, h, i: (h, b * nq + i, 0)),
        ),
        scratch_shapes=[pltpu.VMEM((seq, 2 * LANES), BF16)],
        compiler_params=_cp("parallel", "parallel", "arbitrary"),
        name=name,
    )(q, kv, kr)


def _mla_bwd(q, kv, kr, do, o, lse, *, seq, heads, name):
    t = q.shape[0]
    tq = _tile(seq, MLA_BLOCK)
    nq = seq // tq
    scale = (C_NOPE + C_ROPE) ** -0.5

    def body(q_ref, kv_ref, kr_ref, do_ref, o_ref, lse_ref, dq_ref, dkv_ref, dkr_ref, kcat_ref, dk_ref, dv_ref):
        h, i = pl.program_id(1), pl.program_id(2)

        @pl.when(i == 0)
        def _():
            kcat_ref[:, :LANES] = kv_ref[:, :LANES]
            kcat_ref[:, LANES:] = kr_ref[...]
            dk_ref[...] = jnp.zeros_like(dk_ref)
            dv_ref[...] = jnp.zeros_like(dv_ref)

        @pl.when((i == 0) & (h == 0))
        def _():
            dkr_ref[...] = jnp.zeros_like(dkr_ref)

        qv = q_ref[...]
        dov = do_ref[...]
        lse_v = lse_ref[...]
        delta = jnp.sum(dov.astype(F32) * o_ref[...].astype(F32), axis=-1, keepdims=True)

        def step(j, dq, diagonal):
            rows = pl.ds(pl.multiple_of(j * tq, tq), tq)
            kj = kcat_ref[rows, :]
            s = lax.dot_general(qv, kj, _NT, preferred_element_type=F32) * scale
            if diagonal:
                s = jnp.where(_chunk_mask(tq), s, NEG_BIG)
            p = jnp.exp(s - lse_v)
            dp = lax.dot_general(dov, kv_ref[rows, LANES:], _NT, preferred_element_type=F32)
            ds = (p * (dp - delta) * scale).astype(BF16)
            dk_ref[rows, :] += lax.dot_general(ds, qv, _TN, preferred_element_type=F32)
            dv_ref[rows, :] += lax.dot_general(p.astype(BF16), dov, _TN, preferred_element_type=F32)
            return dq + jnp.dot(ds, kj, preferred_element_type=F32)

        dq = lax.fori_loop(0, i, lambda j, c: step(j, c, False), jnp.zeros((tq, 2 * LANES), F32))
        dq_ref[...] = step(i, dq, True)

        @pl.when(i == nq - 1)
        def _():
            dkv_ref[:, :LANES] = dk_ref[:, :LANES].astype(dkv_ref.dtype)
            dkv_ref[:, LANES:] = dv_ref[...].astype(dkv_ref.dtype)
            dkr_ref[...] += dk_ref[:, LANES:]

    return pl.pallas_call(
        body,
        out_shape=(
            jax.ShapeDtypeStruct((t, heads * 2 * LANES), F32),
            jax.ShapeDtypeStruct((t, heads * 2 * LANES), BF16),
            jax.ShapeDtypeStruct((t, LANES), F32),
        ),
        grid=(t // seq, heads, nq),
        in_specs=[
            pl.BlockSpec((tq, 2 * LANES), lambda b, h, i: (b * nq + i, h)),
            pl.BlockSpec((seq, 2 * LANES), lambda b, h, i: (b, h)),
            pl.BlockSpec((seq, LANES), lambda b, h, i: (b, 0)),
            pl.BlockSpec((tq, C_V), lambda b, h, i: (b * nq + i, h)),
            pl.BlockSpec((tq, C_V), lambda b, h, i: (b * nq + i, h)),
            pl.BlockSpec((None, tq, 1), lambda b, h, i: (h, b * nq + i, 0)),
        ],
        out_specs=(
            pl.BlockSpec((tq, 2 * LANES), lambda b, h, i: (b * nq + i, h)),
            pl.BlockSpec((seq, 2 * LANES), lambda b, h, i: (b, h)),
            pl.BlockSpec((seq, LANES), lambda b, h, i: (b, 0)),
        ),
        scratch_shapes=[pltpu.VMEM((seq, 2 * LANES), BF16), pltpu.VMEM((seq, 2 * LANES), F32), pltpu.VMEM((seq, C_V), F32)],
        compiler_params=_cp("parallel", "arbitrary", "arbitrary"),
        name=name,
    )(q, kv, kr, do, o, lse)


def _softmax_rows(s):
    e = jnp.exp(s - jnp.max(s, axis=-1, keepdims=True))
    return e / jnp.sum(e, axis=-1, keepdims=True)


def _mem_fwd(q, k, v, *, seq, n_mem, name):
    t, d = q.shape
    hd = d // MEM_HEADS
    tq = _tile(seq, 512)
    nq = seq // tq
    scale = hd ** -0.5

    def body(q_ref, k_ref, v_ref, o_ref):
        s = lax.dot_general(q_ref[...], k_ref[...], _NT, preferred_element_type=F32) * scale
        p = _softmax_rows(s)
        o_ref[...] = jnp.dot(p.astype(BF16), v_ref[...], preferred_element_type=F32).astype(o_ref.dtype)

    return pl.pallas_call(
        body,
        out_shape=jax.ShapeDtypeStruct((t, d), BF16),
        grid=(t // seq, MEM_HEADS, nq),
        in_specs=[
            pl.BlockSpec((tq, hd), lambda b, h, i: (b * nq + i, h)),
            pl.BlockSpec((n_mem, hd), lambda b, h, i: (b, h)),
            pl.BlockSpec((n_mem, hd), lambda b, h, i: (b, h)),
        ],
        out_specs=pl.BlockSpec((tq, hd), lambda b, h, i: (b * nq + i, h)),
        compiler_params=_cp("parallel", "parallel", "arbitrary"),
        name=name,
    )(q, k, v)


def _mem_bwd(q, k, v, do, *, seq, n_mem, name):
    t, d = q.shape
    hd = d // MEM_HEADS
    tq = _tile(seq, 512)
    nq = seq // tq
    scale = hd ** -0.5

    def body(q_ref, k_ref, v_ref, do_ref, dq_ref, dk_ref, dv_ref):
        @pl.when(pl.program_id(2) == 0)
        def _():
            dk_ref[...] = jnp.zeros_like(dk_ref)
            dv_ref[...] = jnp.zeros_like(dv_ref)

        qv, kv_, vv, dov = q_ref[...], k_ref[...], v_ref[...], do_ref[...]
        p = _softmax_rows(lax.dot_general(qv, kv_, _NT, preferred_element_type=F32) * scale)
        dp = lax.dot_general(dov, vv, _NT, preferred_element_type=F32)
        ds = (p * (dp - jnp.sum(dp * p, axis=-1, keepdims=True)) * scale).astype(BF16)
        dq_ref[...] = jnp.dot(ds, kv_, preferred_element_type=F32).astype(dq_ref.dtype)
        dk_ref[...] += lax.dot_general(ds, qv, _TN, preferred_element_type=F32)
        dv_ref[...] += lax.dot_general(p.astype(BF16), dov, _TN, preferred_element_type=F32)

    return pl.pallas_call(
        body,
        out_shape=(
            jax.ShapeDtypeStruct((t, d), BF16),
            jax.ShapeDtypeStruct(k.shape, F32),
            jax.ShapeDtypeStruct(v.shape, F32),
        ),
        grid=(t // seq, MEM_HEADS, nq),
        in_specs=[
            pl.BlockSpec((tq, hd), lambda b, h, i: (b * nq + i, h)),
            pl.BlockSpec((n_mem, hd), lambda b, h, i: (b, h)),
            pl.BlockSpec((n_mem, hd), lambda b, h, i: (b, h)),
            pl.BlockSpec((tq, hd), lambda b, h, i: (b * nq + i, h)),
        ],
        out_specs=(
            pl.BlockSpec((tq, hd), lambda b, h, i: (b * nq + i, h)),
            pl.BlockSpec((n_mem, hd), lambda b, h, i: (b, h)),
            pl.BlockSpec((n_mem, hd), lambda b, h, i: (b, h)),
        ),
        compiler_params=_cp("parallel", "parallel", "arbitrary"),
        name=name,
    )(q, k, v, do)


def _me():
    return lax.axis_index("x"), lax.axis_index("y"), lax.axis_index("c")


def _dev_index(px, py, pc):
    return 4 * px + 2 * py + pc


def _chunk(ref, idx, axis, size):
    start = pl.multiple_of(idx * size, size)
    return ref.at[pl.ds(start, size), :] if axis == 0 else ref.at[:, pl.ds(start, size)]


def _all_gather_small(v):
    m, n = v.shape

    def body(x_ref, out_ref, send_sems, recv_sems, local_sem):
        x, y, c = _me()
        me, sibling = (x, y, c), (x, y, 1 - c)
        chips = [(1 - x, y), (x, 1 - y), (1 - x, 1 - y)]

        def rows(px, py, pc):
            return out_ref.at[_dev_index(px, py, pc)]

        def copy(k, block, to, src=None):
            return pltpu.make_async_remote_copy(
                src_ref=rows(*block) if src is None else src, dst_ref=rows(*block),
                send_sem=send_sems.at[k], recv_sem=recv_sems.at[k], device_id=to, device_id_type=MESH)

        mine = pltpu.make_async_copy(x_ref, rows(*me), local_sem)
        mine.start()
        first = [copy(0, me, sibling, src=x_ref)]
        first += [copy(1 + j, me, (*chip, c), src=x_ref) for j, chip in enumerate(chips)]
        for cp in first:
            cp.start()
        passed = [copy(4 + j, (*chip, c), sibling) for j, chip in enumerate(chips)]
        for j, chip in enumerate(chips):
            copy(1 + j, (*chip, c), me).wait_recv()
            passed[j].start()
        copy(0, sibling, me).wait_recv()
        for j, chip in enumerate(chips):
            copy(4 + j, (*chip, 1 - c), me).wait_recv()
        for cp in first + passed:
            cp.wait_send()
        mine.wait()

    return pl.pallas_call(
        body,
        out_shape=jax.ShapeDtypeStruct((N_DEV, m, n), v.dtype),
        in_specs=[pl.BlockSpec(memory_space=pltpu.VMEM)],
        out_specs=pl.BlockSpec(memory_space=pltpu.VMEM),
        scratch_shapes=[pltpu.SemaphoreType.DMA((7,)), pltpu.SemaphoreType.DMA((7,)), pltpu.SemaphoreType.DMA],
        name="all_gather_small",
    )(v)


def _cast_place(w, axis, dev_arr, *, name):
    r, c = w.shape
    tr = _tile(r, 256)
    nr = r // tr
    full = (N_DEV * r, c) if axis == 0 else (r, N_DEV * c)

    def body(dev_ref, w_ref, o_ref):
        o_ref[...] = w_ref[...].astype(o_ref.dtype)

    if axis == 0:
        out_spec = pl.BlockSpec((tr, c), lambda i, dev: (dev[0] * nr + i, 0))
    else:
        out_spec = pl.BlockSpec((tr, c), lambda i, dev: (i, dev[0]))
    return pl.pallas_call(
        body,
        out_shape=jax.ShapeDtypeStruct(full, BF16),
        grid_spec=pltpu.PrefetchScalarGridSpec(
            num_scalar_prefetch=1, grid=(nr,), in_specs=[pl.BlockSpec((tr, c), lambda i, dev: (i, 0))], out_specs=out_spec),
        compiler_params=_cp("parallel"),
        name=name,
    )(dev_arr, w)


def _all_gather_weights(fulls, axes, *, name):
    n = len(fulls)
    sizes = [s.shape[a] // N_DEV for s, a in zip(fulls, axes)]

    def body(*refs):
        outs = refs[n:2 * n]
        send_sems, recv_sems = refs[2 * n:]
        x, y, c = _me()
        me, sibling = (x, y, c), (x, y, 1 - c)
        chips = [(1 - x, y), (x, 1 - y), (1 - x, 1 - y)]

        def rows(t, dev):
            return _chunk(outs[t], _dev_index(*dev), axes[t], sizes[t])

        def copy(t, k, block, to):
            return pltpu.make_async_remote_copy(
                src_ref=rows(t, block), dst_ref=rows(t, block),
                send_sem=send_sems.at[t, k], recv_sem=recv_sems.at[t, k], device_id=to, device_id_type=MESH)

        first, passed = [], []
        for t in range(n):
            f = [copy(t, 0, me, sibling)]
            f += [copy(t, 1 + j, me, (*chip, c)) for j, chip in enumerate(chips)]
            for cp in f:
                cp.start()
            first.append(f)
        for t in range(n):
            p = [copy(t, 4 + j, (*chip, c), sibling) for j, chip in enumerate(chips)]
            for j, chip in enumerate(chips):
                copy(t, 1 + j, (*chip, c), me).wait_recv()
                p[j].start()
            passed.append(p)
        for t in range(n):
            copy(t, 0, sibling, me).wait_recv()
            for j, chip in enumerate(chips):
                copy(t, 4 + j, (*chip, 1 - c), me).wait_recv()
            for cp in first[t] + passed[t]:
                cp.wait_send()

    any_spec = pl.BlockSpec(memory_space=pl.ANY)
    return pl.pallas_call(
        body,
        out_shape=tuple(jax.ShapeDtypeStruct(s.shape, s.dtype) for s in fulls),
        in_specs=[any_spec] * n,
        out_specs=tuple([any_spec] * n),
        input_output_aliases={t: t for t in range(n)},
        scratch_shapes=[pltpu.SemaphoreType.DMA((n, 7)), pltpu.SemaphoreType.DMA((n, 7))],
        name=name,
    )(*fulls)


def _scatter_d2d(grads, axes, *, name):
    n = len(grads)
    sizes = [g.shape[a] // N_DEV for g, a in zip(grads, axes)]

    def body(*refs):
        ins, gots = refs[:n], refs[n:2 * n]
        send_sems, recv_sems = refs[2 * n:]
        x, y, c = _me()
        sibling = (x, y, 1 - c)
        started = []
        for t in range(n):
            for k in range(4):
                give = pltpu.make_async_remote_copy(
                    src_ref=_chunk(ins[t], 2 * k + 1 - c, axes[t], sizes[t]), dst_ref=_chunk(gots[t], k, axes[t], sizes[t]),
                    send_sem=send_sems.at[t, k], recv_sem=recv_sems.at[t, k], device_id=sibling, device_id_type=MESH)
                give.start()
                started.append(give)
        for give in started:
            give.wait_recv()
            give.wait_send()

    out_shape = []
    for g, a in zip(grads, axes):
        sh = list(g.shape)
        sh[a] //= 2
        out_shape.append(jax.ShapeDtypeStruct(tuple(sh), g.dtype))
    any_spec = pl.BlockSpec(memory_space=pl.ANY)
    return pl.pallas_call(
        body,
        out_shape=tuple(out_shape),
        in_specs=[any_spec] * n,
        out_specs=tuple([any_spec] * n),
        scratch_shapes=[pltpu.SemaphoreType.DMA((n, 4)), pltpu.SemaphoreType.DMA((n, 4))],
        name=name,
    )(*grads)


def _scatter_ici(sums, axes, *, name):
    n = len(sums)
    sizes = [g.shape[a] // 4 for g, a in zip(sums, axes)]

    def body(*refs):
        ins, outs = refs[:n], refs[n:2 * n]
        send_sems, recv_sems = refs[2 * n:]
        x, y, c = _me()
        my_chip = 2 * x + y
        peers = []
        for j in range(1, 4):
            px, py = (1 - x if (j >> 1) & 1 else x), (1 - y if j & 1 else y)
            peers.append((px, py))

        def copy(t, j, chip_src, chip_dst, to):
            return pltpu.make_async_remote_copy(
                src_ref=_chunk(ins[t], chip_src, axes[t], sizes[t]), dst_ref=_chunk(outs[t], chip_dst, axes[t], sizes[t]),
                send_sem=send_sems.at[t, j], recv_sem=recv_sems.at[t, j], device_id=to, device_id_type=MESH)

        sent = []
        for t in range(n):
            for j, (px, py) in enumerate(peers):
                cp = copy(t, j, 2 * px + py, my_chip, (px, py, c))
                cp.start()
                sent.append(cp)
        for t in range(n):
            for j, (px, py) in enumerate(peers):
                copy(t, j, my_chip, 2 * px + py, (px, py, c)).wait_recv()
        for cp in sent:
            cp.wait_send()

    any_spec = pl.BlockSpec(memory_space=pl.ANY)
    return pl.pallas_call(
        body,
        out_shape=tuple(jax.ShapeDtypeStruct(g.shape, g.dtype) for g in sums),
        in_specs=[any_spec] * n,
        out_specs=tuple([any_spec] * n),
        scratch_shapes=[pltpu.SemaphoreType.DMA((n, 3)), pltpu.SemaphoreType.DMA((n, 3))],
        name=name,
    )(*sums)


def _add_pair(g, got, axis, core_arr, *, name):
    r, c = got.shape
    if axis == 0:
        rows, cols = r // 4, c
    else:
        rows, cols = r, c // 4
    tr = _tile(rows, 512)
    nr = rows // tr

    def body(core_ref, g_ref, got_ref, o_ref):
        o_ref[...] = (g_ref[...].astype(F32) + got_ref[...].astype(F32)).astype(o_ref.dtype)

    if axis == 0:
        g_spec = pl.BlockSpec((tr, cols), lambda k, i, core: ((2 * k + core[0]) * nr + i, 0))
        spec = pl.BlockSpec((tr, cols), lambda k, i, core: (k * nr + i, 0))
    else:
        g_spec = pl.BlockSpec((tr, cols), lambda k, i, core: (i, 2 * k + core[0]))
        spec = pl.BlockSpec((tr, cols), lambda k, i, core: (i, k))
    return pl.pallas_call(
        body,
        out_shape=jax.ShapeDtypeStruct(got.shape, got.dtype),
        grid_spec=pltpu.PrefetchScalarGridSpec(num_scalar_prefetch=1, grid=(4, nr), in_specs=[g_spec, spec], out_specs=spec),
        compiler_params=_cp("parallel", "parallel"),
        name=name,
    )(core_arr, g, got)


def _adam_math(w, g, m, v):
    m = ADAM_B1 * m + (1.0 - ADAM_B1) * g
    v = ADAM_B2 * v + (1.0 - ADAM_B2) * (g * g)
    m_hat = m / (1.0 - ADAM_B1 ** ADAM_STEP)
    v_hat = v / (1.0 - ADAM_B2 ** ADAM_STEP)
    delta = -ADAM_LR * (m_hat / (jnp.sqrt(v_hat) + ADAM_EPS) + ADAM_WD * w)
    return delta, m, v


def _adamw(w, m, v, parts, axis, chip_arr, *, name):
    n_l, r, c = w.shape
    tr = _tile(r, 128)
    nr = r // tr
    n_chip = 4

    def body(*refs):
        w_ref, m_ref, v_ref = refs[1:4]
        p_refs = refs[4:4 + n_chip * n_l]
        g_ref, d_ref, nm_ref, nv_ref = refs[4 + n_chip * n_l:]
        for l in range(n_l):
            @pl.when(pl.program_id(0) == l)
            def _(l=l):
                g = p_refs[n_chip * l][...].astype(F32)
                for k in range(1, n_chip):
                    g = g + p_refs[n_chip * l + k][...].astype(F32)
                delta, nm, nv = _adam_math(w_ref[...], g, m_ref[...], v_ref[...])
                g_ref[...] = g
                d_ref[...] = delta
                nm_ref[...] = nm
                nv_ref[...] = nv

    def part_spec(ll, k):
        if axis == 0:
            return pl.BlockSpec((tr, c), lambda l, i, chips: (chips[k] * nr + jnp.where(l == ll, i, 0), 0))
        return pl.BlockSpec((tr, c), lambda l, i, chips: (jnp.where(l == ll, i, 0), chips[k]))

    wspec = pl.BlockSpec((None, tr, c), lambda l, i, chips: (l, i, 0))
    pspecs = [part_spec(ll, k) for ll in range(n_l) for k in range(n_chip)]
    sds = jax.ShapeDtypeStruct(w.shape, F32)
    return pl.pallas_call(
        body,
        out_shape=(sds, sds, sds, sds),
        grid_spec=pltpu.PrefetchScalarGridSpec(
            num_scalar_prefetch=1, grid=(n_l, nr), in_specs=[wspec, wspec, wspec] + pspecs, out_specs=(wspec, wspec, wspec, wspec)),
        compiler_params=_cp("arbitrary", "arbitrary"),
        name=name,
    )(chip_arr, w, m, v, *[parts[ll][0 if k == 0 else 1] for ll in range(n_l) for k in range(n_chip)])


def _adamw_small(w, m, v, parts, *, name):
    r, c = w.shape
    tr = _tile(r, 512)

    def body(w_ref, m_ref, v_ref, p_ref, g_ref, d_ref, nm_ref, nv_ref):
        g = p_ref[0]
        for s in range(1, N_DEV):
            g = g + p_ref[s]
        delta, nm, nv = _adam_math(w_ref[...], g, m_ref[...], v_ref[...])
        g_ref[...] = g
        d_ref[...] = delta
        nm_ref[...] = nm
        nv_ref[...] = nv

    spec = pl.BlockSpec((tr, c), lambda i: (i, 0))
    sds = jax.ShapeDtypeStruct(w.shape, F32)
    return pl.pallas_call(
        body,
        out_shape=(sds, sds, sds, sds),
        grid=(r // tr,),
        in_specs=[spec, spec, spec, pl.BlockSpec((N_DEV, tr, c), lambda i: (0, i, 0))],
        out_specs=(spec, spec, spec, spec),
        compiler_params=_cp("parallel"),
        name=name,
    )(w, m, v, parts)


def _pack(arrs):
    flat = jnp.concatenate([a.reshape(-1).astype(F32) for a in arrs])
    pad = (-flat.shape[0]) % (8 * LANES)
    return jnp.pad(flat, (0, pad)).reshape(-1, LANES)


def _unpack(packed, shapes):
    flat = packed.reshape(-1)
    out, off = [], 0
    for sh in shapes:
        n = math.prod(sh)
        out.append(flat[off:off + n].reshape(sh))
        off += n
    return out


def _unpack_gathered(packed, shapes, axis):
    flat = packed.reshape(N_DEV, -1)
    out, off = [], 0
    for sh, ax in zip(shapes, axis):
        n = math.prod(sh)
        blocks = flat[:, off:off + n].reshape((N_DEV, *sh))
        out.append(jnp.concatenate([blocks[d] for d in range(N_DEV)], axis=ax))
        off += n
    return out


def _my_block(full, axis, size):
    idx = _dev_index(*_me())
    return lax.dynamic_slice_in_dim(full, idx * size, size, axis)


def kernel(x, mem, positions, norm_mix_g, norm_mem_q_g, norm_mem_kv_g, norm_ffn_g, final_norm_g, ab_w_in, a_v_norm_g, a_w_s, a_b_s, b_conv_w, ab_w_out, c_w_in, c_q_norm_g, c_kv_norm_g, c_w_uq, c_w_ukv, c_w_out, m_wq, m_wk, m_wv, m_wo, f_w1, f_w2, loss_target, m_norm_mix_g, m_norm_mem_q_g, m_norm_mem_kv_g, m_norm_ffn_g, m_final_norm_g, m_ab_w_in, m_a_v_norm_g, m_a_w_s, m_a_b_s, m_b_conv_w, m_ab_w_out, m_c_w_in, m_c_q_norm_g, m_c_kv_norm_g, m_c_w_uq, m_c_w_ukv, m_c_w_out, m_m_wq, m_m_wk, m_m_wv, m_m_wo, m_f_w1, m_f_w2, v_norm_mix_g, v_norm_mem_q_g, v_norm_mem_kv_g, v_norm_ffn_g, v_final_norm_g, v_ab_w_in, v_a_v_norm_g, v_a_w_s, v_a_b_s, v_b_conv_w, v_ab_w_out, v_c_w_in, v_c_q_norm_g, v_c_kv_norm_g, v_c_w_uq, v_c_w_ukv, v_c_w_out, v_m_wq, v_m_wk, v_m_wv, v_m_wo, v_f_w1, v_f_w2):
    bl, seq, d = x.shape
    t = bl * seq
    n_mem = mem.shape[1]
    depth = norm_mix_g.shape[0]
    n_even, n_odd = ab_w_in.shape[0], c_w_in.shape[0]
    a_width = a_v_norm_g.shape[1]
    groups = a_w_s.shape[1]
    heads = d // C_NOPE
    q_rank = c_q_norm_g.shape[1] * N_DEV
    kv_rank = c_kv_norm_g.shape[1] * N_DEV

    big = {
        "ab_w_in": (ab_w_in, m_ab_w_in, v_ab_w_in, 1), "ab_w_out": (ab_w_out, m_ab_w_out, v_ab_w_out, 0),
        "c_w_in": (c_w_in, m_c_w_in, v_c_w_in, 0), "c_w_uq": (c_w_uq, m_c_w_uq, v_c_w_uq, 1),
        "c_w_ukv": (c_w_ukv, m_c_w_ukv, v_c_w_ukv, 1), "c_w_out": (c_w_out, m_c_w_out, v_c_w_out, 0),
        "m_wq": (m_wq, m_m_wq, v_m_wq, 0), "m_wk": (m_wk, m_m_wk, v_m_wk, 0), "m_wv": (m_wv, m_m_wv, v_m_wv, 0),
        "m_wo": (m_wo, m_m_wo, v_m_wo, 0), "f_w1": (f_w1, m_f_w1, v_f_w1, 1), "f_w2": (f_w2, m_f_w2, v_f_w2, 0),
    }
    even_names = ["ab_w_in", "ab_w_out"]
    odd_names = ["c_w_in", "c_w_uq", "c_w_ukv", "c_w_out"]
    all_names = ["m_wq", "m_wk", "m_wv", "m_wo", "f_w1", "f_w2"]

    def layer_names(layer):
        return (even_names if layer % 2 == 0 else odd_names) + all_names

    def sub(name, layer):
        return layer // 2 if name in even_names or name in odd_names else layer

    split_small = [(c_q_norm_g, m_c_q_norm_g, v_c_q_norm_g, 1), (c_kv_norm_g, m_c_kv_norm_g, v_c_kv_norm_g, 1), (b_conv_w, m_b_conv_w, v_b_conv_w, 2)]
    shard_shapes = [s[0].shape for s in split_small] * 3
    shard_axes = [s[3] for s in split_small] * 3
    gathered = _all_gather_small(_pack([s[0] for s in split_small] + [s[1] for s in split_small] + [s[2] for s in split_small]))
    whole = _unpack_gathered(gathered, shard_shapes, shard_axes)
    q_norm_full, kv_norm_full, conv_full = whole[0:3]
    small_w = [norm_mix_g, norm_mem_q_g, norm_mem_kv_g, norm_ffn_g, final_norm_g, a_v_norm_g, a_w_s, a_b_s] + whole[0:3]
    small_m = [m_norm_mix_g, m_norm_mem_q_g, m_norm_mem_kv_g, m_norm_ffn_g, m_final_norm_g, m_a_v_norm_g, m_a_w_s, m_a_b_s] + whole[3:6]
    small_v = [v_norm_mix_g, v_norm_mem_q_g, v_norm_mem_kv_g, v_norm_ffn_g, v_final_norm_g, v_a_v_norm_g, v_a_w_s, v_a_b_s] + whole[6:9]

    inv = ROPE_THETA ** (-jnp.arange(0, C_ROPE, 2, dtype=F32) / C_ROPE)
    ang = positions.astype(F32).reshape(t, 1) * inv
    cos, sin, zero = jnp.cos(ang), jnp.sin(ang), jnp.zeros((t, C_ROPE // 2), F32)
    rc = jnp.concatenate([cos, cos, zero, zero], axis=1)
    rs1 = jnp.concatenate([-sin, zero, zero, zero], axis=1)
    rs2 = jnp.concatenate([zero, sin, zero, zero], axis=1)

    my_x, my_y, my_c = _me()
    my_chip = 2 * my_x + my_y
    dev_arr = jnp.stack([_dev_index(my_x, my_y, my_c)]).astype(jnp.int32)
    core_arr = jnp.stack([my_c]).astype(jnp.int32)
    chip_arr = jnp.stack([my_chip, my_chip ^ 1, my_chip ^ 2, my_chip ^ 3]).astype(jnp.int32)

    wts = []
    for layer in range(depth):
        names = layer_names(layer)
        placed = [_cast_place(big[nm][0][sub(nm, layer)], big[nm][3], dev_arr, name="cast_place") for nm in names]
        full = _all_gather_weights(placed, [big[nm][3] for nm in names], name=f"all_gather_l{layer}")
        wts.append(dict(zip(names, full)))

    def group_cols(w_in):
        return w_in.reshape(d, 5, groups, LANES).transpose(0, 2, 1, 3).reshape(d, 5 * a_width)

    def ungroup_cols(g_in):
        return g_in.reshape(d, groups, 5, LANES).transpose(0, 2, 1, 3).reshape(d, 5 * a_width)

    def group_rows(w_out):
        return w_out.reshape(2, groups, LANES, d).transpose(1, 0, 2, 3).reshape(2 * a_width, d)

    def ungroup_rows(g_out):
        return g_out.reshape(groups, 2, LANES, d).transpose(1, 0, 2, 3).reshape(2 * a_width, d)

    def pad_heads(w_uq):
        w3 = w_uq.reshape(q_rank, heads, C_NOPE + C_ROPE)
        return jnp.pad(w3, ((0, 0), (0, 0), (0, LANES - C_ROPE))).reshape(q_rank, heads * 2 * LANES)

    def unpad_heads(g_uq):
        return g_uq.reshape(q_rank, heads, 2 * LANES)[:, :, :C_NOPE + C_ROPE].reshape(q_rank, heads * (C_NOPE + C_ROPE))

    x2 = x.reshape(t, d)
    mem2 = mem.reshape(bl * n_mem, d)
    saved = []

    for layer in range(depth):
        w = wts[layer]
        sv = {"x0": x2}
        xn = _rmsnorm_fwd(x2, norm_mix_g[layer], name="norm_mix")
        sv["xn"] = xn
        if layer % 2 == 0:
            e = layer // 2
            w_in_g = group_cols(w["ab_w_in"])
            w_out_g = group_rows(w["ab_w_out"])
            z = _mm(xn, w_in_g, out_dtypes=(F32,), name="ab_in")
            ws_e = a_w_s[e]
            bs_e = a_b_s[e].reshape(groups, GMLP_BLOCK, 1)
            gn_e = a_v_norm_g[e].reshape(groups, 1, LANES)
            cw_e = conv_full[e].reshape(3, groups, LANES).transpose(1, 0, 2)
            ycat = _gmlp_conv_fwd(z, ws_e, bs_e, gn_e, cw_e, seq=seq, name="gmlp_conv_fwd")
            x2 = _mm(ycat, w_out_g, out_dtypes=(F32,), extras=[(x2, "mn")], epilogue=_epi_add, name="ab_out")
            sv.update(z=z, ycat=ycat, w_in_g=w_in_g, w_out_g=w_out_g, ws=ws_e, bs=bs_e, gn=gn_e, cw=cw_e)
        else:
            o = layer // 2
            w_uq_p = pad_heads(w["c_w_uq"])
            z = _mm(xn, w["c_w_in"], out_dtypes=(F32,), name="c_in")
            qn = _rmsnorm_fwd(z, q_norm_full[o], width=q_rank, col=0, name="norm_cq")
            kvn = _rmsnorm_fwd(z, kv_norm_full[o], width=kv_rank, col=q_rank // kv_rank, name="norm_ckv")
            qh = _mm(qn, w_uq_p, extras=[(rc, "m"), (rs1, "m"), (rs2, "m")], epilogue=_epi_rope_heads, tn=1024, name="c_uq")
            kvh = _mm(kvn, w["c_w_ukv"], name="c_ukv")
            k_rope = jnp.pad(z[:, q_rank + kv_rank:], ((0, 0), (0, LANES - C_ROPE)))
            kr = _rope_rows(k_rope, rc, rs1, rs2, transpose=False, out_dtype=BF16, name="rope_k")
            att, lse = _mla_fwd(qh, kvh, kr, seq=seq, heads=heads, name="mla_fwd")
            x2 = _mm(att, w["c_w_out"], out_dtypes=(F32,), extras=[(x2, "mn")], epilogue=_epi_add, name="c_out")
            sv.update(z=z, qn=qn, kvn=kvn, qh=qh, kvh=kvh, kr=kr, att=att, lse=lse, w_uq_p=w_uq_p)
        sv["x1"] = x2
        xq = _rmsnorm_fwd(x2, norm_mem_q_g[layer], name="norm_mem_q")
        memn = _rmsnorm_fwd(mem2, norm_mem_kv_g[layer], name="norm_mem_kv")
        mq = _mm(xq, w["m_wq"], name="m_q")
        mk = _mm(memn, w["m_wk"], name="m_k")
        mv = _mm(memn, w["m_wv"], name="m_v")
        mo = _mem_fwd(mq, mk, mv, seq=seq, n_mem=n_mem, name="mem_fwd")
        x2 = _mm(mo, w["m_wo"], out_dtypes=(F32,), extras=[(x2, "mn")], epilogue=_epi_add, name="m_o")
        sv.update(xq=xq, memn=memn, mq=mq, mk=mk, mv=mv, mo=mo)
        sv["x2"] = x2
        xf = _rmsnorm_fwd(x2, norm_ffn_g[layer], name="norm_ffn")
        hpre, hact = _mm(xf, w["f_w1"], out_dtypes=(BF16, BF16), epilogue=_epi_relu2, name="f_1")
        x2 = _mm(hact, w["f_w2"], out_dtypes=(F32,), extras=[(x2, "mn")], epilogue=_epi_add, name="f_2")
        sv.update(xf=xf, hpre=hpre, hact=hact)
        saved.append(sv)

    loss_part, dx, dg_final = _final_loss(x2, final_norm_g, loss_target.reshape(t, d), name="final_loss")
    loss = lax.psum(loss_part[0, 0], ("x", "y", "c"))

    dg_mix, dg_mq, dg_mkv, dg_ffn = [None] * depth, [None] * depth, [None] * depth, [None] * depth
    d_av, d_ws, d_bs = [None] * n_even, [None] * n_even, [None] * n_even
    d_conv, d_qn, d_kvn = [None] * n_even, [None] * n_odd, [None] * n_odd
    parts = {nm: [None] * big[nm][0].shape[0] for nm in big}

    for layer in reversed(range(depth)):
        w, sv = wts[layer], saved[layer]
        grads = {}
        dh = _mm(dx, w["f_w2"], tb=True, extras=[(sv["hpre"], "mn")], epilogue=_epi_drelu2, name="f_2_dx")
        grads["f_w2"] = _mm(sv["hact"], dx, ta=True, name="f_2_dw")
        grads["f_w1"] = _mm(sv["xf"], dh, ta=True, name="f_1_dw")
        dxf = _mm(dh, w["f_w1"], tb=True, name="f_1_dx")
        dx, dg_ffn[layer] = _rmsnorm_bwd(sv["x2"], norm_ffn_g[layer], dxf, dx, name="norm_ffn_bwd")
        dmo = _mm(dx, w["m_wo"], tb=True, name="m_o_dx")
        grads["m_wo"] = _mm(sv["mo"], dx, ta=True, name="m_o_dw")
        dmq, dmk, dmv = _mem_bwd(sv["mq"], sv["mk"], sv["mv"], dmo, seq=seq, n_mem=n_mem, name="mem_bwd")
        grads["m_wq"] = _mm(sv["xq"], dmq, ta=True, name="m_q_dw")
        grads["m_wk"] = _mm(sv["memn"], dmk, ta=True, name="m_k_dw")
        grads["m_wv"] = _mm(sv["memn"], dmv, ta=True, name="m_v_dw")
        dxq = _mm(dmq, w["m_wq"], tb=True, name="m_q_dx")
        dmemn_k = _mm(dmk, w["m_wk"], tb=True, out_dtypes=(F32,), name="m_k_dx")
        dmemn = _mm(dmv, w["m_wv"], tb=True, out_dtypes=(F32,), extras=[(dmemn_k, "mn")], epilogue=_epi_add, name="m_v_dx")
        _, dg_mkv[layer] = _rmsnorm_bwd(mem2, norm_mem_kv_g[layer], dmemn, name="norm_mem_kv_bwd")
        dx, dg_mq[layer] = _rmsnorm_bwd(sv["x1"], norm_mem_q_g[layer], dxq, dx, name="norm_mem_q_bwd")
        if layer % 2 == 0:
            e = layer // 2
            dycat = _mm(dx, sv["w_out_g"], tb=True, name="ab_out_dx")
            grads["ab_w_out"] = ungroup_rows(_mm(sv["ycat"], dx, ta=True, name="ab_out_dw"))
            dz, dws, dbs, dgn, dcw = _gmlp_conv_bwd(sv["z"], dycat, sv["ws"], sv["bs"], sv["gn"], sv["cw"], seq=seq, name="gmlp_conv_bwd")
            d_ws[e], d_bs[e], d_av[e] = dws, dbs.reshape(groups, GMLP_BLOCK), dgn.reshape(a_width)
            d_conv[e] = dcw.transpose(1, 0, 2).reshape(3, a_width)
            grads["ab_w_in"] = ungroup_cols(_mm(sv["xn"], dz, ta=True, name="ab_in_dw"))
            dxn = _mm(dz, sv["w_in_g"], tb=True, name="ab_in_dx")
        else:
            o = layer // 2
            datt = _mm(dx, w["c_w_out"], tb=True, name="c_out_dx")
            grads["c_w_out"] = _mm(sv["att"], dx, ta=True, name="c_out_dw")
            dqh, dkvh, dkr = _mla_bwd(sv["qh"], sv["kvh"], sv["kr"], datt, sv["att"], sv["lse"], seq=seq, heads=heads, name="mla_bwd")
            dq_pre = _rope_rows(dqh, rc, rs1, rs2, transpose=True, out_dtype=BF16, name="rope_q_bwd")
            dk_rope = _rope_rows(dkr, rc, rs1, rs2, transpose=True, out_dtype=F32, name="rope_k_bwd")
            grads["c_w_uq"] = unpad_heads(_mm(sv["qn"], dq_pre, ta=True, name="c_uq_dw"))
            grads["c_w_ukv"] = _mm(sv["kvn"], dkvh, ta=True, name="c_ukv_dw")
            dqn = _mm(dq_pre, sv["w_uq_p"], tb=True, name="c_uq_dx")
            dkvn = _mm(dkvh, w["c_w_ukv"], tb=True, name="c_ukv_dx")
            dcq, d_qn[o] = _rmsnorm_bwd(sv["z"], q_norm_full[o], dqn, width=q_rank, col=0, name="norm_cq_bwd")
            dckv, d_kvn[o] = _rmsnorm_bwd(sv["z"], kv_norm_full[o], dkvn, width=kv_rank, col=q_rank // kv_rank, name="norm_ckv_bwd")
            dz = jnp.concatenate([dcq, dckv, dk_rope[:, :C_ROPE]], axis=1)
            grads["c_w_in"] = _mm(sv["xn"], dz, ta=True, name="c_in_dw")
            dxn = _mm(dz, w["c_w_in"], tb=True, name="c_in_dx")
        dx, dg_mix[layer] = _rmsnorm_bwd(sv["x0"], norm_mix_g[layer], dxn, dx, name="norm_mix_bwd")
        names = layer_names(layer)
        axes = [big[nm][3] for nm in names]
        got = _scatter_d2d([grads[nm] for nm in names], axes, name=f"scatter_d2d_l{layer}")
        sums = [_add_pair(grads[nm], g, ax, core_arr, name="add_pair") for nm, g, ax in zip(names, got, axes)]
        recv = _scatter_ici(sums, axes, name=f"scatter_ici_l{layer}")
        for nm, s_, r in zip(names, sums, recv):
            parts[nm][sub(nm, layer)] = (s_, r)

    grad_x = dx.reshape(bl, seq, d)

    big_out = {}
    for nm, (wv, mv_, vv, ax) in big.items():
        big_out[nm] = _adamw(wv, mv_, vv, parts[nm], ax, chip_arr, name=f"adamw_{nm}")

    small_g = [jnp.concatenate(dg_mix), jnp.concatenate(dg_mq), jnp.concatenate(dg_mkv), jnp.concatenate(dg_ffn), dg_final.reshape(d),
               jnp.stack(d_av), jnp.stack(d_ws), jnp.stack(d_bs), jnp.concatenate(d_qn), jnp.concatenate(d_kvn), jnp.stack(d_conv)]
    small_shapes = [a.shape for a in small_w]
    g_all = _all_gather_small(_pack(small_g))
    sg, sd, sm, svv = _adamw_small(_pack(small_w), _pack(small_m), _pack(small_v), g_all, name="adamw_small")
    small_names = ["norm_mix_g", "norm_mem_q_g", "norm_mem_kv_g", "norm_ffn_g", "final_norm_g", "a_v_norm_g", "a_w_s", "a_b_s", "c_q_norm_g", "c_kv_norm_g", "b_conv_w"]
    small_out = {}
    unpacked = [_unpack(p, small_shapes) for p in (sg, sd, sm, svv)]
    for i, nm in enumerate(small_names):
        vals = [u[i] for u in unpacked]
        if i >= 8:
            full_w, _, _, ax = split_small[i - 8]
            vals = [_my_block(val, ax, full_w.shape[ax]) for val in vals]
        small_out[nm] = vals

    order = ["norm_mix_g", "norm_mem_q_g", "norm_mem_kv_g", "norm_ffn_g", "final_norm_g", "ab_w_in", "a_v_norm_g", "a_w_s", "a_b_s", "b_conv_w", "ab_w_out",
             "c_w_in", "c_q_norm_g", "c_kv_norm_g", "c_w_uq", "c_w_ukv", "c_w_out", "m_wq", "m_wk", "m_wv", "m_wo", "f_w1", "f_w2"]
    res = [loss, grad_x]
    for kind in range(4):
        for nm in order:
            res.append(big_out[nm][kind] if nm in big_out else small_out[nm][kind])
    return tuple(res)
```

```python
import functools
import math

import jax
import jax.numpy as jnp
from jax import lax
from jax.experimental import pallas as pl
from jax.experimental.pallas import tpu as pltpu

F32 = jnp.float32
BF16 = jnp.bfloat16
MESH = pl.DeviceIdType.MESH
N_DEV = 8
LANES = 128

RMS_EPS = 1e-6
CHUNK = 64
GMLP_BLOCK = 128
C_NOPE = 128
C_ROPE = 64
C_V = 128
ROPE_THETA = 10000.0
MEM_HEADS = 4
ADAM_LR = 0.001
ADAM_B1 = 0.9
ADAM_B2 = 0.999
ADAM_EPS = 1e-08
ADAM_WD = 0.01
ADAM_STEP = 10
NEG_BIG = -1e30
VMEM_LIMIT = 56 * 1024 * 1024


def _cp(*sem):
    return pltpu.CompilerParams(dimension_semantics=sem, vmem_limit_bytes=VMEM_LIMIT)


def _tile(n, pref):
    t = min(n, pref)
    while n % t and t > LANES:
        t //= 2
    return n if n % t else t


def _mm(a, b, *, ta=False, tb=False, out_dtypes=(BF16,), extras=(), epilogue=None, tm=1024, tn=1024, tk=2048, name):
    m, k = (a.shape[1], a.shape[0]) if ta else a.shape
    n = b.shape[0] if tb else b.shape[1]
    assert (b.shape[1] if tb else b.shape[0]) == k, (a.shape, b.shape, ta, tb)
    tm, tn, tk = _tile(m, tm), _tile(n, tn), _tile(k, tk)
    nk = k // tk
    ne, no = len(extras), len(out_dtypes)
    dims = (((0 if ta else 1,), (1 if tb else 0,)), ((), ()))

    def body(*refs):
        a_ref, b_ref = refs[0], refs[1]
        e_refs = refs[2:2 + ne]
        o_refs = refs[2 + ne:2 + ne + no]

        def finish(acc):
            vals = [e[...] for e, (_, kind) in zip(e_refs, extras) if kind != "after"]
            outs = (acc,) if epilogue is None else epilogue(acc, *vals)
            for o, val in zip(o_refs, outs):
                o[...] = val.astype(o.dtype)

        prod = lax.dot_general(a_ref[...].astype(BF16), b_ref[...].astype(BF16), dims, preferred_element_type=F32)
        if nk == 1:
            finish(prod)
        else:
            acc_ref = refs[-1]
            kk = pl.program_id(2)

            @pl.when(kk == 0)
            def _():
                acc_ref[...] = prod

            @pl.when((kk > 0) & (kk < nk - 1))
            def _():
                acc_ref[...] += prod

            @pl.when(kk == nk - 1)
            def _():
                finish(acc_ref[...] + prod)

    a_spec = pl.BlockSpec((tk, tm), lambda i, j, kk: (kk, i)) if ta else pl.BlockSpec((tm, tk), lambda i, j, kk: (i, kk))
    b_spec = pl.BlockSpec((tn, tk), lambda i, j, kk: (j, kk)) if tb else pl.BlockSpec((tk, tn), lambda i, j, kk: (kk, j))
    e_specs = []
    for arr, kind in extras:
        if kind == "mn":
            e_specs.append(pl.BlockSpec((tm, tn), lambda i, j, kk: (i, j)))
        elif kind == "after":
            e_specs.append(pl.BlockSpec(arr.shape, lambda i, j, kk: (0, 0)))
        else:
            e_specs.append(pl.BlockSpec((tm, arr.shape[1]), lambda i, j, kk: (i, 0)))
    outs = pl.pallas_call(
        body,
        out_shape=tuple(jax.ShapeDtypeStruct((m, n), d) for d in out_dtypes),
        grid=(m // tm, n // tn, nk),
        in_specs=[a_spec, b_spec] + e_specs,
        out_specs=tuple(pl.BlockSpec((tm, tn), lambda i, j, kk: (i, j)) for _ in out_dtypes),
        scratch_shapes=[pltpu.VMEM((tm, tn), F32)] if nk > 1 else [],
        compiler_params=_cp("parallel", "parallel", "arbitrary"),
        name=name,
    )(a, b, *[arr for arr, _ in extras])
    return outs[0] if no == 1 else outs


def _epi_add(acc, r):
    return (acc + r,)


def _epi_relu2(acc):
    h = jnp.maximum(acc, 0.0)
    return (acc, h * h)


def _epi_drelu2(acc, h):
    return (acc * (2.0 * jnp.maximum(h.astype(F32), 0.0)),)


def _rope(x, c, s1, s2):
    return x * c + pltpu.roll(x, LANES - C_ROPE // 2, 1) * s1 + pltpu.roll(x, C_ROPE // 2, 1) * s2


def _rope_t(dy, c, s1, s2):
    return dy * c + pltpu.roll(dy * s1, C_ROPE // 2, 1) + pltpu.roll(dy * s2, LANES - C_ROPE // 2, 1)


def _epi_rope_heads(acc, c, s1, s2):
    parts = []
    for h0 in range(0, acc.shape[1], 2 * LANES):
        parts.append(acc[:, h0:h0 + LANES])
        parts.append(_rope(acc[:, h0 + LANES:h0 + 2 * LANES], c, s1, s2))
    return (jnp.concatenate(parts, axis=1),)


def _rmsnorm_fwd(x, g, *, width=None, col=0, name):
    t = x.shape[0]
    w = x.shape[1] if width is None else width
    tm = _tile(t, 512)

    def body(x_ref, g_ref, o_ref):
        xv = x_ref[...].astype(F32)
        r = lax.rsqrt(jnp.mean(xv * xv, axis=-1, keepdims=True) + RMS_EPS)
        o_ref[...] = (xv * r * g_ref[...]).astype(o_ref.dtype)

    return pl.pallas_call(
        body,
        out_shape=jax.ShapeDtypeStruct((t, w), BF16),
        grid=(t // tm,),
        in_specs=[pl.BlockSpec((tm, w), lambda i: (i, col)), pl.BlockSpec((1, w), lambda i: (0, 0))],
        out_specs=pl.BlockSpec((tm, w), lambda i: (i, 0)),
        compiler_params=_cp("parallel"),
        name=name,
    )(x, g.reshape(1, w))


def _rmsnorm_bwd(x, g, dy, res=None, *, width=None, col=0, name):
    t = x.shape[0]
    w = x.shape[1] if width is None else width
    tm = _tile(t, 256)
    has_res = res is not None

    def body(*refs):
        x_ref, g_ref, dy_ref = refs[:3]
        res_ref = refs[3] if has_res else None
        dx_ref, dg_ref = refs[-2], refs[-1]
        xv = x_ref[...].astype(F32)
        r = lax.rsqrt(jnp.mean(xv * xv, axis=-1, keepdims=True) + RMS_EPS)
        xh = xv * r
        dyv = dy_ref[...].astype(F32)
        dxh = dyv * g_ref[...]
        dx = r * (dxh - xh * jnp.mean(dxh * xh, axis=-1, keepdims=True))
        if has_res:
            dx = dx + res_ref[...]
        dx_ref[...] = dx

        @pl.when(pl.program_id(0) == 0)
        def _():
            dg_ref[...] = jnp.zeros_like(dg_ref)

        dg_ref[...] += jnp.sum(dyv * xh, axis=0, keepdims=True)

    in_specs = [pl.BlockSpec((tm, w), lambda i: (i, col)), pl.BlockSpec((1, w), lambda i: (0, 0)), pl.BlockSpec((tm, w), lambda i: (i, 0))]
    args = [x, g.reshape(1, w), dy]
    if has_res:
        in_specs.append(pl.BlockSpec((tm, w), lambda i: (i, 0)))
        args.append(res)
    return pl.pallas_call(
        body,
        out_shape=(jax.ShapeDtypeStruct((t, w), F32), jax.ShapeDtypeStruct((1, w), F32)),
        grid=(t // tm,),
        in_specs=in_specs,
        out_specs=(pl.BlockSpec((tm, w), lambda i: (i, 0)), pl.BlockSpec((1, w), lambda i: (0, 0))),
        compiler_params=_cp("arbitrary"),
        name=name,
    )(*args)


def _final_loss(x, g, target, *, name):
    t, d = x.shape
    tm = _tile(t, 256)

    def body(x_ref, g_ref, t_ref, loss_ref, dx_ref, dg_ref):
        xv = x_ref[...]
        r = lax.rsqrt(jnp.mean(xv * xv, axis=-1, keepdims=True) + RMS_EPS)
        xh = xv * r
        gv = g_ref[...]
        err = xh * gv - t_ref[...]
        dyv = err * (1.0 / d)
        dxh = dyv * gv
        dx_ref[...] = r * (dxh - xh * jnp.mean(dxh * xh, axis=-1, keepdims=True))

        @pl.when(pl.program_id(0) == 0)
        def _():
            dg_ref[...] = jnp.zeros_like(dg_ref)
            loss_ref[...] = jnp.zeros_like(loss_ref)

        dg_ref[...] += jnp.sum(dyv * xh, axis=0, keepdims=True)
        row = jnp.sum(err * err, axis=-1, keepdims=True) * (0.5 / d)
        loss_ref[...] += jnp.sum(row, axis=0, keepdims=True)

    return pl.pallas_call(
        body,
        out_shape=(jax.ShapeDtypeStruct((1, 1), F32), jax.ShapeDtypeStruct((t, d), F32), jax.ShapeDtypeStruct((1, d), F32)),
        grid=(t // tm,),
        in_specs=[pl.BlockSpec((tm, d), lambda i: (i, 0)), pl.BlockSpec((1, d), lambda i: (0, 0)), pl.BlockSpec((tm, d), lambda i: (i, 0))],
        out_specs=(pl.BlockSpec((1, 1), lambda i: (0, 0)), pl.BlockSpec((tm, d), lambda i: (i, 0)), pl.BlockSpec((1, d), lambda i: (0, 0))),
        compiler_params=_cp("arbitrary"),
        name=name,
    )(x, g.reshape(1, d), target)


_GELU_C = math.sqrt(2.0 / math.pi)
_GELU_A = 0.044715


def _gelu(x):
    t = jnp.tanh(_GELU_C * (x + _GELU_A * x * x * x))
    return 0.5 * x * (1.0 + t), t


def _dgelu(x, t):
    return 0.5 * (1.0 + t) + 0.5 * x * (1.0 - t * t) * (_GELU_C * (1.0 + 3.0 * _GELU_A * x * x))


def _spatial_mask():
    ci = lax.broadcasted_iota(jnp.int32, (GMLP_BLOCK, GMLP_BLOCK), 0) // CHUNK
    cj = lax.broadcasted_iota(jnp.int32, (GMLP_BLOCK, GMLP_BLOCK), 1) // CHUNK
    return (cj <= ci).astype(F32)


def _shift_down(x, k):
    rows = lax.broadcasted_iota(jnp.int32, x.shape, 0)
    return jnp.where(rows >= k, pltpu.roll(x, k, 0), 0.0)


def _shift_up(x, k):
    n = x.shape[0]
    rows = lax.broadcasted_iota(jnp.int32, x.shape, 0)
    return jnp.where(rows < n - k, pltpu.roll(x, n - k, 0), 0.0)


def _gmlp_conv_fwd(z, ws, bs, gn, cw, *, seq, name):
    t = z.shape[0]
    g_n = ws.shape[0]
    nb = seq // GMLP_BLOCK
    L = LANES

    def body(z_ref, ws_ref, bs_ref, gn_ref, cw_ref, y_ref):
        u = z_ref[:, 0:L]
        v = z_ref[:, L:2 * L]
        gu, _ = _gelu(u)
        gv, _ = _gelu(v)
        r = lax.rsqrt(jnp.mean(gv * gv, axis=-1, keepdims=True) + RMS_EPS)
        vg = (gv * r * gn_ref[0]).astype(BF16)
        wm = (ws_ref[0] * _spatial_mask()).astype(BF16)
        bias = bs_ref[0]
        for n in range(nb):
            sl = slice(n * GMLP_BLOCK, (n + 1) * GMLP_BLOCK)
            mixed = jnp.dot(wm, vg[sl], preferred_element_type=F32) + bias
            y_ref[sl, 0:L] = (gu[sl] * mixed).astype(y_ref.dtype)
        bg = z_ref[:, 2 * L:3 * L]
        zc = z_ref[:, 3 * L:4 * L] * z_ref[:, 4 * L:5 * L]
        w0, w1, w2 = cw_ref[0, 0:1, :], cw_ref[0, 1:2, :], cw_ref[0, 2:3, :]
        conv = w0 * _shift_down(zc, 2) + w1 * _shift_down(zc, 1) + w2 * zc
        y_ref[:, L:2 * L] = (bg * conv).astype(y_ref.dtype)

    return pl.pallas_call(
        body,
        out_shape=jax.ShapeDtypeStruct((t, 2 * L * g_n), BF16),
        grid=(g_n, t // seq),
        in_specs=[
            pl.BlockSpec((seq, 5 * L), lambda g, b: (b, g)),
            pl.BlockSpec((1, GMLP_BLOCK, GMLP_BLOCK), lambda g, b: (g, 0, 0)),
            pl.BlockSpec((1, GMLP_BLOCK, 1), lambda g, b: (g, 0, 0)),
            pl.BlockSpec((1, 1, L), lambda g, b: (g, 0, 0)),
            pl.BlockSpec((1, 3, L), lambda g, b: (g, 0, 0)),
        ],
        out_specs=pl.BlockSpec((seq, 2 * L), lambda g, b: (b, g)),
        compiler_params=_cp("parallel", "parallel"),
        name=name,
    )(z, ws, bs, gn, cw)


def _gmlp_conv_bwd(z, dy, ws, bs, gn, cw, *, seq, name):
    t = z.shape[0]
    g_n = ws.shape[0]
    nb = seq // GMLP_BLOCK
    L = LANES
    dims_nt = (((1,), (1,)), ((), ()))
    dims_tn = (((0,), (0,)), ((), ()))

    def body(z_ref, dy_ref, ws_ref, bs_ref, gn_ref, cw_ref, dz_ref, dws_ref, dbs_ref, dgn_ref, dcw_ref, dvg_ref):
        @pl.when(pl.program_id(1) == 0)
        def _():
            dws_ref[...] = jnp.zeros_like(dws_ref)
            dbs_ref[...] = jnp.zeros_like(dbs_ref)
            dgn_ref[...] = jnp.zeros_like(dgn_ref)
            dcw_ref[...] = jnp.zeros_like(dcw_ref)

        u = z_ref[:, 0:L]
        v = z_ref[:, L:2 * L]
        gu, tu = _gelu(u)
        gv, tv = _gelu(v)
        r = lax.rsqrt(jnp.mean(gv * gv, axis=-1, keepdims=True) + RMS_EPS)
        gvh = gv * r
        gnv = gn_ref[0]
        vg = (gvh * gnv).astype(BF16)
        mask = _spatial_mask()
        wm = (ws_ref[0] * mask).astype(BF16)
        bias = bs_ref[0]
        dya = dy_ref[:, 0:L].astype(F32)
        dws = jnp.zeros((GMLP_BLOCK, GMLP_BLOCK), F32)
        dbs = jnp.zeros((GMLP_BLOCK, 1), F32)
        for n in range(nb):
            sl = slice(n * GMLP_BLOCK, (n + 1) * GMLP_BLOCK)
            mixed = jnp.dot(wm, vg[sl], preferred_element_type=F32) + bias
            dmixed = dya[sl] * gu[sl]
            dgu = dya[sl] * mixed
            dz_ref[sl, 0:L] = (dgu * _dgelu(u[sl], tu[sl])).astype(dz_ref.dtype)
            dmb = dmixed.astype(BF16)
            dws = dws + lax.dot_general(dmb, vg[sl], dims_nt, preferred_element_type=F32)
            dbs = dbs + jnp.sum(dmixed, axis=1, keepdims=True)
            dvg_ref[sl, :] = lax.dot_general(wm, dmb, dims_tn, preferred_element_type=F32)
        dws_ref[0] += dws * mask
        dbs_ref[0] += dbs
        dvg = dvg_ref[...]
        dgn_ref[0] += jnp.sum(dvg * gvh, axis=0, keepdims=True)
        dvh = dvg * gnv
        dgv = r * (dvh - gvh * jnp.mean(dvh * gvh, axis=-1, keepdims=True))
        dz_ref[:, L:2 * L] = (dgv * _dgelu(v, tv)).astype(dz_ref.dtype)

        bg = z_ref[:, 2 * L:3 * L]
        cg = z_ref[:, 3 * L:4 * L]
        hh = z_ref[:, 4 * L:5 * L]
        zc = cg * hh
        w0, w1, w2 = cw_ref[0, 0:1, :], cw_ref[0, 1:2, :], cw_ref[0, 2:3, :]
        z1 = _shift_down(zc, 1)
        z2 = _shift_down(zc, 2)
        conv = w0 * z2 + w1 * z1 + w2 * zc
        dyb = dy_ref[:, L:2 * L].astype(F32)
        dconv = dyb * bg
        dz_ref[:, 2 * L:3 * L] = (dyb * conv).astype(dz_ref.dtype)
        dzc = w2 * dconv + w1 * _shift_up(dconv, 1) + w0 * _shift_up(dconv, 2)
        dz_ref[:, 3 * L:4 * L] = (dzc * hh).astype(dz_ref.dtype)
        dz_ref[:, 4 * L:5 * L] = (dzc * cg).astype(dz_ref.dtype)
        dcw_ref[0, 0:1, :] += jnp.sum(dconv * z2, axis=0, keepdims=True)
        dcw_ref[0, 1:2, :] += jnp.sum(dconv * z1, axis=0, keepdims=True)
        dcw_ref[0, 2:3, :] += jnp.sum(dconv * zc, axis=0, keepdims=True)

    return pl.pallas_call(
        body,
        out_shape=(
            jax.ShapeDtypeStruct((t, 5 * L * g_n), BF16),
            jax.ShapeDtypeStruct(ws.shape, F32),
            jax.ShapeDtypeStruct(bs.shape, F32),
            jax.ShapeDtypeStruct(gn.shape, F32),
            jax.ShapeDtypeStruct(cw.shape, F32),
        ),
        grid=(g_n, t // seq),
        in_specs=[
            pl.BlockSpec((seq, 5 * L), lambda g, b: (b, g)),
            pl.BlockSpec((seq, 2 * L), lambda g, b: (b, g)),
            pl.BlockSpec((1, GMLP_BLOCK, GMLP_BLOCK), lambda g, b: (g, 0, 0)),
            pl.BlockSpec((1, GMLP_BLOCK, 1), lambda g, b: (g, 0, 0)),
            pl.BlockSpec((1, 1, L), lambda g, b: (g, 0, 0)),
            pl.BlockSpec((1, 3, L), lambda g, b: (g, 0, 0)),
        ],
        out_specs=(
            pl.BlockSpec((seq, 5 * L), lambda g, b: (b, g)),
            pl.BlockSpec((1, GMLP_BLOCK, GMLP_BLOCK), lambda g, b: (g, 0, 0)),
            pl.BlockSpec((1, GMLP_BLOCK, 1), lambda g, b: (g, 0, 0)),
            pl.BlockSpec((1, 1, L), lambda g, b: (g, 0, 0)),
            pl.BlockSpec((1, 3, L), lambda g, b: (g, 0, 0)),
        ),
        scratch_shapes=[pltpu.VMEM((seq, L), F32)],
        compiler_params=_cp("parallel", "arbitrary"),
        name=name,
    )(z, dy, ws, bs, gn, cw)


def _rope_rows(x, c, s1, s2, *, transpose, out_dtype, name):
    t, w = x.shape
    tm = _tile(t, 512)
    only = w == LANES
    fn = _rope_t if transpose else _rope

    def body(x_ref, c_ref, s1_ref, s2_ref, o_ref):
        cv, s1v, s2v = c_ref[...], s1_ref[...], s2_ref[...]
        if only:
            o_ref[...] = fn(x_ref[...].astype(F32), cv, s1v, s2v).astype(o_ref.dtype)
        else:
            for h0 in range(0, w, 2 * LANES):
                o_ref[:, h0:h0 + LANES] = x_ref[:, h0:h0 + LANES].astype(o_ref.dtype)
                o_ref[:, h0 + LANES:h0 + 2 * LANES] = fn(x_ref[:, h0 + LANES:h0 + 2 * LANES].astype(F32), cv, s1v, s2v).astype(o_ref.dtype)

    tab = pl.BlockSpec((tm, LANES), lambda i: (i, 0))
    return pl.pallas_call(
        body,
        out_shape=jax.ShapeDtypeStruct((t, w), out_dtype),
        grid=(t // tm,),
        in_specs=[pl.BlockSpec((tm, w), lambda i: (i, 0)), tab, tab, tab],
        out_specs=pl.BlockSpec((tm, w), lambda i: (i, 0)),
        compiler_params=_cp("parallel"),
        name=name,
    )(x, c, s1, s2)


_NT = (((1,), (1,)), ((), ()))
_TN = (((0,), (0,)), ((), ()))


def _chunk_mask(tq):
    qc = lax.broadcasted_iota(jnp.int32, (tq, tq), 0) // CHUNK
    kc = lax.broadcasted_iota(jnp.int32, (tq, tq), 1) // CHUNK
    return kc <= qc


MLA_BLOCK = 512


def _mla_fwd(q, kv, kr, *, seq, heads, name):
    t = q.shape[0]
    tq = _tile(seq, MLA_BLOCK)
    nq = seq // tq
    scale = (C_NOPE + C_ROPE) ** -0.5

    def body(q_ref, kv_ref, kr_ref, o_ref, lse_ref, kcat_ref):
        i = pl.program_id(2)

        @pl.when(i == 0)
        def _():
            kcat_ref[:, :LANES] = kv_ref[:, :LANES]
            kcat_ref[:, LANES:] = kr_ref[...]

        qv = q_ref[...]

        def step(j, carry, diagonal):
            m_run, l_run, acc = carry
            rows = pl.ds(pl.multiple_of(j * tq, tq), tq)
            s = lax.dot_general(qv, kcat_ref[rows, :], _NT, preferred_element_type=F32) * scale
            if diagonal:
                s = jnp.where(_chunk_mask(tq), s, NEG_BIG)
            m_new = jnp.maximum(m_run, jnp.max(s, axis=-1, keepdims=True))
            alpha = jnp.exp(m_run - m_new)
            p = jnp.exp(s - m_new)
            l_new = alpha * l_run + jnp.sum(p, axis=-1, keepdims=True)
            acc = alpha * acc + jnp.dot(p.astype(BF16), kv_ref[rows, LANES:], preferred_element_type=F32)
            return m_new, l_new, acc

        init = (jnp.full((tq, 1), NEG_BIG, F32), jnp.zeros((tq, 1), F32), jnp.zeros((tq, C_V), F32))
        carry = lax.fori_loop(0, i, lambda j, c: step(j, c, False), init)
        m_run, l_run, acc = step(i, carry, True)
        o_ref[...] = (acc / l_run).astype(o_ref.dtype)
        lse_ref[...] = m_run + jnp.log(l_run)

    return pl.pallas_call(
        body,
        out_shape=(jax.ShapeDtypeStruct((t, heads * C_V), BF16), jax.ShapeDtypeStruct((heads, t, 1), F32)),
        grid=(t // seq, heads, nq),
        in_specs=[
            pl.BlockSpec((tq, 2 * LANES), lambda b, h, i: (b * nq + i, h)),
            pl.BlockSpec((seq, 2 * LANES), lambda b, h, i: (b, h)),
            pl.BlockSpec((seq, LANES), lambda b, h, i: (b, 0)),
        ],
        out_specs=(
            pl.BlockSpec((tq, C_V), lambda b, h, i: (b * nq + i, h)),
            pl.BlockSpec((None, tq, 1), lambda b, h, i: (h, b * nq + i, 0)),
        ),
        scratch_shapes=[pltpu.VMEM((seq, 2 * LANES), BF16)],
        compiler_params=_cp("parallel", "parallel", "arbitrary"),
        name=name,
    )(q, kv, kr)


def _mla_bwd(q, kv, kr, do, o, lse, *, seq, heads, name):
    t = q.shape[0]
    tq = _tile(seq, MLA_BLOCK)
    nq = seq // tq
    scale = (C_NOPE + C_ROPE) ** -0.5

    def body(q_ref, kv_ref, kr_ref, do_ref, o_ref, lse_ref, dq_ref, dkv_ref, dkr_ref, kcat_ref, dk_ref, dv_ref):
        h, i = pl.program_id(1), pl.program_id(2)

        @pl.when(i == 0)
        def _():
            kcat_ref[:, :LANES] = kv_ref[:, :LANES]
            kcat_ref[:, LANES:] = kr_ref[...]
            dk_ref[...] = jnp.zeros_like(dk_ref)
            dv_ref[...] = jnp.zeros_like(dv_ref)

        @pl.when((i == 0) & (h == 0))
        def _():
            dkr_ref[...] = jnp.zeros_like(dkr_ref)

        qv = q_ref[...]
        dov = do_ref[...]
        lse_v = lse_ref[...]
        delta = jnp.sum(dov.astype(F32) * o_ref[...].astype(F32), axis=-1, keepdims=True)

        def step(j, dq, diagonal):
            rows = pl.ds(pl.multiple_of(j * tq, tq), tq)
            kj = kcat_ref[rows, :]
            s = lax.dot_general(qv, kj, _NT, preferred_element_type=F32) * scale
            if diagonal:
                s = jnp.where(_chunk_mask(tq), s, NEG_BIG)
            p = jnp.exp(s - lse_v)
            dp = lax.dot_general(dov, kv_ref[rows, LANES:], _NT, preferred_element_type=F32)
            ds = (p * (dp - delta) * scale).astype(BF16)
            dk_ref[rows, :] += lax.dot_general(ds, qv, _TN, preferred_element_type=F32)
            dv_ref[rows, :] += lax.dot_general(p.astype(BF16), dov, _TN, preferred_element_type=F32)
            return dq + jnp.dot(ds, kj, preferred_element_type=F32)

        dq = lax.fori_loop(0, i, lambda j, c: step(j, c, False), jnp.zeros((tq, 2 * LANES), F32))
        dq_ref[...] = step(i, dq, True)

        @pl.when(i == nq - 1)
        def _():
            dkv_ref[:, :LANES] = dk_ref[:, :LANES].astype(dkv_ref.dtype)
            dkv_ref[:, LANES:] = dv_ref[...].astype(dkv_ref.dtype)
            dkr_ref[...] += dk_ref[:, LANES:]

    return pl.pallas_call(
        body,
        out_shape=(
            jax.ShapeDtypeStruct((t, heads * 2 * LANES), F32),
            jax.ShapeDtypeStruct((t, heads * 2 * LANES), BF16),
            jax.ShapeDtypeStruct((t, LANES), F32),
        ),
        grid=(t // seq, heads, nq),
        in_specs=[
            pl.BlockSpec((tq, 2 * LANES), lambda b, h, i: (b * nq + i, h)),
            pl.BlockSpec((seq, 2 * LANES), lambda b, h, i: (b, h)),
            pl.BlockSpec((seq, LANES), lambda b, h, i: (b, 0)),
            pl.BlockSpec((tq, C_V), lambda b, h, i: (b * nq + i, h)),
            pl.BlockSpec((tq, C_V), lambda b, h, i: (b * nq + i, h)),
            pl.BlockSpec((None, tq, 1), lambda b, h, i: (h, b * nq + i, 0)),
        ],
        out_specs=(
            pl.BlockSpec((tq, 2 * LANES), lambda b, h, i: (b * nq + i, h)),
            pl.BlockSpec((seq, 2 * LANES), lambda b, h, i: (b, h)),
            pl.BlockSpec((seq, LANES), lambda b, h, i: (b, 0)),
        ),
        scratch_shapes=[pltpu.VMEM((seq, 2 * LANES), BF16), pltpu.VMEM((seq, 2 * LANES), F32), pltpu.VMEM((seq, C_V), F32)],
        compiler_params=_cp("parallel", "arbitrary", "arbitrary"),
        name=name,
    )(q, kv, kr, do, o, lse)


def _softmax_rows(s):
    e = jnp.exp(s - jnp.max(s, axis=-1, keepdims=True))
    return e / jnp.sum(e, axis=-1, keepdims=True)


def _mem_fwd(q, k, v, *, seq, n_mem, name):
    t, d = q.shape
    hd = d // MEM_HEADS
    tq = _tile(seq, 512)
    nq = seq // tq
    scale = hd ** -0.5

    def body(q_ref, k_ref, v_ref, o_ref):
        s = lax.dot_general(q_ref[...], k_ref[...], _NT, preferred_element_type=F32) * scale
        p = _softmax_rows(s)
        o_ref[...] = jnp.dot(p.astype(BF16), v_ref[...], preferred_element_type=F32).astype(o_ref.dtype)

    return pl.pallas_call(
        body,
        out_shape=jax.ShapeDtypeStruct((t, d), BF16),
        grid=(t // seq, MEM_HEADS, nq),
        in_specs=[
            pl.BlockSpec((tq, hd), lambda b, h, i: (b * nq + i, h)),
            pl.BlockSpec((n_mem, hd), lambda b, h, i: (b, h)),
            pl.BlockSpec((n_mem, hd), lambda b, h, i: (b, h)),
        ],
        out_specs=pl.BlockSpec((tq, hd), lambda b, h, i: (b * nq + i, h)),
        compiler_params=_cp("parallel", "parallel", "arbitrary"),
        name=name,
    )(q, k, v)


def _mem_bwd(q, k, v, do, *, seq, n_mem, name):
    t, d = q.shape
    hd = d // MEM_HEADS
    tq = _tile(seq, 512)
    nq = seq // tq
    scale = hd ** -0.5

    def body(q_ref, k_ref, v_ref, do_ref, dq_ref, dk_ref, dv_ref):
        @pl.when(pl.program_id(2) == 0)
        def _():
            dk_ref[...] = jnp.zeros_like(dk_ref)
            dv_ref[...] = jnp.zeros_like(dv_ref)

        qv, kv_, vv, dov = q_ref[...], k_ref[...], v_ref[...], do_ref[...]
        p = _softmax_rows(lax.dot_general(qv, kv_, _NT, preferred_element_type=F32) * scale)
        dp = lax.dot_general(dov, vv, _NT, preferred_element_type=F32)
        ds = (p * (dp - jnp.sum(dp * p, axis=-1, keepdims=True)) * scale).astype(BF16)
        dq_ref[...] = jnp.dot(ds, kv_, preferred_element_type=F32).astype(dq_ref.dtype)
        dk_ref[...] += lax.dot_general(ds, qv, _TN, preferred_element_type=F32)
        dv_ref[...] += lax.dot_general(p.astype(BF16), dov, _TN, preferred_element_type=F32)

    return pl.pallas_call(
        body,
        out_shape=(
            jax.ShapeDtypeStruct((t, d), BF16),
            jax.ShapeDtypeStruct(k.shape, F32),
            jax.ShapeDtypeStruct(v.shape, F32),
        ),
        grid=(t // seq, MEM_HEADS, nq),
        in_specs=[
            pl.BlockSpec((tq, hd), lambda b, h, i: (b * nq + i, h)),
            pl.BlockSpec((n_mem, hd), lambda b, h, i: (b, h)),
            pl.BlockSpec((n_mem, hd), lambda b, h, i: (b, h)),
            pl.BlockSpec((tq, hd), lambda b, h, i: (b * nq + i, h)),
        ],
        out_specs=(
            pl.BlockSpec((tq, hd), lambda b, h, i: (b * nq + i, h)),
            pl.BlockSpec((n_mem, hd), lambda b, h, i: (b, h)),
            pl.BlockSpec((n_mem, hd), lambda b, h, i: (b, h)),
        ),
        compiler_params=_cp("parallel", "parallel", "arbitrary"),
        name=name,
    )(q, k, v, do)


def _me():
    return lax.axis_index("x"), lax.axis_index("y"), lax.axis_index("c")


def _dev_index(px, py, pc):
    return 4 * px + 2 * py + pc


def _chunk(ref, idx, axis, size):
    start = pl.multiple_of(idx * size, size)
    return ref.at[pl.ds(start, size), :] if axis == 0 else ref.at[:, pl.ds(start, size)]


def _all_gather_small(v):
    m, n = v.shape

    def body(x_ref, out_ref, send_sems, recv_sems, local_sem):
        x, y, c = _me()
        me, sibling = (x, y, c), (x, y, 1 - c)
        chips = [(1 - x, y), (x, 1 - y), (1 - x, 1 - y)]

        def rows(px, py, pc):
            return out_ref.at[_dev_index(px, py, pc)]

        def copy(k, block, to, src=None):
            return pltpu.make_async_remote_copy(
                src_ref=rows(*block) if src is None else src, dst_ref=rows(*block),
                send_sem=send_sems.at[k], recv_sem=recv_sems.at[k], device_id=to, device_id_type=MESH)

        mine = pltpu.make_async_copy(x_ref, rows(*me), local_sem)
        mine.start()
        first = [copy(0, me, sibling, src=x_ref)]
        first += [copy(1 + j, me, (*chip, c), src=x_ref) for j, chip in enumerate(chips)]
        for cp in first:
            cp.start()
        passed = [copy(4 + j, (*chip, c), sibling) for j, chip in enumerate(chips)]
        for j, chip in enumerate(chips):
            copy(1 + j, (*chip, c), me).wait_recv()
            passed[j].start()
        copy(0, sibling, me).wait_recv()
        for j, chip in enumerate(chips):
            copy(4 + j, (*chip, 1 - c), me).wait_recv()
        for cp in first + passed:
            cp.wait_send()
        mine.wait()

    return pl.pallas_call(
        body,
        out_shape=jax.ShapeDtypeStruct((N_DEV, m, n), v.dtype),
        in_specs=[pl.BlockSpec(memory_space=pltpu.VMEM)],
        out_specs=pl.BlockSpec(memory_space=pltpu.VMEM),
        scratch_shapes=[pltpu.SemaphoreType.DMA((7,)), pltpu.SemaphoreType.DMA((7,)), pltpu.SemaphoreType.DMA],
        name="all_gather_small",
    )(v)


def _cast_place(w, axis, dev_arr, *, name):
    r, c = w.shape
    tr = _tile(r, 256)
    nr = r // tr
    full = (N_DEV * r, c) if axis == 0 else (r, N_DEV * c)

    def body(dev_ref, w_ref, o_ref):
        o_ref[...] = w_ref[...].astype(o_ref.dtype)

    if axis == 0:
        out_spec = pl.BlockSpec((tr, c), lambda i, dev: (dev[0] * nr + i, 0))
    else:
        out_spec = pl.BlockSpec((tr, c), lambda i, dev: (i, dev[0]))
    return pl.pallas_call(
        body,
        out_shape=jax.ShapeDtypeStruct(full, BF16),
        grid_spec=pltpu.PrefetchScalarGridSpec(
            num_scalar_prefetch=1, grid=(nr,), in_specs=[pl.BlockSpec((tr, c), lambda i, dev: (i, 0))], out_specs=out_spec),
        compiler_params=_cp("parallel"),
        name=name,
    )(dev_arr, w)


_ANY = pl.BlockSpec(memory_space=pl.ANY)
_SEM = pl.BlockSpec(memory_space=pltpu.SEMAPHORE)
_EFFECT = pltpu.SideEffectType.DATAFLOW_SIDE_EFFECTING


def _other_chips(x, y):
    return [((1 - x if (j >> 1) & 1 else x), (1 - y if j & 1 else y)) for j in range(1, 4)]


def _sem(sems, t, j):
    return sems[j] if isinstance(sems, (list, tuple)) else sems.at[t, j]


N_PEER_CHIPS = 3


def _gather_copy(refs, axes, sizes, send_sems, recv_sems, t, j, chunk_dev, to):
    rows = _chunk(refs[t], _dev_index(*chunk_dev), axes[t], sizes[t])
    return pltpu.make_async_remote_copy(
        src_ref=rows, dst_ref=rows, send_sem=_sem(send_sems, t, j), recv_sem=_sem(recv_sems, t, j), device_id=to, device_id_type=MESH)


def _gather_ici_start(fulls, axes, *, name):
    n = len(fulls)
    sizes = [s.shape[a] // N_DEV for s, a in zip(fulls, axes)]

    p = N_PEER_CHIPS

    def body(*refs):
        ins = refs[:n]
        send_sems, recv_sems = list(refs[n:n + p]), list(refs[n + p:n + 2 * p])
        token = refs[-1]
        x, y, c = _me()
        for t in range(n):
            for j, chip in enumerate(_other_chips(x, y)):
                _gather_copy(ins, axes, sizes, send_sems, recv_sems, t, j, (x, y, c), (*chip, c)).start()
        token[...] = jnp.zeros_like(token)

    outs = pl.pallas_call(
        body,
        out_shape=(*([pltpu.SemaphoreType.DMA(())] * (2 * p)),
                   *[jax.ShapeDtypeStruct(s.shape, s.dtype) for s in fulls], jax.ShapeDtypeStruct((8, LANES), F32)),
        in_specs=[_ANY] * n,
        out_specs=(*([_SEM] * (2 * p)), *([_ANY] * n), pl.BlockSpec(memory_space=pltpu.VMEM)),
        input_output_aliases={t: 2 * p + t for t in range(n)},
        compiler_params=pltpu.CompilerParams(has_side_effects=_EFFECT),
        name=name,
    )(*fulls)
    return list(outs[:p]), list(outs[p:2 * p]), list(outs[2 * p:2 * p + n]), outs[-1]


def _gather_ici_wait(send_sems, recv_sems, fulls, axes, after, *, name):
    n = len(fulls)
    sizes = [s.shape[a] // N_DEV for s, a in zip(fulls, axes)]

    p = N_PEER_CHIPS

    def body(*refs):
        ins = refs[:n]
        send_sems_, recv_sems_ = list(refs[n:n + p]), list(refs[n + p:n + 2 * p])
        token = refs[-1]
        x, y, c = _me()
        for t in range(n):
            for j, chip in enumerate(_other_chips(x, y)):
                _gather_copy(ins, axes, sizes, send_sems_, recv_sems_, t, j, (x, y, c), (*chip, c)).wait_send()
                _gather_copy(ins, axes, sizes, send_sems_, recv_sems_, t, j, (*chip, c), (*chip, c)).wait_recv()
        token[...] = jnp.zeros_like(token)

    outs = pl.pallas_call(
        body,
        out_shape=(*[jax.ShapeDtypeStruct(s.shape, s.dtype) for s in fulls], jax.ShapeDtypeStruct((8, LANES), F32)),
        in_specs=[_ANY] * n + [_SEM] * (2 * p) + [_ANY],
        out_specs=(*([_ANY] * n), pl.BlockSpec(memory_space=pltpu.VMEM)),
        input_output_aliases={t: t for t in range(n)},
        compiler_params=pltpu.CompilerParams(has_side_effects=_EFFECT),
        name=name,
    )(*fulls, *send_sems, *recv_sems, after)
    return list(outs[:n]), outs[-1]


def _gather_d2d(fulls, axes, *, name):
    n = len(fulls)
    sizes = [s.shape[a] // N_DEV for s, a in zip(fulls, axes)]

    def body(*refs):
        outs = refs[n:2 * n]
        send_sems, recv_sems = refs[2 * n:]
        x, y, c = _me()
        sibling = (x, y, 1 - c)
        started = []
        for t in range(n):
            for k in range(4):
                kx, ky = k >> 1, k & 1
                cp = _gather_copy(outs, axes, sizes, send_sems, recv_sems, t, k, (kx, ky, c), sibling)
                cp.start()
                started.append(cp)
        for t in range(n):
            for k in range(4):
                kx, ky = k >> 1, k & 1
                _gather_copy(outs, axes, sizes, send_sems, recv_sems, t, k, (kx, ky, 1 - c), sibling).wait_recv()
        for cp in started:
            cp.wait_send()

    return list(pl.pallas_call(
        body,
        out_shape=tuple(jax.ShapeDtypeStruct(s.shape, s.dtype) for s in fulls),
        in_specs=[_ANY] * n,
        out_specs=tuple([_ANY] * n),
        input_output_aliases={t: t for t in range(n)},
        scratch_shapes=[pltpu.SemaphoreType.DMA((n, 4)), pltpu.SemaphoreType.DMA((n, 4))],
        name=name,
    )(*fulls))


def _scatter_d2d(grads, axes, *, name):
    n = len(grads)
    sizes = [g.shape[a] // N_DEV for g, a in zip(grads, axes)]

    def body(*refs):
        ins, gots = refs[:n], refs[n:2 * n]
        send_sems, recv_sems = refs[2 * n:]
        x, y, c = _me()
        sibling = (x, y, 1 - c)
        started = []
        for t in range(n):
            for k in range(4):
                give = pltpu.make_async_remote_copy(
                    src_ref=_chunk(ins[t], 2 * k + 1 - c, axes[t], sizes[t]), dst_ref=_chunk(gots[t], k, axes[t], sizes[t]),
                    send_sem=send_sems.at[t, k], recv_sem=recv_sems.at[t, k], device_id=sibling, device_id_type=MESH)
                give.start()
                started.append(give)
        for give in started:
            give.wait_recv()
            give.wait_send()

    out_shape = []
    for g, a in zip(grads, axes):
        sh = list(g.shape)
        sh[a] //= 2
        out_shape.append(jax.ShapeDtypeStruct(tuple(sh), g.dtype))
    any_spec = pl.BlockSpec(memory_space=pl.ANY)
    return pl.pallas_call(
        body,
        out_shape=tuple(out_shape),
        in_specs=[any_spec] * n,
        out_specs=tuple([any_spec] * n),
        scratch_shapes=[pltpu.SemaphoreType.DMA((n, 4)), pltpu.SemaphoreType.DMA((n, 4))],
        name=name,
    )(*grads)


def _scatter_copy(srcs, dsts, axes, sizes, send_sems, recv_sems, t, j, chip_src, chip_dst, to):
    return pltpu.make_async_remote_copy(
        src_ref=_chunk(srcs[t], chip_src, axes[t], sizes[t]), dst_ref=_chunk(dsts[t], chip_dst, axes[t], sizes[t]),
        send_sem=_sem(send_sems, t, j), recv_sem=_sem(recv_sems, t, j), device_id=to, device_id_type=MESH)


def _scatter_ici_start(sums, lands, axes, *, name):
    n = len(sums)
    sizes = [g.shape[a] // 4 for g, a in zip(sums, axes)]

    p = N_PEER_CHIPS

    def body(*refs):
        srcs, dsts = refs[:n], refs[n:2 * n]
        send_sems, recv_sems = list(refs[2 * n:2 * n + p]), list(refs[2 * n + p:2 * n + 2 * p])
        token = refs[-1]
        x, y, c = _me()
        for t in range(n):
            for j, (px, py) in enumerate(_other_chips(x, y)):
                _scatter_copy(srcs, dsts, axes, sizes, send_sems, recv_sems, t, j, 2 * px + py, 2 * x + y, (px, py, c)).start()
        token[...] = jnp.zeros_like(token)

    shapes = [jax.ShapeDtypeStruct(g.shape, g.dtype) for g in sums]
    outs = pl.pallas_call(
        body,
        out_shape=(*([pltpu.SemaphoreType.DMA(())] * (2 * p)), *shapes, *shapes, jax.ShapeDtypeStruct((8, LANES), F32)),
        in_specs=[_ANY] * (2 * n),
        out_specs=(*([_SEM] * (2 * p)), *([_ANY] * (2 * n)), pl.BlockSpec(memory_space=pltpu.VMEM)),
        input_output_aliases={t: 2 * p + t for t in range(2 * n)},
        compiler_params=pltpu.CompilerParams(has_side_effects=_EFFECT),
        name=name,
    )(*sums, *lands)
    return list(outs[:p]), list(outs[p:2 * p]), list(outs[2 * p:2 * p + n]), list(outs[2 * p + n:2 * p + 2 * n]), outs[-1]


def _scatter_ici_wait(send_sems, recv_sems, sums, lands, axes, after, *, name):
    n = len(sums)
    sizes = [g.shape[a] // 4 for g, a in zip(sums, axes)]

    p = N_PEER_CHIPS

    def body(*refs):
        srcs, dsts = refs[:n], refs[n:2 * n]
        send_sems_, recv_sems_ = list(refs[2 * n:2 * n + p]), list(refs[2 * n + p:2 * n + 2 * p])
        x, y, c = _me()
        for t in range(n):
            for j, (px, py) in enumerate(_other_chips(x, y)):
                _scatter_copy(srcs, dsts, axes, sizes, send_sems_, recv_sems_, t, j, 2 * px + py, 2 * x + y, (px, py, c)).wait_send()
                _scatter_copy(srcs, dsts, axes, sizes, send_sems_, recv_sems_, t, j, 2 * x + y, 2 * px + py, (px, py, c)).wait_recv()

    shapes = [jax.ShapeDtypeStruct(g.shape, g.dtype) for g in sums]
    outs = pl.pallas_call(
        body,
        out_shape=(*shapes, *shapes),
        in_specs=[_ANY] * (2 * n) + [_SEM] * (2 * p) + [_ANY],
        out_specs=tuple([_ANY] * (2 * n)),
        input_output_aliases={t: t for t in range(2 * n)},
        compiler_params=pltpu.CompilerParams(has_side_effects=_EFFECT),
        name=name,
    )(*sums, *lands, *send_sems, *recv_sems, after)
    return list(outs[:n]), list(outs[n:])


def _add_pair(g, got, axis, core_arr, *, name):
    r, c = got.shape
    if axis == 0:
        rows, cols = r // 4, c
    else:
        rows, cols = r, c // 4
    tr = _tile(rows, 512)
    nr = rows // tr

    def body(core_ref, g_ref, got_ref, o_ref):
        o_ref[...] = (g_ref[...].astype(F32) + got_ref[...].astype(F32)).astype(o_ref.dtype)

    if axis == 0:
        g_spec = pl.BlockSpec((tr, cols), lambda k, i, core: ((2 * k + core[0]) * nr + i, 0))
        spec = pl.BlockSpec((tr, cols), lambda k, i, core: (k * nr + i, 0))
    else:
        g_spec = pl.BlockSpec((tr, cols), lambda k, i, core: (i, 2 * k + core[0]))
        spec = pl.BlockSpec((tr, cols), lambda k, i, core: (i, k))
    return pl.pallas_call(
        body,
        out_shape=jax.ShapeDtypeStruct(got.shape, got.dtype),
        grid_spec=pltpu.PrefetchScalarGridSpec(num_scalar_prefetch=1, grid=(4, nr), in_specs=[g_spec, spec], out_specs=spec),
        compiler_params=_cp("parallel", "parallel"),
        name=name,
    )(core_arr, g, got)


def _adam_math(w, g, m, v):
    m = ADAM_B1 * m + (1.0 - ADAM_B1) * g
    v = ADAM_B2 * v + (1.0 - ADAM_B2) * (g * g)
    m_hat = m / (1.0 - ADAM_B1 ** ADAM_STEP)
    v_hat = v / (1.0 - ADAM_B2 ** ADAM_STEP)
    delta = -ADAM_LR * (m_hat / (jnp.sqrt(v_hat) + ADAM_EPS) + ADAM_WD * w)
    return delta, m, v


def _adamw(w, m, v, parts, axis, chip_arr, *, name):
    n_l, r, c = w.shape
    tr = _tile(r, 128)
    nr = r // tr
    n_chip = 4

    def body(*refs):
        w_ref, m_ref, v_ref = refs[1:4]
        p_refs = refs[4:4 + n_chip * n_l]
        g_ref, d_ref, nm_ref, nv_ref = refs[4 + n_chip * n_l:]
        for l in range(n_l):
            @pl.when(pl.program_id(0) == l)
            def _(l=l):
                g = p_refs[n_chip * l][...].astype(F32)
                for k in range(1, n_chip):
                    g = g + p_refs[n_chip * l + k][...].astype(F32)
                delta, nm, nv = _adam_math(w_ref[...], g, m_ref[...], v_ref[...])
                g_ref[...] = g
                d_ref[...] = delta
                nm_ref[...] = nm
                nv_ref[...] = nv

    def part_spec(ll, k):
        if axis == 0:
            return pl.BlockSpec((tr, c), lambda l, i, chips: (chips[k] * nr + jnp.where(l == ll, i, 0), 0))
        return pl.BlockSpec((tr, c), lambda l, i, chips: (jnp.where(l == ll, i, 0), chips[k]))

    wspec = pl.BlockSpec((None, tr, c), lambda l, i, chips: (l, i, 0))
    pspecs = [part_spec(ll, k) for ll in range(n_l) for k in range(n_chip)]
    sds = jax.ShapeDtypeStruct(w.shape, F32)
    return pl.pallas_call(
        body,
        out_shape=(sds, sds, sds, sds),
        grid_spec=pltpu.PrefetchScalarGridSpec(
            num_scalar_prefetch=1, grid=(n_l, nr), in_specs=[wspec, wspec, wspec] + pspecs, out_specs=(wspec, wspec, wspec, wspec)),
        compiler_params=_cp("arbitrary", "arbitrary"),
        name=name,
    )(chip_arr, w, m, v, *[parts[ll][0 if k == 0 else 1] for ll in range(n_l) for k in range(n_chip)])


def _adamw_small(w, m, v, parts, *, name):
    r, c = w.shape
    tr = _tile(r, 512)

    def body(w_ref, m_ref, v_ref, p_ref, g_ref, d_ref, nm_ref, nv_ref):
        g = p_ref[0]
        for s in range(1, N_DEV):
            g = g + p_ref[s]
        delta, nm, nv = _adam_math(w_ref[...], g, m_ref[...], v_ref[...])
        g_ref[...] = g
        d_ref[...] = delta
        nm_ref[...] = nm
        nv_ref[...] = nv

    spec = pl.BlockSpec((tr, c), lambda i: (i, 0))
    sds = jax.ShapeDtypeStruct(w.shape, F32)
    return pl.pallas_call(
        body,
        out_shape=(sds, sds, sds, sds),
        grid=(r // tr,),
        in_specs=[spec, spec, spec, pl.BlockSpec((N_DEV, tr, c), lambda i: (0, i, 0))],
        out_specs=(spec, spec, spec, spec),
        compiler_params=_cp("parallel"),
        name=name,
    )(w, m, v, parts)


def _pack(arrs):
    flat = jnp.concatenate([a.reshape(-1).astype(F32) for a in arrs])
    pad = (-flat.shape[0]) % (8 * LANES)
    return jnp.pad(flat, (0, pad)).reshape(-1, LANES)


def _unpack(packed, shapes):
    flat = packed.reshape(-1)
    out, off = [], 0
    for sh in shapes:
        n = math.prod(sh)
        out.append(flat[off:off + n].reshape(sh))
        off += n
    return out


def _unpack_gathered(packed, shapes, axis):
    flat = packed.reshape(N_DEV, -1)
    out, off = [], 0
    for sh, ax in zip(shapes, axis):
        n = math.prod(sh)
        blocks = flat[:, off:off + n].reshape((N_DEV, *sh))
        out.append(jnp.concatenate([blocks[d] for d in range(N_DEV)], axis=ax))
        off += n
    return out


def _my_block(full, axis, size):
    idx = _dev_index(*_me())
    return lax.dynamic_slice_in_dim(full, idx * size, size, axis)


def kernel(x, mem, positions, norm_mix_g, norm_mem_q_g, norm_mem_kv_g, norm_ffn_g, final_norm_g, ab_w_in, a_v_norm_g, a_w_s, a_b_s, b_conv_w, ab_w_out, c_w_in, c_q_norm_g, c_kv_norm_g, c_w_uq, c_w_ukv, c_w_out, m_wq, m_wk, m_wv, m_wo, f_w1, f_w2, loss_target, m_norm_mix_g, m_norm_mem_q_g, m_norm_mem_kv_g, m_norm_ffn_g, m_final_norm_g, m_ab_w_in, m_a_v_norm_g, m_a_w_s, m_a_b_s, m_b_conv_w, m_ab_w_out, m_c_w_in, m_c_q_norm_g, m_c_kv_norm_g, m_c_w_uq, m_c_w_ukv, m_c_w_out, m_m_wq, m_m_wk, m_m_wv, m_m_wo, m_f_w1, m_f_w2, v_norm_mix_g, v_norm_mem_q_g, v_norm_mem_kv_g, v_norm_ffn_g, v_final_norm_g, v_ab_w_in, v_a_v_norm_g, v_a_w_s, v_a_b_s, v_b_conv_w, v_ab_w_out, v_c_w_in, v_c_q_norm_g, v_c_kv_norm_g, v_c_w_uq, v_c_w_ukv, v_c_w_out, v_m_wq, v_m_wk, v_m_wv, v_m_wo, v_f_w1, v_f_w2):
    bl, seq, d = x.shape
    t = bl * seq
    n_mem = mem.shape[1]
    depth = norm_mix_g.shape[0]
    n_even, n_odd = ab_w_in.shape[0], c_w_in.shape[0]
    a_width = a_v_norm_g.shape[1]
    groups = a_w_s.shape[1]
    heads = d // C_NOPE
    q_rank = c_q_norm_g.shape[1] * N_DEV
    kv_rank = c_kv_norm_g.shape[1] * N_DEV

    big = {
        "ab_w_in": (ab_w_in, m_ab_w_in, v_ab_w_in, 1), "ab_w_out": (ab_w_out, m_ab_w_out, v_ab_w_out, 0),
        "c_w_in": (c_w_in, m_c_w_in, v_c_w_in, 0), "c_w_uq": (c_w_uq, m_c_w_uq, v_c_w_uq, 1),
        "c_w_ukv": (c_w_ukv, m_c_w_ukv, v_c_w_ukv, 1), "c_w_out": (c_w_out, m_c_w_out, v_c_w_out, 0),
        "m_wq": (m_wq, m_m_wq, v_m_wq, 0), "m_wk": (m_wk, m_m_wk, v_m_wk, 0), "m_wv": (m_wv, m_m_wv, v_m_wv, 0),
        "m_wo": (m_wo, m_m_wo, v_m_wo, 0), "f_w1": (f_w1, m_f_w1, v_f_w1, 1), "f_w2": (f_w2, m_f_w2, v_f_w2, 0),
    }
    even_names = ["ab_w_in", "ab_w_out"]
    odd_names = ["c_w_in", "c_w_uq", "c_w_ukv", "c_w_out"]
    all_names = ["m_wq", "m_wk", "m_wv", "m_wo", "f_w1", "f_w2"]

    def layer_names(layer):
        return (even_names if layer % 2 == 0 else odd_names) + all_names

    def sub(name, layer):
        return layer // 2 if name in even_names or name in odd_names else layer

    split_small = [(c_q_norm_g, m_c_q_norm_g, v_c_q_norm_g, 1), (c_kv_norm_g, m_c_kv_norm_g, v_c_kv_norm_g, 1), (b_conv_w, m_b_conv_w, v_b_conv_w, 2)]
    shard_shapes = [s[0].shape for s in split_small] * 3
    shard_axes = [s[3] for s in split_small] * 3
    gathered = _all_gather_small(_pack([s[0] for s in split_small] + [s[1] for s in split_small] + [s[2] for s in split_small]))
    whole = _unpack_gathered(gathered, shard_shapes, shard_axes)
    q_norm_full, kv_norm_full, conv_full = whole[0:3]
    small_w = [norm_mix_g, norm_mem_q_g, norm_mem_kv_g, norm_ffn_g, final_norm_g, a_v_norm_g, a_w_s, a_b_s] + whole[0:3]
    small_m = [m_norm_mix_g, m_norm_mem_q_g, m_norm_mem_kv_g, m_norm_ffn_g, m_final_norm_g, m_a_v_norm_g, m_a_w_s, m_a_b_s] + whole[3:6]
    small_v = [v_norm_mix_g, v_norm_mem_q_g, v_norm_mem_kv_g, v_norm_ffn_g, v_final_norm_g, v_a_v_norm_g, v_a_w_s, v_a_b_s] + whole[6:9]

    inv = ROPE_THETA ** (-jnp.arange(0, C_ROPE, 2, dtype=F32) / C_ROPE)
    ang = positions.astype(F32).reshape(t, 1) * inv
    cos, sin, zero = jnp.cos(ang), jnp.sin(ang), jnp.zeros((t, C_ROPE // 2), F32)
    rc = jnp.concatenate([cos, cos, zero, zero], axis=1)
    rs1 = jnp.concatenate([-sin, zero, zero, zero], axis=1)
    rs2 = jnp.concatenate([zero, sin, zero, zero], axis=1)

    my_x, my_y, my_c = _me()
    my_chip = 2 * my_x + my_y
    dev_arr = jnp.stack([_dev_index(my_x, my_y, my_c)]).astype(jnp.int32)
    core_arr = jnp.stack([my_c]).astype(jnp.int32)
    chip_arr = jnp.stack([my_chip, my_chip ^ 1, my_chip ^ 2, my_chip ^ 3]).astype(jnp.int32)

    def start_gather(layer, token):
        names = layer_names(layer)
        dev = dev_arr if token is None else dev_arr + token[0, 0:1].astype(jnp.int32)
        placed = [_cast_place(big[nm][0][sub(nm, layer)], big[nm][3], dev, name="cast_place") for nm in names]
        return _gather_ici_start(placed, [big[nm][3] for nm in names], name=f"gather_start_l{layer}")

    wts = []

    def group_cols(w_in):
        return w_in.reshape(d, 5, groups, LANES).transpose(0, 2, 1, 3).reshape(d, 5 * a_width)

    def ungroup_cols(g_in):
        return g_in.reshape(d, groups, 5, LANES).transpose(0, 2, 1, 3).reshape(d, 5 * a_width)

    def group_rows(w_out):
        return w_out.reshape(2, groups, LANES, d).transpose(1, 0, 2, 3).reshape(2 * a_width, d)

    def ungroup_rows(g_out):
        return g_out.reshape(groups, 2, LANES, d).transpose(1, 0, 2, 3).reshape(2 * a_width, d)

    def pad_heads(w_uq):
        w3 = w_uq.reshape(q_rank, heads, C_NOPE + C_ROPE)
        return jnp.pad(w3, ((0, 0), (0, 0), (0, LANES - C_ROPE))).reshape(q_rank, heads * 2 * LANES)

    def unpad_heads(g_uq):
        return g_uq.reshape(q_rank, heads, 2 * LANES)[:, :, :C_NOPE + C_ROPE].reshape(q_rank, heads * (C_NOPE + C_ROPE))

    x2 = x.reshape(t, d)
    mem2 = mem.reshape(bl * n_mem, d)
    saved = []

    pending = start_gather(0, None)
    for layer in range(depth):
        names = layer_names(layer)
        axes = [big[nm][3] for nm in names]
        send_s, recv_s, flight, token = pending
        landed, token = _gather_ici_wait(send_s, recv_s, flight, axes, x2, name=f"gather_wait_l{layer}")
        if layer + 1 < depth:
            pending = start_gather(layer + 1, token)
            token = pending[3]
        w = dict(zip(names, _gather_d2d(landed, axes, name=f"gather_d2d_l{layer}")))
        wts.append(w)
        sv = {"x0": x2}
        xn = _rmsnorm_fwd(x2, norm_mix_g[layer], name="norm_mix")
        sv["xn"] = xn
        if layer % 2 == 0:
            e = layer // 2
            w_in_g = group_cols(w["ab_w_in"])
            w_out_g = group_rows(w["ab_w_out"])
            z = _mm(xn, w_in_g, out_dtypes=(F32,), extras=[(token, "after")], name="ab_in")
            ws_e = a_w_s[e]
            bs_e = a_b_s[e].reshape(groups, GMLP_BLOCK, 1)
            gn_e = a_v_norm_g[e].reshape(groups, 1, LANES)
            cw_e = conv_full[e].reshape(3, groups, LANES).transpose(1, 0, 2)
            ycat = _gmlp_conv_fwd(z, ws_e, bs_e, gn_e, cw_e, seq=seq, name="gmlp_conv_fwd")
            x2 = _mm(ycat, w_out_g, out_dtypes=(F32,), extras=[(x2, "mn")], epilogue=_epi_add, name="ab_out")
            sv.update(z=z, ycat=ycat, w_in_g=w_in_g, w_out_g=w_out_g, ws=ws_e, bs=bs_e, gn=gn_e, cw=cw_e)
        else:
            o = layer // 2
            w_uq_p = pad_heads(w["c_w_uq"])
            z = _mm(xn, w["c_w_in"], out_dtypes=(F32,), extras=[(token, "after")], name="c_in")
            qn = _rmsnorm_fwd(z, q_norm_full[o], width=q_rank, col=0, name="norm_cq")
            kvn = _rmsnorm_fwd(z, kv_norm_full[o], width=kv_rank, col=q_rank // kv_rank, name="norm_ckv")
            qh = _mm(qn, w_uq_p, extras=[(rc, "m"), (rs1, "m"), (rs2, "m")], epilogue=_epi_rope_heads, tn=1024, name="c_uq")
            kvh = _mm(kvn, w["c_w_ukv"], name="c_ukv")
            k_rope = jnp.pad(z[:, q_rank + kv_rank:], ((0, 0), (0, LANES - C_ROPE)))
            kr = _rope_rows(k_rope, rc, rs1, rs2, transpose=False, out_dtype=BF16, name="rope_k")
            att, lse = _mla_fwd(qh, kvh, kr, seq=seq, heads=heads, name="mla_fwd")
            x2 = _mm(att, w["c_w_out"], out_dtypes=(F32,), extras=[(x2, "mn")], epilogue=_epi_add, name="c_out")
            sv.update(z=z, qn=qn, kvn=kvn, qh=qh, kvh=kvh, kr=kr, att=att, lse=lse, w_uq_p=w_uq_p)
        sv["x1"] = x2
        xq = _rmsnorm_fwd(x2, norm_mem_q_g[layer], name="norm_mem_q")
        memn = _rmsnorm_fwd(mem2, norm_mem_kv_g[layer], name="norm_mem_kv")
        mq = _mm(xq, w["m_wq"], name="m_q")
        mk = _mm(memn, w["m_wk"], name="m_k")
        mv = _mm(memn, w["m_wv"], name="m_v")
        mo = _mem_fwd(mq, mk, mv, seq=seq, n_mem=n_mem, name="mem_fwd")
        x2 = _mm(mo, w["m_wo"], out_dtypes=(F32,), extras=[(x2, "mn")], epilogue=_epi_add, name="m_o")
        sv.update(xq=xq, memn=memn, mq=mq, mk=mk, mv=mv, mo=mo)
        sv["x2"] = x2
        xf = _rmsnorm_fwd(x2, norm_ffn_g[layer], name="norm_ffn")
        hpre, hact = _mm(xf, w["f_w1"], out_dtypes=(BF16, BF16), epilogue=_epi_relu2, name="f_1")
        x2 = _mm(hact, w["f_w2"], out_dtypes=(F32,), extras=[(x2, "mn")], epilogue=_epi_add, name="f_2")
        sv.update(xf=xf, hpre=hpre, hact=hact)
        saved.append(sv)

    loss_part, dx, dg_final = _final_loss(x2, final_norm_g, loss_target.reshape(t, d), name="final_loss")
    loss = lax.psum(loss_part[0, 0], ("x", "y", "c"))

    dg_mix, dg_mq, dg_mkv, dg_ffn = [None] * depth, [None] * depth, [None] * depth, [None] * depth
    d_av, d_ws, d_bs = [None] * n_even, [None] * n_even, [None] * n_even
    d_conv, d_qn, d_kvn = [None] * n_even, [None] * n_odd, [None] * n_odd
    parts = {nm: [None] * big[nm][0].shape[0] for nm in big}

    in_flight = []
    for layer in reversed(range(depth)):
        w, sv = wts[layer], saved[layer]
        grads = {}
        dh = _mm(dx, w["f_w2"], tb=True, extras=[(sv["hpre"], "mn"), (token, "after")], epilogue=_epi_drelu2, name="f_2_dx")
        grads["f_w2"] = _mm(sv["hact"], dx, ta=True, name="f_2_dw")
        grads["f_w1"] = _mm(sv["xf"], dh, ta=True, name="f_1_dw")
        dxf = _mm(dh, w["f_w1"], tb=True, name="f_1_dx")
        dx, dg_ffn[layer] = _rmsnorm_bwd(sv["x2"], norm_ffn_g[layer], dxf, dx, name="norm_ffn_bwd")
        dmo = _mm(dx, w["m_wo"], tb=True, name="m_o_dx")
        grads["m_wo"] = _mm(sv["mo"], dx, ta=True, name="m_o_dw")
        dmq, dmk, dmv = _mem_bwd(sv["mq"], sv["mk"], sv["mv"], dmo, seq=seq, n_mem=n_mem, name="mem_bwd")
        grads["m_wq"] = _mm(sv["xq"], dmq, ta=True, name="m_q_dw")
        grads["m_wk"] = _mm(sv["memn"], dmk, ta=True, name="m_k_dw")
        grads["m_wv"] = _mm(sv["memn"], dmv, ta=True, name="m_v_dw")
        dxq = _mm(dmq, w["m_wq"], tb=True, name="m_q_dx")
        dmemn_k = _mm(dmk, w["m_wk"], tb=True, out_dtypes=(F32,), name="m_k_dx")
        dmemn = _mm(dmv, w["m_wv"], tb=True, out_dtypes=(F32,), extras=[(dmemn_k, "mn")], epilogue=_epi_add, name="m_v_dx")
        _, dg_mkv[layer] = _rmsnorm_bwd(mem2, norm_mem_kv_g[layer], dmemn, name="norm_mem_kv_bwd")
        dx, dg_mq[layer] = _rmsnorm_bwd(sv["x1"], norm_mem_q_g[layer], dxq, dx, name="norm_mem_q_bwd")
        if layer % 2 == 0:
            e = layer // 2
            dycat = _mm(dx, sv["w_out_g"], tb=True, name="ab_out_dx")
            grads["ab_w_out"] = ungroup_rows(_mm(sv["ycat"], dx, ta=True, name="ab_out_dw"))
            dz, dws, dbs, dgn, dcw = _gmlp_conv_bwd(sv["z"], dycat, sv["ws"], sv["bs"], sv["gn"], sv["cw"], seq=seq, name="gmlp_conv_bwd")
            d_ws[e], d_bs[e], d_av[e] = dws, dbs.reshape(groups, GMLP_BLOCK), dgn.reshape(a_width)
            d_conv[e] = dcw.transpose(1, 0, 2).reshape(3, a_width)
            grads["ab_w_in"] = ungroup_cols(_mm(sv["xn"], dz, ta=True, name="ab_in_dw"))
            dxn = _mm(dz, sv["w_in_g"], tb=True, name="ab_in_dx")
        else:
            o = layer // 2
            datt = _mm(dx, w["c_w_out"], tb=True, name="c_out_dx")
            grads["c_w_out"] = _mm(sv["att"], dx, ta=True, name="c_out_dw")
            dqh, dkvh, dkr = _mla_bwd(sv["qh"], sv["kvh"], sv["kr"], datt, sv["att"], sv["lse"], seq=seq, heads=heads, name="mla_bwd")
            dq_pre = _rope_rows(dqh, rc, rs1, rs2, transpose=True, out_dtype=BF16, name="rope_q_bwd")
            dk_rope = _rope_rows(dkr, rc, rs1, rs2, transpose=True, out_dtype=F32, name="rope_k_bwd")
            grads["c_w_uq"] = unpad_heads(_mm(sv["qn"], dq_pre, ta=True, name="c_uq_dw"))
            grads["c_w_ukv"] = _mm(sv["kvn"], dkvh, ta=True, name="c_ukv_dw")
            dqn = _mm(dq_pre, sv["w_uq_p"], tb=True, name="c_uq_dx")
            dkvn = _mm(dkvh, w["c_w_ukv"], tb=True, name="c_ukv_dx")
            dcq, d_qn[o] = _rmsnorm_bwd(sv["z"], q_norm_full[o], dqn, width=q_rank, col=0, name="norm_cq_bwd")
            dckv, d_kvn[o] = _rmsnorm_bwd(sv["z"], kv_norm_full[o], dkvn, width=kv_rank, col=q_rank // kv_rank, name="norm_ckv_bwd")
            dz = jnp.concatenate([dcq, dckv, dk_rope[:, :C_ROPE]], axis=1)
            grads["c_w_in"] = _mm(sv["xn"], dz, ta=True, name="c_in_dw")
            dxn = _mm(dz, w["c_w_in"], tb=True, name="c_in_dx")
        dx, dg_mix[layer] = _rmsnorm_bwd(sv["x0"], norm_mix_g[layer], dxn, dx, name="norm_mix_bwd")
        names = layer_names(layer)
        axes = [big[nm][3] for nm in names]
        got = _scatter_d2d([grads[nm] for nm in names], axes, name=f"scatter_d2d_l{layer}")
        sums = [_add_pair(grads[nm], g, ax, core_arr, name="add_pair") for nm, g, ax in zip(names, got, axes)]
        lands = [pltpu.with_memory_space_constraint(lax.empty(s_.shape, s_.dtype), pltpu.HBM) for s_ in sums]
        send_s, recv_s, sums, lands, token = _scatter_ici_start(sums, lands, axes, name=f"scatter_start_l{layer}")
        in_flight.append((layer, names, axes, send_s, recv_s, sums, lands))

    for layer, names, axes, send_s, recv_s, sums, lands in in_flight:
        sums, lands = _scatter_ici_wait(send_s, recv_s, sums, lands, axes, dx, name=f"scatter_wait_l{layer}")
        for nm, s_, r in zip(names, sums, lands):
            parts[nm][sub(nm, layer)] = (s_, r)

    grad_x = dx.reshape(bl, seq, d)

    big_out = {}
    for nm, (wv, mv_, vv, ax) in big.items():
        big_out[nm] = _adamw(wv, mv_, vv, parts[nm], ax, chip_arr, name=f"adamw_{nm}")

    small_g = [jnp.concatenate(dg_mix), jnp.concatenate(dg_mq), jnp.concatenate(dg_mkv), jnp.concatenate(dg_ffn), dg_final.reshape(d),
               jnp.stack(d_av), jnp.stack(d_ws), jnp.stack(d_bs), jnp.concatenate(d_qn), jnp.concatenate(d_kvn), jnp.stack(d_conv)]
    small_shapes = [a.shape for a in small_w]
    g_all = _all_gather_small(_pack(small_g))
    sg, sd, sm, svv = _adamw_small(_pack(small_w), _pack(small_m), _pack(small_v), g_all, name="adamw_small")
    small_names = ["norm_mix_g", "norm_mem_q_g", "norm_mem_kv_g", "norm_ffn_g", "final_norm_g", "a_v_norm_g", "a_w_s", "a_b_s", "c_q_norm_g", "c_kv_norm_g", "b_conv_w"]
    small_out = {}
    unpacked = [_unpack(p, small_shapes) for p in (sg, sd, sm, svv)]
    for i, nm in enumerate(small_names):
        vals = [u[i] for u in unpacked]
        if i >= 8:
            full_w, _, _, ax = split_small[i - 8]
            vals = [_my_block(val, ax, full_w.shape[ax]) for val in vals]
        small_out[nm] = vals

    order = ["norm_mix_g", "norm_mem_q_g", "norm_mem_kv_g", "norm_ffn_g", "final_norm_g", "ab_w_in", "a_v_norm_g", "a_w_s", "a_b_s", "b_conv_w", "ab_w_out",
             "c_w_in", "c_q_norm_g", "c_kv_norm_g", "c_w_uq", "c_w_ukv", "c_w_out", "m_wq", "m_wk", "m_wv", "m_wo", "f_w1", "f_w2"]
    res = [loss, grad_x]
    for kind in range(4):
        for nm in order:
            res.append(big_out[nm][kind] if nm in big_out else small_out[nm][kind])
    return tuple(res)
```

```python
import functools
import math

import jax
import jax.numpy as jnp
from jax import lax
from jax.experimental import pallas as pl
from jax.experimental.pallas import tpu as pltpu

F32 = jnp.float32
BF16 = jnp.bfloat16
MESH = pl.DeviceIdType.MESH
N_DEV = 8
LANES = 128

RMS_EPS = 1e-6
CHUNK = 64
GMLP_BLOCK = 128
C_NOPE = 128
C_ROPE = 64
C_V = 128
ROPE_THETA = 10000.0
MEM_HEADS = 4
ADAM_LR = 0.001
ADAM_B1 = 0.9
ADAM_B2 = 0.999
ADAM_EPS = 1e-08
ADAM_WD = 0.01
ADAM_STEP = 10
NEG_BIG = -1e30
VMEM_LIMIT = 56 * 1024 * 1024


def _cp(*sem):
    return pltpu.CompilerParams(dimension_semantics=sem, vmem_limit_bytes=VMEM_LIMIT)


def _tile(n, pref):
    t = min(n, pref)
    while n % t and t > LANES:
        t //= 2
    return n if n % t else t


def _mm(a, b, *, ta=False, tb=False, out_dtypes=(BF16,), extras=(), epilogue=None, tm=1024, tn=1024, tk=2048, name):
    m, k = (a.shape[1], a.shape[0]) if ta else a.shape
    n = b.shape[0] if tb else b.shape[1]
    assert (b.shape[1] if tb else b.shape[0]) == k, (a.shape, b.shape, ta, tb)
    tm, tn, tk = _tile(m, tm), _tile(n, tn), _tile(k, tk)
    nk = k // tk
    ne, no = len(extras), len(out_dtypes)
    dims = (((0 if ta else 1,), (1 if tb else 0,)), ((), ()))

    def body(*refs):
        a_ref, b_ref = refs[0], refs[1]
        e_refs = refs[2:2 + ne]
        o_refs = refs[2 + ne:2 + ne + no]

        def finish(acc):
            vals = [e[...] for e, (_, kind) in zip(e_refs, extras) if kind != "after"]
            outs = (acc,) if epilogue is None else epilogue(acc, *vals)
            for o, val in zip(o_refs, outs):
                o[...] = val.astype(o.dtype)

        prod = lax.dot_general(a_ref[...].astype(BF16), b_ref[...].astype(BF16), dims, preferred_element_type=F32)
        if nk == 1:
            finish(prod)
        else:
            acc_ref = refs[-1]
            kk = pl.program_id(2)

            @pl.when(kk == 0)
            def _():
                acc_ref[...] = prod

            @pl.when((kk > 0) & (kk < nk - 1))
            def _():
                acc_ref[...] += prod

            @pl.when(kk == nk - 1)
            def _():
                finish(acc_ref[...] + prod)

    a_spec = pl.BlockSpec((tk, tm), lambda i, j, kk: (kk, i)) if ta else pl.BlockSpec((tm, tk), lambda i, j, kk: (i, kk))
    b_spec = pl.BlockSpec((tn, tk), lambda i, j, kk: (j, kk)) if tb else pl.BlockSpec((tk, tn), lambda i, j, kk: (kk, j))
    e_specs = []
    for arr, kind in extras:
        if kind == "mn":
            e_specs.append(pl.BlockSpec((tm, tn), lambda i, j, kk: (i, j)))
        elif kind == "after":
            e_specs.append(pl.BlockSpec(arr.shape, lambda i, j, kk: (0, 0)))
        else:
            e_specs.append(pl.BlockSpec((tm, arr.shape[1]), lambda i, j, kk: (i, 0)))
    outs = pl.pallas_call(
        body,
        out_shape=tuple(jax.ShapeDtypeStruct((m, n), d) for d in out_dtypes),
        grid=(m // tm, n // tn, nk),
        in_specs=[a_spec, b_spec] + e_specs,
        out_specs=tuple(pl.BlockSpec((tm, tn), lambda i, j, kk: (i, j)) for _ in out_dtypes),
        scratch_shapes=[pltpu.VMEM((tm, tn), F32)] if nk > 1 else [],
        compiler_params=_cp("parallel", "parallel", "arbitrary"),
        name=name,
    )(a, b, *[arr for arr, _ in extras])
    return outs[0] if no == 1 else outs


def _epi_add(acc, r):
    return (acc + r,)


def _epi_relu2(acc):
    h = jnp.maximum(acc, 0.0)
    return (acc, h * h)


def _epi_drelu2(acc, h):
    return (acc * (2.0 * jnp.maximum(h.astype(F32), 0.0)),)


def _rope(x, c, s1, s2):
    return x * c + pltpu.roll(x, LANES - C_ROPE // 2, 1) * s1 + pltpu.roll(x, C_ROPE // 2, 1) * s2


def _rope_t(dy, c, s1, s2):
    return dy * c + pltpu.roll(dy * s1, C_ROPE // 2, 1) + pltpu.roll(dy * s2, LANES - C_ROPE // 2, 1)


def _epi_rope_heads(acc, c, s1, s2):
    parts = []
    for h0 in range(0, acc.shape[1], 2 * LANES):
        parts.append(acc[:, h0:h0 + LANES])
        parts.append(_rope(acc[:, h0 + LANES:h0 + 2 * LANES], c, s1, s2))
    return (jnp.concatenate(parts, axis=1),)


def _rmsnorm_fwd(x, g, *, width=None, col=0, name):
    t = x.shape[0]
    w = x.shape[1] if width is None else width
    tm = _tile(t, 512)

    def body(x_ref, g_ref, o_ref):
        xv = x_ref[...].astype(F32)
        r = lax.rsqrt(jnp.mean(xv * xv, axis=-1, keepdims=True) + RMS_EPS)
        o_ref[...] = (xv * r * g_ref[...]).astype(o_ref.dtype)

    return pl.pallas_call(
        body,
        out_shape=jax.ShapeDtypeStruct((t, w), BF16),
        grid=(t // tm,),
        in_specs=[pl.BlockSpec((tm, w), lambda i: (i, col)), pl.BlockSpec((1, w), lambda i: (0, 0))],
        out_specs=pl.BlockSpec((tm, w), lambda i: (i, 0)),
        compiler_params=_cp("parallel"),
        name=name,
    )(x, g.reshape(1, w))


def _rmsnorm_bwd(x, g, dy, res=None, *, width=None, col=0, name):
    t = x.shape[0]
    w = x.shape[1] if width is None else width
    tm = _tile(t, 256)
    has_res = res is not None

    def body(*refs):
        x_ref, g_ref, dy_ref = refs[:3]
        res_ref = refs[3] if has_res else None
        dx_ref, dg_ref = refs[-2], refs[-1]
        xv = x_ref[...].astype(F32)
        r = lax.rsqrt(jnp.mean(xv * xv, axis=-1, keepdims=True) + RMS_EPS)
        xh = xv * r
        dyv = dy_ref[...].astype(F32)
        dxh = dyv * g_ref[...]
        dx = r * (dxh - xh * jnp.mean(dxh * xh, axis=-1, keepdims=True))
        if has_res:
            dx = dx + res_ref[...]
        dx_ref[...] = dx

        @pl.when(pl.program_id(0) == 0)
        def _():
            dg_ref[...] = jnp.zeros_like(dg_ref)

        dg_ref[...] += jnp.sum(dyv * xh, axis=0, keepdims=True)

    in_specs = [pl.BlockSpec((tm, w), lambda i: (i, col)), pl.BlockSpec((1, w), lambda i: (0, 0)), pl.BlockSpec((tm, w), lambda i: (i, 0))]
    args = [x, g.reshape(1, w), dy]
    if has_res:
        in_specs.append(pl.BlockSpec((tm, w), lambda i: (i, 0)))
        args.append(res)
    return pl.pallas_call(
        body,
        out_shape=(jax.ShapeDtypeStruct((t, w), F32), jax.ShapeDtypeStruct((1, w), F32)),
        grid=(t // tm,),
        in_specs=in_specs,
        out_specs=(pl.BlockSpec((tm, w), lambda i: (i, 0)), pl.BlockSpec((1, w), lambda i: (0, 0))),
        compiler_params=_cp("arbitrary"),
        name=name,
    )(*args)


def _final_loss(x, g, target, *, name):
    t, d = x.shape
    tm = _tile(t, 256)

    def body(x_ref, g_ref, t_ref, loss_ref, dx_ref, dg_ref):
        xv = x_ref[...]
        r = lax.rsqrt(jnp.mean(xv * xv, axis=-1, keepdims=True) + RMS_EPS)
        xh = xv * r
        gv = g_ref[...]
        err = xh * gv - t_ref[...]
        dyv = err * (1.0 / d)
        dxh = dyv * gv
        dx_ref[...] = r * (dxh - xh * jnp.mean(dxh * xh, axis=-1, keepdims=True))

        @pl.when(pl.program_id(0) == 0)
        def _():
            dg_ref[...] = jnp.zeros_like(dg_ref)
            loss_ref[...] = jnp.zeros_like(loss_ref)

        dg_ref[...] += jnp.sum(dyv * xh, axis=0, keepdims=True)
        row = jnp.sum(err * err, axis=-1, keepdims=True) * (0.5 / d)
        loss_ref[...] += jnp.sum(row, axis=0, keepdims=True)

    return pl.pallas_call(
        body,
        out_shape=(jax.ShapeDtypeStruct((1, 1), F32), jax.ShapeDtypeStruct((t, d), F32), jax.ShapeDtypeStruct((1, d), F32)),
        grid=(t // tm,),
        in_specs=[pl.BlockSpec((tm, d), lambda i: (i, 0)), pl.BlockSpec((1, d), lambda i: (0, 0)), pl.BlockSpec((tm, d), lambda i: (i, 0))],
        out_specs=(pl.BlockSpec((1, 1), lambda i: (0, 0)), pl.BlockSpec((tm, d), lambda i: (i, 0)), pl.BlockSpec((1, d), lambda i: (0, 0))),
        compiler_params=_cp("arbitrary"),
        name=name,
    )(x, g.reshape(1, d), target)


_GELU_C = math.sqrt(2.0 / math.pi)
_GELU_A = 0.044715


def _gelu(x):
    t = jnp.tanh(_GELU_C * (x + _GELU_A * x * x * x))
    return 0.5 * x * (1.0 + t), t


def _dgelu(x, t):
    return 0.5 * (1.0 + t) + 0.5 * x * (1.0 - t * t) * (_GELU_C * (1.0 + 3.0 * _GELU_A * x * x))


def _spatial_mask():
    ci = lax.broadcasted_iota(jnp.int32, (GMLP_BLOCK, GMLP_BLOCK), 0) // CHUNK
    cj = lax.broadcasted_iota(jnp.int32, (GMLP_BLOCK, GMLP_BLOCK), 1) // CHUNK
    return (cj <= ci).astype(F32)


def _shift_down(x, k):
    rows = lax.broadcasted_iota(jnp.int32, x.shape, 0)
    return jnp.where(rows >= k, pltpu.roll(x, k, 0), 0.0)


def _shift_up(x, k):
    n = x.shape[0]
    rows = lax.broadcasted_iota(jnp.int32, x.shape, 0)
    return jnp.where(rows < n - k, pltpu.roll(x, n - k, 0), 0.0)


def _gmlp_conv_fwd(z, ws, bs, gn, cw, *, seq, name):
    t = z.shape[0]
    g_n = ws.shape[0]
    nb = seq // GMLP_BLOCK
    L = LANES

    def body(z_ref, ws_ref, bs_ref, gn_ref, cw_ref, y_ref):
        u = z_ref[:, 0:L]
        v = z_ref[:, L:2 * L]
        gu, _ = _gelu(u)
        gv, _ = _gelu(v)
        r = lax.rsqrt(jnp.mean(gv * gv, axis=-1, keepdims=True) + RMS_EPS)
        vg = (gv * r * gn_ref[0]).astype(BF16)
        wm = (ws_ref[0] * _spatial_mask()).astype(BF16)
        bias = bs_ref[0]
        for n in range(nb):
            sl = slice(n * GMLP_BLOCK, (n + 1) * GMLP_BLOCK)
            mixed = jnp.dot(wm, vg[sl], preferred_element_type=F32) + bias
            y_ref[sl, 0:L] = (gu[sl] * mixed).astype(y_ref.dtype)
        bg = z_ref[:, 2 * L:3 * L]
        zc = z_ref[:, 3 * L:4 * L] * z_ref[:, 4 * L:5 * L]
        w0, w1, w2 = cw_ref[0, 0:1, :], cw_ref[0, 1:2, :], cw_ref[0, 2:3, :]
        conv = w0 * _shift_down(zc, 2) + w1 * _shift_down(zc, 1) + w2 * zc
        y_ref[:, L:2 * L] = (bg * conv).astype(y_ref.dtype)

    return pl.pallas_call(
        body,
        out_shape=jax.ShapeDtypeStruct((t, 2 * L * g_n), BF16),
        grid=(g_n, t // seq),
        in_specs=[
            pl.BlockSpec((seq, 5 * L), lambda g, b: (b, g)),
            pl.BlockSpec((1, GMLP_BLOCK, GMLP_BLOCK), lambda g, b: (g, 0, 0)),
            pl.BlockSpec((1, GMLP_BLOCK, 1), lambda g, b: (g, 0, 0)),
            pl.BlockSpec((1, 1, L), lambda g, b: (g, 0, 0)),
            pl.BlockSpec((1, 3, L), lambda g, b: (g, 0, 0)),
        ],
        out_specs=pl.BlockSpec((seq, 2 * L), lambda g, b: (b, g)),
        compiler_params=_cp("parallel", "parallel"),
        name=name,
    )(z, ws, bs, gn, cw)


def _gmlp_conv_bwd(z, dy, ws, bs, gn, cw, *, seq, name):
    t = z.shape[0]
    g_n = ws.shape[0]
    nb = seq // GMLP_BLOCK
    L = LANES
    dims_nt = (((1,), (1,)), ((), ()))
    dims_tn = (((0,), (0,)), ((), ()))

    def body(z_ref, dy_ref, ws_ref, bs_ref, gn_ref, cw_ref, dz_ref, dws_ref, dbs_ref, dgn_ref, dcw_ref, dvg_ref):
        @pl.when(pl.program_id(1) == 0)
        def _():
            dws_ref[...] = jnp.zeros_like(dws_ref)
            dbs_ref[...] = jnp.zeros_like(dbs_ref)
            dgn_ref[...] = jnp.zeros_like(dgn_ref)
            dcw_ref[...] = jnp.zeros_like(dcw_ref)

        u = z_ref[:, 0:L]
        v = z_ref[:, L:2 * L]
        gu, tu = _gelu(u)
        gv, tv = _gelu(v)
        r = lax.rsqrt(jnp.mean(gv * gv, axis=-1, keepdims=True) + RMS_EPS)
        gvh = gv * r
        gnv = gn_ref[0]
        vg = (gvh * gnv).astype(BF16)
        mask = _spatial_mask()
        wm = (ws_ref[0] * mask).astype(BF16)
        bias = bs_ref[0]
        dya = dy_ref[:, 0:L].astype(F32)
        dws = jnp.zeros((GMLP_BLOCK, GMLP_BLOCK), F32)
        dbs = jnp.zeros((GMLP_BLOCK, 1), F32)
        for n in range(nb):
            sl = slice(n * GMLP_BLOCK, (n + 1) * GMLP_BLOCK)
            mixed = jnp.dot(wm, vg[sl], preferred_element_type=F32) + bias
            dmixed = dya[sl] * gu[sl]
            dgu = dya[sl] * mixed
            dz_ref[sl, 0:L] = (dgu * _dgelu(u[sl], tu[sl])).astype(dz_ref.dtype)
            dmb = dmixed.astype(BF16)
            dws = dws + lax.dot_general(dmb, vg[sl], dims_nt, preferred_element_type=F32)
            dbs = dbs + jnp.sum(dmixed, axis=1, keepdims=True)
            dvg_ref[sl, :] = lax.dot_general(wm, dmb, dims_tn, preferred_element_type=F32)
        dws_ref[0] += dws * mask
        dbs_ref[0] += dbs
        dvg = dvg_ref[...]
        dgn_ref[0] += jnp.sum(dvg * gvh, axis=0, keepdims=True)
        dvh = dvg * gnv
        dgv = r * (dvh - gvh * jnp.mean(dvh * gvh, axis=-1, keepdims=True))
        dz_ref[:, L:2 * L] = (dgv * _dgelu(v, tv)).astype(dz_ref.dtype)

        bg = z_ref[:, 2 * L:3 * L]
        cg = z_ref[:, 3 * L:4 * L]
        hh = z_ref[:, 4 * L:5 * L]
        zc = cg * hh
        w0, w1, w2 = cw_ref[0, 0:1, :], cw_ref[0, 1:2, :], cw_ref[0, 2:3, :]
        z1 = _shift_down(zc, 1)
        z2 = _shift_down(zc, 2)
        conv = w0 * z2 + w1 * z1 + w2 * zc
        dyb = dy_ref[:, L:2 * L].astype(F32)
        dconv = dyb * bg
        dz_ref[:, 2 * L:3 * L] = (dyb * conv).astype(dz_ref.dtype)
        dzc = w2 * dconv + w1 * _shift_up(dconv, 1) + w0 * _shift_up(dconv, 2)
        dz_ref[:, 3 * L:4 * L] = (dzc * hh).astype(dz_ref.dtype)
        dz_ref[:, 4 * L:5 * L] = (dzc * cg).astype(dz_ref.dtype)
        dcw_ref[0, 0:1, :] += jnp.sum(dconv * z2, axis=0, keepdims=True)
        dcw_ref[0, 1:2, :] += jnp.sum(dconv * z1, axis=0, keepdims=True)
        dcw_ref[0, 2:3, :] += jnp.sum(dconv * zc, axis=0, keepdims=True)

    return pl.pallas_call(
        body,
        out_shape=(
            jax.ShapeDtypeStruct((t, 5 * L * g_n), BF16),
            jax.ShapeDtypeStruct(ws.shape, F32),
            jax.ShapeDtypeStruct(bs.shape, F32),
            jax.ShapeDtypeStruct(gn.shape, F32),
            jax.ShapeDtypeStruct(cw.shape, F32),
        ),
        grid=(g_n, t // seq),
        in_specs=[
            pl.BlockSpec((seq, 5 * L), lambda g, b: (b, g)),
            pl.BlockSpec((seq, 2 * L), lambda g, b: (b, g)),
            pl.BlockSpec((1, GMLP_BLOCK, GMLP_BLOCK), lambda g, b: (g, 0, 0)),
            pl.BlockSpec((1, GMLP_BLOCK, 1), lambda g, b: (g, 0, 0)),
            pl.BlockSpec((1, 1, L), lambda g, b: (g, 0, 0)),
            pl.BlockSpec((1, 3, L), lambda g, b: (g, 0, 0)),
        ],
        out_specs=(
            pl.BlockSpec((seq, 5 * L), lambda g, b: (b, g)),
            pl.BlockSpec((1, GMLP_BLOCK, GMLP_BLOCK), lambda g, b: (g, 0, 0)),
            pl.BlockSpec((1, GMLP_BLOCK, 1), lambda g, b: (g, 0, 0)),
            pl.BlockSpec((1, 1, L), lambda g, b: (g, 0, 0)),
            pl.BlockSpec((1, 3, L), lambda g, b: (g, 0, 0)),
        ),
        scratch_shapes=[pltpu.VMEM((seq, L), F32)],
        compiler_params=_cp("parallel", "arbitrary"),
        name=name,
    )(z, dy, ws, bs, gn, cw)


def _rope_rows(x, c, s1, s2, *, transpose, out_dtype, name):
    t, w = x.shape
    tm = _tile(t, 512)
    only = w == LANES
    fn = _rope_t if transpose else _rope

    def body(x_ref, c_ref, s1_ref, s2_ref, o_ref):
        cv, s1v, s2v = c_ref[...], s1_ref[...], s2_ref[...]
        if only:
            o_ref[...] = fn(x_ref[...].astype(F32), cv, s1v, s2v).astype(o_ref.dtype)
        else:
            for h0 in range(0, w, 2 * LANES):
                o_ref[:, h0:h0 + LANES] = x_ref[:, h0:h0 + LANES].astype(o_ref.dtype)
                o_ref[:, h0 + LANES:h0 + 2 * LANES] = fn(x_ref[:, h0 + LANES:h0 + 2 * LANES].astype(F32), cv, s1v, s2v).astype(o_ref.dtype)

    tab = pl.BlockSpec((tm, LANES), lambda i: (i, 0))
    return pl.pallas_call(
        body,
        out_shape=jax.ShapeDtypeStruct((t, w), out_dtype),
        grid=(t // tm,),
        in_specs=[pl.BlockSpec((tm, w), lambda i: (i, 0)), tab, tab, tab],
        out_specs=pl.BlockSpec((tm, w), lambda i: (i, 0)),
        compiler_params=_cp("parallel"),
        name=name,
    )(x, c, s1, s2)


_NT = (((1,), (1,)), ((), ()))
_TN = (((0,), (0,)), ((), ()))


def _chunk_mask(tq):
    qc = lax.broadcasted_iota(jnp.int32, (tq, tq), 0) // CHUNK
    kc = lax.broadcasted_iota(jnp.int32, (tq, tq), 1) // CHUNK
    return kc <= qc


MLA_BLOCK = 512


def _mla_fwd(q, kv, kr, *, seq, heads, name):
    t = q.shape[0]
    tq = _tile(seq, MLA_BLOCK)
    nq = seq // tq
    scale = (C_NOPE + C_ROPE) ** -0.5

    def body(q_ref, kv_ref, kr_ref, o_ref, lse_ref, kcat_ref):
        i = pl.program_id(2)

        @pl.when(i == 0)
        def _():
            kcat_ref[:, :LANES] = kv_ref[:, :LANES]
            kcat_ref[:, LANES:] = kr_ref[...]

        qv = q_ref[...]

        def step(j, carry, diagonal):
            m_run, l_run, acc = carry
            rows = pl.ds(pl.multiple_of(j * tq, tq), tq)
            s = lax.dot_general(qv, kcat_ref[rows, :], _NT, preferred_element_type=F32) * scale
            if diagonal:
                s = jnp.where(_chunk_mask(tq), s, NEG_BIG)
            m_new = jnp.maximum(m_run, jnp.max(s, axis=-1, keepdims=True))
            alpha = jnp.exp(m_run - m_new)
            p = jnp.exp(s - m_new)
            l_new = alpha * l_run + jnp.sum(p, axis=-1, keepdims=True)
            acc = alpha * acc + jnp.dot(p.astype(BF16), kv_ref[rows, LANES:], preferred_element_type=F32)
            return m_new, l_new, acc

        init = (jnp.full((tq, 1), NEG_BIG, F32), jnp.zeros((tq, 1), F32), jnp.zeros((tq, C_V), F32))
        carry = lax.fori_loop(0, i, lambda j, c: step(j, c, False), init)
        m_run, l_run, acc = step(i, carry, True)
        o_ref[...] = (acc / l_run).astype(o_ref.dtype)
        lse_ref[...] = m_run + jnp.log(l_run)

    return pl.pallas_call(
        body,
        out_shape=(jax.ShapeDtypeStruct((t, heads * C_V), BF16), jax.ShapeDtypeStruct((heads, t, 1), F32)),
        grid=(t // seq, heads, nq),
        in_specs=[
            pl.BlockSpec((tq, 2 * LANES), lambda b, h, i: (b * nq + i, h)),
            pl.BlockSpec((seq, 2 * LANES), lambda b, h, i: (b, h)),
            pl.BlockSpec((seq, LANES), lambda b, h, i: (b, 0)),
        ],
        out_specs=(
            pl.BlockSpec((tq, C_V), lambda b, h, i: (b * nq + i, h)),
            pl.BlockSpec((None, tq, 1), lambda b, h, i: (h, b * nq + i, 0)),
        ),
        scratch_shapes=[pltpu.VMEM((seq, 2 * LANES), BF16)],
        compiler_params=_cp("parallel", "parallel", "arbitrary"),
        name=name,
    )(q, kv, kr)


def _mla_bwd(q, kv, kr, do, o, lse, *, seq, heads, name):
    t = q.shape[0]
    tq = _tile(seq, MLA_BLOCK)
    nq = seq // tq
    scale = (C_NOPE + C_ROPE) ** -0.5

    def body(q_ref, kv_ref, kr_ref, do_ref, o_ref, lse_ref, dq_ref, dkv_ref, dkr_ref, kcat_ref, dk_ref, dv_ref):
        h, i = pl.program_id(1), pl.program_id(2)

        @pl.when(i == 0)
        def _():
            kcat_ref[:, :LANES] = kv_ref[:, :LANES]
            kcat_ref[:, LANES:] = kr_ref[...]
            dk_ref[...] = jnp.zeros_like(dk_ref)
            dv_ref[...] = jnp.zeros_like(dv_ref)

        @pl.when((i == 0) & (h == 0))
        def _():
            dkr_ref[...] = jnp.zeros_like(dkr_ref)

        qv = q_ref[...]
        dov = do_ref[...]
        lse_v = lse_ref[...]
        delta = jnp.sum(dov.astype(F32) * o_ref[...].astype(F32), axis=-1, keepdims=True)

        def step(j, dq, diagonal):
            rows = pl.ds(pl.multiple_of(j * tq, tq), tq)
            kj = kcat_ref[rows, :]
            s = lax.dot_general(qv, kj, _NT, preferred_element_type=F32) * scale
            if diagonal:
                s = jnp.where(_chunk_mask(tq), s, NEG_BIG)
            p = jnp.exp(s - lse_v)
            dp = lax.dot_general(dov, kv_ref[rows, LANES:], _NT, preferred_element_type=F32)
            ds = (p * (dp - delta) * scale).astype(BF16)
            dk_ref[rows, :] += lax.dot_general(ds, qv, _TN, preferred_element_type=F32)
            dv_ref[rows, :] += lax.dot_general(p.astype(BF16), dov, _TN, preferred_element_type=F32)
            return dq + jnp.dot(ds, kj, preferred_element_type=F32)

        dq = lax.fori_loop(0, i, lambda j, c: step(j, c, False), jnp.zeros((tq, 2 * LANES), F32))
        dq_ref[...] = step(i, dq, True)

        @pl.when(i == nq - 1)
        def _():
            dkv_ref[:, :LANES] = dk_ref[:, :LANES].astype(dkv_ref.dtype)
            dkv_ref[:, LANES:] = dv_ref[...].astype(dkv_ref.dtype)
            dkr_ref[...] += dk_ref[:, LANES:]

    return pl.pallas_call(
        body,
        out_shape=(
            jax.ShapeDtypeStruct((t, heads * 2 * LANES), F32),
            jax.ShapeDtypeStruct((t, heads * 2 * LANES), BF16),
            jax.ShapeDtypeStruct((t, LANES), F32),
        ),
        grid=(t // seq, heads, nq),
        in_specs=[
            pl.BlockSpec((tq, 2 * LANES), lambda b, h, i: (b * nq + i, h)),
            pl.BlockSpec((seq, 2 * LANES), lambda b, h, i: (b, h)),
            pl.BlockSpec((seq, LANES), lambda b, h, i: (b, 0)),
            pl.BlockSpec((tq, C_V), lambda b, h, i: (b * nq + i, h)),
            pl.BlockSpec((tq, C_V), lambda b, h, i: (b * nq + i, h)),
            pl.BlockSpec((None, tq, 1), lambda b, h, i: (h, b * nq + i, 0)),
        ],
        out_specs=(
            pl.BlockSpec((tq, 2 * LANES), lambda b, h, i: (b * nq + i, h)),
            pl.BlockSpec((seq, 2 * LANES), lambda b, h, i: (b, h)),
            pl.BlockSpec((seq, LANES), lambda b, h, i: (b, 0)),
        ),
        scratch_shapes=[pltpu.VMEM((seq, 2 * LANES), BF16), pltpu.VMEM((seq, 2 * LANES), F32), pltpu.VMEM((seq, C_V), F32)],
        compiler_params=_cp("parallel", "arbitrary", "arbitrary"),
        name=name,
    )(q, kv, kr, do, o, lse)


def _softmax_rows(s):
    e = jnp.exp(s - jnp.max(s, axis=-1, keepdims=True))
    return e / jnp.sum(e, axis=-1, keepdims=True)


def _mem_fwd(q, k, v, *, seq, n_mem, name):
    t, d = q.shape
    hd = d // MEM_HEADS
    tq = _tile(seq, 512)
    nq = seq // tq
    scale = hd ** -0.5

    def body(q_ref, k_ref, v_ref, o_ref):
        s = lax.dot_general(q_ref[...], k_ref[...], _NT, preferred_element_type=F32) * scale
        p = _softmax_rows(s)
        o_ref[...] = jnp.dot(p.astype(BF16), v_ref[...], preferred_element_type=F32).astype(o_ref.dtype)

    return pl.pallas_call(
        body,
        out_shape=jax.ShapeDtypeStruct((t, d), BF16),
        grid=(t // seq, MEM_HEADS, nq),
        in_specs=[
            pl.BlockSpec((tq, hd), lambda b, h, i: (b * nq + i, h)),
            pl.BlockSpec((n_mem, hd), lambda b, h, i: (b, h)),
            pl.BlockSpec((n_mem, hd), lambda b, h, i: (b, h)),
        ],
        out_specs=pl.BlockSpec((tq, hd), lambda b, h, i: (b * nq + i, h)),
        compiler_params=_cp("parallel", "parallel", "arbitrary"),
        name=name,
    )(q, k, v)


def _mem_bwd(q, k, v, do, *, seq, n_mem, name):
    t, d = q.shape
    hd = d // MEM_HEADS
    tq = _tile(seq, 512)
    nq = seq // tq
    scale = hd ** -0.5

    def body(q_ref, k_ref, v_ref, do_ref, dq_ref, dk_ref, dv_ref):
        @pl.when(pl.program_id(2) == 0)
        def _():
            dk_ref[...] = jnp.zeros_like(dk_ref)
            dv_ref[...] = jnp.zeros_like(dv_ref)

        qv, kv_, vv, dov = q_ref[...], k_ref[...], v_ref[...], do_ref[...]
        p = _softmax_rows(lax.dot_general(qv, kv_, _NT, preferred_element_type=F32) * scale)
        dp = lax.dot_general(dov, vv, _NT, preferred_element_type=F32)
        ds = (p * (dp - jnp.sum(dp * p, axis=-1, keepdims=True)) * scale).astype(BF16)
        dq_ref[...] = jnp.dot(ds, kv_, preferred_element_type=F32).astype(dq_ref.dtype)
        dk_ref[...] += lax.dot_general(ds, qv, _TN, preferred_element_type=F32)
        dv_ref[...] += lax.dot_general(p.astype(BF16), dov, _TN, preferred_element_type=F32)

    return pl.pallas_call(
        body,
        out_shape=(
            jax.ShapeDtypeStruct((t, d), BF16),
            jax.ShapeDtypeStruct(k.shape, F32),
            jax.ShapeDtypeStruct(v.shape, F32),
        ),
        grid=(t // seq, MEM_HEADS, nq),
        in_specs=[
            pl.BlockSpec((tq, hd), lambda b, h, i: (b * nq + i, h)),
            pl.BlockSpec((n_mem, hd), lambda b, h, i: (b, h)),
            pl.BlockSpec((n_mem, hd), lambda b, h, i: (b, h)),
            pl.BlockSpec((tq, hd), lambda b, h, i: (b * nq + i, h)),
        ],
        out_specs=(
            pl.BlockSpec((tq, hd), lambda b, h, i: (b * nq + i, h)),
            pl.BlockSpec((n_mem, hd), lambda b, h, i: (b, h)),
            pl.BlockSpec((n_mem, hd), lambda b, h, i: (b, h)),
        ),
        compiler_params=_cp("parallel", "parallel", "arbitrary"),
        name=name,
    )(q, k, v, do)


def _me():
    return lax.axis_index("x"), lax.axis_index("y"), lax.axis_index("c")


def _dev_index(px, py, pc):
    return 4 * px + 2 * py + pc


def _chunk(ref, idx, axis, size):
    start = pl.multiple_of(idx * size, size)
    return ref.at[pl.ds(start, size), :] if axis == 0 else ref.at[:, pl.ds(start, size)]


def _all_gather_small(v):
    m, n = v.shape

    def body(x_ref, out_ref, send_sems, recv_sems, local_sem):
        x, y, c = _me()
        me, sibling = (x, y, c), (x, y, 1 - c)
        chips = [(1 - x, y), (x, 1 - y), (1 - x, 1 - y)]

        def rows(px, py, pc):
            return out_ref.at[_dev_index(px, py, pc)]

        def copy(k, block, to, src=None):
            return pltpu.make_async_remote_copy(
                src_ref=rows(*block) if src is None else src, dst_ref=rows(*block),
                send_sem=send_sems.at[k], recv_sem=recv_sems.at[k], device_id=to, device_id_type=MESH)

        mine = pltpu.make_async_copy(x_ref, rows(*me), local_sem)
        mine.start()
        first = [copy(0, me, sibling, src=x_ref)]
        first += [copy(1 + j, me, (*chip, c), src=x_ref) for j, chip in enumerate(chips)]
        for cp in first:
            cp.start()
        passed = [copy(4 + j, (*chip, c), sibling) for j, chip in enumerate(chips)]
        for j, chip in enumerate(chips):
            copy(1 + j, (*chip, c), me).wait_recv()
            passed[j].start()
        copy(0, sibling, me).wait_recv()
        for j, chip in enumerate(chips):
            copy(4 + j, (*chip, 1 - c), me).wait_recv()
        for cp in first + passed:
            cp.wait_send()
        mine.wait()

    return pl.pallas_call(
        body,
        out_shape=jax.ShapeDtypeStruct((N_DEV, m, n), v.dtype),
        in_specs=[pl.BlockSpec(memory_space=pltpu.VMEM)],
        out_specs=pl.BlockSpec(memory_space=pltpu.VMEM),
        scratch_shapes=[pltpu.SemaphoreType.DMA((7,)), pltpu.SemaphoreType.DMA((7,)), pltpu.SemaphoreType.DMA],
        name="all_gather_small",
    )(v)


def _cast_place(w, axis, dev_arr, *, name):
    r, c = w.shape
    tr = _tile(r, 256)
    nr = r // tr
    full = (N_DEV * r, c) if axis == 0 else (r, N_DEV * c)

    def body(dev_ref, w_ref, o_ref):
        o_ref[...] = w_ref[...].astype(o_ref.dtype)

    if axis == 0:
        out_spec = pl.BlockSpec((tr, c), lambda i, dev: (dev[0] * nr + i, 0))
    else:
        out_spec = pl.BlockSpec((tr, c), lambda i, dev: (i, dev[0]))
    return pl.pallas_call(
        body,
        out_shape=jax.ShapeDtypeStruct(full, BF16),
        grid_spec=pltpu.PrefetchScalarGridSpec(
            num_scalar_prefetch=1, grid=(nr,), in_specs=[pl.BlockSpec((tr, c), lambda i, dev: (i, 0))], out_specs=out_spec),
        compiler_params=_cp("parallel"),
        name=name,
    )(dev_arr, w)


_ANY = pl.BlockSpec(memory_space=pl.ANY)
_SEM = pl.BlockSpec(memory_space=pltpu.SEMAPHORE)
_EFFECT = pltpu.SideEffectType.DATAFLOW_SIDE_EFFECTING
N_PEER_CHIPS = 3


def _other_chips(x, y):
    return [((1 - x if (j >> 1) & 1 else x), (1 - y if j & 1 else y)) for j in range(1, 4)]


def _remote(src, dst, send_sem, recv_sem, to):
    return pltpu.make_async_remote_copy(src_ref=src, dst_ref=dst, send_sem=send_sem, recv_sem=recv_sem, device_id=to, device_id_type=MESH)


def _plan_gather_ici(axes, sizes):
    def plan(refs, send_sems, recv_sems):
        x, y, c = _me()
        out = []
        for t in range(len(axes)):
            for j, chip in enumerate(_other_chips(x, y)):
                mine = _chunk(refs[t], _dev_index(x, y, c), axes[t], sizes[t])
                theirs = _chunk(refs[t], _dev_index(*chip, c), axes[t], sizes[t])
                out.append((_remote(mine, mine, send_sems[j], recv_sems[j], (*chip, c)), _remote(theirs, theirs, send_sems[j], recv_sems[j], (*chip, c))))
        return out
    return plan


def _plan_gather_d2d(axes, sizes):
    def plan(refs, send_sems, recv_sems):
        x, y, c = _me()
        out = []
        for t in range(len(axes)):
            for k in range(4):
                have = _chunk(refs[t], 2 * k + c, axes[t], sizes[t])
                get = _chunk(refs[t], 2 * k + 1 - c, axes[t], sizes[t])
                out.append((_remote(have, have, send_sems[0], recv_sems[0], (x, y, 1 - c)), _remote(get, get, send_sems[0], recv_sems[0], (x, y, 1 - c))))
        return out
    return plan


def _plan_scatter_d2d(axes, sizes):
    n = len(axes)

    def plan(refs, send_sems, recv_sems):
        x, y, c = _me()
        out = []
        for t in range(n):
            for k in range(4):
                give = _chunk(refs[t], 2 * k + 1 - c, axes[t], sizes[t])
                land = _chunk(refs[n + t], k, axes[t], sizes[t])
                cp = _remote(give, land, send_sems[0], recv_sems[0], (x, y, 1 - c))
                out.append((cp, cp))
        return out
    return plan


def _plan_scatter_ici(axes, sizes):
    n = len(axes)

    def plan(refs, send_sems, recv_sems):
        x, y, c = _me()
        out = []
        for t in range(n):
            for j, (px, py) in enumerate(_other_chips(x, y)):
                send = _remote(_chunk(refs[t], 2 * px + py, axes[t], sizes[t]), _chunk(refs[n + t], 2 * x + y, axes[t], sizes[t]),
                               send_sems[j], recv_sems[j], (px, py, c))
                recv = _remote(_chunk(refs[t], 2 * x + y, axes[t], sizes[t]), _chunk(refs[n + t], 2 * px + py, axes[t], sizes[t]),
                               send_sems[j], recv_sems[j], (px, py, c))
                out.append((send, recv))
        return out
    return plan


def _exchange_start(arrays, n_sems, plan, *, name):
    n = len(arrays)

    def body(*refs):
        send_sems, recv_sems = list(refs[n:n + n_sems]), list(refs[n + n_sems:n + 2 * n_sems])
        for send, _ in plan(refs[:n], send_sems, recv_sems):
            send.start()
        refs[-1][...] = jnp.zeros_like(refs[-1])

    outs = pl.pallas_call(
        body,
        out_shape=(*([pltpu.SemaphoreType.DMA(())] * (2 * n_sems)),
                   *[jax.ShapeDtypeStruct(a.shape, a.dtype) for a in arrays], jax.ShapeDtypeStruct((8, LANES), F32)),
        in_specs=[_ANY] * n,
        out_specs=(*([_SEM] * (2 * n_sems)), *([_ANY] * n), pl.BlockSpec(memory_space=pltpu.VMEM)),
        input_output_aliases={t: 2 * n_sems + t for t in range(n)},
        compiler_params=pltpu.CompilerParams(has_side_effects=_EFFECT),
        name=name,
    )(*arrays)
    return list(outs[:2 * n_sems]), list(outs[2 * n_sems:2 * n_sems + n]), outs[-1]


def _exchange_wait(sems, arrays, plan, after, *, name):
    n = len(arrays)
    n_sems = len(sems) // 2

    def body(*refs):
        send_sems, recv_sems = list(refs[n:n + n_sems]), list(refs[n + n_sems:n + 2 * n_sems])
        for send, recv in plan(refs[:n], send_sems, recv_sems):
            send.wait_send()
            recv.wait_recv()
        refs[-1][...] = jnp.zeros_like(refs[-1])

    outs = pl.pallas_call(
        body,
        out_shape=(*[jax.ShapeDtypeStruct(a.shape, a.dtype) for a in arrays], jax.ShapeDtypeStruct((8, LANES), F32)),
        in_specs=[_ANY] * n + [_SEM] * (2 * n_sems) + [_ANY],
        out_specs=(*([_ANY] * n), pl.BlockSpec(memory_space=pltpu.VMEM)),
        input_output_aliases={t: t for t in range(n)},
        compiler_params=pltpu.CompilerParams(has_side_effects=_EFFECT),
        name=name,
    )(*arrays, *sems, after)
    return list(outs[:n]), outs[-1]


def _add_pair(g, got, axis, core_arr, *, name):
    r, c = got.shape
    if axis == 0:
        rows, cols = r // 4, c
    else:
        rows, cols = r, c // 4
    tr = _tile(rows, 512)
    nr = rows // tr

    def body(core_ref, g_ref, got_ref, o_ref):
        o_ref[...] = (g_ref[...].astype(F32) + got_ref[...].astype(F32)).astype(o_ref.dtype)

    if axis == 0:
        g_spec = pl.BlockSpec((tr, cols), lambda k, i, core: ((2 * k + core[0]) * nr + i, 0))
        spec = pl.BlockSpec((tr, cols), lambda k, i, core: (k * nr + i, 0))
    else:
        g_spec = pl.BlockSpec((tr, cols), lambda k, i, core: (i, 2 * k + core[0]))
        spec = pl.BlockSpec((tr, cols), lambda k, i, core: (i, k))
    return pl.pallas_call(
        body,
        out_shape=jax.ShapeDtypeStruct(got.shape, got.dtype),
        grid_spec=pltpu.PrefetchScalarGridSpec(num_scalar_prefetch=1, grid=(4, nr), in_specs=[g_spec, spec], out_specs=spec),
        compiler_params=_cp("parallel", "parallel"),
        name=name,
    )(core_arr, g, got)


def _adam_math(w, g, m, v):
    m = ADAM_B1 * m + (1.0 - ADAM_B1) * g
    v = ADAM_B2 * v + (1.0 - ADAM_B2) * (g * g)
    m_hat = m / (1.0 - ADAM_B1 ** ADAM_STEP)
    v_hat = v / (1.0 - ADAM_B2 ** ADAM_STEP)
    delta = -ADAM_LR * (m_hat / (jnp.sqrt(v_hat) + ADAM_EPS) + ADAM_WD * w)
    return delta, m, v


def _adamw(w, m, v, parts, axis, chip_arr, layers, prev, after, *, name):
    n_l, r, c = w.shape
    first, count = layers
    tr = _tile(r, 128)
    nr = r // tr
    n_chip = 4
    n_in = 4 + n_chip * count

    def body(*refs):
        w_ref, m_ref, v_ref = refs[1:4]
        p_refs = refs[4:n_in]
        g_ref, d_ref, nm_ref, nv_ref = refs[-4:]
        for l in range(count):
            @pl.when(pl.program_id(0) == l)
            def _(l=l):
                g = p_refs[n_chip * l][...].astype(F32)
                for k in range(1, n_chip):
                    g = g + p_refs[n_chip * l + k][...].astype(F32)
                delta, nm, nv = _adam_math(w_ref[...], g, m_ref[...], v_ref[...])
                g_ref[...] = g
                d_ref[...] = delta
                nm_ref[...] = nm
                nv_ref[...] = nv

    def part_spec(ll, k):
        if axis == 0:
            return pl.BlockSpec((tr, c), lambda l, i, chips: (chips[k] * nr + jnp.where(l == ll, i, 0), 0))
        return pl.BlockSpec((tr, c), lambda l, i, chips: (jnp.where(l == ll, i, 0), chips[k]))

    wspec = pl.BlockSpec((None, tr, c), lambda l, i, chips: (first + l, i, 0))
    pspecs = [part_spec(ll, k) for ll in range(count) for k in range(n_chip)]
    sds = jax.ShapeDtypeStruct(w.shape, F32)
    kept = [] if prev is None else list(prev)
    return pl.pallas_call(
        body,
        out_shape=(sds, sds, sds, sds),
        grid_spec=pltpu.PrefetchScalarGridSpec(
            num_scalar_prefetch=1, grid=(count, nr),
            in_specs=[wspec, wspec, wspec] + pspecs + [_ANY] * (1 + len(kept)), out_specs=(wspec, wspec, wspec, wspec)),
        input_output_aliases={n_in + 1 + k: k for k in range(len(kept))},
        compiler_params=_cp("arbitrary", "arbitrary"),
        name=name,
    )(chip_arr, w, m, v, *[parts[first + ll][0 if k == 0 else 1] for ll in range(count) for k in range(n_chip)], after, *kept)


def _adamw_small(w, m, v, parts, *, name):
    r, c = w.shape
    tr = _tile(r, 512)

    def body(w_ref, m_ref, v_ref, p_ref, g_ref, d_ref, nm_ref, nv_ref):
        g = p_ref[0]
        for s in range(1, N_DEV):
            g = g + p_ref[s]
        delta, nm, nv = _adam_math(w_ref[...], g, m_ref[...], v_ref[...])
        g_ref[...] = g
        d_ref[...] = delta
        nm_ref[...] = nm
        nv_ref[...] = nv

    spec = pl.BlockSpec((tr, c), lambda i: (i, 0))
    sds = jax.ShapeDtypeStruct(w.shape, F32)
    return pl.pallas_call(
        body,
        out_shape=(sds, sds, sds, sds),
        grid=(r // tr,),
        in_specs=[spec, spec, spec, pl.BlockSpec((N_DEV, tr, c), lambda i: (0, i, 0))],
        out_specs=(spec, spec, spec, spec),
        compiler_params=_cp("parallel"),
        name=name,
    )(w, m, v, parts)


def _pack(arrs):
    flat = jnp.concatenate([a.reshape(-1).astype(F32) for a in arrs])
    pad = (-flat.shape[0]) % (8 * LANES)
    return jnp.pad(flat, (0, pad)).reshape(-1, LANES)


def _unpack(packed, shapes):
    flat = packed.reshape(-1)
    out, off = [], 0
    for sh in shapes:
        n = math.prod(sh)
        out.append(flat[off:off + n].reshape(sh))
        off += n
    return out


def _unpack_gathered(packed, shapes, axis):
    flat = packed.reshape(N_DEV, -1)
    out, off = [], 0
    for sh, ax in zip(shapes, axis):
        n = math.prod(sh)
        blocks = flat[:, off:off + n].reshape((N_DEV, *sh))
        out.append(jnp.concatenate([blocks[d] for d in range(N_DEV)], axis=ax))
        off += n
    return out


def _my_block(full, axis, size):
    idx = _dev_index(*_me())
    return lax.dynamic_slice_in_dim(full, idx * size, size, axis)


def kernel(x, mem, positions, norm_mix_g, norm_mem_q_g, norm_mem_kv_g, norm_ffn_g, final_norm_g, ab_w_in, a_v_norm_g, a_w_s, a_b_s, b_conv_w, ab_w_out, c_w_in, c_q_norm_g, c_kv_norm_g, c_w_uq, c_w_ukv, c_w_out, m_wq, m_wk, m_wv, m_wo, f_w1, f_w2, loss_target, m_norm_mix_g, m_norm_mem_q_g, m_norm_mem_kv_g, m_norm_ffn_g, m_final_norm_g, m_ab_w_in, m_a_v_norm_g, m_a_w_s, m_a_b_s, m_b_conv_w, m_ab_w_out, m_c_w_in, m_c_q_norm_g, m_c_kv_norm_g, m_c_w_uq, m_c_w_ukv, m_c_w_out, m_m_wq, m_m_wk, m_m_wv, m_m_wo, m_f_w1, m_f_w2, v_norm_mix_g, v_norm_mem_q_g, v_norm_mem_kv_g, v_norm_ffn_g, v_final_norm_g, v_ab_w_in, v_a_v_norm_g, v_a_w_s, v_a_b_s, v_b_conv_w, v_ab_w_out, v_c_w_in, v_c_q_norm_g, v_c_kv_norm_g, v_c_w_uq, v_c_w_ukv, v_c_w_out, v_m_wq, v_m_wk, v_m_wv, v_m_wo, v_f_w1, v_f_w2):
    bl, seq, d = x.shape
    t = bl * seq
    n_mem = mem.shape[1]
    depth = norm_mix_g.shape[0]
    n_even, n_odd = ab_w_in.shape[0], c_w_in.shape[0]
    a_width = a_v_norm_g.shape[1]
    groups = a_w_s.shape[1]
    heads = d // C_NOPE
    q_rank = c_q_norm_g.shape[1] * N_DEV
    kv_rank = c_kv_norm_g.shape[1] * N_DEV

    big = {
        "ab_w_in": (ab_w_in, m_ab_w_in, v_ab_w_in, 1), "ab_w_out": (ab_w_out, m_ab_w_out, v_ab_w_out, 0),
        "c_w_in": (c_w_in, m_c_w_in, v_c_w_in, 0), "c_w_uq": (c_w_uq, m_c_w_uq, v_c_w_uq, 1),
        "c_w_ukv": (c_w_ukv, m_c_w_ukv, v_c_w_ukv, 1), "c_w_out": (c_w_out, m_c_w_out, v_c_w_out, 0),
        "m_wq": (m_wq, m_m_wq, v_m_wq, 0), "m_wk": (m_wk, m_m_wk, v_m_wk, 0), "m_wv": (m_wv, m_m_wv, v_m_wv, 0),
        "m_wo": (m_wo, m_m_wo, v_m_wo, 0), "f_w1": (f_w1, m_f_w1, v_f_w1, 1), "f_w2": (f_w2, m_f_w2, v_f_w2, 0),
    }
    even_names = ["ab_w_in", "ab_w_out"]
    odd_names = ["c_w_in", "c_w_uq", "c_w_ukv", "c_w_out"]
    all_names = ["m_wq", "m_wk", "m_wv", "m_wo", "f_w1", "f_w2"]

    def layer_names(layer):
        return (even_names if layer % 2 == 0 else odd_names) + all_names

    def sub(name, layer):
        return layer // 2 if name in even_names or name in odd_names else layer

    split_small = [(c_q_norm_g, m_c_q_norm_g, v_c_q_norm_g, 1), (c_kv_norm_g, m_c_kv_norm_g, v_c_kv_norm_g, 1), (b_conv_w, m_b_conv_w, v_b_conv_w, 2)]
    shard_shapes = [s[0].shape for s in split_small] * 3
    shard_axes = [s[3] for s in split_small] * 3
    gathered = _all_gather_small(_pack([s[0] for s in split_small] + [s[1] for s in split_small] + [s[2] for s in split_small]))
    whole = _unpack_gathered(gathered, shard_shapes, shard_axes)
    q_norm_full, kv_norm_full, conv_full = whole[0:3]
    small_w = [norm_mix_g, norm_mem_q_g, norm_mem_kv_g, norm_ffn_g, final_norm_g, a_v_norm_g, a_w_s, a_b_s] + whole[0:3]
    small_m = [m_norm_mix_g, m_norm_mem_q_g, m_norm_mem_kv_g, m_norm_ffn_g, m_final_norm_g, m_a_v_norm_g, m_a_w_s, m_a_b_s] + whole[3:6]
    small_v = [v_norm_mix_g, v_norm_mem_q_g, v_norm_mem_kv_g, v_norm_ffn_g, v_final_norm_g, v_a_v_norm_g, v_a_w_s, v_a_b_s] + whole[6:9]

    inv = ROPE_THETA ** (-jnp.arange(0, C_ROPE, 2, dtype=F32) / C_ROPE)
    ang = positions.astype(F32).reshape(t, 1) * inv
    cos, sin, zero = jnp.cos(ang), jnp.sin(ang), jnp.zeros((t, C_ROPE // 2), F32)
    rc = jnp.concatenate([cos, cos, zero, zero], axis=1)
    rs1 = jnp.concatenate([-sin, zero, zero, zero], axis=1)
    rs2 = jnp.concatenate([zero, sin, zero, zero], axis=1)

    my_x, my_y, my_c = _me()
    my_chip = 2 * my_x + my_y
    dev_arr = jnp.stack([_dev_index(my_x, my_y, my_c)]).astype(jnp.int32)
    core_arr = jnp.stack([my_c]).astype(jnp.int32)
    chip_arr = jnp.stack([my_chip, my_chip ^ 1, my_chip ^ 2, my_chip ^ 3]).astype(jnp.int32)

    n_stage = 3 * depth
    chain = [None]
    gat = {}
    issued = [0]

    def stage_names(i):
        layer, gid = divmod(i, 3)
        return layer, [even_names if layer % 2 == 0 else odd_names, all_names[:4], all_names[4:]][gid]

    def gather_ici_start(i):
        layer, names = stage_names(i)
        dev = dev_arr if chain[0] is None else dev_arr + chain[0][0, 0:1].astype(jnp.int32)
        placed = [_cast_place(big[nm][0][sub(nm, layer)], big[nm][3], dev, name="cast_place") for nm in names]
        axes = [big[nm][3] for nm in names]
        sizes = [p.shape[a] // N_DEV for p, a in zip(placed, axes)]
        st = dict(names=names, ici=_plan_gather_ici(axes, sizes), d2d=_plan_gather_d2d(axes, sizes))
        st["sems"], st["arrays"], chain[0] = _exchange_start(placed, N_PEER_CHIPS, st["ici"], name=f"gather_ici_start_s{i}")
        gat[i] = st

    def gather_stage(i, after):
        while issued[0] < min(i + 3, n_stage):
            gather_ici_start(issued[0])
            issued[0] += 1
        ahead = [i, i + 1] if i >= 3 and i + 1 < n_stage else [i]
        for j in ahead:
            st = gat[j]
            if "landed" not in st:
                st["arrays"], chain[0] = _exchange_wait(st["sems"], st["arrays"], st["ici"], after, name=f"gather_ici_wait_s{j}")
                st["landed"] = True
                after = chain[0]
            if "sems2" not in st:
                st["sems2"], st["arrays"], chain[0] = _exchange_start(st["arrays"], 1, st["d2d"], name=f"gather_d2d_start_s{j}")
        st = gat[i]
        st["arrays"], chain[0] = _exchange_wait(st["sems2"], st["arrays"], st["d2d"], chain[0], name=f"gather_d2d_wait_s{i}")
        return dict(zip(st["names"], st["arrays"]))

    wts = []

    def group_cols(w_in):
        return w_in.reshape(d, 5, groups, LANES).transpose(0, 2, 1, 3).reshape(d, 5 * a_width)

    def ungroup_cols(g_in):
        return g_in.reshape(d, groups, 5, LANES).transpose(0, 2, 1, 3).reshape(d, 5 * a_width)

    def group_rows(w_out):
        return w_out.reshape(2, groups, LANES, d).transpose(1, 0, 2, 3).reshape(2 * a_width, d)

    def ungroup_rows(g_out):
        return g_out.reshape(groups, 2, LANES, d).transpose(1, 0, 2, 3).reshape(2 * a_width, d)

    def pad_heads(w_uq):
        w3 = w_uq.reshape(q_rank, heads, C_NOPE + C_ROPE)
        return jnp.pad(w3, ((0, 0), (0, 0), (0, LANES - C_ROPE))).reshape(q_rank, heads * 2 * LANES)

    def unpad_heads(g_uq):
        return g_uq.reshape(q_rank, heads, 2 * LANES)[:, :, :C_NOPE + C_ROPE].reshape(q_rank, heads * (C_NOPE + C_ROPE))

    x2 = x.reshape(t, d)
    mem2 = mem.reshape(bl * n_mem, d)
    saved = []

    for layer in range(depth):
        w = gather_stage(3 * layer, x2)
        wts.append(w)
        sv = {"x0": x2}
        xn = _rmsnorm_fwd(x2, norm_mix_g[layer], name="norm_mix")
        sv["xn"] = xn
        if layer % 2 == 0:
            e = layer // 2
            w_in_g = group_cols(w["ab_w_in"])
            w_out_g = group_rows(w["ab_w_out"])
            z = _mm(xn, w_in_g, out_dtypes=(F32,), extras=[(chain[0], "after")], name="ab_in")
            ws_e = a_w_s[e]
            bs_e = a_b_s[e].reshape(groups, GMLP_BLOCK, 1)
            gn_e = a_v_norm_g[e].reshape(groups, 1, LANES)
            cw_e = conv_full[e].reshape(3, groups, LANES).transpose(1, 0, 2)
            ycat = _gmlp_conv_fwd(z, ws_e, bs_e, gn_e, cw_e, seq=seq, name="gmlp_conv_fwd")
            x2 = _mm(ycat, w_out_g, out_dtypes=(F32,), extras=[(x2, "mn")], epilogue=_epi_add, name="ab_out")
            sv.update(z=z, ycat=ycat, w_in_g=w_in_g, w_out_g=w_out_g, ws=ws_e, bs=bs_e, gn=gn_e, cw=cw_e)
        else:
            o = layer // 2
            w_uq_p = pad_heads(w["c_w_uq"])
            z = _mm(xn, w["c_w_in"], out_dtypes=(F32,), extras=[(chain[0], "after")], name="c_in")
            qn = _rmsnorm_fwd(z, q_norm_full[o], width=q_rank, col=0, name="norm_cq")
            kvn = _rmsnorm_fwd(z, kv_norm_full[o], width=kv_rank, col=q_rank // kv_rank, name="norm_ckv")
            qh = _mm(qn, w_uq_p, extras=[(rc, "m"), (rs1, "m"), (rs2, "m")], epilogue=_epi_rope_heads, tn=1024, name="c_uq")
            kvh = _mm(kvn, w["c_w_ukv"], name="c_ukv")
            k_rope = jnp.pad(z[:, q_rank + kv_rank:], ((0, 0), (0, LANES - C_ROPE)))
            kr = _rope_rows(k_rope, rc, rs1, rs2, transpose=False, out_dtype=BF16, name="rope_k")
            att, lse = _mla_fwd(qh, kvh, kr, seq=seq, heads=heads, name="mla_fwd")
            x2 = _mm(att, w["c_w_out"], out_dtypes=(F32,), extras=[(x2, "mn")], epilogue=_epi_add, name="c_out")
            sv.update(z=z, qn=qn, kvn=kvn, qh=qh, kvh=kvh, kr=kr, att=att, lse=lse, w_uq_p=w_uq_p)
        sv["x1"] = x2
        w.update(gather_stage(3 * layer + 1, x2))
        xq = _rmsnorm_fwd(x2, norm_mem_q_g[layer], name="norm_mem_q")
        memn = _rmsnorm_fwd(mem2, norm_mem_kv_g[layer], name="norm_mem_kv")
        mq = _mm(xq, w["m_wq"], extras=[(chain[0], "after")], name="m_q")
        mk = _mm(memn, w["m_wk"], name="m_k")
        mv = _mm(memn, w["m_wv"], name="m_v")
        mo = _mem_fwd(mq, mk, mv, seq=seq, n_mem=n_mem, name="mem_fwd")
        x2 = _mm(mo, w["m_wo"], out_dtypes=(F32,), extras=[(x2, "mn")], epilogue=_epi_add, name="m_o")
        sv.update(xq=xq, memn=memn, mq=mq, mk=mk, mv=mv, mo=mo)
        sv["x2"] = x2
        w.update(gather_stage(3 * layer + 2, x2))
        xf = _rmsnorm_fwd(x2, norm_ffn_g[layer], name="norm_ffn")
        hpre, hact = _mm(xf, w["f_w1"], out_dtypes=(BF16, BF16), extras=[(chain[0], "after")], epilogue=_epi_relu2, name="f_1")
        x2 = _mm(hact, w["f_w2"], out_dtypes=(F32,), extras=[(x2, "mn")], epilogue=_epi_add, name="f_2")
        sv.update(xf=xf, hpre=hpre, hact=hact)
        saved.append(sv)

    loss_part, dx, dg_final = _final_loss(x2, final_norm_g, loss_target.reshape(t, d), name="final_loss")
    loss = lax.psum(loss_part[0, 0], ("x", "y", "c"))

    dg_mix, dg_mq, dg_mkv, dg_ffn = [None] * depth, [None] * depth, [None] * depth, [None] * depth
    d_av, d_ws, d_bs = [None] * n_even, [None] * n_even, [None] * n_even
    d_conv, d_qn, d_kvn = [None] * n_even, [None] * n_odd, [None] * n_odd
    parts = {nm: [None] * big[nm][0].shape[0] for nm in big}

    sca = []

    def scatter_finish_d2d(st, after):
        n = len(st["names"])
        arrays, chain[0] = _exchange_wait(st["sems"], st["arrays"], st["d2d"], after, name=f"scatter_d2d_wait_s{st['id']}")
        sums = [_add_pair(g, got, ax, core_arr, name="add_pair") for g, got, ax in zip(arrays[:n], arrays[n:], st["axes"])]
        lands = [pltpu.with_memory_space_constraint(lax.empty(s_.shape, s_.dtype), pltpu.HBM) for s_ in sums]
        st["sems2"], st["arrays"], chain[0] = _exchange_start(sums + lands, N_PEER_CHIPS, st["ici"], name=f"scatter_ici_start_s{st['id']}")

    def scatter_stage(layer, names, grads, after):
        if sca and "sems2" not in sca[-1]:
            scatter_finish_d2d(sca[-1], after)
        axes = [big[nm][3] for nm in names]
        arrs = [grads[nm] for nm in names]
        sizes8 = [g.shape[a] // N_DEV for g, a in zip(arrs, axes)]
        gots = []
        for g, a in zip(arrs, axes):
            sh = list(g.shape)
            sh[a] //= 2
            gots.append(pltpu.with_memory_space_constraint(lax.empty(tuple(sh), g.dtype), pltpu.HBM))
        st = dict(id=len(sca), layer=layer, names=names, axes=axes, d2d=_plan_scatter_d2d(axes, sizes8),
                  ici=_plan_scatter_ici(axes, sizes8))
        st["sems"], st["arrays"], chain[0] = _exchange_start(arrs + gots, 1, st["d2d"], name=f"scatter_d2d_start_s{st['id']}")
        sca.append(st)

    def scatter_collect(st, after):
        n = len(st["names"])
        arrays, chain[0] = _exchange_wait(st["sems2"], st["arrays"], st["ici"], after, name=f"scatter_ici_wait_s{st['id']}")
        for nm, s_, r in zip(st["names"], arrays[:n], arrays[n:]):
            parts[nm][sub(nm, st["layer"])] = (s_, r)

    for layer in reversed(range(depth)):
        w, sv = wts[layer], saved[layer]
        grads = {}
        dh = _mm(dx, w["f_w2"], tb=True, extras=[(sv["hpre"], "mn"), (chain[0], "after")], epilogue=_epi_drelu2, name="f_2_dx")
        grads["f_w2"] = _mm(sv["hact"], dx, ta=True, name="f_2_dw")
        grads["f_w1"] = _mm(sv["xf"], dh, ta=True, name="f_1_dw")
        dxf = _mm(dh, w["f_w1"], tb=True, name="f_1_dx")
        dx, dg_ffn[layer] = _rmsnorm_bwd(sv["x2"], norm_ffn_g[layer], dxf, dx, name="norm_ffn_bwd")
        scatter_stage(layer, all_names[4:], grads, dx)
        dmo = _mm(dx, w["m_wo"], tb=True, extras=[(chain[0], "after")], name="m_o_dx")
        grads["m_wo"] = _mm(sv["mo"], dx, ta=True, name="m_o_dw")
        dmq, dmk, dmv = _mem_bwd(sv["mq"], sv["mk"], sv["mv"], dmo, seq=seq, n_mem=n_mem, name="mem_bwd")
        grads["m_wq"] = _mm(sv["xq"], dmq, ta=True, name="m_q_dw")
        grads["m_wk"] = _mm(sv["memn"], dmk, ta=True, name="m_k_dw")
        grads["m_wv"] = _mm(sv["memn"], dmv, ta=True, name="m_v_dw")
        dxq = _mm(dmq, w["m_wq"], tb=True, name="m_q_dx")
        dmemn_k = _mm(dmk, w["m_wk"], tb=True, out_dtypes=(F32,), name="m_k_dx")
        dmemn = _mm(dmv, w["m_wv"], tb=True, out_dtypes=(F32,), extras=[(dmemn_k, "mn")], epilogue=_epi_add, name="m_v_dx")
        _, dg_mkv[layer] = _rmsnorm_bwd(mem2, norm_mem_kv_g[layer], dmemn, name="norm_mem_kv_bwd")
        dx, dg_mq[layer] = _rmsnorm_bwd(sv["x1"], norm_mem_q_g[layer], dxq, dx, name="norm_mem_q_bwd")
        scatter_stage(layer, all_names[:4], grads, dx)
        if layer % 2 == 0:
            e = layer // 2
            dycat = _mm(dx, sv["w_out_g"], tb=True, extras=[(chain[0], "after")], name="ab_out_dx")
            grads["ab_w_out"] = ungroup_rows(_mm(sv["ycat"], dx, ta=True, name="ab_out_dw"))
            dz, dws, dbs, dgn, dcw = _gmlp_conv_bwd(sv["z"], dycat, sv["ws"], sv["bs"], sv["gn"], sv["cw"], seq=seq, name="gmlp_conv_bwd")
            d_ws[e], d_bs[e], d_av[e] = dws, dbs.reshape(groups, GMLP_BLOCK), dgn.reshape(a_width)
            d_conv[e] = dcw.transpose(1, 0, 2).reshape(3, a_width)
            grads["ab_w_in"] = ungroup_cols(_mm(sv["xn"], dz, ta=True, name="ab_in_dw"))
            dxn = _mm(dz, sv["w_in_g"], tb=True, name="ab_in_dx")
        else:
            o = layer // 2
            datt = _mm(dx, w["c_w_out"], tb=True, extras=[(chain[0], "after")], name="c_out_dx")
            grads["c_w_out"] = _mm(sv["att"], dx, ta=True, name="c_out_dw")
            dqh, dkvh, dkr = _mla_bwd(sv["qh"], sv["kvh"], sv["kr"], datt, sv["att"], sv["lse"], seq=seq, heads=heads, name="mla_bwd")
            dq_pre = _rope_rows(dqh, rc, rs1, rs2, transpose=True, out_dtype=BF16, name="rope_q_bwd")
            dk_rope = _rope_rows(dkr, rc, rs1, rs2, transpose=True, out_dtype=F32, name="rope_k_bwd")
            grads["c_w_uq"] = unpad_heads(_mm(sv["qn"], dq_pre, ta=True, name="c_uq_dw"))
            grads["c_w_ukv"] = _mm(sv["kvn"], dkvh, ta=True, name="c_ukv_dw")
            dqn = _mm(dq_pre, sv["w_uq_p"], tb=True, name="c_uq_dx")
            dkvn = _mm(dkvh, w["c_w_ukv"], tb=True, name="c_ukv_dx")
            dcq, d_qn[o] = _rmsnorm_bwd(sv["z"], q_norm_full[o], dqn, width=q_rank, col=0, name="norm_cq_bwd")
            dckv, d_kvn[o] = _rmsnorm_bwd(sv["z"], kv_norm_full[o], dkvn, width=kv_rank, col=q_rank // kv_rank, name="norm_ckv_bwd")
            dz = jnp.concatenate([dcq, dckv, dk_rope[:, :C_ROPE]], axis=1)
            grads["c_w_in"] = _mm(sv["xn"], dz, ta=True, name="c_in_dw")
            dxn = _mm(dz, w["c_w_in"], tb=True, name="c_in_dx")
        dx, dg_mix[layer] = _rmsnorm_bwd(sv["x0"], norm_mix_g[layer], dxn, dx, name="norm_mix_bwd")
        scatter_stage(layer, even_names if layer % 2 == 0 else odd_names, grads, dx)

    grad_x = dx.reshape(bl, seq, d)

    scatter_finish_d2d(sca[-1], dx)
    for st in sca[:-3]:
        scatter_collect(st, chain[0])
    big_out = {}
    for nm, (wv, mv_, vv, ax) in big.items():
        n_sub = wv.shape[0]
        first = 0 if nm in odd_names else 1
        if first < n_sub:
            big_out[nm] = _adamw(wv, mv_, vv, parts[nm], ax, chip_arr, (first, n_sub - first), None, chain[0], name=f"adamw_{nm}")
    for st in sca[-3:]:
        scatter_collect(st, chain[0])
    for nm, (wv, mv_, vv, ax) in big.items():
        if nm not in odd_names:
            big_out[nm] = _adamw(wv, mv_, vv, parts[nm], ax, chip_arr, (0, 1), big_out.get(nm), chain[0], name=f"adamw_{nm}_first")

    small_g = [jnp.concatenate(dg_mix), jnp.concatenate(dg_mq), jnp.concatenate(dg_mkv), jnp.concatenate(dg_ffn), dg_final.reshape(d),
               jnp.stack(d_av), jnp.stack(d_ws), jnp.stack(d_bs), jnp.concatenate(d_qn), jnp.concatenate(d_kvn), jnp.stack(d_conv)]
    small_shapes = [a.shape for a in small_w]
    g_all = _all_gather_small(_pack(small_g))
    sg, sd, sm, svv = _adamw_small(_pack(small_w), _pack(small_m), _pack(small_v), g_all, name="adamw_small")
    small_names = ["norm_mix_g", "norm_mem_q_g", "norm_mem_kv_g", "norm_ffn_g", "final_norm_g", "a_v_norm_g", "a_w_s", "a_b_s", "c_q_norm_g", "c_kv_norm_g", "b_conv_w"]
    small_out = {}
    unpacked = [_unpack(p, small_shapes) for p in (sg, sd, sm, svv)]
    for i, nm in enumerate(small_names):
        vals = [u[i] for u in unpacked]
        if i >= 8:
            full_w, _, _, ax = split_small[i - 8]
            vals = [_my_block(val, ax, full_w.shape[ax]) for val in vals]
        small_out[nm] = vals

    order = ["norm_mix_g", "norm_mem_q_g", "norm_mem_kv_g", "norm_ffn_g", "final_norm_g", "ab_w_in", "a_v_norm_g", "a_w_s", "a_b_s", "b_conv_w", "ab_w_out",
             "c_w_in", "c_q_norm_g", "c_kv_norm_g", "c_w_uq", "c_w_ukv", "c_w_out", "m_wq", "m_wk", "m_wv", "m_wo", "f_w1", "f_w2"]
    res = [loss, grad_x]
    for kind in range(4):
        for nm in order:
            res.append(big_out[nm][kind] if nm in big_out else small_out[nm][kind])
    return tuple(res)
```

```python
import functools
import math

import jax
import jax.numpy as jnp
from jax import lax
from jax.experimental import pallas as pl
from jax.experimental.pallas import tpu as pltpu

F32 = jnp.float32
BF16 = jnp.bfloat16
MESH = pl.DeviceIdType.MESH
N_DEV = 8
LANES = 128

RMS_EPS = 1e-6
CHUNK = 64
GMLP_BLOCK = 128
C_NOPE = 128
C_ROPE = 64
C_V = 128
ROPE_THETA = 10000.0
MEM_HEADS = 4
ADAM_LR = 0.001
ADAM_B1 = 0.9
ADAM_B2 = 0.999
ADAM_EPS = 1e-08
ADAM_WD = 0.01
ADAM_STEP = 10
NEG_BIG = -1e30
VMEM_LIMIT = 56 * 1024 * 1024


def _cp(*sem):
    return pltpu.CompilerParams(dimension_semantics=sem, vmem_limit_bytes=VMEM_LIMIT)


def _tile(n, pref):
    t = min(n, pref)
    while n % t and t > LANES:
        t //= 2
    return n if n % t else t


def _mm(a, b, *, ta=False, tb=False, out_dtypes=(BF16,), extras=(), epilogue=None, tm=1024, tn=1024, tk=2048, name):
    m, k = (a.shape[1], a.shape[0]) if ta else a.shape
    n = b.shape[0] if tb else b.shape[1]
    assert (b.shape[1] if tb else b.shape[0]) == k, (a.shape, b.shape, ta, tb)
    tm, tn, tk = _tile(m, tm), _tile(n, tn), _tile(k, tk)
    nk = k // tk
    ne, no = len(extras), len(out_dtypes)
    dims = (((0 if ta else 1,), (1 if tb else 0,)), ((), ()))

    def body(*refs):
        a_ref, b_ref = refs[0], refs[1]
        e_refs = refs[2:2 + ne]
        o_refs = refs[2 + ne:2 + ne + no]

        def finish(acc):
            vals = [e[...] for e, (_, kind) in zip(e_refs, extras) if kind != "after"]
            outs = (acc,) if epilogue is None else epilogue(acc, *vals)
            for o, val in zip(o_refs, outs):
                o[...] = val.astype(o.dtype)

        prod = lax.dot_general(a_ref[...].astype(BF16), b_ref[...].astype(BF16), dims, preferred_element_type=F32)
        if nk == 1:
            finish(prod)
        else:
            acc_ref = refs[-1]
            kk = pl.program_id(2)

            @pl.when(kk == 0)
            def _():
                acc_ref[...] = prod

            @pl.when((kk > 0) & (kk < nk - 1))
            def _():
                acc_ref[...] += prod

            @pl.when(kk == nk - 1)
            def _():
                finish(acc_ref[...] + prod)

    a_spec = pl.BlockSpec((tk, tm), lambda i, j, kk: (kk, i)) if ta else pl.BlockSpec((tm, tk), lambda i, j, kk: (i, kk))
    b_spec = pl.BlockSpec((tn, tk), lambda i, j, kk: (j, kk)) if tb else pl.BlockSpec((tk, tn), lambda i, j, kk: (kk, j))
    e_specs = []
    for arr, kind in extras:
        if kind == "mn":
            e_specs.append(pl.BlockSpec((tm, tn), lambda i, j, kk: (i, j)))
        elif kind == "after":
            e_specs.append(pl.BlockSpec(arr.shape, lambda i, j, kk: (0, 0)))
        else:
            e_specs.append(pl.BlockSpec((tm, arr.shape[1]), lambda i, j, kk: (i, 0)))
    outs = pl.pallas_call(
        body,
        out_shape=tuple(jax.ShapeDtypeStruct((m, n), d) for d in out_dtypes),
        grid=(m // tm, n // tn, nk),
        in_specs=[a_spec, b_spec] + e_specs,
        out_specs=tuple(pl.BlockSpec((tm, tn), lambda i, j, kk: (i, j)) for _ in out_dtypes),
        scratch_shapes=[pltpu.VMEM((tm, tn), F32)] if nk > 1 else [],
        compiler_params=_cp("parallel", "parallel", "arbitrary"),
        name=name,
    )(a, b, *[arr for arr, _ in extras])
    return outs[0] if no == 1 else outs


def _epi_add(acc, r):
    return (acc + r,)


def _epi_relu2(acc):
    h = jnp.maximum(acc, 0.0)
    return (acc, h * h)


def _epi_drelu2(acc, h):
    return (acc * (2.0 * jnp.maximum(h.astype(F32), 0.0)),)


def _rope(x, c, s1, s2):
    return x * c + pltpu.roll(x, LANES - C_ROPE // 2, 1) * s1 + pltpu.roll(x, C_ROPE // 2, 1) * s2


def _rope_t(dy, c, s1, s2):
    return dy * c + pltpu.roll(dy * s1, C_ROPE // 2, 1) + pltpu.roll(dy * s2, LANES - C_ROPE // 2, 1)


def _epi_rope_heads(acc, c, s1, s2):
    parts = []
    for h0 in range(0, acc.shape[1], 2 * LANES):
        parts.append(acc[:, h0:h0 + LANES])
        parts.append(_rope(acc[:, h0 + LANES:h0 + 2 * LANES], c, s1, s2))
    return (jnp.concatenate(parts, axis=1),)


def _rmsnorm_fwd(x, g, *, width=None, col=0, name):
    t = x.shape[0]
    w = x.shape[1] if width is None else width
    tm = _tile(t, 512)

    def body(x_ref, g_ref, o_ref):
        xv = x_ref[...].astype(F32)
        r = lax.rsqrt(jnp.mean(xv * xv, axis=-1, keepdims=True) + RMS_EPS)
        o_ref[...] = (xv * r * g_ref[...]).astype(o_ref.dtype)

    return pl.pallas_call(
        body,
        out_shape=jax.ShapeDtypeStruct((t, w), BF16),
        grid=(t // tm,),
        in_specs=[pl.BlockSpec((tm, w), lambda i: (i, col)), pl.BlockSpec((1, w), lambda i: (0, 0))],
        out_specs=pl.BlockSpec((tm, w), lambda i: (i, 0)),
        compiler_params=_cp("parallel"),
        name=name,
    )(x, g.reshape(1, w))


def _rmsnorm_bwd(x, g, dy, res=None, *, width=None, col=0, name):
    t = x.shape[0]
    w = x.shape[1] if width is None else width
    tm = _tile(t, 256)
    has_res = res is not None

    def body(*refs):
        x_ref, g_ref, dy_ref = refs[:3]
        res_ref = refs[3] if has_res else None
        dx_ref, dg_ref = refs[-2], refs[-1]
        xv = x_ref[...].astype(F32)
        r = lax.rsqrt(jnp.mean(xv * xv, axis=-1, keepdims=True) + RMS_EPS)
        xh = xv * r
        dyv = dy_ref[...].astype(F32)
        dxh = dyv * g_ref[...]
        dx = r * (dxh - xh * jnp.mean(dxh * xh, axis=-1, keepdims=True))
        if has_res:
            dx = dx + res_ref[...]
        dx_ref[...] = dx

        @pl.when(pl.program_id(0) == 0)
        def _():
            dg_ref[...] = jnp.zeros_like(dg_ref)

        dg_ref[...] += jnp.sum(dyv * xh, axis=0, keepdims=True)

    in_specs = [pl.BlockSpec((tm, w), lambda i: (i, col)), pl.BlockSpec((1, w), lambda i: (0, 0)), pl.BlockSpec((tm, w), lambda i: (i, 0))]
    args = [x, g.reshape(1, w), dy]
    if has_res:
        in_specs.append(pl.BlockSpec((tm, w), lambda i: (i, 0)))
        args.append(res)
    return pl.pallas_call(
        body,
        out_shape=(jax.ShapeDtypeStruct((t, w), F32), jax.ShapeDtypeStruct((1, w), F32)),
        grid=(t // tm,),
        in_specs=in_specs,
        out_specs=(pl.BlockSpec((tm, w), lambda i: (i, 0)), pl.BlockSpec((1, w), lambda i: (0, 0))),
        compiler_params=_cp("arbitrary"),
        name=name,
    )(*args)


def _final_loss(x, g, target, *, name):
    t, d = x.shape
    tm = _tile(t, 256)

    def body(x_ref, g_ref, t_ref, loss_ref, dx_ref, dg_ref):
        xv = x_ref[...]
        r = lax.rsqrt(jnp.mean(xv * xv, axis=-1, keepdims=True) + RMS_EPS)
        xh = xv * r
        gv = g_ref[...]
        err = xh * gv - t_ref[...]
        dyv = err * (1.0 / d)
        dxh = dyv * gv
        dx_ref[...] = r * (dxh - xh * jnp.mean(dxh * xh, axis=-1, keepdims=True))

        @pl.when(pl.program_id(0) == 0)
        def _():
            dg_ref[...] = jnp.zeros_like(dg_ref)
            loss_ref[...] = jnp.zeros_like(loss_ref)

        dg_ref[...] += jnp.sum(dyv * xh, axis=0, keepdims=True)
        row = jnp.sum(err * err, axis=-1, keepdims=True) * (0.5 / d)
        loss_ref[...] += jnp.sum(row, axis=0, keepdims=True)

    return pl.pallas_call(
        body,
        out_shape=(jax.ShapeDtypeStruct((1, 1), F32), jax.ShapeDtypeStruct((t, d), F32), jax.ShapeDtypeStruct((1, d), F32)),
        grid=(t // tm,),
        in_specs=[pl.BlockSpec((tm, d), lambda i: (i, 0)), pl.BlockSpec((1, d), lambda i: (0, 0)), pl.BlockSpec((tm, d), lambda i: (i, 0))],
        out_specs=(pl.BlockSpec((1, 1), lambda i: (0, 0)), pl.BlockSpec((tm, d), lambda i: (i, 0)), pl.BlockSpec((1, d), lambda i: (0, 0))),
        compiler_params=_cp("arbitrary"),
        name=name,
    )(x, g.reshape(1, d), target)


_GELU_C = math.sqrt(2.0 / math.pi)
_GELU_A = 0.044715


def _gelu(x):
    t = jnp.tanh(_GELU_C * (x + _GELU_A * x * x * x))
    return 0.5 * x * (1.0 + t), t


def _dgelu(x, t):
    return 0.5 * (1.0 + t) + 0.5 * x * (1.0 - t * t) * (_GELU_C * (1.0 + 3.0 * _GELU_A * x * x))


def _spatial_mask():
    ci = lax.broadcasted_iota(jnp.int32, (GMLP_BLOCK, GMLP_BLOCK), 0) // CHUNK
    cj = lax.broadcasted_iota(jnp.int32, (GMLP_BLOCK, GMLP_BLOCK), 1) // CHUNK
    return (cj <= ci).astype(F32)


def _shift_down(x, k):
    rows = lax.broadcasted_iota(jnp.int32, x.shape, 0)
    return jnp.where(rows >= k, pltpu.roll(x, k, 0), 0.0)


def _shift_up(x, k):
    n = x.shape[0]
    rows = lax.broadcasted_iota(jnp.int32, x.shape, 0)
    return jnp.where(rows < n - k, pltpu.roll(x, n - k, 0), 0.0)


def _gmlp_conv_fwd(z, ws, bs, gn, cw, *, seq, name):
    t = z.shape[0]
    g_n = ws.shape[0]
    nb = seq // GMLP_BLOCK
    L = LANES

    def body(z_ref, ws_ref, bs_ref, gn_ref, cw_ref, y_ref):
        u = z_ref[:, 0:L]
        v = z_ref[:, L:2 * L]
        gu, _ = _gelu(u)
        gv, _ = _gelu(v)
        r = lax.rsqrt(jnp.mean(gv * gv, axis=-1, keepdims=True) + RMS_EPS)
        vg = (gv * r * gn_ref[0]).astype(BF16)
        wm = (ws_ref[0] * _spatial_mask()).astype(BF16)
        bias = bs_ref[0]
        for n in range(nb):
            sl = slice(n * GMLP_BLOCK, (n + 1) * GMLP_BLOCK)
            mixed = jnp.dot(wm, vg[sl], preferred_element_type=F32) + bias
            y_ref[sl, 0:L] = (gu[sl] * mixed).astype(y_ref.dtype)
        bg = z_ref[:, 2 * L:3 * L]
        zc = z_ref[:, 3 * L:4 * L] * z_ref[:, 4 * L:5 * L]
        w0, w1, w2 = cw_ref[0, 0:1, :], cw_ref[0, 1:2, :], cw_ref[0, 2:3, :]
        conv = w0 * _shift_down(zc, 2) + w1 * _shift_down(zc, 1) + w2 * zc
        y_ref[:, L:2 * L] = (bg * conv).astype(y_ref.dtype)

    return pl.pallas_call(
        body,
        out_shape=jax.ShapeDtypeStruct((t, 2 * L * g_n), BF16),
        grid=(g_n, t // seq),
        in_specs=[
            pl.BlockSpec((seq, 5 * L), lambda g, b: (b, g)),
            pl.BlockSpec((1, GMLP_BLOCK, GMLP_BLOCK), lambda g, b: (g, 0, 0)),
            pl.BlockSpec((1, GMLP_BLOCK, 1), lambda g, b: (g, 0, 0)),
            pl.BlockSpec((1, 1, L), lambda g, b: (g, 0, 0)),
            pl.BlockSpec((1, 3, L), lambda g, b: (g, 0, 0)),
        ],
        out_specs=pl.BlockSpec((seq, 2 * L), lambda g, b: (b, g)),
        compiler_params=_cp("parallel", "parallel"),
        name=name,
    )(z, ws, bs, gn, cw)


def _gmlp_conv_bwd(z, dy, ws, bs, gn, cw, *, seq, name):
    t = z.shape[0]
    g_n = ws.shape[0]
    nb = seq // GMLP_BLOCK
    L = LANES
    dims_nt = (((1,), (1,)), ((), ()))
    dims_tn = (((0,), (0,)), ((), ()))

    def body(z_ref, dy_ref, ws_ref, bs_ref, gn_ref, cw_ref, dz_ref, dws_ref, dbs_ref, dgn_ref, dcw_ref, dvg_ref):
        @pl.when(pl.program_id(1) == 0)
        def _():
            dws_ref[...] = jnp.zeros_like(dws_ref)
            dbs_ref[...] = jnp.zeros_like(dbs_ref)
            dgn_ref[...] = jnp.zeros_like(dgn_ref)
            dcw_ref[...] = jnp.zeros_like(dcw_ref)

        u = z_ref[:, 0:L]
        v = z_ref[:, L:2 * L]
        gu, tu = _gelu(u)
        gv, tv = _gelu(v)
        r = lax.rsqrt(jnp.mean(gv * gv, axis=-1, keepdims=True) + RMS_EPS)
        gvh = gv * r
        gnv = gn_ref[0]
        vg = (gvh * gnv).astype(BF16)
        mask = _spatial_mask()
        wm = (ws_ref[0] * mask).astype(BF16)
        bias = bs_ref[0]
        dya = dy_ref[:, 0:L].astype(F32)
        dws = jnp.zeros((GMLP_BLOCK, GMLP_BLOCK), F32)
        dbs = jnp.zeros((GMLP_BLOCK, 1), F32)
        for n in range(nb):
            sl = slice(n * GMLP_BLOCK, (n + 1) * GMLP_BLOCK)
            mixed = jnp.dot(wm, vg[sl], preferred_element_type=F32) + bias
            dmixed = dya[sl] * gu[sl]
            dgu = dya[sl] * mixed
            dz_ref[sl, 0:L] = (dgu * _dgelu(u[sl], tu[sl])).astype(dz_ref.dtype)
            dmb = dmixed.astype(BF16)
            dws = dws + lax.dot_general(dmb, vg[sl], dims_nt, preferred_element_type=F32)
            dbs = dbs + jnp.sum(dmixed, axis=1, keepdims=True)
            dvg_ref[sl, :] = lax.dot_general(wm, dmb, dims_tn, preferred_element_type=F32)
        dws_ref[0] += dws * mask
        dbs_ref[0] += dbs
        dvg = dvg_ref[...]
        dgn_ref[0] += jnp.sum(dvg * gvh, axis=0, keepdims=True)
        dvh = dvg * gnv
        dgv = r * (dvh - gvh * jnp.mean(dvh * gvh, axis=-1, keepdims=True))
        dz_ref[:, L:2 * L] = (dgv * _dgelu(v, tv)).astype(dz_ref.dtype)

        bg = z_ref[:, 2 * L:3 * L]
        cg = z_ref[:, 3 * L:4 * L]
        hh = z_ref[:, 4 * L:5 * L]
        zc = cg * hh
        w0, w1, w2 = cw_ref[0, 0:1, :], cw_ref[0, 1:2, :], cw_ref[0, 2:3, :]
        z1 = _shift_down(zc, 1)
        z2 = _shift_down(zc, 2)
        conv = w0 * z2 + w1 * z1 + w2 * zc
        dyb = dy_ref[:, L:2 * L].astype(F32)
        dconv = dyb * bg
        dz_ref[:, 2 * L:3 * L] = (dyb * conv).astype(dz_ref.dtype)
        dzc = w2 * dconv + w1 * _shift_up(dconv, 1) + w0 * _shift_up(dconv, 2)
        dz_ref[:, 3 * L:4 * L] = (dzc * hh).astype(dz_ref.dtype)
        dz_ref[:, 4 * L:5 * L] = (dzc * cg).astype(dz_ref.dtype)
        dcw_ref[0, 0:1, :] += jnp.sum(dconv * z2, axis=0, keepdims=True)
        dcw_ref[0, 1:2, :] += jnp.sum(dconv * z1, axis=0, keepdims=True)
        dcw_ref[0, 2:3, :] += jnp.sum(dconv * zc, axis=0, keepdims=True)

    return pl.pallas_call(
        body,
        out_shape=(
            jax.ShapeDtypeStruct((t, 5 * L * g_n), BF16),
            jax.ShapeDtypeStruct(ws.shape, F32),
            jax.ShapeDtypeStruct(bs.shape, F32),
            jax.ShapeDtypeStruct(gn.shape, F32),
            jax.ShapeDtypeStruct(cw.shape, F32),
        ),
        grid=(g_n, t // seq),
        in_specs=[
            pl.BlockSpec((seq, 5 * L), lambda g, b: (b, g)),
            pl.BlockSpec((seq, 2 * L), lambda g, b: (b, g)),
            pl.BlockSpec((1, GMLP_BLOCK, GMLP_BLOCK), lambda g, b: (g, 0, 0)),
            pl.BlockSpec((1, GMLP_BLOCK, 1), lambda g, b: (g, 0, 0)),
            pl.BlockSpec((1, 1, L), lambda g, b: (g, 0, 0)),
            pl.BlockSpec((1, 3, L), lambda g, b: (g, 0, 0)),
        ],
        out_specs=(
            pl.BlockSpec((seq, 5 * L), lambda g, b: (b, g)),
            pl.BlockSpec((1, GMLP_BLOCK, GMLP_BLOCK), lambda g, b: (g, 0, 0)),
            pl.BlockSpec((1, GMLP_BLOCK, 1), lambda g, b: (g, 0, 0)),
            pl.BlockSpec((1, 1, L), lambda g, b: (g, 0, 0)),
            pl.BlockSpec((1, 3, L), lambda g, b: (g, 0, 0)),
        ),
        scratch_shapes=[pltpu.VMEM((seq, L), F32)],
        compiler_params=_cp("parallel", "arbitrary"),
        name=name,
    )(z, dy, ws, bs, gn, cw)


def _rope_rows(x, c, s1, s2, *, transpose, out_dtype, name):
    t, w = x.shape
    tm = _tile(t, 512)
    only = w == LANES
    fn = _rope_t if transpose else _rope

    def body(x_ref, c_ref, s1_ref, s2_ref, o_ref):
        cv, s1v, s2v = c_ref[...], s1_ref[...], s2_ref[...]
        if only:
            o_ref[...] = fn(x_ref[...].astype(F32), cv, s1v, s2v).astype(o_ref.dtype)
        else:
            for h0 in range(0, w, 2 * LANES):
                o_ref[:, h0:h0 + LANES] = x_ref[:, h0:h0 + LANES].astype(o_ref.dtype)
                o_ref[:, h0 + LANES:h0 + 2 * LANES] = fn(x_ref[:, h0 + LANES:h0 + 2 * LANES].astype(F32), cv, s1v, s2v).astype(o_ref.dtype)

    tab = pl.BlockSpec((tm, LANES), lambda i: (i, 0))
    return pl.pallas_call(
        body,
        out_shape=jax.ShapeDtypeStruct((t, w), out_dtype),
        grid=(t // tm,),
        in_specs=[pl.BlockSpec((tm, w), lambda i: (i, 0)), tab, tab, tab],
        out_specs=pl.BlockSpec((tm, w), lambda i: (i, 0)),
        compiler_params=_cp("parallel"),
        name=name,
    )(x, c, s1, s2)


_NT = (((1,), (1,)), ((), ()))
_TN = (((0,), (0,)), ((), ()))


def _chunk_mask(tq):
    qc = lax.broadcasted_iota(jnp.int32, (tq, tq), 0) // CHUNK
    kc = lax.broadcasted_iota(jnp.int32, (tq, tq), 1) // CHUNK
    return kc <= qc


MLA_BLOCK = 512


def _mla_fwd(q, kv, kr, *, seq, heads, name):
    t = q.shape[0]
    tq = _tile(seq, MLA_BLOCK)
    nq = seq // tq
    scale = (C_NOPE + C_ROPE) ** -0.5

    def body(q_ref, kv_ref, kr_ref, o_ref, lse_ref, kcat_ref):
        i = pl.program_id(2)

        @pl.when(i == 0)
        def _():
            kcat_ref[:, :LANES] = kv_ref[:, :LANES]
            kcat_ref[:, LANES:] = kr_ref[...]

        qv = q_ref[...]

        def step(j, carry, diagonal):
            m_run, l_run, acc = carry
            rows = pl.ds(pl.multiple_of(j * tq, tq), tq)
            s = lax.dot_general(qv, kcat_ref[rows, :], _NT, preferred_element_type=F32) * scale
            if diagonal:
                s = jnp.where(_chunk_mask(tq), s, NEG_BIG)
            m_new = jnp.maximum(m_run, jnp.max(s, axis=-1, keepdims=True))
            alpha = jnp.exp(m_run - m_new)
            p = jnp.exp(s - m_new)
            l_new = alpha * l_run + jnp.sum(p, axis=-1, keepdims=True)
            acc = alpha * acc + jnp.dot(p.astype(BF16), kv_ref[rows, LANES:], preferred_element_type=F32)
            return m_new, l_new, acc

        init = (jnp.full((tq, 1), NEG_BIG, F32), jnp.zeros((tq, 1), F32), jnp.zeros((tq, C_V), F32))
        carry = lax.fori_loop(0, i, lambda j, c: step(j, c, False), init)
        m_run, l_run, acc = step(i, carry, True)
        o_ref[...] = (acc / l_run).astype(o_ref.dtype)
        lse_ref[...] = m_run + jnp.log(l_run)

    return pl.pallas_call(
        body,
        out_shape=(jax.ShapeDtypeStruct((t, heads * C_V), BF16), jax.ShapeDtypeStruct((heads, t, 1), F32)),
        grid=(t // seq, heads, nq),
        in_specs=[
            pl.BlockSpec((tq, 2 * LANES), lambda b, h, i: (b * nq + i, h)),
            pl.BlockSpec((seq, 2 * LANES), lambda b, h, i: (b, h)),
            pl.BlockSpec((seq, LANES), lambda b, h, i: (b, 0)),
        ],
        out_specs=(
            pl.BlockSpec((tq, C_V), lambda b, h, i: (b * nq + i, h)),
            pl.BlockSpec((None, tq, 1), lambda b, h, i: (h, b * nq + i, 0)),
        ),
        scratch_shapes=[pltpu.VMEM((seq, 2 * LANES), BF16)],
        compiler_params=_cp("parallel", "parallel", "arbitrary"),
        name=name,
    )(q, kv, kr)


def _mla_bwd(q, kv, kr, do, o, lse, *, seq, heads, name):
    t = q.shape[0]
    tq = _tile(seq, MLA_BLOCK)
    nq = seq // tq
    scale = (C_NOPE + C_ROPE) ** -0.5

    def body(q_ref, kv_ref, kr_ref, do_ref, o_ref, lse_ref, dq_ref, dkv_ref, dkr_ref, kcat_ref, dk_ref, dv_ref):
        h, i = pl.program_id(1), pl.program_id(2)

        @pl.when(i == 0)
        def _():
            kcat_ref[:, :LANES] = kv_ref[:, :LANES]
            kcat_ref[:, LANES:] = kr_ref[...]
            dk_ref[...] = jnp.zeros_like(dk_ref)
            dv_ref[...] = jnp.zeros_like(dv_ref)

        @pl.when((i == 0) & (h == 0))
        def _():
            dkr_ref[...] = jnp.zeros_like(dkr_ref)

        qv = q_ref[...]
        dov = do_ref[...]
        lse_v = lse_ref[...]
        delta = jnp.sum(dov.astype(F32) * o_ref[...].astype(F32), axis=-1, keepdims=True)

        def step(j, dq, diagonal):
            rows = pl.ds(pl.multiple_of(j * tq, tq), tq)
            kj = kcat_ref[rows, :]
            s = lax.dot_general(qv, kj, _NT, preferred_element_type=F32) * scale
            if diagonal:
                s = jnp.where(_chunk_mask(tq), s, NEG_BIG)
            p = jnp.exp(s - lse_v)
            dp = lax.dot_general(dov, kv_ref[rows, LANES:], _NT, preferred_element_type=F32)
            ds = (p * (dp - delta) * scale).astype(BF16)
            dk_ref[rows, :] += lax.dot_general(ds, qv, _TN, preferred_element_type=F32)
            dv_ref[rows, :] += lax.dot_general(p.astype(BF16), dov, _TN, preferred_element_type=F32)
            return dq + jnp.dot(ds, kj, preferred_element_type=F32)

        dq = lax.fori_loop(0, i, lambda j, c: step(j, c, False), jnp.zeros((tq, 2 * LANES), F32))
        dq_ref[...] = step(i, dq, True)

        @pl.when(i == nq - 1)
        def _():
            dkv_ref[:, :LANES] = dk_ref[:, :LANES].astype(dkv_ref.dtype)
            dkv_ref[:, LANES:] = dv_ref[...].astype(dkv_ref.dtype)
            dkr_ref[...] += dk_ref[:, LANES:]

    return pl.pallas_call(
        body,
        out_shape=(
            jax.ShapeDtypeStruct((t, heads * 2 * LANES), F32),
            jax.ShapeDtypeStruct((t, heads * 2 * LANES), BF16),
            jax.ShapeDtypeStruct((t, LANES), F32),
        ),
        grid=(t // seq, heads, nq),
        in_specs=[
            pl.BlockSpec((tq, 2 * LANES), lambda b, h, i: (b * nq + i, h)),
            pl.BlockSpec((seq, 2 * LANES), lambda b, h, i: (b, h)),
            pl.BlockSpec((seq, LANES), lambda b, h, i: (b, 0)),
            pl.BlockSpec((tq, C_V), lambda b, h, i: (b * nq + i, h)),
            pl.BlockSpec((tq, C_V), lambda b, h, i: (b * nq + i, h)),
            pl.BlockSpec((None, tq, 1), lambda b, h, i: (h, b * nq + i, 0)),
        ],
        out_specs=(
            pl.BlockSpec((tq, 2 * LANES), lambda b, h, i: (b * nq + i, h)),
            pl.BlockSpec((seq, 2 * LANES), lambda b, h, i: (b, h)),
            pl.BlockSpec((seq, LANES), lambda b, h, i: (b, 0)),
        ),
        scratch_shapes=[pltpu.VMEM((seq, 2 * LANES), BF16), pltpu.VMEM((seq, 2 * LANES), F32), pltpu.VMEM((seq, C_V), F32)],
        compiler_params=_cp("parallel", "arbitrary", "arbitrary"),
        name=name,
    )(q, kv, kr, do, o, lse)


def _softmax_rows(s):
    e = jnp.exp(s - jnp.max(s, axis=-1, keepdims=True))
    return e / jnp.sum(e, axis=-1, keepdims=True)


def _mem_fwd(q, k, v, *, seq, n_mem, name):
    t, d = q.shape
    hd = d // MEM_HEADS
    tq = _tile(seq, 512)
    nq = seq // tq
    scale = hd ** -0.5

    def body(q_ref, k_ref, v_ref, o_ref):
        s = lax.dot_general(q_ref[...], k_ref[...], _NT, preferred_element_type=F32) * scale
        p = _softmax_rows(s)
        o_ref[...] = jnp.dot(p.astype(BF16), v_ref[...], preferred_element_type=F32).astype(o_ref.dtype)

    return pl.pallas_call(
        body,
        out_shape=jax.ShapeDtypeStruct((t, d), BF16),
        grid=(t // seq, MEM_HEADS, nq),
        in_specs=[
            pl.BlockSpec((tq, hd), lambda b, h, i: (b * nq + i, h)),
            pl.BlockSpec((n_mem, hd), lambda b, h, i: (b, h)),
            pl.BlockSpec((n_mem, hd), lambda b, h, i: (b, h)),
        ],
        out_specs=pl.BlockSpec((tq, hd), lambda b, h, i: (b * nq + i, h)),
        compiler_params=_cp("parallel", "parallel", "arbitrary"),
        name=name,
    )(q, k, v)


def _mem_bwd(q, k, v, do, *, seq, n_mem, name):
    t, d = q.shape
    hd = d // MEM_HEADS
    tq = _tile(seq, 512)
    nq = seq // tq
    scale = hd ** -0.5

    def body(q_ref, k_ref, v_ref, do_ref, dq_ref, dk_ref, dv_ref):
        @pl.when(pl.program_id(2) == 0)
        def _():
            dk_ref[...] = jnp.zeros_like(dk_ref)
            dv_ref[...] = jnp.zeros_like(dv_ref)

        qv, kv_, vv, dov = q_ref[...], k_ref[...], v_ref[...], do_ref[...]
        p = _softmax_rows(lax.dot_general(qv, kv_, _NT, preferred_element_type=F32) * scale)
        dp = lax.dot_general(dov, vv, _NT, preferred_element_type=F32)
        ds = (p * (dp - jnp.sum(dp * p, axis=-1, keepdims=True)) * scale).astype(BF16)
        dq_ref[...] = jnp.dot(ds, kv_, preferred_element_type=F32).astype(dq_ref.dtype)
        dk_ref[...] += lax.dot_general(ds, qv, _TN, preferred_element_type=F32)
        dv_ref[...] += lax.dot_general(p.astype(BF16), dov, _TN, preferred_element_type=F32)

    return pl.pallas_call(
        body,
        out_shape=(
            jax.ShapeDtypeStruct((t, d), BF16),
            jax.ShapeDtypeStruct(k.shape, F32),
            jax.ShapeDtypeStruct(v.shape, F32),
        ),
        grid=(t // seq, MEM_HEADS, nq),
        in_specs=[
            pl.BlockSpec((tq, hd), lambda b, h, i: (b * nq + i, h)),
            pl.BlockSpec((n_mem, hd), lambda b, h, i: (b, h)),
            pl.BlockSpec((n_mem, hd), lambda b, h, i: (b, h)),
            pl.BlockSpec((tq, hd), lambda b, h, i: (b * nq + i, h)),
        ],
        out_specs=(
            pl.BlockSpec((tq, hd), lambda b, h, i: (b * nq + i, h)),
            pl.BlockSpec((n_mem, hd), lambda b, h, i: (b, h)),
            pl.BlockSpec((n_mem, hd), lambda b, h, i: (b, h)),
        ),
        compiler_params=_cp("parallel", "parallel", "arbitrary"),
        name=name,
    )(q, k, v, do)


def _me():
    return lax.axis_index("x"), lax.axis_index("y"), lax.axis_index("c")


def _dev_index(px, py, pc):
    return 4 * px + 2 * py + pc


def _chunk(ref, idx, axis, size):
    start = pl.multiple_of(idx * size, size)
    return ref.at[pl.ds(start, size), :] if axis == 0 else ref.at[:, pl.ds(start, size)]


def _all_gather_small(v):
    m, n = v.shape

    def body(x_ref, out_ref, send_sems, recv_sems, local_sem):
        x, y, c = _me()
        me, sibling = (x, y, c), (x, y, 1 - c)
        chips = [(1 - x, y), (x, 1 - y), (1 - x, 1 - y)]

        def rows(px, py, pc):
            return out_ref.at[_dev_index(px, py, pc)]

        def copy(k, block, to, src=None):
            return pltpu.make_async_remote_copy(
                src_ref=rows(*block) if src is None else src, dst_ref=rows(*block),
                send_sem=send_sems.at[k], recv_sem=recv_sems.at[k], device_id=to, device_id_type=MESH)

        mine = pltpu.make_async_copy(x_ref, rows(*me), local_sem)
        mine.start()
        first = [copy(0, me, sibling, src=x_ref)]
        first += [copy(1 + j, me, (*chip, c), src=x_ref) for j, chip in enumerate(chips)]
        for cp in first:
            cp.start()
        passed = [copy(4 + j, (*chip, c), sibling) for j, chip in enumerate(chips)]
        for j, chip in enumerate(chips):
            copy(1 + j, (*chip, c), me).wait_recv()
            passed[j].start()
        copy(0, sibling, me).wait_recv()
        for j, chip in enumerate(chips):
            copy(4 + j, (*chip, 1 - c), me).wait_recv()
        for cp in first + passed:
            cp.wait_send()
        mine.wait()

    return pl.pallas_call(
        body,
        out_shape=jax.ShapeDtypeStruct((N_DEV, m, n), v.dtype),
        in_specs=[pl.BlockSpec(memory_space=pltpu.VMEM)],
        out_specs=pl.BlockSpec(memory_space=pltpu.VMEM),
        scratch_shapes=[pltpu.SemaphoreType.DMA((7,)), pltpu.SemaphoreType.DMA((7,)), pltpu.SemaphoreType.DMA],
        name="all_gather_small",
    )(v)


def _cast_place(w, axis, dev_arr, *, name):
    r, c = w.shape
    tr = _tile(r, 256)
    nr = r // tr
    full = (N_DEV * r, c) if axis == 0 else (r, N_DEV * c)

    def body(dev_ref, w_ref, o_ref):
        o_ref[...] = w_ref[...].astype(o_ref.dtype)

    if axis == 0:
        out_spec = pl.BlockSpec((tr, c), lambda i, dev: (dev[0] * nr + i, 0))
    else:
        out_spec = pl.BlockSpec((tr, c), lambda i, dev: (i, dev[0]))
    return pl.pallas_call(
        body,
        out_shape=jax.ShapeDtypeStruct(full, BF16),
        grid_spec=pltpu.PrefetchScalarGridSpec(
            num_scalar_prefetch=1, grid=(nr,), in_specs=[pl.BlockSpec((tr, c), lambda i, dev: (i, 0))], out_specs=out_spec),
        compiler_params=_cp("parallel"),
        name=name,
    )(dev_arr, w)


_ANY = pl.BlockSpec(memory_space=pl.ANY)
_SEM = pl.BlockSpec(memory_space=pltpu.SEMAPHORE)
_EFFECT = pltpu.SideEffectType.DATAFLOW_SIDE_EFFECTING
N_PEER_CHIPS = 3


def _other_chips(x, y):
    return [((1 - x if (j >> 1) & 1 else x), (1 - y if j & 1 else y)) for j in range(1, 4)]


def _remote(src, dst, send_sem, recv_sem, to):
    return pltpu.make_async_remote_copy(src_ref=src, dst_ref=dst, send_sem=send_sem, recv_sem=recv_sem, device_id=to, device_id_type=MESH)


def _plan_gather_ici(axes, sizes):
    def plan(refs, send_sems, recv_sems):
        x, y, c = _me()
        out = []
        for t in range(len(axes)):
            for j, chip in enumerate(_other_chips(x, y)):
                mine = _chunk(refs[t], _dev_index(x, y, c), axes[t], sizes[t])
                theirs = _chunk(refs[t], _dev_index(*chip, c), axes[t], sizes[t])
                out.append((_remote(mine, mine, send_sems[j], recv_sems[j], (*chip, c)), _remote(theirs, theirs, send_sems[j], recv_sems[j], (*chip, c))))
        return out
    return plan


def _plan_gather_d2d(axes, sizes):
    def plan(refs, send_sems, recv_sems):
        x, y, c = _me()
        out = []
        for t in range(len(axes)):
            for k in range(4):
                have = _chunk(refs[t], 2 * k + c, axes[t], sizes[t])
                get = _chunk(refs[t], 2 * k + 1 - c, axes[t], sizes[t])
                out.append((_remote(have, have, send_sems[0], recv_sems[0], (x, y, 1 - c)), _remote(get, get, send_sems[0], recv_sems[0], (x, y, 1 - c))))
        return out
    return plan


def _plan_scatter_d2d(axes, sizes):
    n = len(axes)

    def plan(refs, send_sems, recv_sems):
        x, y, c = _me()
        out = []
        for t in range(n):
            for k in range(4):
                give = _chunk(refs[t], 2 * k + 1 - c, axes[t], sizes[t])
                land = _chunk(refs[n + t], k, axes[t], sizes[t])
                cp = _remote(give, land, send_sems[0], recv_sems[0], (x, y, 1 - c))
                out.append((cp, cp))
        return out
    return plan


def _plan_scatter_ici(axes, sizes):
    n = len(axes)

    def plan(refs, send_sems, recv_sems):
        x, y, c = _me()
        out = []
        for t in range(n):
            for j, (px, py) in enumerate(_other_chips(x, y)):
                send = _remote(_chunk(refs[t], 2 * px + py, axes[t], sizes[t]), _chunk(refs[n + t], 2 * x + y, axes[t], sizes[t]),
                               send_sems[j], recv_sems[j], (px, py, c))
                recv = _remote(_chunk(refs[t], 2 * x + y, axes[t], sizes[t]), _chunk(refs[n + t], 2 * px + py, axes[t], sizes[t]),
                               send_sems[j], recv_sems[j], (px, py, c))
                out.append((send, recv))
        return out
    return plan


def _exchange_start(arrays, n_sems, plan, after, *, name):
    n = len(arrays)
    m = len(after)

    def body(*refs):
        send_sems, recv_sems = list(refs[n + m:n + m + n_sems]), list(refs[n + m + n_sems:n + m + 2 * n_sems])
        for send, _ in plan(refs[:n], send_sems, recv_sems):
            send.start()
        refs[-1][...] = jnp.zeros_like(refs[-1])

    outs = pl.pallas_call(
        body,
        out_shape=(*([pltpu.SemaphoreType.DMA(())] * (2 * n_sems)),
                   *[jax.ShapeDtypeStruct(a.shape, a.dtype) for a in arrays], jax.ShapeDtypeStruct((8, LANES), F32)),
        in_specs=[_ANY] * (n + m),
        out_specs=(*([_SEM] * (2 * n_sems)), *([_ANY] * n), pl.BlockSpec(memory_space=pltpu.VMEM)),
        input_output_aliases={t: 2 * n_sems + t for t in range(n)},
        compiler_params=pltpu.CompilerParams(has_side_effects=_EFFECT),
        name=name,
    )(*arrays, *after)
    return list(outs[:2 * n_sems]), list(outs[2 * n_sems:2 * n_sems + n]), outs[-1]


def _exchange_wait(sems, arrays, plan, after, *, name):
    n = len(arrays)
    n_sems = len(sems) // 2

    def body(*refs):
        send_sems, recv_sems = list(refs[n:n + n_sems]), list(refs[n + n_sems:n + 2 * n_sems])
        for send, recv in plan(refs[:n], send_sems, recv_sems):
            send.wait_send()
            recv.wait_recv()
        refs[-1][...] = jnp.zeros_like(refs[-1])

    outs = pl.pallas_call(
        body,
        out_shape=(*[jax.ShapeDtypeStruct(a.shape, a.dtype) for a in arrays], jax.ShapeDtypeStruct((8, LANES), F32)),
        in_specs=[_ANY] * n + [_SEM] * (2 * n_sems) + [_ANY] * len(after),
        out_specs=(*([_ANY] * n), pl.BlockSpec(memory_space=pltpu.VMEM)),
        input_output_aliases={t: t for t in range(n)},
        compiler_params=pltpu.CompilerParams(has_side_effects=_EFFECT),
        name=name,
    )(*arrays, *sems, *after)
    return list(outs[:n]), outs[-1]


def _add_pair(g, got, axis, core_arr, *, name):
    r, c = got.shape
    if axis == 0:
        rows, cols = r // 4, c
    else:
        rows, cols = r, c // 4
    tr = _tile(rows, 512)
    nr = rows // tr

    def body(core_ref, g_ref, got_ref, o_ref):
        o_ref[...] = (g_ref[...].astype(F32) + got_ref[...].astype(F32)).astype(o_ref.dtype)

    if axis == 0:
        g_spec = pl.BlockSpec((tr, cols), lambda k, i, core: ((2 * k + core[0]) * nr + i, 0))
        spec = pl.BlockSpec((tr, cols), lambda k, i, core: (k * nr + i, 0))
    else:
        g_spec = pl.BlockSpec((tr, cols), lambda k, i, core: (i, 2 * k + core[0]))
        spec = pl.BlockSpec((tr, cols), lambda k, i, core: (i, k))
    return pl.pallas_call(
        body,
        out_shape=jax.ShapeDtypeStruct(got.shape, got.dtype),
        grid_spec=pltpu.PrefetchScalarGridSpec(num_scalar_prefetch=1, grid=(4, nr), in_specs=[g_spec, spec], out_specs=spec),
        compiler_params=_cp("parallel", "parallel"),
        name=name,
    )(core_arr, g, got)


def _adam_math(w, g, m, v):
    m = ADAM_B1 * m + (1.0 - ADAM_B1) * g
    v = ADAM_B2 * v + (1.0 - ADAM_B2) * (g * g)
    m_hat = m / (1.0 - ADAM_B1 ** ADAM_STEP)
    v_hat = v / (1.0 - ADAM_B2 ** ADAM_STEP)
    delta = -ADAM_LR * (m_hat / (jnp.sqrt(v_hat) + ADAM_EPS) + ADAM_WD * w)
    return delta, m, v


def _adamw(w, m, v, parts, axis, chip_arr, layers, prev, after, *, name):
    n_l, r, c = w.shape
    first, count = layers
    tr = _tile(r, 128)
    nr = r // tr
    n_chip = 4
    n_in = 4 + n_chip * count

    def body(*refs):
        w_ref, m_ref, v_ref = refs[1:4]
        p_refs = refs[4:n_in]
        g_ref, d_ref, nm_ref, nv_ref = refs[-4:]
        for l in range(count):
            @pl.when(pl.program_id(0) == l)
            def _(l=l):
                g = p_refs[n_chip * l][...].astype(F32)
                for k in range(1, n_chip):
                    g = g + p_refs[n_chip * l + k][...].astype(F32)
                delta, nm, nv = _adam_math(w_ref[...], g, m_ref[...], v_ref[...])
                g_ref[...] = g
                d_ref[...] = delta
                nm_ref[...] = nm
                nv_ref[...] = nv

    def part_spec(ll, k):
        if axis == 0:
            return pl.BlockSpec((tr, c), lambda l, i, chips: (chips[k] * nr + jnp.where(l == ll, i, 0), 0))
        return pl.BlockSpec((tr, c), lambda l, i, chips: (jnp.where(l == ll, i, 0), chips[k]))

    wspec = pl.BlockSpec((None, tr, c), lambda l, i, chips: (first + l, i, 0))
    pspecs = [part_spec(ll, k) for ll in range(count) for k in range(n_chip)]
    sds = jax.ShapeDtypeStruct(w.shape, F32)
    kept = [] if prev is None else list(prev)
    return pl.pallas_call(
        body,
        out_shape=(sds, sds, sds, sds),
        grid_spec=pltpu.PrefetchScalarGridSpec(
            num_scalar_prefetch=1, grid=(count, nr),
            in_specs=[wspec, wspec, wspec] + pspecs + [_ANY] * (1 + len(kept)), out_specs=(wspec, wspec, wspec, wspec)),
        input_output_aliases={n_in + 1 + k: k for k in range(len(kept))},
        compiler_params=_cp("arbitrary", "arbitrary"),
        name=name,
    )(chip_arr, w, m, v, *[parts[first + ll][0 if k == 0 else 1] for ll in range(count) for k in range(n_chip)], after, *kept)


def _adamw_small(w, m, v, parts, *, name):
    r, c = w.shape
    tr = _tile(r, 512)

    def body(w_ref, m_ref, v_ref, p_ref, g_ref, d_ref, nm_ref, nv_ref):
        g = p_ref[0]
        for s in range(1, N_DEV):
            g = g + p_ref[s]
        delta, nm, nv = _adam_math(w_ref[...], g, m_ref[...], v_ref[...])
        g_ref[...] = g
        d_ref[...] = delta
        nm_ref[...] = nm
        nv_ref[...] = nv

    spec = pl.BlockSpec((tr, c), lambda i: (i, 0))
    sds = jax.ShapeDtypeStruct(w.shape, F32)
    return pl.pallas_call(
        body,
        out_shape=(sds, sds, sds, sds),
        grid=(r // tr,),
        in_specs=[spec, spec, spec, pl.BlockSpec((N_DEV, tr, c), lambda i: (0, i, 0))],
        out_specs=(spec, spec, spec, spec),
        compiler_params=_cp("parallel"),
        name=name,
    )(w, m, v, parts)


def _pack(arrs):
    flat = jnp.concatenate([a.reshape(-1).astype(F32) for a in arrs])
    pad = (-flat.shape[0]) % (8 * LANES)
    return jnp.pad(flat, (0, pad)).reshape(-1, LANES)


def _unpack(packed, shapes):
    flat = packed.reshape(-1)
    out, off = [], 0
    for sh in shapes:
        n = math.prod(sh)
        out.append(flat[off:off + n].reshape(sh))
        off += n
    return out


def _unpack_gathered(packed, shapes, axis):
    flat = packed.reshape(N_DEV, -1)
    out, off = [], 0
    for sh, ax in zip(shapes, axis):
        n = math.prod(sh)
        blocks = flat[:, off:off + n].reshape((N_DEV, *sh))
        out.append(jnp.concatenate([blocks[d] for d in range(N_DEV)], axis=ax))
        off += n
    return out


def _my_block(full, axis, size):
    idx = _dev_index(*_me())
    return lax.dynamic_slice_in_dim(full, idx * size, size, axis)


def kernel(x, mem, positions, norm_mix_g, norm_mem_q_g, norm_mem_kv_g, norm_ffn_g, final_norm_g, ab_w_in, a_v_norm_g, a_w_s, a_b_s, b_conv_w, ab_w_out, c_w_in, c_q_norm_g, c_kv_norm_g, c_w_uq, c_w_ukv, c_w_out, m_wq, m_wk, m_wv, m_wo, f_w1, f_w2, loss_target, m_norm_mix_g, m_norm_mem_q_g, m_norm_mem_kv_g, m_norm_ffn_g, m_final_norm_g, m_ab_w_in, m_a_v_norm_g, m_a_w_s, m_a_b_s, m_b_conv_w, m_ab_w_out, m_c_w_in, m_c_q_norm_g, m_c_kv_norm_g, m_c_w_uq, m_c_w_ukv, m_c_w_out, m_m_wq, m_m_wk, m_m_wv, m_m_wo, m_f_w1, m_f_w2, v_norm_mix_g, v_norm_mem_q_g, v_norm_mem_kv_g, v_norm_ffn_g, v_final_norm_g, v_ab_w_in, v_a_v_norm_g, v_a_w_s, v_a_b_s, v_b_conv_w, v_ab_w_out, v_c_w_in, v_c_q_norm_g, v_c_kv_norm_g, v_c_w_uq, v_c_w_ukv, v_c_w_out, v_m_wq, v_m_wk, v_m_wv, v_m_wo, v_f_w1, v_f_w2):
    bl, seq, d = x.shape
    t = bl * seq
    n_mem = mem.shape[1]
    depth = norm_mix_g.shape[0]
    n_even, n_odd = ab_w_in.shape[0], c_w_in.shape[0]
    a_width = a_v_norm_g.shape[1]
    groups = a_w_s.shape[1]
    heads = d // C_NOPE
    q_rank = c_q_norm_g.shape[1] * N_DEV
    kv_rank = c_kv_norm_g.shape[1] * N_DEV

    big = {
        "ab_w_in": (ab_w_in, m_ab_w_in, v_ab_w_in, 1), "ab_w_out": (ab_w_out, m_ab_w_out, v_ab_w_out, 0),
        "c_w_in": (c_w_in, m_c_w_in, v_c_w_in, 0), "c_w_uq": (c_w_uq, m_c_w_uq, v_c_w_uq, 1),
        "c_w_ukv": (c_w_ukv, m_c_w_ukv, v_c_w_ukv, 1), "c_w_out": (c_w_out, m_c_w_out, v_c_w_out, 0),
        "m_wq": (m_wq, m_m_wq, v_m_wq, 0), "m_wk": (m_wk, m_m_wk, v_m_wk, 0), "m_wv": (m_wv, m_m_wv, v_m_wv, 0),
        "m_wo": (m_wo, m_m_wo, v_m_wo, 0), "f_w1": (f_w1, m_f_w1, v_f_w1, 1), "f_w2": (f_w2, m_f_w2, v_f_w2, 0),
    }
    even_names = ["ab_w_in", "ab_w_out"]
    odd_names = ["c_w_in", "c_w_uq", "c_w_ukv", "c_w_out"]
    all_names = ["m_wq", "m_wk", "m_wv", "m_wo", "f_w1", "f_w2"]

    def layer_names(layer):
        return (even_names if layer % 2 == 0 else odd_names) + all_names

    def sub(name, layer):
        return layer // 2 if name in even_names or name in odd_names else layer

    split_small = [(c_q_norm_g, m_c_q_norm_g, v_c_q_norm_g, 1), (c_kv_norm_g, m_c_kv_norm_g, v_c_kv_norm_g, 1), (b_conv_w, m_b_conv_w, v_b_conv_w, 2)]
    shard_shapes = [s[0].shape for s in split_small] * 3
    shard_axes = [s[3] for s in split_small] * 3
    gathered = _all_gather_small(_pack([s[0] for s in split_small] + [s[1] for s in split_small] + [s[2] for s in split_small]))
    whole = _unpack_gathered(gathered, shard_shapes, shard_axes)
    q_norm_full, kv_norm_full, conv_full = whole[0:3]
    small_w = [norm_mix_g, norm_mem_q_g, norm_mem_kv_g, norm_ffn_g, final_norm_g, a_v_norm_g, a_w_s, a_b_s] + whole[0:3]
    small_m = [m_norm_mix_g, m_norm_mem_q_g, m_norm_mem_kv_g, m_norm_ffn_g, m_final_norm_g, m_a_v_norm_g, m_a_w_s, m_a_b_s] + whole[3:6]
    small_v = [v_norm_mix_g, v_norm_mem_q_g, v_norm_mem_kv_g, v_norm_ffn_g, v_final_norm_g, v_a_v_norm_g, v_a_w_s, v_a_b_s] + whole[6:9]

    inv = ROPE_THETA ** (-jnp.arange(0, C_ROPE, 2, dtype=F32) / C_ROPE)
    ang = positions.astype(F32).reshape(t, 1) * inv
    cos, sin, zero = jnp.cos(ang), jnp.sin(ang), jnp.zeros((t, C_ROPE // 2), F32)
    rc = jnp.concatenate([cos, cos, zero, zero], axis=1)
    rs1 = jnp.concatenate([-sin, zero, zero, zero], axis=1)
    rs2 = jnp.concatenate([zero, sin, zero, zero], axis=1)

    my_x, my_y, my_c = _me()
    my_chip = 2 * my_x + my_y
    dev_arr = jnp.stack([_dev_index(my_x, my_y, my_c)]).astype(jnp.int32)
    core_arr = jnp.stack([my_c]).astype(jnp.int32)
    chip_arr = jnp.stack([my_chip, my_chip ^ 1, my_chip ^ 2, my_chip ^ 3]).astype(jnp.int32)

    n_stage = 3 * depth
    chain = [None]
    gat = {}
    issued = [0]

    def stage_names(i):
        layer, gid = divmod(i, 3)
        return layer, [even_names if layer % 2 == 0 else odd_names, all_names[:4], all_names[4:]][gid]

    def order(after):
        return [a for a in (chain[0], after) if a is not None]

    def gather_ici_start(i, after):
        layer, names = stage_names(i)
        placed = [_cast_place(big[nm][0][sub(nm, layer)], big[nm][3], dev_arr, name="cast_place") for nm in names]
        axes = [big[nm][3] for nm in names]
        sizes = [p.shape[a] // N_DEV for p, a in zip(placed, axes)]
        st = dict(names=names, ici=_plan_gather_ici(axes, sizes), d2d=_plan_gather_d2d(axes, sizes))
        st["sems"], st["arrays"], chain[0] = _exchange_start(placed, N_PEER_CHIPS, st["ici"], order(after), name=f"gather_ici_start_s{i}")
        gat[i] = st

    def gather_stage(i, after):
        while issued[0] < min(i + 3, n_stage):
            gather_ici_start(issued[0], after)
            issued[0] += 1
        ahead = [i, i + 1] if i >= 3 and i + 1 < n_stage else [i]
        for j in ahead:
            st = gat[j]
            if "landed" not in st:
                st["arrays"], chain[0] = _exchange_wait(st["sems"], st["arrays"], st["ici"], order(after), name=f"gather_ici_wait_s{j}")
                st["landed"] = True
            if "sems2" not in st:
                st["sems2"], st["arrays"], chain[0] = _exchange_start(st["arrays"], 1, st["d2d"], order(after), name=f"gather_d2d_start_s{j}")
        st = gat[i]
        st["arrays"], chain[0] = _exchange_wait(st["sems2"], st["arrays"], st["d2d"], order(after), name=f"gather_d2d_wait_s{i}")
        return dict(zip(st["names"], st["arrays"]))

    wts = []

    def group_cols(w_in):
        return w_in.reshape(d, 5, groups, LANES).transpose(0, 2, 1, 3).reshape(d, 5 * a_width)

    def ungroup_cols(g_in):
        return g_in.reshape(d, groups, 5, LANES).transpose(0, 2, 1, 3).reshape(d, 5 * a_width)

    def group_rows(w_out):
        return w_out.reshape(2, groups, LANES, d).transpose(1, 0, 2, 3).reshape(2 * a_width, d)

    def ungroup_rows(g_out):
        return g_out.reshape(groups, 2, LANES, d).transpose(1, 0, 2, 3).reshape(2 * a_width, d)

    def pad_heads(w_uq):
        w3 = w_uq.reshape(q_rank, heads, C_NOPE + C_ROPE)
        return jnp.pad(w3, ((0, 0), (0, 0), (0, LANES - C_ROPE))).reshape(q_rank, heads * 2 * LANES)

    def unpad_heads(g_uq):
        return g_uq.reshape(q_rank, heads, 2 * LANES)[:, :, :C_NOPE + C_ROPE].reshape(q_rank, heads * (C_NOPE + C_ROPE))

    x2 = x.reshape(t, d)
    mem2 = mem.reshape(bl * n_mem, d)
    saved = []

    for layer in range(depth):
        w = gather_stage(3 * layer, x2)
        wts.append(w)
        sv = {"x0": x2}
        xn = _rmsnorm_fwd(x2, norm_mix_g[layer], name="norm_mix")
        sv["xn"] = xn
        if layer % 2 == 0:
            e = layer // 2
            w_in_g = group_cols(w["ab_w_in"])
            w_out_g = group_rows(w["ab_w_out"])
            z = _mm(xn, w_in_g, out_dtypes=(F32,), extras=[(chain[0], "after")], name="ab_in")
            ws_e = a_w_s[e]
            bs_e = a_b_s[e].reshape(groups, GMLP_BLOCK, 1)
            gn_e = a_v_norm_g[e].reshape(groups, 1, LANES)
            cw_e = conv_full[e].reshape(3, groups, LANES).transpose(1, 0, 2)
            ycat = _gmlp_conv_fwd(z, ws_e, bs_e, gn_e, cw_e, seq=seq, name="gmlp_conv_fwd")
            x2 = _mm(ycat, w_out_g, out_dtypes=(F32,), extras=[(x2, "mn")], epilogue=_epi_add, name="ab_out")
            sv.update(z=z, ycat=ycat, w_in_g=w_in_g, w_out_g=w_out_g, ws=ws_e, bs=bs_e, gn=gn_e, cw=cw_e)
        else:
            o = layer // 2
            w_uq_p = pad_heads(w["c_w_uq"])
            z = _mm(xn, w["c_w_in"], out_dtypes=(F32,), extras=[(chain[0], "after")], name="c_in")
            qn = _rmsnorm_fwd(z, q_norm_full[o], width=q_rank, col=0, name="norm_cq")
            kvn = _rmsnorm_fwd(z, kv_norm_full[o], width=kv_rank, col=q_rank // kv_rank, name="norm_ckv")
            qh = _mm(qn, w_uq_p, extras=[(rc, "m"), (rs1, "m"), (rs2, "m")], epilogue=_epi_rope_heads, tn=1024, name="c_uq")
            kvh = _mm(kvn, w["c_w_ukv"], name="c_ukv")
            k_rope = jnp.pad(z[:, q_rank + kv_rank:], ((0, 0), (0, LANES - C_ROPE)))
            kr = _rope_rows(k_rope, rc, rs1, rs2, transpose=False, out_dtype=BF16, name="rope_k")
            att, lse = _mla_fwd(qh, kvh, kr, seq=seq, heads=heads, name="mla_fwd")
            x2 = _mm(att, w["c_w_out"], out_dtypes=(F32,), extras=[(x2, "mn")], epilogue=_epi_add, name="c_out")
            sv.update(z=z, qn=qn, kvn=kvn, qh=qh, kvh=kvh, kr=kr, att=att, lse=lse, w_uq_p=w_uq_p)
        sv["x1"] = x2
        w.update(gather_stage(3 * layer + 1, x2))
        xq = _rmsnorm_fwd(x2, norm_mem_q_g[layer], name="norm_mem_q")
        memn = _rmsnorm_fwd(mem2, norm_mem_kv_g[layer], name="norm_mem_kv")
        mq = _mm(xq, w["m_wq"], extras=[(chain[0], "after")], name="m_q")
        mk = _mm(memn, w["m_wk"], name="m_k")
        mv = _mm(memn, w["m_wv"], name="m_v")
        mo = _mem_fwd(mq, mk, mv, seq=seq, n_mem=n_mem, name="mem_fwd")
        x2 = _mm(mo, w["m_wo"], out_dtypes=(F32,), extras=[(x2, "mn")], epilogue=_epi_add, name="m_o")
        sv.update(xq=xq, memn=memn, mq=mq, mk=mk, mv=mv, mo=mo)
        sv["x2"] = x2
        w.update(gather_stage(3 * layer + 2, x2))
        xf = _rmsnorm_fwd(x2, norm_ffn_g[layer], name="norm_ffn")
        hpre, hact = _mm(xf, w["f_w1"], out_dtypes=(BF16, BF16), extras=[(chain[0], "after")], epilogue=_epi_relu2, name="f_1")
        x2 = _mm(hact, w["f_w2"], out_dtypes=(F32,), extras=[(x2, "mn")], epilogue=_epi_add, name="f_2")
        sv.update(xf=xf, hpre=hpre, hact=hact)
        saved.append(sv)

    loss_part, dx, dg_final = _final_loss(x2, final_norm_g, loss_target.reshape(t, d), name="final_loss")
    loss = lax.psum(loss_part[0, 0], ("x", "y", "c"))

    dg_mix, dg_mq, dg_mkv, dg_ffn = [None] * depth, [None] * depth, [None] * depth, [None] * depth
    d_av, d_ws, d_bs = [None] * n_even, [None] * n_even, [None] * n_even
    d_conv, d_qn, d_kvn = [None] * n_even, [None] * n_odd, [None] * n_odd
    parts = {nm: [None] * big[nm][0].shape[0] for nm in big}

    sca = []

    def scatter_finish_d2d(st, after):
        n = len(st["names"])
        arrays, chain[0] = _exchange_wait(st["sems"], st["arrays"], st["d2d"], order(after), name=f"scatter_d2d_wait_s{st['id']}")
        sums = [_add_pair(g, got, ax, core_arr, name="add_pair") for g, got, ax in zip(arrays[:n], arrays[n:], st["axes"])]
        lands = [pltpu.with_memory_space_constraint(lax.empty(s_.shape, s_.dtype), pltpu.HBM) for s_ in sums]
        st["sems2"], st["arrays"], chain[0] = _exchange_start(sums + lands, N_PEER_CHIPS, st["ici"], order(after), name=f"scatter_ici_start_s{st['id']}")

    def scatter_stage(layer, names, grads, after):
        if sca and "sems2" not in sca[-1]:
            scatter_finish_d2d(sca[-1], after)
        axes = [big[nm][3] for nm in names]
        arrs = [grads[nm] for nm in names]
        sizes8 = [g.shape[a] // N_DEV for g, a in zip(arrs, axes)]
        gots = []
        for g, a in zip(arrs, axes):
            sh = list(g.shape)
            sh[a] //= 2
            gots.append(pltpu.with_memory_space_constraint(lax.empty(tuple(sh), g.dtype), pltpu.HBM))
        st = dict(id=len(sca), layer=layer, names=names, axes=axes, d2d=_plan_scatter_d2d(axes, sizes8),
                  ici=_plan_scatter_ici(axes, sizes8))
        st["sems"], st["arrays"], chain[0] = _exchange_start(arrs + gots, 1, st["d2d"], order(after), name=f"scatter_d2d_start_s{st['id']}")
        sca.append(st)

    def scatter_collect(st, after):
        n = len(st["names"])
        arrays, chain[0] = _exchange_wait(st["sems2"], st["arrays"], st["ici"], order(after), name=f"scatter_ici_wait_s{st['id']}")
        for nm, s_, r in zip(st["names"], arrays[:n], arrays[n:]):
            parts[nm][sub(nm, st["layer"])] = (s_, r)

    for layer in reversed(range(depth)):
        w, sv = wts[layer], saved[layer]
        grads = {}
        dh = _mm(dx, w["f_w2"], tb=True, extras=[(sv["hpre"], "mn"), (chain[0], "after")], epilogue=_epi_drelu2, name="f_2_dx")
        grads["f_w2"] = _mm(sv["hact"], dx, ta=True, name="f_2_dw")
        grads["f_w1"] = _mm(sv["xf"], dh, ta=True, name="f_1_dw")
        dxf = _mm(dh, w["f_w1"], tb=True, name="f_1_dx")
        dx, dg_ffn[layer] = _rmsnorm_bwd(sv["x2"], norm_ffn_g[layer], dxf, dx, name="norm_ffn_bwd")
        scatter_stage(layer, all_names[4:], grads, dx)
        dmo = _mm(dx, w["m_wo"], tb=True, extras=[(chain[0], "after")], name="m_o_dx")
        grads["m_wo"] = _mm(sv["mo"], dx, ta=True, name="m_o_dw")
        dmq, dmk, dmv = _mem_bwd(sv["mq"], sv["mk"], sv["mv"], dmo, seq=seq, n_mem=n_mem, name="mem_bwd")
        grads["m_wq"] = _mm(sv["xq"], dmq, ta=True, name="m_q_dw")
        grads["m_wk"] = _mm(sv["memn"], dmk, ta=True, name="m_k_dw")
        grads["m_wv"] = _mm(sv["memn"], dmv, ta=True, name="m_v_dw")
        dxq = _mm(dmq, w["m_wq"], tb=True, name="m_q_dx")
        dmemn_k = _mm(dmk, w["m_wk"], tb=True, out_dtypes=(F32,), name="m_k_dx")
        dmemn = _mm(dmv, w["m_wv"], tb=True, out_dtypes=(F32,), extras=[(dmemn_k, "mn")], epilogue=_epi_add, name="m_v_dx")
        _, dg_mkv[layer] = _rmsnorm_bwd(mem2, norm_mem_kv_g[layer], dmemn, name="norm_mem_kv_bwd")
        dx, dg_mq[layer] = _rmsnorm_bwd(sv["x1"], norm_mem_q_g[layer], dxq, dx, name="norm_mem_q_bwd")
        scatter_stage(layer, all_names[:4], grads, dx)
        if layer % 2 == 0:
            e = layer // 2
            dycat = _mm(dx, sv["w_out_g"], tb=True, extras=[(chain[0], "after")], name="ab_out_dx")
            grads["ab_w_out"] = ungroup_rows(_mm(sv["ycat"], dx, ta=True, name="ab_out_dw"))
            dz, dws, dbs, dgn, dcw = _gmlp_conv_bwd(sv["z"], dycat, sv["ws"], sv["bs"], sv["gn"], sv["cw"], seq=seq, name="gmlp_conv_bwd")
            d_ws[e], d_bs[e], d_av[e] = dws, dbs.reshape(groups, GMLP_BLOCK), dgn.reshape(a_width)
            d_conv[e] = dcw.transpose(1, 0, 2).reshape(3, a_width)
            grads["ab_w_in"] = ungroup_cols(_mm(sv["xn"], dz, ta=True, name="ab_in_dw"))
            dxn = _mm(dz, sv["w_in_g"], tb=True, name="ab_in_dx")
        else:
            o = layer // 2
            datt = _mm(dx, w["c_w_out"], tb=True, extras=[(chain[0], "after")], name="c_out_dx")
            grads["c_w_out"] = _mm(sv["att"], dx, ta=True, name="c_out_dw")
            dqh, dkvh, dkr = _mla_bwd(sv["qh"], sv["kvh"], sv["kr"], datt, sv["att"], sv["lse"], seq=seq, heads=heads, name="mla_bwd")
            dq_pre = _rope_rows(dqh, rc, rs1, rs2, transpose=True, out_dtype=BF16, name="rope_q_bwd")
            dk_rope = _rope_rows(dkr, rc, rs1, rs2, transpose=True, out_dtype=F32, name="rope_k_bwd")
            grads["c_w_uq"] = unpad_heads(_mm(sv["qn"], dq_pre, ta=True, name="c_uq_dw"))
            grads["c_w_ukv"] = _mm(sv["kvn"], dkvh, ta=True, name="c_ukv_dw")
            dqn = _mm(dq_pre, sv["w_uq_p"], tb=True, name="c_uq_dx")
            dkvn = _mm(dkvh, w["c_w_ukv"], tb=True, name="c_ukv_dx")
            dcq, d_qn[o] = _rmsnorm_bwd(sv["z"], q_norm_full[o], dqn, width=q_rank, col=0, name="norm_cq_bwd")
            dckv, d_kvn[o] = _rmsnorm_bwd(sv["z"], kv_norm_full[o], dkvn, width=kv_rank, col=q_rank // kv_rank, name="norm_ckv_bwd")
            dz = jnp.concatenate([dcq, dckv, dk_rope[:, :C_ROPE]], axis=1)
            grads["c_w_in"] = _mm(sv["xn"], dz, ta=True, name="c_in_dw")
            dxn = _mm(dz, w["c_w_in"], tb=True, name="c_in_dx")
        dx, dg_mix[layer] = _rmsnorm_bwd(sv["x0"], norm_mix_g[layer], dxn, dx, name="norm_mix_bwd")
        scatter_stage(layer, even_names if layer % 2 == 0 else odd_names, grads, dx)

    grad_x = dx.reshape(bl, seq, d)

    scatter_finish_d2d(sca[-1], dx)
    for st in sca[:-3]:
        scatter_collect(st, dx)
    big_out = {}
    done = chain[0]
    for nm, (wv, mv_, vv, ax) in big.items():
        n_sub = wv.shape[0]
        first = 0 if nm in odd_names else 1
        if first < n_sub:
            big_out[nm] = _adamw(wv, mv_, vv, parts[nm], ax, chip_arr, (first, n_sub - first), None, done, name=f"adamw_{nm}")
            done = big_out[nm][0]
    for st in sca[-3:]:
        scatter_collect(st, done)
    for nm, (wv, mv_, vv, ax) in big.items():
        if nm not in odd_names:
            big_out[nm] = _adamw(wv, mv_, vv, parts[nm], ax, chip_arr, (0, 1), big_out.get(nm), chain[0], name=f"adamw_{nm}_first")

    small_g = [jnp.concatenate(dg_mix), jnp.concatenate(dg_mq), jnp.concatenate(dg_mkv), jnp.concatenate(dg_ffn), dg_final.reshape(d),
               jnp.stack(d_av), jnp.stack(d_ws), jnp.stack(d_bs), jnp.concatenate(d_qn), jnp.concatenate(d_kvn), jnp.stack(d_conv)]
    small_shapes = [a.shape for a in small_w]
    g_all = _all_gather_small(_pack(small_g))
    sg, sd, sm, svv = _adamw_small(_pack(small_w), _pack(small_m), _pack(small_v), g_all, name="adamw_small")
    small_names = ["norm_mix_g", "norm_mem_q_g", "norm_mem_kv_g", "norm_ffn_g", "final_norm_g", "a_v_norm_g", "a_w_s", "a_b_s", "c_q_norm_g", "c_kv_norm_g", "b_conv_w"]
    small_out = {}
    unpacked = [_unpack(p, small_shapes) for p in (sg, sd, sm, svv)]
    for i, nm in enumerate(small_names):
        vals = [u[i] for u in unpacked]
        if i >= 8:
            full_w, _, _, ax = split_small[i - 8]
            vals = [_my_block(val, ax, full_w.shape[ax]) for val in vals]
        small_out[nm] = vals

    order = ["norm_mix_g", "norm_mem_q_g", "norm_mem_kv_g", "norm_ffn_g", "final_norm_g", "ab_w_in", "a_v_norm_g", "a_w_s", "a_b_s", "b_conv_w", "ab_w_out",
             "c_w_in", "c_q_norm_g", "c_kv_norm_g", "c_w_uq", "c_w_ukv", "c_w_out", "m_wq", "m_wk", "m_wv", "m_wo", "f_w1", "f_w2"]
    res = [loss, grad_x]
    for kind in range(4):
        for nm in order:
            res.append(big_out[nm][kind] if nm in big_out else small_out[nm][kind])
    return tuple(res)
```

```python
import functools
import math

import jax
import jax.numpy as jnp
from jax import lax
from jax.experimental import pallas as pl
from jax.experimental.pallas import tpu as pltpu

F32 = jnp.float32
BF16 = jnp.bfloat16
MESH = pl.DeviceIdType.MESH
N_DEV = 8
LANES = 128

RMS_EPS = 1e-6
CHUNK = 64
GMLP_BLOCK = 128
C_NOPE = 128
C_ROPE = 64
C_V = 128
ROPE_THETA = 10000.0
MEM_HEADS = 4
ADAM_LR = 0.001
ADAM_B1 = 0.9
ADAM_B2 = 0.999
ADAM_EPS = 1e-08
ADAM_WD = 0.01
ADAM_STEP = 10
NEG_BIG = -1e30
VMEM_LIMIT = 56 * 1024 * 1024


def _cp(*sem):
    return pltpu.CompilerParams(dimension_semantics=sem, vmem_limit_bytes=VMEM_LIMIT)


def _tile(n, pref):
    t = min(n, pref)
    while n % t and t > LANES:
        t //= 2
    return n if n % t else t


def _mm(a, b, *, ta=False, tb=False, out_dtypes=(BF16,), extras=(), epilogue=None, tm=1024, tn=1024, tk=2048, name):
    m, k = (a.shape[1], a.shape[0]) if ta else a.shape
    n = b.shape[0] if tb else b.shape[1]
    assert (b.shape[1] if tb else b.shape[0]) == k, (a.shape, b.shape, ta, tb)
    tm, tn, tk = _tile(m, tm), _tile(n, tn), _tile(k, tk)
    nk = k // tk
    ne, no = len(extras), len(out_dtypes)
    dims = (((0 if ta else 1,), (1 if tb else 0,)), ((), ()))

    def body(*refs):
        a_ref, b_ref = refs[0], refs[1]
        e_refs = refs[2:2 + ne]
        o_refs = refs[2 + ne:2 + ne + no]

        def finish(acc):
            vals = [e[...] for e, (_, kind) in zip(e_refs, extras) if kind != "after"]
            outs = (acc,) if epilogue is None else epilogue(acc, *vals)
            for o, val in zip(o_refs, outs):
                o[...] = val.astype(o.dtype)

        prod = lax.dot_general(a_ref[...].astype(BF16), b_ref[...].astype(BF16), dims, preferred_element_type=F32)
        if nk == 1:
            finish(prod)
        else:
            acc_ref = refs[-1]
            kk = pl.program_id(2)

            @pl.when(kk == 0)
            def _():
                acc_ref[...] = prod

            @pl.when((kk > 0) & (kk < nk - 1))
            def _():
                acc_ref[...] += prod

            @pl.when(kk == nk - 1)
            def _():
                finish(acc_ref[...] + prod)

    a_spec = pl.BlockSpec((tk, tm), lambda i, j, kk: (kk, i)) if ta else pl.BlockSpec((tm, tk), lambda i, j, kk: (i, kk))
    b_spec = pl.BlockSpec((tn, tk), lambda i, j, kk: (j, kk)) if tb else pl.BlockSpec((tk, tn), lambda i, j, kk: (kk, j))
    e_specs = []
    for arr, kind in extras:
        if kind == "mn":
            e_specs.append(pl.BlockSpec((tm, tn), lambda i, j, kk: (i, j)))
        elif kind == "after":
            e_specs.append(pl.BlockSpec(arr.shape, lambda i, j, kk: (0, 0)))
        else:
            e_specs.append(pl.BlockSpec((tm, arr.shape[1]), lambda i, j, kk: (i, 0)))
    outs = pl.pallas_call(
        body,
        out_shape=tuple(jax.ShapeDtypeStruct((m, n), d) for d in out_dtypes),
        grid=(m // tm, n // tn, nk),
        in_specs=[a_spec, b_spec] + e_specs,
        out_specs=tuple(pl.BlockSpec((tm, tn), lambda i, j, kk: (i, j)) for _ in out_dtypes),
        scratch_shapes=[pltpu.VMEM((tm, tn), F32)] if nk > 1 else [],
        compiler_params=_cp("parallel", "parallel", "arbitrary"),
        name=name,
    )(a, b, *[arr for arr, _ in extras])
    return outs[0] if no == 1 else outs


def _epi_add(acc, r):
    return (acc + r,)


def _epi_relu2(acc):
    h = jnp.maximum(acc, 0.0)
    return (acc, h * h)


def _epi_drelu2(acc, h):
    return (acc * (2.0 * jnp.maximum(h.astype(F32), 0.0)),)


def _rope(x, c, s1, s2):
    return x * c + pltpu.roll(x, LANES - C_ROPE // 2, 1) * s1 + pltpu.roll(x, C_ROPE // 2, 1) * s2


def _rope_t(dy, c, s1, s2):
    return dy * c + pltpu.roll(dy * s1, C_ROPE // 2, 1) + pltpu.roll(dy * s2, LANES - C_ROPE // 2, 1)


def _epi_rope_heads(acc, c, s1, s2):
    parts = []
    for h0 in range(0, acc.shape[1], 2 * LANES):
        parts.append(acc[:, h0:h0 + LANES])
        parts.append(_rope(acc[:, h0 + LANES:h0 + 2 * LANES], c, s1, s2))
    return (jnp.concatenate(parts, axis=1),)


def _rmsnorm_fwd(x, g, *, width=None, col=0, name):
    t = x.shape[0]
    w = x.shape[1] if width is None else width
    tm = _tile(t, 512)

    def body(x_ref, g_ref, o_ref):
        xv = x_ref[...].astype(F32)
        r = lax.rsqrt(jnp.mean(xv * xv, axis=-1, keepdims=True) + RMS_EPS)
        o_ref[...] = (xv * r * g_ref[...]).astype(o_ref.dtype)

    return pl.pallas_call(
        body,
        out_shape=jax.ShapeDtypeStruct((t, w), BF16),
        grid=(t // tm,),
        in_specs=[pl.BlockSpec((tm, w), lambda i: (i, col)), pl.BlockSpec((1, w), lambda i: (0, 0))],
        out_specs=pl.BlockSpec((tm, w), lambda i: (i, 0)),
        compiler_params=_cp("parallel"),
        name=name,
    )(x, g.reshape(1, w))


def _rmsnorm_bwd(x, g, dy, res=None, *, width=None, col=0, name):
    t = x.shape[0]
    w = x.shape[1] if width is None else width
    tm = _tile(t, 256)
    has_res = res is not None

    def body(*refs):
        x_ref, g_ref, dy_ref = refs[:3]
        res_ref = refs[3] if has_res else None
        dx_ref, dg_ref = refs[-2], refs[-1]
        xv = x_ref[...].astype(F32)
        r = lax.rsqrt(jnp.mean(xv * xv, axis=-1, keepdims=True) + RMS_EPS)
        xh = xv * r
        dyv = dy_ref[...].astype(F32)
        dxh = dyv * g_ref[...]
        dx = r * (dxh - xh * jnp.mean(dxh * xh, axis=-1, keepdims=True))
        if has_res:
            dx = dx + res_ref[...]
        dx_ref[...] = dx

        @pl.when(pl.program_id(0) == 0)
        def _():
            dg_ref[...] = jnp.zeros_like(dg_ref)

        dg_ref[...] += jnp.sum(dyv * xh, axis=0, keepdims=True)

    in_specs = [pl.BlockSpec((tm, w), lambda i: (i, col)), pl.BlockSpec((1, w), lambda i: (0, 0)), pl.BlockSpec((tm, w), lambda i: (i, 0))]
    args = [x, g.reshape(1, w), dy]
    if has_res:
        in_specs.append(pl.BlockSpec((tm, w), lambda i: (i, 0)))
        args.append(res)
    return pl.pallas_call(
        body,
        out_shape=(jax.ShapeDtypeStruct((t, w), F32), jax.ShapeDtypeStruct((1, w), F32)),
        grid=(t // tm,),
        in_specs=in_specs,
        out_specs=(pl.BlockSpec((tm, w), lambda i: (i, 0)), pl.BlockSpec((1, w), lambda i: (0, 0))),
        compiler_params=_cp("arbitrary"),
        name=name,
    )(*args)


def _final_loss(x, g, target, *, name):
    t, d = x.shape
    tm = _tile(t, 256)

    def body(x_ref, g_ref, t_ref, loss_ref, dx_ref, dg_ref):
        xv = x_ref[...]
        r = lax.rsqrt(jnp.mean(xv * xv, axis=-1, keepdims=True) + RMS_EPS)
        xh = xv * r
        gv = g_ref[...]
        err = xh * gv - t_ref[...]
        dyv = err * (1.0 / d)
        dxh = dyv * gv
        dx_ref[...] = r * (dxh - xh * jnp.mean(dxh * xh, axis=-1, keepdims=True))

        @pl.when(pl.program_id(0) == 0)
        def _():
            dg_ref[...] = jnp.zeros_like(dg_ref)
            loss_ref[...] = jnp.zeros_like(loss_ref)

        dg_ref[...] += jnp.sum(dyv * xh, axis=0, keepdims=True)
        row = jnp.sum(err * err, axis=-1, keepdims=True) * (0.5 / d)
        loss_ref[...] += jnp.sum(row, axis=0, keepdims=True)

    return pl.pallas_call(
        body,
        out_shape=(jax.ShapeDtypeStruct((1, 1), F32), jax.ShapeDtypeStruct((t, d), F32), jax.ShapeDtypeStruct((1, d), F32)),
        grid=(t // tm,),
        in_specs=[pl.BlockSpec((tm, d), lambda i: (i, 0)), pl.BlockSpec((1, d), lambda i: (0, 0)), pl.BlockSpec((tm, d), lambda i: (i, 0))],
        out_specs=(pl.BlockSpec((1, 1), lambda i: (0, 0)), pl.BlockSpec((tm, d), lambda i: (i, 0)), pl.BlockSpec((1, d), lambda i: (0, 0))),
        compiler_params=_cp("arbitrary"),
        name=name,
    )(x, g.reshape(1, d), target)


_GELU_C = math.sqrt(2.0 / math.pi)
_GELU_A = 0.044715


def _gelu(x):
    t = jnp.tanh(_GELU_C * (x + _GELU_A * x * x * x))
    return 0.5 * x * (1.0 + t), t


def _dgelu(x, t):
    return 0.5 * (1.0 + t) + 0.5 * x * (1.0 - t * t) * (_GELU_C * (1.0 + 3.0 * _GELU_A * x * x))


def _spatial_mask():
    ci = lax.broadcasted_iota(jnp.int32, (GMLP_BLOCK, GMLP_BLOCK), 0) // CHUNK
    cj = lax.broadcasted_iota(jnp.int32, (GMLP_BLOCK, GMLP_BLOCK), 1) // CHUNK
    return (cj <= ci).astype(F32)


def _shift_down(x, k):
    rows = lax.broadcasted_iota(jnp.int32, x.shape, 0)
    return jnp.where(rows >= k, pltpu.roll(x, k, 0), 0.0)


def _shift_up(x, k):
    n = x.shape[0]
    rows = lax.broadcasted_iota(jnp.int32, x.shape, 0)
    return jnp.where(rows < n - k, pltpu.roll(x, n - k, 0), 0.0)


def _gmlp_conv_fwd(z, ws, bs, gn, cw, *, seq, name):
    t = z.shape[0]
    g_n = ws.shape[0]
    nb = seq // GMLP_BLOCK
    L = LANES

    def body(z_ref, ws_ref, bs_ref, gn_ref, cw_ref, y_ref):
        u = z_ref[:, 0:L]
        v = z_ref[:, L:2 * L]
        gu, _ = _gelu(u)
        gv, _ = _gelu(v)
        r = lax.rsqrt(jnp.mean(gv * gv, axis=-1, keepdims=True) + RMS_EPS)
        vg = (gv * r * gn_ref[0]).astype(BF16)
        wm = (ws_ref[0] * _spatial_mask()).astype(BF16)
        bias = bs_ref[0]
        for n in range(nb):
            sl = slice(n * GMLP_BLOCK, (n + 1) * GMLP_BLOCK)
            mixed = jnp.dot(wm, vg[sl], preferred_element_type=F32) + bias
            y_ref[sl, 0:L] = (gu[sl] * mixed).astype(y_ref.dtype)
        bg = z_ref[:, 2 * L:3 * L]
        zc = z_ref[:, 3 * L:4 * L] * z_ref[:, 4 * L:5 * L]
        w0, w1, w2 = cw_ref[0, 0:1, :], cw_ref[0, 1:2, :], cw_ref[0, 2:3, :]
        conv = w0 * _shift_down(zc, 2) + w1 * _shift_down(zc, 1) + w2 * zc
        y_ref[:, L:2 * L] = (bg * conv).astype(y_ref.dtype)

    return pl.pallas_call(
        body,
        out_shape=jax.ShapeDtypeStruct((t, 2 * L * g_n), BF16),
        grid=(g_n, t // seq),
        in_specs=[
            pl.BlockSpec((seq, 5 * L), lambda g, b: (b, g)),
            pl.BlockSpec((1, GMLP_BLOCK, GMLP_BLOCK), lambda g, b: (g, 0, 0)),
            pl.BlockSpec((1, GMLP_BLOCK, 1), lambda g, b: (g, 0, 0)),
            pl.BlockSpec((1, 1, L), lambda g, b: (g, 0, 0)),
            pl.BlockSpec((1, 3, L), lambda g, b: (g, 0, 0)),
        ],
        out_specs=pl.BlockSpec((seq, 2 * L), lambda g, b: (b, g)),
        compiler_params=_cp("parallel", "parallel"),
        name=name,
    )(z, ws, bs, gn, cw)


def _gmlp_conv_bwd(z, dy, ws, bs, gn, cw, *, seq, name):
    t = z.shape[0]
    g_n = ws.shape[0]
    nb = seq // GMLP_BLOCK
    L = LANES
    dims_nt = (((1,), (1,)), ((), ()))
    dims_tn = (((0,), (0,)), ((), ()))

    def body(z_ref, dy_ref, ws_ref, bs_ref, gn_ref, cw_ref, dz_ref, dws_ref, dbs_ref, dgn_ref, dcw_ref, dvg_ref):
        @pl.when(pl.program_id(1) == 0)
        def _():
            dws_ref[...] = jnp.zeros_like(dws_ref)
            dbs_ref[...] = jnp.zeros_like(dbs_ref)
            dgn_ref[...] = jnp.zeros_like(dgn_ref)
            dcw_ref[...] = jnp.zeros_like(dcw_ref)

        u = z_ref[:, 0:L]
        v = z_ref[:, L:2 * L]
        gu, tu = _gelu(u)
        gv, tv = _gelu(v)
        r = lax.rsqrt(jnp.mean(gv * gv, axis=-1, keepdims=True) + RMS_EPS)
        gvh = gv * r
        gnv = gn_ref[0]
        vg = (gvh * gnv).astype(BF16)
        mask = _spatial_mask()
        wm = (ws_ref[0] * mask).astype(BF16)
        bias = bs_ref[0]
        dya = dy_ref[:, 0:L].astype(F32)
        dws = jnp.zeros((GMLP_BLOCK, GMLP_BLOCK), F32)
        dbs = jnp.zeros((GMLP_BLOCK, 1), F32)
        for n in range(nb):
            sl = slice(n * GMLP_BLOCK, (n + 1) * GMLP_BLOCK)
            mixed = jnp.dot(wm, vg[sl], preferred_element_type=F32) + bias
            dmixed = dya[sl] * gu[sl]
            dgu = dya[sl] * mixed
            dz_ref[sl, 0:L] = (dgu * _dgelu(u[sl], tu[sl])).astype(dz_ref.dtype)
            dmb = dmixed.astype(BF16)
            dws = dws + lax.dot_general(dmb, vg[sl], dims_nt, preferred_element_type=F32)
            dbs = dbs + jnp.sum(dmixed, axis=1, keepdims=True)
            dvg_ref[sl, :] = lax.dot_general(wm, dmb, dims_tn, preferred_element_type=F32)
        dws_ref[0] += dws * mask
        dbs_ref[0] += dbs
        dvg = dvg_ref[...]
        dgn_ref[0] += jnp.sum(dvg * gvh, axis=0, keepdims=True)
        dvh = dvg * gnv
        dgv = r * (dvh - gvh * jnp.mean(dvh * gvh, axis=-1, keepdims=True))
        dz_ref[:, L:2 * L] = (dgv * _dgelu(v, tv)).astype(dz_ref.dtype)

        bg = z_ref[:, 2 * L:3 * L]
        cg = z_ref[:, 3 * L:4 * L]
        hh = z_ref[:, 4 * L:5 * L]
        zc = cg * hh
        w0, w1, w2 = cw_ref[0, 0:1, :], cw_ref[0, 1:2, :], cw_ref[0, 2:3, :]
        z1 = _shift_down(zc, 1)
        z2 = _shift_down(zc, 2)
        conv = w0 * z2 + w1 * z1 + w2 * zc
        dyb = dy_ref[:, L:2 * L].astype(F32)
        dconv = dyb * bg
        dz_ref[:, 2 * L:3 * L] = (dyb * conv).astype(dz_ref.dtype)
        dzc = w2 * dconv + w1 * _shift_up(dconv, 1) + w0 * _shift_up(dconv, 2)
        dz_ref[:, 3 * L:4 * L] = (dzc * hh).astype(dz_ref.dtype)
        dz_ref[:, 4 * L:5 * L] = (dzc * cg).astype(dz_ref.dtype)
        dcw_ref[0, 0:1, :] += jnp.sum(dconv * z2, axis=0, keepdims=True)
        dcw_ref[0, 1:2, :] += jnp.sum(dconv * z1, axis=0, keepdims=True)
        dcw_ref[0, 2:3, :] += jnp.sum(dconv * zc, axis=0, keepdims=True)

    return pl.pallas_call(
        body,
        out_shape=(
            jax.ShapeDtypeStruct((t, 5 * L * g_n), BF16),
            jax.ShapeDtypeStruct(ws.shape, F32),
            jax.ShapeDtypeStruct(bs.shape, F32),
            jax.ShapeDtypeStruct(gn.shape, F32),
            jax.ShapeDtypeStruct(cw.shape, F32),
        ),
        grid=(g_n, t // seq),
        in_specs=[
            pl.BlockSpec((seq, 5 * L), lambda g, b: (b, g)),
            pl.BlockSpec((seq, 2 * L), lambda g, b: (b, g)),
            pl.BlockSpec((1, GMLP_BLOCK, GMLP_BLOCK), lambda g, b: (g, 0, 0)),
            pl.BlockSpec((1, GMLP_BLOCK, 1), lambda g, b: (g, 0, 0)),
            pl.BlockSpec((1, 1, L), lambda g, b: (g, 0, 0)),
            pl.BlockSpec((1, 3, L), lambda g, b: (g, 0, 0)),
        ],
        out_specs=(
            pl.BlockSpec((seq, 5 * L), lambda g, b: (b, g)),
            pl.BlockSpec((1, GMLP_BLOCK, GMLP_BLOCK), lambda g, b: (g, 0, 0)),
            pl.BlockSpec((1, GMLP_BLOCK, 1), lambda g, b: (g, 0, 0)),
            pl.BlockSpec((1, 1, L), lambda g, b: (g, 0, 0)),
            pl.BlockSpec((1, 3, L), lambda g, b: (g, 0, 0)),
        ),
        scratch_shapes=[pltpu.VMEM((seq, L), F32)],
        compiler_params=_cp("parallel", "arbitrary"),
        name=name,
    )(z, dy, ws, bs, gn, cw)


def _rope_rows(x, c, s1, s2, *, transpose, out_dtype, name):
    t, w = x.shape
    tm = _tile(t, 512)
    only = w == LANES
    fn = _rope_t if transpose else _rope

    def body(x_ref, c_ref, s1_ref, s2_ref, o_ref):
        cv, s1v, s2v = c_ref[...], s1_ref[...], s2_ref[...]
        if only:
            o_ref[...] = fn(x_ref[...].astype(F32), cv, s1v, s2v).astype(o_ref.dtype)
        else:
            for h0 in range(0, w, 2 * LANES):
                o_ref[:, h0:h0 + LANES] = x_ref[:, h0:h0 + LANES].astype(o_ref.dtype)
                o_ref[:, h0 + LANES:h0 + 2 * LANES] = fn(x_ref[:, h0 + LANES:h0 + 2 * LANES].astype(F32), cv, s1v, s2v).astype(o_ref.dtype)

    tab = pl.BlockSpec((tm, LANES), lambda i: (i, 0))
    return pl.pallas_call(
        body,
        out_shape=jax.ShapeDtypeStruct((t, w), out_dtype),
        grid=(t // tm,),
        in_specs=[pl.BlockSpec((tm, w), lambda i: (i, 0)), tab, tab, tab],
        out_specs=pl.BlockSpec((tm, w), lambda i: (i, 0)),
        compiler_params=_cp("parallel"),
        name=name,
    )(x, c, s1, s2)


_NT = (((1,), (1,)), ((), ()))
_TN = (((0,), (0,)), ((), ()))


def _chunk_mask(tq):
    qc = lax.broadcasted_iota(jnp.int32, (tq, tq), 0) // CHUNK
    kc = lax.broadcasted_iota(jnp.int32, (tq, tq), 1) // CHUNK
    return kc <= qc


MLA_BLOCK = 512


def _mla_fwd(q, kv, kr, *, seq, heads, name):
    t = q.shape[0]
    tq = _tile(seq, MLA_BLOCK)
    nq = seq // tq
    scale = (C_NOPE + C_ROPE) ** -0.5

    def body(q_ref, kv_ref, kr_ref, o_ref, lse_ref, kcat_ref):
        i = pl.program_id(2)

        @pl.when(i == 0)
        def _():
            kcat_ref[:, :LANES] = kv_ref[:, :LANES]
            kcat_ref[:, LANES:] = kr_ref[...]

        qv = q_ref[...]

        def step(j, carry, diagonal):
            m_run, l_run, acc = carry
            rows = pl.ds(pl.multiple_of(j * tq, tq), tq)
            s = lax.dot_general(qv, kcat_ref[rows, :], _NT, preferred_element_type=F32) * scale
            if diagonal:
                s = jnp.where(_chunk_mask(tq), s, NEG_BIG)
            m_new = jnp.maximum(m_run, jnp.max(s, axis=-1, keepdims=True))
            alpha = jnp.exp(m_run - m_new)
            p = jnp.exp(s - m_new)
            l_new = alpha * l_run + jnp.sum(p, axis=-1, keepdims=True)
            acc = alpha * acc + jnp.dot(p.astype(BF16), kv_ref[rows, LANES:], preferred_element_type=F32)
            return m_new, l_new, acc

        init = (jnp.full((tq, 1), NEG_BIG, F32), jnp.zeros((tq, 1), F32), jnp.zeros((tq, C_V), F32))
        carry = lax.fori_loop(0, i, lambda j, c: step(j, c, False), init)
        m_run, l_run, acc = step(i, carry, True)
        o_ref[...] = (acc / l_run).astype(o_ref.dtype)
        lse_ref[...] = m_run + jnp.log(l_run)

    return pl.pallas_call(
        body,
        out_shape=(jax.ShapeDtypeStruct((t, heads * C_V), BF16), jax.ShapeDtypeStruct((heads, t, 1), F32)),
        grid=(t // seq, heads, nq),
        in_specs=[
            pl.BlockSpec((tq, 2 * LANES), lambda b, h, i: (b * nq + i, h)),
            pl.BlockSpec((seq, 2 * LANES), lambda b, h, i: (b, h)),
            pl.BlockSpec((seq, LANES), lambda b, h, i: (b, 0)),
        ],
        out_specs=(
            pl.BlockSpec((tq, C_V), lambda b, h, i: (b * nq + i, h)),
            pl.BlockSpec((None, tq, 1), lambda b, h, i: (h, b * nq + i, 0)),
        ),
        scratch_shapes=[pltpu.VMEM((seq, 2 * LANES), BF16)],
        compiler_params=_cp("parallel", "parallel", "arbitrary"),
        name=name,
    )(q, kv, kr)


def _mla_bwd(q, kv, kr, do, o, lse, rc, rs1, rs2, *, seq, heads, name):
    t = q.shape[0]
    tq = _tile(seq, MLA_BLOCK)
    nq = seq // tq
    scale = (C_NOPE + C_ROPE) ** -0.5

    def body(q_ref, kv_ref, kr_ref, do_ref, o_ref, lse_ref, rc_ref, rs1_ref, rs2_ref, dq_ref, dkv_ref, dkr_ref, kcat_ref, dk_ref, dv_ref):
        h, i = pl.program_id(1), pl.program_id(2)

        @pl.when(i == 0)
        def _():
            kcat_ref[:, :LANES] = kv_ref[:, :LANES]
            kcat_ref[:, LANES:] = kr_ref[...]
            dk_ref[...] = jnp.zeros_like(dk_ref)
            dv_ref[...] = jnp.zeros_like(dv_ref)

        @pl.when((i == 0) & (h == 0))
        def _():
            dkr_ref[...] = jnp.zeros_like(dkr_ref)

        qv = q_ref[...]
        dov = do_ref[...]
        lse_v = lse_ref[...]
        delta = jnp.sum(dov.astype(F32) * o_ref[...].astype(F32), axis=-1, keepdims=True)

        def step(j, dq, diagonal):
            rows = pl.ds(pl.multiple_of(j * tq, tq), tq)
            kj = kcat_ref[rows, :]
            s = lax.dot_general(qv, kj, _NT, preferred_element_type=F32) * scale
            if diagonal:
                s = jnp.where(_chunk_mask(tq), s, NEG_BIG)
            p = jnp.exp(s - lse_v)
            dp = lax.dot_general(dov, kv_ref[rows, LANES:], _NT, preferred_element_type=F32)
            ds = (p * (dp - delta) * scale).astype(BF16)
            dk_ref[rows, :] += lax.dot_general(ds, qv, _TN, preferred_element_type=F32)
            dv_ref[rows, :] += lax.dot_general(p.astype(BF16), dov, _TN, preferred_element_type=F32)
            return dq + jnp.dot(ds, kj, preferred_element_type=F32)

        dq = lax.fori_loop(0, i, lambda j, c: step(j, c, False), jnp.zeros((tq, 2 * LANES), F32))
        dq = step(i, dq, True)
        dq_ref[:, :LANES] = dq[:, :LANES].astype(dq_ref.dtype)
        dq_ref[:, LANES:] = _rope_t(dq[:, LANES:], rc_ref[...], rs1_ref[...], rs2_ref[...]).astype(dq_ref.dtype)

        @pl.when(i == nq - 1)
        def _():
            dkv_ref[:, :LANES] = dk_ref[:, :LANES].astype(dkv_ref.dtype)
            dkv_ref[:, LANES:] = dv_ref[...].astype(dkv_ref.dtype)
            dkr_ref[...] += dk_ref[:, LANES:]

    return pl.pallas_call(
        body,
        out_shape=(
            jax.ShapeDtypeStruct((t, heads * 2 * LANES), BF16),
            jax.ShapeDtypeStruct((t, heads * 2 * LANES), BF16),
            jax.ShapeDtypeStruct((t, LANES), F32),
        ),
        grid=(t // seq, heads, nq),
        in_specs=[
            pl.BlockSpec((tq, 2 * LANES), lambda b, h, i: (b * nq + i, h)),
            pl.BlockSpec((seq, 2 * LANES), lambda b, h, i: (b, h)),
            pl.BlockSpec((seq, LANES), lambda b, h, i: (b, 0)),
            pl.BlockSpec((tq, C_V), lambda b, h, i: (b * nq + i, h)),
            pl.BlockSpec((tq, C_V), lambda b, h, i: (b * nq + i, h)),
            pl.BlockSpec((None, tq, 1), lambda b, h, i: (h, b * nq + i, 0)),
            pl.BlockSpec((tq, LANES), lambda b, h, i: (b * nq + i, 0)),
            pl.BlockSpec((tq, LANES), lambda b, h, i: (b * nq + i, 0)),
            pl.BlockSpec((tq, LANES), lambda b, h, i: (b * nq + i, 0)),
        ],
        out_specs=(
            pl.BlockSpec((tq, 2 * LANES), lambda b, h, i: (b * nq + i, h)),
            pl.BlockSpec((seq, 2 * LANES), lambda b, h, i: (b, h)),
            pl.BlockSpec((seq, LANES), lambda b, h, i: (b, 0)),
        ),
        scratch_shapes=[pltpu.VMEM((seq, 2 * LANES), BF16), pltpu.VMEM((seq, 2 * LANES), F32), pltpu.VMEM((seq, C_V), F32)],
        compiler_params=_cp("parallel", "arbitrary", "arbitrary"),
        name=name,
    )(q, kv, kr, do, o, lse, rc, rs1, rs2)


def _softmax_rows(s):
    e = jnp.exp(s - jnp.max(s, axis=-1, keepdims=True))
    return e / jnp.sum(e, axis=-1, keepdims=True)


def _mem_fwd(q, k, v, *, seq, n_mem, name):
    t, d = q.shape
    hd = d // MEM_HEADS
    tq = _tile(seq, 512)
    nq = seq // tq
    scale = hd ** -0.5

    def body(q_ref, k_ref, v_ref, o_ref):
        s = lax.dot_general(q_ref[...], k_ref[...], _NT, preferred_element_type=F32) * scale
        p = _softmax_rows(s)
        o_ref[...] = jnp.dot(p.astype(BF16), v_ref[...], preferred_element_type=F32).astype(o_ref.dtype)

    return pl.pallas_call(
        body,
        out_shape=jax.ShapeDtypeStruct((t, d), BF16),
        grid=(t // seq, MEM_HEADS, nq),
        in_specs=[
            pl.BlockSpec((tq, hd), lambda b, h, i: (b * nq + i, h)),
            pl.BlockSpec((n_mem, hd), lambda b, h, i: (b, h)),
            pl.BlockSpec((n_mem, hd), lambda b, h, i: (b, h)),
        ],
        out_specs=pl.BlockSpec((tq, hd), lambda b, h, i: (b * nq + i, h)),
        compiler_params=_cp("parallel", "parallel", "arbitrary"),
        name=name,
    )(q, k, v)


def _mem_bwd(q, k, v, do, *, seq, n_mem, name):
    t, d = q.shape
    hd = d // MEM_HEADS
    tq = _tile(seq, 512)
    nq = seq // tq
    scale = hd ** -0.5

    def body(q_ref, k_ref, v_ref, do_ref, dq_ref, dk_ref, dv_ref):
        @pl.when(pl.program_id(2) == 0)
        def _():
            dk_ref[...] = jnp.zeros_like(dk_ref)
            dv_ref[...] = jnp.zeros_like(dv_ref)

        qv, kv_, vv, dov = q_ref[...], k_ref[...], v_ref[...], do_ref[...]
        p = _softmax_rows(lax.dot_general(qv, kv_, _NT, preferred_element_type=F32) * scale)
        dp = lax.dot_general(dov, vv, _NT, preferred_element_type=F32)
        ds = (p * (dp - jnp.sum(dp * p, axis=-1, keepdims=True)) * scale).astype(BF16)
        dq_ref[...] = jnp.dot(ds, kv_, preferred_element_type=F32).astype(dq_ref.dtype)
        dk_ref[...] += lax.dot_general(ds, qv, _TN, preferred_element_type=F32)
        dv_ref[...] += lax.dot_general(p.astype(BF16), dov, _TN, preferred_element_type=F32)

    return pl.pallas_call(
        body,
        out_shape=(
            jax.ShapeDtypeStruct((t, d), BF16),
            jax.ShapeDtypeStruct(k.shape, F32),
            jax.ShapeDtypeStruct(v.shape, F32),
        ),
        grid=(t // seq, MEM_HEADS, nq),
        in_specs=[
            pl.BlockSpec((tq, hd), lambda b, h, i: (b * nq + i, h)),
            pl.BlockSpec((n_mem, hd), lambda b, h, i: (b, h)),
            pl.BlockSpec((n_mem, hd), lambda b, h, i: (b, h)),
            pl.BlockSpec((tq, hd), lambda b, h, i: (b * nq + i, h)),
        ],
        out_specs=(
            pl.BlockSpec((tq, hd), lambda b, h, i: (b * nq + i, h)),
            pl.BlockSpec((n_mem, hd), lambda b, h, i: (b, h)),
            pl.BlockSpec((n_mem, hd), lambda b, h, i: (b, h)),
        ),
        compiler_params=_cp("parallel", "parallel", "arbitrary"),
        name=name,
    )(q, k, v, do)


def _me():
    return lax.axis_index("x"), lax.axis_index("y"), lax.axis_index("c")


def _dev_index(px, py, pc):
    return 4 * px + 2 * py + pc


def _chunk(ref, idx, axis, size):
    start = pl.multiple_of(idx * size, size)
    return ref.at[pl.ds(start, size), :] if axis == 0 else ref.at[:, pl.ds(start, size)]


def _all_gather_small(v):
    m, n = v.shape

    def body(x_ref, out_ref, send_sems, recv_sems, local_sem):
        x, y, c = _me()
        me, sibling = (x, y, c), (x, y, 1 - c)
        chips = [(1 - x, y), (x, 1 - y), (1 - x, 1 - y)]

        def rows(px, py, pc):
            return out_ref.at[_dev_index(px, py, pc)]

        def copy(k, block, to, src=None):
            return pltpu.make_async_remote_copy(
                src_ref=rows(*block) if src is None else src, dst_ref=rows(*block),
                send_sem=send_sems.at[k], recv_sem=recv_sems.at[k], device_id=to, device_id_type=MESH)

        mine = pltpu.make_async_copy(x_ref, rows(*me), local_sem)
        mine.start()
        first = [copy(0, me, sibling, src=x_ref)]
        first += [copy(1 + j, me, (*chip, c), src=x_ref) for j, chip in enumerate(chips)]
        for cp in first:
            cp.start()
        passed = [copy(4 + j, (*chip, c), sibling) for j, chip in enumerate(chips)]
        for j, chip in enumerate(chips):
            copy(1 + j, (*chip, c), me).wait_recv()
            passed[j].start()
        copy(0, sibling, me).wait_recv()
        for j, chip in enumerate(chips):
            copy(4 + j, (*chip, 1 - c), me).wait_recv()
        for cp in first + passed:
            cp.wait_send()
        mine.wait()

    return pl.pallas_call(
        body,
        out_shape=jax.ShapeDtypeStruct((N_DEV, m, n), v.dtype),
        in_specs=[pl.BlockSpec(memory_space=pltpu.VMEM)],
        out_specs=pl.BlockSpec(memory_space=pltpu.VMEM),
        scratch_shapes=[pltpu.SemaphoreType.DMA((7,)), pltpu.SemaphoreType.DMA((7,)), pltpu.SemaphoreType.DMA],
        name="all_gather_small",
    )(v)


def _cast_place(w, layer, axis, dev_arr, *, name):
    _, r, c = w.shape
    tr = _tile(r, 256)
    nr = r // tr
    full = (N_DEV * r, c) if axis == 0 else (r, N_DEV * c)

    def body(dev_ref, w_ref, o_ref):
        o_ref[...] = w_ref[...].astype(o_ref.dtype)

    if axis == 0:
        out_spec = pl.BlockSpec((tr, c), lambda i, dev: (dev[0] * nr + i, 0))
    else:
        out_spec = pl.BlockSpec((tr, c), lambda i, dev: (i, dev[0]))
    return pl.pallas_call(
        body,
        out_shape=jax.ShapeDtypeStruct(full, BF16),
        grid_spec=pltpu.PrefetchScalarGridSpec(
            num_scalar_prefetch=1, grid=(nr,), in_specs=[pl.BlockSpec((None, tr, c), lambda i, dev: (layer, i, 0))], out_specs=out_spec),
        compiler_params=_cp("parallel"),
        name=name,
    )(dev_arr, w)


_ANY = pl.BlockSpec(memory_space=pl.ANY)
_SEM = pl.BlockSpec(memory_space=pltpu.SEMAPHORE)
_EFFECT = pltpu.SideEffectType.DATAFLOW_SIDE_EFFECTING
N_PEER_CHIPS = 3


def _other_chips(x, y):
    return [((1 - x if (j >> 1) & 1 else x), (1 - y if j & 1 else y)) for j in range(1, 4)]


def _remote(src, dst, send_sem, recv_sem, to):
    return pltpu.make_async_remote_copy(src_ref=src, dst_ref=dst, send_sem=send_sem, recv_sem=recv_sem, device_id=to, device_id_type=MESH)


def _plan_gather_ici(axes, sizes):
    def plan(refs, send_sems, recv_sems):
        x, y, c = _me()
        out = []
        for t in range(len(axes)):
            for j, chip in enumerate(_other_chips(x, y)):
                mine = _chunk(refs[t], _dev_index(x, y, c), axes[t], sizes[t])
                theirs = _chunk(refs[t], _dev_index(*chip, c), axes[t], sizes[t])
                out.append((_remote(mine, mine, send_sems[j], recv_sems[j], (*chip, c)), _remote(theirs, theirs, send_sems[j], recv_sems[j], (*chip, c))))
        return out
    return plan


def _plan_gather_d2d(axes, sizes):
    def plan(refs, send_sems, recv_sems):
        x, y, c = _me()
        out = []
        for t in range(len(axes)):
            for k in range(4):
                have = _chunk(refs[t], 2 * k + c, axes[t], sizes[t])
                get = _chunk(refs[t], 2 * k + 1 - c, axes[t], sizes[t])
                out.append((_remote(have, have, send_sems[0], recv_sems[0], (x, y, 1 - c)), _remote(get, get, send_sems[0], recv_sems[0], (x, y, 1 - c))))
        return out
    return plan


def _plan_scatter_d2d(axes, sizes):
    n = len(axes)

    def plan(refs, send_sems, recv_sems):
        x, y, c = _me()
        out = []
        for t in range(n):
            for k in range(4):
                give = _chunk(refs[t], 2 * k + 1 - c, axes[t], sizes[t])
                land = _chunk(refs[n + t], k, axes[t], sizes[t])
                cp = _remote(give, land, send_sems[0], recv_sems[0], (x, y, 1 - c))
                out.append((cp, cp))
        return out
    return plan


def _plan_scatter_ici(axes, sizes):
    n = len(axes)

    def plan(refs, send_sems, recv_sems):
        x, y, c = _me()
        out = []
        for t in range(n):
            for j, (px, py) in enumerate(_other_chips(x, y)):
                send = _remote(_chunk(refs[t], 2 * px + py, axes[t], sizes[t]), _chunk(refs[n + t], 2 * x + y, axes[t], sizes[t]),
                               send_sems[j], recv_sems[j], (px, py, c))
                recv = _remote(_chunk(refs[t], 2 * x + y, axes[t], sizes[t]), _chunk(refs[n + t], 2 * px + py, axes[t], sizes[t]),
                               send_sems[j], recv_sems[j], (px, py, c))
                out.append((send, recv))
        return out
    return plan


def _exchange_start(arrays, n_sems, plan, after, *, name):
    n = len(arrays)
    m = len(after)

    def body(*refs):
        send_sems, recv_sems = list(refs[n + m:n + m + n_sems]), list(refs[n + m + n_sems:n + m + 2 * n_sems])
        for send, _ in plan(refs[:n], send_sems, recv_sems):
            send.start()
        refs[-1][...] = jnp.zeros_like(refs[-1])

    outs = pl.pallas_call(
        body,
        out_shape=(*([pltpu.SemaphoreType.DMA(())] * (2 * n_sems)),
                   *[jax.ShapeDtypeStruct(a.shape, a.dtype) for a in arrays], jax.ShapeDtypeStruct((8, LANES), F32)),
        in_specs=[_ANY] * (n + m),
        out_specs=(*([_SEM] * (2 * n_sems)), *([_ANY] * n), pl.BlockSpec(memory_space=pltpu.VMEM)),
        input_output_aliases={t: 2 * n_sems + t for t in range(n)},
        compiler_params=pltpu.CompilerParams(has_side_effects=_EFFECT),
        name=name,
    )(*arrays, *after)
    return list(outs[:2 * n_sems]), list(outs[2 * n_sems:2 * n_sems + n]), outs[-1]


def _exchange_wait(sems, arrays, plan, after, *, name):
    n = len(arrays)
    n_sems = len(sems) // 2

    def body(*refs):
        send_sems, recv_sems = list(refs[n:n + n_sems]), list(refs[n + n_sems:n + 2 * n_sems])
        for send, recv in plan(refs[:n], send_sems, recv_sems):
            send.wait_send()
            recv.wait_recv()
        refs[-1][...] = jnp.zeros_like(refs[-1])

    outs = pl.pallas_call(
        body,
        out_shape=(*[jax.ShapeDtypeStruct(a.shape, a.dtype) for a in arrays], jax.ShapeDtypeStruct((8, LANES), F32)),
        in_specs=[_ANY] * n + [_SEM] * (2 * n_sems) + [_ANY] * len(after),
        out_specs=(*([_ANY] * n), pl.BlockSpec(memory_space=pltpu.VMEM)),
        input_output_aliases={t: t for t in range(n)},
        compiler_params=pltpu.CompilerParams(has_side_effects=_EFFECT),
        name=name,
    )(*arrays, *sems, *after)
    return list(outs[:n]), outs[-1]


def _add_pair(g, got, axis, core_arr, *, name):
    r, c = got.shape
    if axis == 0:
        rows, cols = r // 4, c
    else:
        rows, cols = r, c // 4
    tr = _tile(rows, 512)
    nr = rows // tr

    def body(core_ref, g_ref, got_ref, o_ref):
        o_ref[...] = (g_ref[...].astype(F32) + got_ref[...].astype(F32)).astype(o_ref.dtype)

    if axis == 0:
        g_spec = pl.BlockSpec((tr, cols), lambda k, i, core: ((2 * k + core[0]) * nr + i, 0))
        spec = pl.BlockSpec((tr, cols), lambda k, i, core: (k * nr + i, 0))
    else:
        g_spec = pl.BlockSpec((tr, cols), lambda k, i, core: (i, 2 * k + core[0]))
        spec = pl.BlockSpec((tr, cols), lambda k, i, core: (i, k))
    return pl.pallas_call(
        body,
        out_shape=jax.ShapeDtypeStruct(got.shape, got.dtype),
        grid_spec=pltpu.PrefetchScalarGridSpec(num_scalar_prefetch=1, grid=(4, nr), in_specs=[g_spec, spec], out_specs=spec),
        compiler_params=_cp("parallel", "parallel"),
        name=name,
    )(core_arr, g, got)


def _adam_math(w, g, m, v):
    m = ADAM_B1 * m + (1.0 - ADAM_B1) * g
    v = ADAM_B2 * v + (1.0 - ADAM_B2) * (g * g)
    m_hat = m / (1.0 - ADAM_B1 ** ADAM_STEP)
    v_hat = v / (1.0 - ADAM_B2 ** ADAM_STEP)
    delta = -ADAM_LR * (m_hat / (jnp.sqrt(v_hat) + ADAM_EPS) + ADAM_WD * w)
    return delta, m, v


def _adamw(w, m, v, parts, axis, chip_arr, layers, prev, after, *, name):
    n_l, r, c = w.shape
    first, count = layers
    tr = _tile(r, 128)
    nr = r // tr
    n_chip = 4
    n_in = 4 + n_chip * count

    def body(*refs):
        w_ref, m_ref, v_ref = refs[1:4]
        p_refs = refs[4:n_in]
        g_ref, d_ref, nm_ref, nv_ref = refs[-4:]
        for l in range(count):
            @pl.when(pl.program_id(0) == l)
            def _(l=l):
                g = p_refs[n_chip * l][...].astype(F32)
                for k in range(1, n_chip):
                    g = g + p_refs[n_chip * l + k][...].astype(F32)
                delta, nm, nv = _adam_math(w_ref[...], g, m_ref[...], v_ref[...])
                g_ref[...] = g
                d_ref[...] = delta
                nm_ref[...] = nm
                nv_ref[...] = nv

    def part_spec(ll, k):
        if axis == 0:
            return pl.BlockSpec((tr, c), lambda l, i, chips: (chips[k] * nr + jnp.where(l == ll, i, 0), 0))
        return pl.BlockSpec((tr, c), lambda l, i, chips: (jnp.where(l == ll, i, 0), chips[k]))

    wspec = pl.BlockSpec((None, tr, c), lambda l, i, chips: (first + l, i, 0))
    pspecs = [part_spec(ll, k) for ll in range(count) for k in range(n_chip)]
    sds = jax.ShapeDtypeStruct(w.shape, F32)
    kept = [] if prev is None else list(prev)
    return pl.pallas_call(
        body,
        out_shape=(sds, sds, sds, sds),
        grid_spec=pltpu.PrefetchScalarGridSpec(
            num_scalar_prefetch=1, grid=(count, nr),
            in_specs=[wspec, wspec, wspec] + pspecs + [_ANY] * (1 + len(kept)), out_specs=(wspec, wspec, wspec, wspec)),
        input_output_aliases={n_in + 1 + k: k for k in range(len(kept))},
        compiler_params=_cp("arbitrary", "arbitrary"),
        name=name,
    )(chip_arr, w, m, v, *[parts[first + ll][0 if k == 0 else 1] for ll in range(count) for k in range(n_chip)], after, *kept)


def _adamw_small(w, m, v, parts, *, name):
    r, c = w.shape
    tr = _tile(r, 512)

    def body(w_ref, m_ref, v_ref, p_ref, g_ref, d_ref, nm_ref, nv_ref):
        g = p_ref[0]
        for s in range(1, N_DEV):
            g = g + p_ref[s]
        delta, nm, nv = _adam_math(w_ref[...], g, m_ref[...], v_ref[...])
        g_ref[...] = g
        d_ref[...] = delta
        nm_ref[...] = nm
        nv_ref[...] = nv

    spec = pl.BlockSpec((tr, c), lambda i: (i, 0))
    sds = jax.ShapeDtypeStruct(w.shape, F32)
    return pl.pallas_call(
        body,
        out_shape=(sds, sds, sds, sds),
        grid=(r // tr,),
        in_specs=[spec, spec, spec, pl.BlockSpec((N_DEV, tr, c), lambda i: (0, i, 0))],
        out_specs=(spec, spec, spec, spec),
        compiler_params=_cp("parallel"),
        name=name,
    )(w, m, v, parts)


def _pack(arrs):
    flat = jnp.concatenate([a.reshape(-1).astype(F32) for a in arrs])
    pad = (-flat.shape[0]) % (8 * LANES)
    return jnp.pad(flat, (0, pad)).reshape(-1, LANES)


def _unpack(packed, shapes):
    flat = packed.reshape(-1)
    out, off = [], 0
    for sh in shapes:
        n = math.prod(sh)
        out.append(flat[off:off + n].reshape(sh))
        off += n
    return out


def _unpack_gathered(packed, shapes, axis):
    flat = packed.reshape(N_DEV, -1)
    out, off = [], 0
    for sh, ax in zip(shapes, axis):
        n = math.prod(sh)
        blocks = flat[:, off:off + n].reshape((N_DEV, *sh))
        out.append(jnp.concatenate([blocks[d] for d in range(N_DEV)], axis=ax))
        off += n
    return out


def _my_block(full, axis, size):
    idx = _dev_index(*_me())
    return lax.dynamic_slice_in_dim(full, idx * size, size, axis)


def kernel(x, mem, positions, norm_mix_g, norm_mem_q_g, norm_mem_kv_g, norm_ffn_g, final_norm_g, ab_w_in, a_v_norm_g, a_w_s, a_b_s, b_conv_w, ab_w_out, c_w_in, c_q_norm_g, c_kv_norm_g, c_w_uq, c_w_ukv, c_w_out, m_wq, m_wk, m_wv, m_wo, f_w1, f_w2, loss_target, m_norm_mix_g, m_norm_mem_q_g, m_norm_mem_kv_g, m_norm_ffn_g, m_final_norm_g, m_ab_w_in, m_a_v_norm_g, m_a_w_s, m_a_b_s, m_b_conv_w, m_ab_w_out, m_c_w_in, m_c_q_norm_g, m_c_kv_norm_g, m_c_w_uq, m_c_w_ukv, m_c_w_out, m_m_wq, m_m_wk, m_m_wv, m_m_wo, m_f_w1, m_f_w2, v_norm_mix_g, v_norm_mem_q_g, v_norm_mem_kv_g, v_norm_ffn_g, v_final_norm_g, v_ab_w_in, v_a_v_norm_g, v_a_w_s, v_a_b_s, v_b_conv_w, v_ab_w_out, v_c_w_in, v_c_q_norm_g, v_c_kv_norm_g, v_c_w_uq, v_c_w_ukv, v_c_w_out, v_m_wq, v_m_wk, v_m_wv, v_m_wo, v_f_w1, v_f_w2):
    bl, seq, d = x.shape
    t = bl * seq
    n_mem = mem.shape[1]
    depth = norm_mix_g.shape[0]
    n_even, n_odd = ab_w_in.shape[0], c_w_in.shape[0]
    a_width = a_v_norm_g.shape[1]
    groups = a_w_s.shape[1]
    heads = d // C_NOPE
    q_rank = c_q_norm_g.shape[1] * N_DEV
    kv_rank = c_kv_norm_g.shape[1] * N_DEV

    big = {
        "ab_w_in": (ab_w_in, m_ab_w_in, v_ab_w_in, 1), "ab_w_out": (ab_w_out, m_ab_w_out, v_ab_w_out, 0),
        "c_w_in": (c_w_in, m_c_w_in, v_c_w_in, 0), "c_w_uq": (c_w_uq, m_c_w_uq, v_c_w_uq, 1),
        "c_w_ukv": (c_w_ukv, m_c_w_ukv, v_c_w_ukv, 1), "c_w_out": (c_w_out, m_c_w_out, v_c_w_out, 0),
        "m_wq": (m_wq, m_m_wq, v_m_wq, 0), "m_wk": (m_wk, m_m_wk, v_m_wk, 0), "m_wv": (m_wv, m_m_wv, v_m_wv, 0),
        "m_wo": (m_wo, m_m_wo, v_m_wo, 0), "f_w1": (f_w1, m_f_w1, v_f_w1, 1), "f_w2": (f_w2, m_f_w2, v_f_w2, 0),
    }
    even_names = ["ab_w_in", "ab_w_out"]
    odd_names = ["c_w_in", "c_w_uq", "c_w_ukv", "c_w_out"]
    all_names = ["m_wq", "m_wk", "m_wv", "m_wo", "f_w1", "f_w2"]

    def layer_names(layer):
        return (even_names if layer % 2 == 0 else odd_names) + all_names

    def sub(name, layer):
        return layer // 2 if name in even_names or name in odd_names else layer

    split_small = [(c_q_norm_g, m_c_q_norm_g, v_c_q_norm_g, 1), (c_kv_norm_g, m_c_kv_norm_g, v_c_kv_norm_g, 1), (b_conv_w, m_b_conv_w, v_b_conv_w, 2)]
    shard_shapes = [s[0].shape for s in split_small] * 3
    shard_axes = [s[3] for s in split_small] * 3
    gathered = _all_gather_small(_pack([s[0] for s in split_small] + [s[1] for s in split_small] + [s[2] for s in split_small]))
    whole = _unpack_gathered(gathered, shard_shapes, shard_axes)
    q_norm_full, kv_norm_full, conv_full = whole[0:3]
    small_w = [norm_mix_g, norm_mem_q_g, norm_mem_kv_g, norm_ffn_g, final_norm_g, a_v_norm_g, a_w_s, a_b_s] + whole[0:3]
    small_m = [m_norm_mix_g, m_norm_mem_q_g, m_norm_mem_kv_g, m_norm_ffn_g, m_final_norm_g, m_a_v_norm_g, m_a_w_s, m_a_b_s] + whole[3:6]
    small_v = [v_norm_mix_g, v_norm_mem_q_g, v_norm_mem_kv_g, v_norm_ffn_g, v_final_norm_g, v_a_v_norm_g, v_a_w_s, v_a_b_s] + whole[6:9]

    inv = ROPE_THETA ** (-jnp.arange(0, C_ROPE, 2, dtype=F32) / C_ROPE)
    ang = positions.astype(F32).reshape(t, 1) * inv
    cos, sin, zero = jnp.cos(ang), jnp.sin(ang), jnp.zeros((t, C_ROPE // 2), F32)
    rc = jnp.concatenate([cos, cos, zero, zero], axis=1)
    rs1 = jnp.concatenate([-sin, zero, zero, zero], axis=1)
    rs2 = jnp.concatenate([zero, sin, zero, zero], axis=1)

    my_x, my_y, my_c = _me()
    my_chip = 2 * my_x + my_y
    dev_arr = jnp.stack([_dev_index(my_x, my_y, my_c)]).astype(jnp.int32)
    core_arr = jnp.stack([my_c]).astype(jnp.int32)
    chip_arr = jnp.stack([my_chip, my_chip ^ 1, my_chip ^ 2, my_chip ^ 3]).astype(jnp.int32)

    n_stage = 3 * depth
    chain = [None]
    gat = {}
    issued = [0]

    def stage_names(i):
        layer, gid = divmod(i, 3)
        return layer, [even_names if layer % 2 == 0 else odd_names, all_names[:4], all_names[4:]][gid]

    def order(after):
        return [a for a in (chain[0], after) if a is not None]

    def gather_ici_start(i, after):
        layer, names = stage_names(i)
        placed = [_cast_place(big[nm][0], sub(nm, layer), big[nm][3], dev_arr, name="cast_place") for nm in names]
        axes = [big[nm][3] for nm in names]
        sizes = [p.shape[a] // N_DEV for p, a in zip(placed, axes)]
        st = dict(names=names, ici=_plan_gather_ici(axes, sizes), d2d=_plan_gather_d2d(axes, sizes))
        st["sems"], st["arrays"], chain[0] = _exchange_start(placed, N_PEER_CHIPS, st["ici"], order(after), name=f"gather_ici_start_s{i}")
        gat[i] = st

    def gather_stage(i, after):
        while issued[0] < min(i + 3, n_stage):
            gather_ici_start(issued[0], after)
            issued[0] += 1
        ahead = [i, i + 1] if i >= 3 and i + 1 < n_stage else [i]
        for j in ahead:
            st = gat[j]
            if "landed" not in st:
                st["arrays"], chain[0] = _exchange_wait(st["sems"], st["arrays"], st["ici"], order(after), name=f"gather_ici_wait_s{j}")
                st["landed"] = True
            if "sems2" not in st:
                st["sems2"], st["arrays"], chain[0] = _exchange_start(st["arrays"], 1, st["d2d"], order(after), name=f"gather_d2d_start_s{j}")
        st = gat[i]
        st["arrays"], chain[0] = _exchange_wait(st["sems2"], st["arrays"], st["d2d"], order(after), name=f"gather_d2d_wait_s{i}")
        return dict(zip(st["names"], st["arrays"]))

    wts = []

    def group_cols(w_in):
        return w_in.reshape(d, 5, groups, LANES).transpose(0, 2, 1, 3).reshape(d, 5 * a_width)

    def ungroup_cols(g_in):
        return g_in.reshape(d, groups, 5, LANES).transpose(0, 2, 1, 3).reshape(d, 5 * a_width)

    def group_rows(w_out):
        return w_out.reshape(2, groups, LANES, d).transpose(1, 0, 2, 3).reshape(2 * a_width, d)

    def ungroup_rows(g_out):
        return g_out.reshape(groups, 2, LANES, d).transpose(1, 0, 2, 3).reshape(2 * a_width, d)

    def pad_heads(w_uq):
        w3 = w_uq.reshape(q_rank, heads, C_NOPE + C_ROPE)
        return jnp.pad(w3, ((0, 0), (0, 0), (0, LANES - C_ROPE))).reshape(q_rank, heads * 2 * LANES)

    def unpad_heads(g_uq):
        return g_uq.reshape(q_rank, heads, 2 * LANES)[:, :, :C_NOPE + C_ROPE].reshape(q_rank, heads * (C_NOPE + C_ROPE))

    x2 = x.reshape(t, d)
    mem2 = mem.reshape(bl * n_mem, d)
    saved = []

    for layer in range(depth):
        w = gather_stage(3 * layer, x2)
        wts.append(w)
        sv = {"x0": x2}
        xn = _rmsnorm_fwd(x2, norm_mix_g[layer], name="norm_mix")
        sv["xn"] = xn
        if layer % 2 == 0:
            e = layer // 2
            w_in_g = group_cols(w["ab_w_in"])
            w_out_g = group_rows(w["ab_w_out"])
            z = _mm(xn, w_in_g, out_dtypes=(F32,), extras=[(chain[0], "after")], name="ab_in")
            ws_e = a_w_s[e]
            bs_e = a_b_s[e].reshape(groups, GMLP_BLOCK, 1)
            gn_e = a_v_norm_g[e].reshape(groups, 1, LANES)
            cw_e = conv_full[e].reshape(3, groups, LANES).transpose(1, 0, 2)
            ycat = _gmlp_conv_fwd(z, ws_e, bs_e, gn_e, cw_e, seq=seq, name="gmlp_conv_fwd")
            x2 = _mm(ycat, w_out_g, out_dtypes=(F32,), extras=[(x2, "mn")], epilogue=_epi_add, name="ab_out")
            sv.update(z=z, ycat=ycat, w_in_g=w_in_g, w_out_g=w_out_g, ws=ws_e, bs=bs_e, gn=gn_e, cw=cw_e)
        else:
            o = layer // 2
            w_uq_p = pad_heads(w["c_w_uq"])
            z = _mm(xn, w["c_w_in"], out_dtypes=(F32,), extras=[(chain[0], "after")], name="c_in")
            qn = _rmsnorm_fwd(z, q_norm_full[o], width=q_rank, col=0, name="norm_cq")
            kvn = _rmsnorm_fwd(z, kv_norm_full[o], width=kv_rank, col=q_rank // kv_rank, name="norm_ckv")
            qh = _mm(qn, w_uq_p, extras=[(rc, "m"), (rs1, "m"), (rs2, "m")], epilogue=_epi_rope_heads, tn=1024, name="c_uq")
            kvh = _mm(kvn, w["c_w_ukv"], name="c_ukv")
            k_rope = jnp.pad(z[:, q_rank + kv_rank:], ((0, 0), (0, LANES - C_ROPE)))
            kr = _rope_rows(k_rope, rc, rs1, rs2, transpose=False, out_dtype=BF16, name="rope_k")
            att, lse = _mla_fwd(qh, kvh, kr, seq=seq, heads=heads, name="mla_fwd")
            x2 = _mm(att, w["c_w_out"], out_dtypes=(F32,), extras=[(x2, "mn")], epilogue=_epi_add, name="c_out")
            sv.update(z=z, qn=qn, kvn=kvn, qh=qh, kvh=kvh, kr=kr, att=att, lse=lse, w_uq_p=w_uq_p)
        sv["x1"] = x2
        w.update(gather_stage(3 * layer + 1, x2))
        xq = _rmsnorm_fwd(x2, norm_mem_q_g[layer], name="norm_mem_q")
        memn = _rmsnorm_fwd(mem2, norm_mem_kv_g[layer], name="norm_mem_kv")
        mq = _mm(xq, w["m_wq"], extras=[(chain[0], "after")], name="m_q")
        mk = _mm(memn, w["m_wk"], name="m_k")
        mv = _mm(memn, w["m_wv"], name="m_v")
        mo = _mem_fwd(mq, mk, mv, seq=seq, n_mem=n_mem, name="mem_fwd")
        x2 = _mm(mo, w["m_wo"], out_dtypes=(F32,), extras=[(x2, "mn")], epilogue=_epi_add, name="m_o")
        sv.update(xq=xq, memn=memn, mq=mq, mk=mk, mv=mv, mo=mo)
        sv["x2"] = x2
        w.update(gather_stage(3 * layer + 2, x2))
        xf = _rmsnorm_fwd(x2, norm_ffn_g[layer], name="norm_ffn")
        hpre, hact = _mm(xf, w["f_w1"], out_dtypes=(BF16, BF16), extras=[(chain[0], "after")], epilogue=_epi_relu2, name="f_1")
        x2 = _mm(hact, w["f_w2"], out_dtypes=(F32,), extras=[(x2, "mn")], epilogue=_epi_add, name="f_2")
        sv.update(xf=xf, hpre=hpre, hact=hact)
        saved.append(sv)

    loss_part, dx, dg_final = _final_loss(x2, final_norm_g, loss_target.reshape(t, d), name="final_loss")
    loss = lax.psum(loss_part[0, 0], ("x", "y", "c"))

    dg_mix, dg_mq, dg_mkv, dg_ffn = [None] * depth, [None] * depth, [None] * depth, [None] * depth
    d_av, d_ws, d_bs = [None] * n_even, [None] * n_even, [None] * n_even
    d_conv, d_qn, d_kvn = [None] * n_even, [None] * n_odd, [None] * n_odd
    parts = {nm: [None] * big[nm][0].shape[0] for nm in big}

    sca = []

    def scatter_finish_d2d(st, after):
        n = len(st["names"])
        arrays, chain[0] = _exchange_wait(st["sems"], st["arrays"], st["d2d"], order(after), name=f"scatter_d2d_wait_s{st['id']}")
        sums = [_add_pair(g, got, ax, core_arr, name="add_pair") for g, got, ax in zip(arrays[:n], arrays[n:], st["axes"])]
        lands = [pltpu.with_memory_space_constraint(lax.empty(s_.shape, s_.dtype), pltpu.HBM) for s_ in sums]
        st["sems2"], st["arrays"], chain[0] = _exchange_start(sums + lands, N_PEER_CHIPS, st["ici"], order(after), name=f"scatter_ici_start_s{st['id']}")

    def scatter_stage(layer, names, grads, after):
        if sca and "sems2" not in sca[-1]:
            scatter_finish_d2d(sca[-1], after)
        axes = [big[nm][3] for nm in names]
        arrs = [grads[nm] for nm in names]
        sizes8 = [g.shape[a] // N_DEV for g, a in zip(arrs, axes)]
        gots = []
        for g, a in zip(arrs, axes):
            sh = list(g.shape)
            sh[a] //= 2
            gots.append(pltpu.with_memory_space_constraint(lax.empty(tuple(sh), g.dtype), pltpu.HBM))
        st = dict(id=len(sca), layer=layer, names=names, axes=axes, d2d=_plan_scatter_d2d(axes, sizes8),
                  ici=_plan_scatter_ici(axes, sizes8))
        st["sems"], st["arrays"], chain[0] = _exchange_start(arrs + gots, 1, st["d2d"], order(after), name=f"scatter_d2d_start_s{st['id']}")
        sca.append(st)

    def scatter_collect(st, after):
        n = len(st["names"])
        arrays, chain[0] = _exchange_wait(st["sems2"], st["arrays"], st["ici"], order(after), name=f"scatter_ici_wait_s{st['id']}")
        for nm, s_, r in zip(st["names"], arrays[:n], arrays[n:]):
            parts[nm][sub(nm, st["layer"])] = (s_, r)

    for layer in reversed(range(depth)):
        w, sv = wts[layer], saved[layer]
        grads = {}
        dh = _mm(dx, w["f_w2"], tb=True, extras=[(sv["hpre"], "mn"), (chain[0], "after")], epilogue=_epi_drelu2, name="f_2_dx")
        grads["f_w2"] = _mm(sv["hact"], dx, ta=True, name="f_2_dw")
        grads["f_w1"] = _mm(sv["xf"], dh, ta=True, name="f_1_dw")
        dxf = _mm(dh, w["f_w1"], tb=True, name="f_1_dx")
        dx, dg_ffn[layer] = _rmsnorm_bwd(sv["x2"], norm_ffn_g[layer], dxf, dx, name="norm_ffn_bwd")
        scatter_stage(layer, all_names[4:], grads, dx)
        dmo = _mm(dx, w["m_wo"], tb=True, extras=[(chain[0], "after")], name="m_o_dx")
        grads["m_wo"] = _mm(sv["mo"], dx, ta=True, name="m_o_dw")
        dmq, dmk, dmv = _mem_bwd(sv["mq"], sv["mk"], sv["mv"], dmo, seq=seq, n_mem=n_mem, name="mem_bwd")
        grads["m_wq"] = _mm(sv["xq"], dmq, ta=True, name="m_q_dw")
        grads["m_wk"] = _mm(sv["memn"], dmk, ta=True, name="m_k_dw")
        grads["m_wv"] = _mm(sv["memn"], dmv, ta=True, name="m_v_dw")
        dxq = _mm(dmq, w["m_wq"], tb=True, name="m_q_dx")
        dmemn_k = _mm(dmk, w["m_wk"], tb=True, out_dtypes=(F32,), name="m_k_dx")
        dmemn = _mm(dmv, w["m_wv"], tb=True, out_dtypes=(F32,), extras=[(dmemn_k, "mn")], epilogue=_epi_add, name="m_v_dx")
        _, dg_mkv[layer] = _rmsnorm_bwd(mem2, norm_mem_kv_g[layer], dmemn, name="norm_mem_kv_bwd")
        dx, dg_mq[layer] = _rmsnorm_bwd(sv["x1"], norm_mem_q_g[layer], dxq, dx, name="norm_mem_q_bwd")
        scatter_stage(layer, all_names[:4], grads, dx)
        if layer % 2 == 0:
            e = layer // 2
            dycat = _mm(dx, sv["w_out_g"], tb=True, extras=[(chain[0], "after")], name="ab_out_dx")
            grads["ab_w_out"] = ungroup_rows(_mm(sv["ycat"], dx, ta=True, name="ab_out_dw"))
            dz, dws, dbs, dgn, dcw = _gmlp_conv_bwd(sv["z"], dycat, sv["ws"], sv["bs"], sv["gn"], sv["cw"], seq=seq, name="gmlp_conv_bwd")
            d_ws[e], d_bs[e], d_av[e] = dws, dbs.reshape(groups, GMLP_BLOCK), dgn.reshape(a_width)
            d_conv[e] = dcw.transpose(1, 0, 2).reshape(3, a_width)
            grads["ab_w_in"] = ungroup_cols(_mm(sv["xn"], dz, ta=True, name="ab_in_dw"))
            dxn = _mm(dz, sv["w_in_g"], tb=True, name="ab_in_dx")
        else:
            o = layer // 2
            datt = _mm(dx, w["c_w_out"], tb=True, extras=[(chain[0], "after")], name="c_out_dx")
            grads["c_w_out"] = _mm(sv["att"], dx, ta=True, name="c_out_dw")
            dq_pre, dkvh, dkr = _mla_bwd(sv["qh"], sv["kvh"], sv["kr"], datt, sv["att"], sv["lse"], rc, rs1, rs2, seq=seq, heads=heads, name="mla_bwd")
            dk_rope = _rope_rows(dkr, rc, rs1, rs2, transpose=True, out_dtype=F32, name="rope_k_bwd")
            grads["c_w_uq"] = unpad_heads(_mm(sv["qn"], dq_pre, ta=True, name="c_uq_dw"))
            grads["c_w_ukv"] = _mm(sv["kvn"], dkvh, ta=True, name="c_ukv_dw")
            dqn = _mm(dq_pre, sv["w_uq_p"], tb=True, name="c_uq_dx")
            dkvn = _mm(dkvh, w["c_w_ukv"], tb=True, name="c_ukv_dx")
            dcq, d_qn[o] = _rmsnorm_bwd(sv["z"], q_norm_full[o], dqn, width=q_rank, col=0, name="norm_cq_bwd")
            dckv, d_kvn[o] = _rmsnorm_bwd(sv["z"], kv_norm_full[o], dkvn, width=kv_rank, col=q_rank // kv_rank, name="norm_ckv_bwd")
            dz = jnp.concatenate([dcq, dckv, dk_rope[:, :C_ROPE]], axis=1)
            grads["c_w_in"] = _mm(sv["xn"], dz, ta=True, name="c_in_dw")
            dxn = _mm(dz, w["c_w_in"], tb=True, name="c_in_dx")
        dx, dg_mix[layer] = _rmsnorm_bwd(sv["x0"], norm_mix_g[layer], dxn, dx, name="norm_mix_bwd")
        scatter_stage(layer, even_names if layer % 2 == 0 else odd_names, grads, dx)

    grad_x = dx.reshape(bl, seq, d)

    scatter_finish_d2d(sca[-1], dx)
    for st in sca[:-3]:
        scatter_collect(st, dx)
    big_out = {}
    done = chain[0]
    for nm, (wv, mv_, vv, ax) in big.items():
        n_sub = wv.shape[0]
        first = 0 if nm in odd_names else 1
        if first < n_sub:
            big_out[nm] = _adamw(wv, mv_, vv, parts[nm], ax, chip_arr, (first, n_sub - first), None, done, name=f"adamw_{nm}")
            done = big_out[nm][0]
    for st in sca[-3:]:
        scatter_collect(st, done)
    for nm, (wv, mv_, vv, ax) in big.items():
        if nm not in odd_names:
            big_out[nm] = _adamw(wv, mv_, vv, parts[nm], ax, chip_arr, (0, 1), big_out.get(nm), chain[0], name=f"adamw_{nm}_first")

    small_g = [jnp.concatenate(dg_mix), jnp.concatenate(dg_mq), jnp.concatenate(dg_mkv), jnp.concatenate(dg_ffn), dg_final.reshape(d),
               jnp.stack(d_av), jnp.stack(d_ws), jnp.stack(d_bs), jnp.concatenate(d_qn), jnp.concatenate(d_kvn), jnp.stack(d_conv)]
    small_shapes = [a.shape for a in small_w]
    g_all = _all_gather_small(_pack(small_g) + chain[0][0, 0])
    sg, sd, sm, svv = _adamw_small(_pack(small_w), _pack(small_m), _pack(small_v), g_all, name="adamw_small")
    small_names = ["norm_mix_g", "norm_mem_q_g", "norm_mem_kv_g", "norm_ffn_g", "final_norm_g", "a_v_norm_g", "a_w_s", "a_b_s", "c_q_norm_g", "c_kv_norm_g", "b_conv_w"]
    small_out = {}
    unpacked = [_unpack(p, small_shapes) for p in (sg, sd, sm, svv)]
    for i, nm in enumerate(small_names):
        vals = [u[i] for u in unpacked]
        if i >= 8:
            full_w, _, _, ax = split_small[i - 8]
            vals = [_my_block(val, ax, full_w.shape[ax]) for val in vals]
        small_out[nm] = vals

    order = ["norm_mix_g", "norm_mem_q_g", "norm_mem_kv_g", "norm_ffn_g", "final_norm_g", "ab_w_in", "a_v_norm_g", "a_w_s", "a_b_s", "b_conv_w", "ab_w_out",
             "c_w_in", "c_q_norm_g", "c_kv_norm_g", "c_w_uq", "c_w_ukv", "c_w_out", "m_wq", "m_wk", "m_wv", "m_wo", "f_w1", "f_w2"]
    res = [loss, grad_x]
    for kind in range(4):
        for nm in order:
            res.append(big_out[nm][kind] if nm in big_out else small_out[nm][kind])
    return tuple(res)
```

```python
import functools
import math

import jax
import jax.numpy as jnp
from jax import lax
from jax.experimental import pallas as pl
from jax.experimental.pallas import tpu as pltpu

F32 = jnp.float32
BF16 = jnp.bfloat16
MESH = pl.DeviceIdType.MESH
N_DEV = 8
LANES = 128

RMS_EPS = 1e-6
CHUNK = 64
GMLP_BLOCK = 128
C_NOPE = 128
C_ROPE = 64
C_V = 128
ROPE_THETA = 10000.0
MEM_HEADS = 4
ADAM_LR = 0.001
ADAM_B1 = 0.9
ADAM_B2 = 0.999
ADAM_EPS = 1e-08
ADAM_WD = 0.01
ADAM_STEP = 10
NEG_BIG = -1e30
VMEM_LIMIT = 56 * 1024 * 1024


def _cp(*sem):
    return pltpu.CompilerParams(dimension_semantics=sem, vmem_limit_bytes=VMEM_LIMIT)


def _tile(n, pref):
    t = min(n, pref)
    while n % t and t > LANES:
        t //= 2
    return n if n % t else t


def _mm(a, b, *, ta=False, tb=False, out_dtypes=(BF16,), extras=(), epilogue=None, tm=1024, tn=1024, tk=2048, name):
    m, k = (a.shape[1], a.shape[0]) if ta else a.shape
    n = b.shape[0] if tb else b.shape[1]
    assert (b.shape[1] if tb else b.shape[0]) == k, (a.shape, b.shape, ta, tb)
    tm, tn, tk = _tile(m, tm), _tile(n, tn), _tile(k, tk)
    nk = k // tk
    ne, no = len(extras), len(out_dtypes)
    dims = (((0 if ta else 1,), (1 if tb else 0,)), ((), ()))

    def body(*refs):
        a_ref, b_ref = refs[0], refs[1]
        e_refs = refs[2:2 + ne]
        o_refs = refs[2 + ne:2 + ne + no]

        def finish(acc):
            vals = [e[...] for e, (_, kind) in zip(e_refs, extras) if kind != "after"]
            outs = (acc,) if epilogue is None else epilogue(acc, *vals)
            for o, val in zip(o_refs, outs):
                o[...] = val.astype(o.dtype)

        prod = lax.dot_general(a_ref[...].astype(BF16), b_ref[...].astype(BF16), dims, preferred_element_type=F32)
        if nk == 1:
            finish(prod)
        else:
            acc_ref = refs[-1]
            kk = pl.program_id(2)

            @pl.when(kk == 0)
            def _():
                acc_ref[...] = prod

            @pl.when((kk > 0) & (kk < nk - 1))
            def _():
                acc_ref[...] += prod

            @pl.when(kk == nk - 1)
            def _():
                finish(acc_ref[...] + prod)

    a_spec = pl.BlockSpec((tk, tm), lambda i, j, kk: (kk, i)) if ta else pl.BlockSpec((tm, tk), lambda i, j, kk: (i, kk))
    b_spec = pl.BlockSpec((tn, tk), lambda i, j, kk: (j, kk)) if tb else pl.BlockSpec((tk, tn), lambda i, j, kk: (kk, j))
    e_specs = []
    for arr, kind in extras:
        if kind == "mn":
            e_specs.append(pl.BlockSpec((tm, tn), lambda i, j, kk: (i, j)))
        elif kind == "after":
            e_specs.append(pl.BlockSpec(arr.shape, lambda i, j, kk: (0, 0)))
        else:
            e_specs.append(pl.BlockSpec((tm, arr.shape[1]), lambda i, j, kk: (i, 0)))
    outs = pl.pallas_call(
        body,
        out_shape=tuple(jax.ShapeDtypeStruct((m, n), d) for d in out_dtypes),
        grid=(m // tm, n // tn, nk),
        in_specs=[a_spec, b_spec] + e_specs,
        out_specs=tuple(pl.BlockSpec((tm, tn), lambda i, j, kk: (i, j)) for _ in out_dtypes),
        scratch_shapes=[pltpu.VMEM((tm, tn), F32)] if nk > 1 else [],
        compiler_params=_cp("parallel", "parallel", "arbitrary"),
        name=name,
    )(a, b, *[arr for arr, _ in extras])
    return outs[0] if no == 1 else outs


def _epi_add(acc, r):
    return (acc + r,)


def _epi_relu2(acc):
    h = jnp.maximum(acc, 0.0)
    return (acc, h * h)


def _epi_drelu2(acc, h):
    return (acc * (2.0 * jnp.maximum(h.astype(F32), 0.0)),)


def _rope(x, c, s1, s2):
    return x * c + pltpu.roll(x, LANES - C_ROPE // 2, 1) * s1 + pltpu.roll(x, C_ROPE // 2, 1) * s2


def _rope_t(dy, c, s1, s2):
    return dy * c + pltpu.roll(dy * s1, C_ROPE // 2, 1) + pltpu.roll(dy * s2, LANES - C_ROPE // 2, 1)


def _epi_rope_heads(acc, c, s1, s2):
    parts = []
    for h0 in range(0, acc.shape[1], 2 * LANES):
        parts.append(acc[:, h0:h0 + LANES])
        parts.append(_rope(acc[:, h0 + LANES:h0 + 2 * LANES], c, s1, s2))
    return (jnp.concatenate(parts, axis=1),)


def _rmsnorm_fwd(x, g, *, width=None, col=0, name):
    t = x.shape[0]
    w = x.shape[1] if width is None else width
    tm = _tile(t, 512)

    def body(x_ref, g_ref, o_ref):
        xv = x_ref[...].astype(F32)
        r = lax.rsqrt(jnp.mean(xv * xv, axis=-1, keepdims=True) + RMS_EPS)
        o_ref[...] = (xv * r * g_ref[...]).astype(o_ref.dtype)

    return pl.pallas_call(
        body,
        out_shape=jax.ShapeDtypeStruct((t, w), BF16),
        grid=(t // tm,),
        in_specs=[pl.BlockSpec((tm, w), lambda i: (i, col)), pl.BlockSpec((1, w), lambda i: (0, 0))],
        out_specs=pl.BlockSpec((tm, w), lambda i: (i, 0)),
        compiler_params=_cp("parallel"),
        name=name,
    )(x, g.reshape(1, w))


def _rmsnorm_bwd(x, g, dy, res=None, *, width=None, col=0, name):
    t = x.shape[0]
    w = x.shape[1] if width is None else width
    tm = _tile(t, 256)
    has_res = res is not None

    def body(*refs):
        x_ref, g_ref, dy_ref = refs[:3]
        res_ref = refs[3] if has_res else None
        dx_ref, dg_ref = refs[-2], refs[-1]
        xv = x_ref[...].astype(F32)
        r = lax.rsqrt(jnp.mean(xv * xv, axis=-1, keepdims=True) + RMS_EPS)
        xh = xv * r
        dyv = dy_ref[...].astype(F32)
        dxh = dyv * g_ref[...]
        dx = r * (dxh - xh * jnp.mean(dxh * xh, axis=-1, keepdims=True))
        if has_res:
            dx = dx + res_ref[...]
        dx_ref[...] = dx

        @pl.when(pl.program_id(0) == 0)
        def _():
            dg_ref[...] = jnp.zeros_like(dg_ref)

        dg_ref[...] += jnp.sum(dyv * xh, axis=0, keepdims=True)

    in_specs = [pl.BlockSpec((tm, w), lambda i: (i, col)), pl.BlockSpec((1, w), lambda i: (0, 0)), pl.BlockSpec((tm, w), lambda i: (i, 0))]
    args = [x, g.reshape(1, w), dy]
    if has_res:
        in_specs.append(pl.BlockSpec((tm, w), lambda i: (i, 0)))
        args.append(res)
    return pl.pallas_call(
        body,
        out_shape=(jax.ShapeDtypeStruct((t, w), F32), jax.ShapeDtypeStruct((1, w), F32)),
        grid=(t // tm,),
        in_specs=in_specs,
        out_specs=(pl.BlockSpec((tm, w), lambda i: (i, 0)), pl.BlockSpec((1, w), lambda i: (0, 0))),
        compiler_params=_cp("arbitrary"),
        name=name,
    )(*args)


def _final_loss(x, g, target, *, name):
    t, d = x.shape
    tm = _tile(t, 256)

    def body(x_ref, g_ref, t_ref, loss_ref, dx_ref, dg_ref):
        xv = x_ref[...]
        r = lax.rsqrt(jnp.mean(xv * xv, axis=-1, keepdims=True) + RMS_EPS)
        xh = xv * r
        gv = g_ref[...]
        err = xh * gv - t_ref[...]
        dyv = err * (1.0 / d)
        dxh = dyv * gv
        dx_ref[...] = r * (dxh - xh * jnp.mean(dxh * xh, axis=-1, keepdims=True))

        @pl.when(pl.program_id(0) == 0)
        def _():
            dg_ref[...] = jnp.zeros_like(dg_ref)
            loss_ref[...] = jnp.zeros_like(loss_ref)

        dg_ref[...] += jnp.sum(dyv * xh, axis=0, keepdims=True)
        row = jnp.sum(err * err, axis=-1, keepdims=True) * (0.5 / d)
        loss_ref[...] += jnp.sum(row, axis=0, keepdims=True)

    return pl.pallas_call(
        body,
        out_shape=(jax.ShapeDtypeStruct((1, 1), F32), jax.ShapeDtypeStruct((t, d), F32), jax.ShapeDtypeStruct((1, d), F32)),
        grid=(t // tm,),
        in_specs=[pl.BlockSpec((tm, d), lambda i: (i, 0)), pl.BlockSpec((1, d), lambda i: (0, 0)), pl.BlockSpec((tm, d), lambda i: (i, 0))],
        out_specs=(pl.BlockSpec((1, 1), lambda i: (0, 0)), pl.BlockSpec((tm, d), lambda i: (i, 0)), pl.BlockSpec((1, d), lambda i: (0, 0))),
        compiler_params=_cp("arbitrary"),
        name=name,
    )(x, g.reshape(1, d), target)


_GELU_C = math.sqrt(2.0 / math.pi)
_GELU_A = 0.044715


def _gelu(x):
    t = jnp.tanh(_GELU_C * (x + _GELU_A * x * x * x))
    return 0.5 * x * (1.0 + t), t


def _dgelu(x, t):
    return 0.5 * (1.0 + t) + 0.5 * x * (1.0 - t * t) * (_GELU_C * (1.0 + 3.0 * _GELU_A * x * x))


def _spatial_mask():
    ci = lax.broadcasted_iota(jnp.int32, (GMLP_BLOCK, GMLP_BLOCK), 0) // CHUNK
    cj = lax.broadcasted_iota(jnp.int32, (GMLP_BLOCK, GMLP_BLOCK), 1) // CHUNK
    return (cj <= ci).astype(F32)


def _shift_down(x, k):
    rows = lax.broadcasted_iota(jnp.int32, x.shape, 0)
    return jnp.where(rows >= k, pltpu.roll(x, k, 0), 0.0)


def _shift_up(x, k):
    n = x.shape[0]
    rows = lax.broadcasted_iota(jnp.int32, x.shape, 0)
    return jnp.where(rows < n - k, pltpu.roll(x, n - k, 0), 0.0)


def _gmlp_conv_fwd(z, ws, bs, gn, cw, *, seq, name):
    t = z.shape[0]
    g_n = ws.shape[0]
    nb = seq // GMLP_BLOCK
    L = LANES

    def body(z_ref, ws_ref, bs_ref, gn_ref, cw_ref, y_ref):
        u = z_ref[:, 0:L]
        v = z_ref[:, L:2 * L]
        gu, _ = _gelu(u)
        gv, _ = _gelu(v)
        r = lax.rsqrt(jnp.mean(gv * gv, axis=-1, keepdims=True) + RMS_EPS)
        vg = (gv * r * gn_ref[0]).astype(BF16)
        wm = (ws_ref[0] * _spatial_mask()).astype(BF16)
        bias = bs_ref[0]
        for n in range(nb):
            sl = slice(n * GMLP_BLOCK, (n + 1) * GMLP_BLOCK)
            mixed = jnp.dot(wm, vg[sl], preferred_element_type=F32) + bias
            y_ref[sl, 0:L] = (gu[sl] * mixed).astype(y_ref.dtype)
        bg = z_ref[:, 2 * L:3 * L]
        zc = z_ref[:, 3 * L:4 * L] * z_ref[:, 4 * L:5 * L]
        w0, w1, w2 = cw_ref[0, 0:1, :], cw_ref[0, 1:2, :], cw_ref[0, 2:3, :]
        conv = w0 * _shift_down(zc, 2) + w1 * _shift_down(zc, 1) + w2 * zc
        y_ref[:, L:2 * L] = (bg * conv).astype(y_ref.dtype)

    return pl.pallas_call(
        body,
        out_shape=jax.ShapeDtypeStruct((t, 2 * L * g_n), BF16),
        grid=(g_n, t // seq),
        in_specs=[
            pl.BlockSpec((seq, 5 * L), lambda g, b: (b, g)),
            pl.BlockSpec((1, GMLP_BLOCK, GMLP_BLOCK), lambda g, b: (g, 0, 0)),
            pl.BlockSpec((1, GMLP_BLOCK, 1), lambda g, b: (g, 0, 0)),
            pl.BlockSpec((1, 1, L), lambda g, b: (g, 0, 0)),
            pl.BlockSpec((1, 3, L), lambda g, b: (g, 0, 0)),
        ],
        out_specs=pl.BlockSpec((seq, 2 * L), lambda g, b: (b, g)),
        compiler_params=_cp("parallel", "parallel"),
        name=name,
    )(z, ws, bs, gn, cw)


def _gmlp_conv_bwd(z, dy, ws, bs, gn, cw, *, seq, name):
    t = z.shape[0]
    g_n = ws.shape[0]
    nb = seq // GMLP_BLOCK
    L = LANES
    dims_nt = (((1,), (1,)), ((), ()))
    dims_tn = (((0,), (0,)), ((), ()))

    def body(z_ref, dy_ref, ws_ref, bs_ref, gn_ref, cw_ref, dz_ref, dws_ref, dbs_ref, dgn_ref, dcw_ref, dvg_ref):
        @pl.when(pl.program_id(1) == 0)
        def _():
            dws_ref[...] = jnp.zeros_like(dws_ref)
            dbs_ref[...] = jnp.zeros_like(dbs_ref)
            dgn_ref[...] = jnp.zeros_like(dgn_ref)
            dcw_ref[...] = jnp.zeros_like(dcw_ref)

        u = z_ref[:, 0:L]
        v = z_ref[:, L:2 * L]
        gu, tu = _gelu(u)
        gv, tv = _gelu(v)
        r = lax.rsqrt(jnp.mean(gv * gv, axis=-1, keepdims=True) + RMS_EPS)
        gvh = gv * r
        gnv = gn_ref[0]
        vg = (gvh * gnv).astype(BF16)
        mask = _spatial_mask()
        wm = (ws_ref[0] * mask).astype(BF16)
        bias = bs_ref[0]
        dya = dy_ref[:, 0:L].astype(F32)
        dws = jnp.zeros((GMLP_BLOCK, GMLP_BLOCK), F32)
        dbs = jnp.zeros((GMLP_BLOCK, 1), F32)
        for n in range(nb):
            sl = slice(n * GMLP_BLOCK, (n + 1) * GMLP_BLOCK)
            mixed = jnp.dot(wm, vg[sl], preferred_element_type=F32) + bias
            dmixed = dya[sl] * gu[sl]
            dgu = dya[sl] * mixed
            dz_ref[sl, 0:L] = (dgu * _dgelu(u[sl], tu[sl])).astype(dz_ref.dtype)
            dmb = dmixed.astype(BF16)
            dws = dws + lax.dot_general(dmb, vg[sl], dims_nt, preferred_element_type=F32)
            dbs = dbs + jnp.sum(dmixed, axis=1, keepdims=True)
            dvg_ref[sl, :] = lax.dot_general(wm, dmb, dims_tn, preferred_element_type=F32)
        dws_ref[0] += dws * mask
        dbs_ref[0] += dbs
        dvg = dvg_ref[...]
        dgn_ref[0] += jnp.sum(dvg * gvh, axis=0, keepdims=True)
        dvh = dvg * gnv
        dgv = r * (dvh - gvh * jnp.mean(dvh * gvh, axis=-1, keepdims=True))
        dz_ref[:, L:2 * L] = (dgv * _dgelu(v, tv)).astype(dz_ref.dtype)

        bg = z_ref[:, 2 * L:3 * L]
        cg = z_ref[:, 3 * L:4 * L]
        hh = z_ref[:, 4 * L:5 * L]
        zc = cg * hh
        w0, w1, w2 = cw_ref[0, 0:1, :], cw_ref[0, 1:2, :], cw_ref[0, 2:3, :]
        z1 = _shift_down(zc, 1)
        z2 = _shift_down(zc, 2)
        conv = w0 * z2 + w1 * z1 + w2 * zc
        dyb = dy_ref[:, L:2 * L].astype(F32)
        dconv = dyb * bg
        dz_ref[:, 2 * L:3 * L] = (dyb * conv).astype(dz_ref.dtype)
        dzc = w2 * dconv + w1 * _shift_up(dconv, 1) + w0 * _shift_up(dconv, 2)
        dz_ref[:, 3 * L:4 * L] = (dzc * hh).astype(dz_ref.dtype)
        dz_ref[:, 4 * L:5 * L] = (dzc * cg).astype(dz_ref.dtype)
        dcw_ref[0, 0:1, :] += jnp.sum(dconv * z2, axis=0, keepdims=True)
        dcw_ref[0, 1:2, :] += jnp.sum(dconv * z1, axis=0, keepdims=True)
        dcw_ref[0, 2:3, :] += jnp.sum(dconv * zc, axis=0, keepdims=True)

    return pl.pallas_call(
        body,
        out_shape=(
            jax.ShapeDtypeStruct((t, 5 * L * g_n), BF16),
            jax.ShapeDtypeStruct(ws.shape, F32),
            jax.ShapeDtypeStruct(bs.shape, F32),
            jax.ShapeDtypeStruct(gn.shape, F32),
            jax.ShapeDtypeStruct(cw.shape, F32),
        ),
        grid=(g_n, t // seq),
        in_specs=[
            pl.BlockSpec((seq, 5 * L), lambda g, b: (b, g)),
            pl.BlockSpec((seq, 2 * L), lambda g, b: (b, g)),
            pl.BlockSpec((1, GMLP_BLOCK, GMLP_BLOCK), lambda g, b: (g, 0, 0)),
            pl.BlockSpec((1, GMLP_BLOCK, 1), lambda g, b: (g, 0, 0)),
            pl.BlockSpec((1, 1, L), lambda g, b: (g, 0, 0)),
            pl.BlockSpec((1, 3, L), lambda g, b: (g, 0, 0)),
        ],
        out_specs=(
            pl.BlockSpec((seq, 5 * L), lambda g, b: (b, g)),
            pl.BlockSpec((1, GMLP_BLOCK, GMLP_BLOCK), lambda g, b: (g, 0, 0)),
            pl.BlockSpec((1, GMLP_BLOCK, 1), lambda g, b: (g, 0, 0)),
            pl.BlockSpec((1, 1, L), lambda g, b: (g, 0, 0)),
            pl.BlockSpec((1, 3, L), lambda g, b: (g, 0, 0)),
        ),
        scratch_shapes=[pltpu.VMEM((seq, L), F32)],
        compiler_params=_cp("parallel", "arbitrary"),
        name=name,
    )(z, dy, ws, bs, gn, cw)


def _rope_rows(x, c, s1, s2, *, transpose, out_dtype, name):
    t, w = x.shape
    tm = _tile(t, 512)
    only = w == LANES
    fn = _rope_t if transpose else _rope

    def body(x_ref, c_ref, s1_ref, s2_ref, o_ref):
        cv, s1v, s2v = c_ref[...], s1_ref[...], s2_ref[...]
        if only:
            o_ref[...] = fn(x_ref[...].astype(F32), cv, s1v, s2v).astype(o_ref.dtype)
        else:
            for h0 in range(0, w, 2 * LANES):
                o_ref[:, h0:h0 + LANES] = x_ref[:, h0:h0 + LANES].astype(o_ref.dtype)
                o_ref[:, h0 + LANES:h0 + 2 * LANES] = fn(x_ref[:, h0 + LANES:h0 + 2 * LANES].astype(F32), cv, s1v, s2v).astype(o_ref.dtype)

    tab = pl.BlockSpec((tm, LANES), lambda i: (i, 0))
    return pl.pallas_call(
        body,
        out_shape=jax.ShapeDtypeStruct((t, w), out_dtype),
        grid=(t // tm,),
        in_specs=[pl.BlockSpec((tm, w), lambda i: (i, 0)), tab, tab, tab],
        out_specs=pl.BlockSpec((tm, w), lambda i: (i, 0)),
        compiler_params=_cp("parallel"),
        name=name,
    )(x, c, s1, s2)


_NT = (((1,), (1,)), ((), ()))
_TN = (((0,), (0,)), ((), ()))


def _chunk_mask(tq):
    qc = lax.broadcasted_iota(jnp.int32, (tq, tq), 0) // CHUNK
    kc = lax.broadcasted_iota(jnp.int32, (tq, tq), 1) // CHUNK
    return kc <= qc


MLA_BLOCK = 512


def _mla_fwd(q, kv, kr, *, seq, heads, name):
    t = q.shape[0]
    tq = _tile(seq, MLA_BLOCK)
    nq = seq // tq
    scale = (C_NOPE + C_ROPE) ** -0.5

    def body(q_ref, kv_ref, kr_ref, o_ref, lse_ref, kcat_ref):
        i = pl.program_id(2)

        @pl.when(i == 0)
        def _():
            kcat_ref[:, :LANES] = kv_ref[:, :LANES]
            kcat_ref[:, LANES:] = kr_ref[...]

        qv = q_ref[...]

        def step(j, carry, diagonal):
            m_run, l_run, acc = carry
            rows = pl.ds(pl.multiple_of(j * tq, tq), tq)
            s = lax.dot_general(qv, kcat_ref[rows, :], _NT, preferred_element_type=F32) * scale
            if diagonal:
                s = jnp.where(_chunk_mask(tq), s, NEG_BIG)
            m_new = jnp.maximum(m_run, jnp.max(s, axis=-1, keepdims=True))
            alpha = jnp.exp(m_run - m_new)
            p = jnp.exp(s - m_new)
            l_new = alpha * l_run + jnp.sum(p, axis=-1, keepdims=True)
            acc = alpha * acc + jnp.dot(p.astype(BF16), kv_ref[rows, LANES:], preferred_element_type=F32)
            return m_new, l_new, acc

        init = (jnp.full((tq, 1), NEG_BIG, F32), jnp.zeros((tq, 1), F32), jnp.zeros((tq, C_V), F32))
        carry = lax.fori_loop(0, i, lambda j, c: step(j, c, False), init)
        m_run, l_run, acc = step(i, carry, True)
        o_ref[...] = (acc / l_run).astype(o_ref.dtype)
        lse_ref[...] = m_run + jnp.log(l_run)

    return pl.pallas_call(
        body,
        out_shape=(jax.ShapeDtypeStruct((t, heads * C_V), BF16), jax.ShapeDtypeStruct((heads, t, 1), F32)),
        grid=(t // seq, heads, nq),
        in_specs=[
            pl.BlockSpec((tq, 2 * LANES), lambda b, h, i: (b * nq + i, h)),
            pl.BlockSpec((seq, 2 * LANES), lambda b, h, i: (b, h)),
            pl.BlockSpec((seq, LANES), lambda b, h, i: (b, 0)),
        ],
        out_specs=(
            pl.BlockSpec((tq, C_V), lambda b, h, i: (b * nq + i, h)),
            pl.BlockSpec((None, tq, 1), lambda b, h, i: (h, b * nq + i, 0)),
        ),
        scratch_shapes=[pltpu.VMEM((seq, 2 * LANES), BF16)],
        compiler_params=_cp("parallel", "parallel", "arbitrary"),
        name=name,
    )(q, kv, kr)


def _mla_bwd(q, kv, kr, do, o, lse, rc, rs1, rs2, *, seq, heads, name):
    t = q.shape[0]
    tq = _tile(seq, MLA_BLOCK)
    nq = seq // tq
    scale = (C_NOPE + C_ROPE) ** -0.5

    def body(q_ref, kv_ref, kr_ref, do_ref, o_ref, lse_ref, rc_ref, rs1_ref, rs2_ref, dq_ref, dkv_ref, dkr_ref, kcat_ref, dk_ref, dv_ref):
        h, i = pl.program_id(1), pl.program_id(2)

        @pl.when(i == 0)
        def _():
            kcat_ref[:, :LANES] = kv_ref[:, :LANES]
            kcat_ref[:, LANES:] = kr_ref[...]
            dk_ref[...] = jnp.zeros_like(dk_ref)
            dv_ref[...] = jnp.zeros_like(dv_ref)

        @pl.when((i == 0) & (h == 0))
        def _():
            dkr_ref[...] = jnp.zeros_like(dkr_ref)

        qv = q_ref[...]
        dov = do_ref[...]
        lse_v = lse_ref[...]
        delta = jnp.sum(dov.astype(F32) * o_ref[...].astype(F32), axis=-1, keepdims=True)

        def step(j, dq, diagonal):
            rows = pl.ds(pl.multiple_of(j * tq, tq), tq)
            kj = kcat_ref[rows, :]
            s = lax.dot_general(qv, kj, _NT, preferred_element_type=F32) * scale
            if diagonal:
                s = jnp.where(_chunk_mask(tq), s, NEG_BIG)
            p = jnp.exp(s - lse_v)
            dp = lax.dot_general(dov, kv_ref[rows, LANES:], _NT, preferred_element_type=F32)
            ds = (p * (dp - delta) * scale).astype(BF16)
            dk_ref[rows, :] += lax.dot_general(ds, qv, _TN, preferred_element_type=F32)
            dv_ref[rows, :] += lax.dot_general(p.astype(BF16), dov, _TN, preferred_element_type=F32)
            return dq + jnp.dot(ds, kj, preferred_element_type=F32)

        dq = lax.fori_loop(0, i, lambda j, c: step(j, c, False), jnp.zeros((tq, 2 * LANES), F32))
        dq = step(i, dq, True)
        dq_ref[:, :LANES] = dq[:, :LANES].astype(dq_ref.dtype)
        dq_ref[:, LANES:] = _rope_t(dq[:, LANES:], rc_ref[...], rs1_ref[...], rs2_ref[...]).astype(dq_ref.dtype)

        @pl.when(i == nq - 1)
        def _():
            dkv_ref[:, :LANES] = dk_ref[:, :LANES].astype(dkv_ref.dtype)
            dkv_ref[:, LANES:] = dv_ref[...].astype(dkv_ref.dtype)
            dkr_ref[...] += dk_ref[:, LANES:]

    return pl.pallas_call(
        body,
        out_shape=(
            jax.ShapeDtypeStruct((t, heads * 2 * LANES), BF16),
            jax.ShapeDtypeStruct((t, heads * 2 * LANES), BF16),
            jax.ShapeDtypeStruct((t, LANES), F32),
        ),
        grid=(t // seq, heads, nq),
        in_specs=[
            pl.BlockSpec((tq, 2 * LANES), lambda b, h, i: (b * nq + i, h)),
            pl.BlockSpec((seq, 2 * LANES), lambda b, h, i: (b, h)),
            pl.BlockSpec((seq, LANES), lambda b, h, i: (b, 0)),
            pl.BlockSpec((tq, C_V), lambda b, h, i: (b * nq + i, h)),
            pl.BlockSpec((tq, C_V), lambda b, h, i: (b * nq + i, h)),
            pl.BlockSpec((None, tq, 1), lambda b, h, i: (h, b * nq + i, 0)),
            pl.BlockSpec((tq, LANES), lambda b, h, i: (b * nq + i, 0)),
            pl.BlockSpec((tq, LANES), lambda b, h, i: (b * nq + i, 0)),
            pl.BlockSpec((tq, LANES), lambda b, h, i: (b * nq + i, 0)),
        ],
        out_specs=(
            pl.BlockSpec((tq, 2 * LANES), lambda b, h, i: (b * nq + i, h)),
            pl.BlockSpec((seq, 2 * LANES), lambda b, h, i: (b, h)),
            pl.BlockSpec((seq, LANES), lambda b, h, i: (b, 0)),
        ),
        scratch_shapes=[pltpu.VMEM((seq, 2 * LANES), BF16), pltpu.VMEM((seq, 2 * LANES), F32), pltpu.VMEM((seq, C_V), F32)],
        compiler_params=_cp("parallel", "arbitrary", "arbitrary"),
        name=name,
    )(q, kv, kr, do, o, lse, rc, rs1, rs2)


def _softmax_rows(s):
    e = jnp.exp(s - jnp.max(s, axis=-1, keepdims=True))
    return e / jnp.sum(e, axis=-1, keepdims=True)


def _mem_fwd(q, k, v, *, seq, n_mem, name):
    t, d = q.shape
    hd = d // MEM_HEADS
    tq = _tile(seq, 512)
    nq = seq // tq
    scale = hd ** -0.5

    def body(q_ref, k_ref, v_ref, o_ref):
        s = lax.dot_general(q_ref[...], k_ref[...], _NT, preferred_element_type=F32) * scale
        p = _softmax_rows(s)
        o_ref[...] = jnp.dot(p.astype(BF16), v_ref[...], preferred_element_type=F32).astype(o_ref.dtype)

    return pl.pallas_call(
        body,
        out_shape=jax.ShapeDtypeStruct((t, d), BF16),
        grid=(t // seq, MEM_HEADS, nq),
        in_specs=[
            pl.BlockSpec((tq, hd), lambda b, h, i: (b * nq + i, h)),
            pl.BlockSpec((n_mem, hd), lambda b, h, i: (b, h)),
            pl.BlockSpec((n_mem, hd), lambda b, h, i: (b, h)),
        ],
        out_specs=pl.BlockSpec((tq, hd), lambda b, h, i: (b * nq + i, h)),
        compiler_params=_cp("parallel", "parallel", "arbitrary"),
        name=name,
    )(q, k, v)


def _mem_bwd(q, k, v, do, *, seq, n_mem, name):
    t, d = q.shape
    hd = d // MEM_HEADS
    tq = _tile(seq, 512)
    nq = seq // tq
    scale = hd ** -0.5

    def body(q_ref, k_ref, v_ref, do_ref, dq_ref, dk_ref, dv_ref):
        @pl.when(pl.program_id(2) == 0)
        def _():
            dk_ref[...] = jnp.zeros_like(dk_ref)
            dv_ref[...] = jnp.zeros_like(dv_ref)

        qv, kv_, vv, dov = q_ref[...], k_ref[...], v_ref[...], do_ref[...]
        p = _softmax_rows(lax.dot_general(qv, kv_, _NT, preferred_element_type=F32) * scale)
        dp = lax.dot_general(dov, vv, _NT, preferred_element_type=F32)
        ds = (p * (dp - jnp.sum(dp * p, axis=-1, keepdims=True)) * scale).astype(BF16)
        dq_ref[...] = jnp.dot(ds, kv_, preferred_element_type=F32).astype(dq_ref.dtype)
        dk_ref[...] += lax.dot_general(ds, qv, _TN, preferred_element_type=F32)
        dv_ref[...] += lax.dot_general(p.astype(BF16), dov, _TN, preferred_element_type=F32)

    return pl.pallas_call(
        body,
        out_shape=(
            jax.ShapeDtypeStruct((t, d), BF16),
            jax.ShapeDtypeStruct(k.shape, F32),
            jax.ShapeDtypeStruct(v.shape, F32),
        ),
        grid=(t // seq, MEM_HEADS, nq),
        in_specs=[
            pl.BlockSpec((tq, hd), lambda b, h, i: (b * nq + i, h)),
            pl.BlockSpec((n_mem, hd), lambda b, h, i: (b, h)),
            pl.BlockSpec((n_mem, hd), lambda b, h, i: (b, h)),
            pl.BlockSpec((tq, hd), lambda b, h, i: (b * nq + i, h)),
        ],
        out_specs=(
            pl.BlockSpec((tq, hd), lambda b, h, i: (b * nq + i, h)),
            pl.BlockSpec((n_mem, hd), lambda b, h, i: (b, h)),
            pl.BlockSpec((n_mem, hd), lambda b, h, i: (b, h)),
        ),
        compiler_params=_cp("parallel", "parallel", "arbitrary"),
        name=name,
    )(q, k, v, do)


def _me():
    return lax.axis_index("x"), lax.axis_index("y"), lax.axis_index("c")


def _dev_index(px, py, pc):
    return 4 * px + 2 * py + pc


def _chunk(ref, idx, axis, size):
    start = pl.multiple_of(idx * size, size)
    return ref.at[pl.ds(start, size), :] if axis == 0 else ref.at[:, pl.ds(start, size)]


def _all_gather_small(v):
    m, n = v.shape

    def body(x_ref, out_ref, send_sems, recv_sems, local_sem):
        x, y, c = _me()
        me, sibling = (x, y, c), (x, y, 1 - c)
        chips = [(1 - x, y), (x, 1 - y), (1 - x, 1 - y)]

        def rows(px, py, pc):
            return out_ref.at[_dev_index(px, py, pc)]

        def copy(k, block, to, src=None):
            return pltpu.make_async_remote_copy(
                src_ref=rows(*block) if src is None else src, dst_ref=rows(*block),
                send_sem=send_sems.at[k], recv_sem=recv_sems.at[k], device_id=to, device_id_type=MESH)

        mine = pltpu.make_async_copy(x_ref, rows(*me), local_sem)
        mine.start()
        first = [copy(0, me, sibling, src=x_ref)]
        first += [copy(1 + j, me, (*chip, c), src=x_ref) for j, chip in enumerate(chips)]
        for cp in first:
            cp.start()
        passed = [copy(4 + j, (*chip, c), sibling) for j, chip in enumerate(chips)]
        for j, chip in enumerate(chips):
            copy(1 + j, (*chip, c), me).wait_recv()
            passed[j].start()
        copy(0, sibling, me).wait_recv()
        for j, chip in enumerate(chips):
            copy(4 + j, (*chip, 1 - c), me).wait_recv()
        for cp in first + passed:
            cp.wait_send()
        mine.wait()

    return pl.pallas_call(
        body,
        out_shape=jax.ShapeDtypeStruct((N_DEV, m, n), v.dtype),
        in_specs=[pl.BlockSpec(memory_space=pltpu.VMEM)],
        out_specs=pl.BlockSpec(memory_space=pltpu.VMEM),
        scratch_shapes=[pltpu.SemaphoreType.DMA((7,)), pltpu.SemaphoreType.DMA((7,)), pltpu.SemaphoreType.DMA],
        name="all_gather_small",
    )(v)


def _cast_place(w, layer, axis, dev_arr, *, name):
    _, r, c = w.shape
    tr = _tile(r, 256)
    nr = r // tr
    full = (N_DEV * r, c) if axis == 0 else (r, N_DEV * c)

    def body(dev_ref, w_ref, o_ref):
        o_ref[...] = w_ref[...].astype(o_ref.dtype)

    if axis == 0:
        out_spec = pl.BlockSpec((tr, c), lambda i, dev: (dev[0] * nr + i, 0))
    else:
        out_spec = pl.BlockSpec((tr, c), lambda i, dev: (i, dev[0]))
    return pl.pallas_call(
        body,
        out_shape=jax.ShapeDtypeStruct(full, BF16),
        grid_spec=pltpu.PrefetchScalarGridSpec(
            num_scalar_prefetch=1, grid=(nr,), in_specs=[pl.BlockSpec((None, tr, c), lambda i, dev: (layer, i, 0))], out_specs=out_spec),
        compiler_params=_cp("parallel"),
        name=name,
    )(dev_arr, w)


_ANY = pl.BlockSpec(memory_space=pl.ANY)
_SEM = pl.BlockSpec(memory_space=pltpu.SEMAPHORE)
_EFFECT = pltpu.SideEffectType.DATAFLOW_SIDE_EFFECTING
N_PEER_CHIPS = 3


def _other_chips(x, y):
    return [((1 - x if (j >> 1) & 1 else x), (1 - y if j & 1 else y)) for j in range(1, 4)]


def _remote(src, dst, send_sem, recv_sem, to):
    return pltpu.make_async_remote_copy(src_ref=src, dst_ref=dst, send_sem=send_sem, recv_sem=recv_sem, device_id=to, device_id_type=MESH)


def _plan_gather_ici(axes, sizes):
    def plan(refs, send_sems, recv_sems):
        x, y, c = _me()
        out = []
        for t in range(len(axes)):
            for j, chip in enumerate(_other_chips(x, y)):
                mine = _chunk(refs[t], _dev_index(x, y, c), axes[t], sizes[t])
                theirs = _chunk(refs[t], _dev_index(*chip, c), axes[t], sizes[t])
                out.append((_remote(mine, mine, send_sems[j], recv_sems[j], (*chip, c)), _remote(theirs, theirs, send_sems[j], recv_sems[j], (*chip, c))))
        return out
    return plan


def _plan_gather_d2d(axes, sizes):
    def plan(refs, send_sems, recv_sems):
        x, y, c = _me()
        out = []
        for t in range(len(axes)):
            for k in range(4):
                have = _chunk(refs[t], 2 * k + c, axes[t], sizes[t])
                get = _chunk(refs[t], 2 * k + 1 - c, axes[t], sizes[t])
                out.append((_remote(have, have, send_sems[0], recv_sems[0], (x, y, 1 - c)), _remote(get, get, send_sems[0], recv_sems[0], (x, y, 1 - c))))
        return out
    return plan


def _plan_scatter_d2d(axes, sizes):
    n = len(axes)

    def plan(refs, send_sems, recv_sems):
        x, y, c = _me()
        out = []
        for t in range(n):
            for k in range(4):
                give = _chunk(refs[t], 2 * k + 1 - c, axes[t], sizes[t])
                land = _chunk(refs[n + t], k, axes[t], sizes[t])
                cp = _remote(give, land, send_sems[0], recv_sems[0], (x, y, 1 - c))
                out.append((cp, cp))
        return out
    return plan


def _plan_scatter_ici(axes, sizes):
    n = len(axes)

    def plan(refs, send_sems, recv_sems):
        x, y, c = _me()
        out = []
        for t in range(n):
            for j, (px, py) in enumerate(_other_chips(x, y)):
                send = _remote(_chunk(refs[t], 2 * px + py, axes[t], sizes[t]), _chunk(refs[n + t], 2 * x + y, axes[t], sizes[t]),
                               send_sems[j], recv_sems[j], (px, py, c))
                recv = _remote(_chunk(refs[t], 2 * x + y, axes[t], sizes[t]), _chunk(refs[n + t], 2 * px + py, axes[t], sizes[t]),
                               send_sems[j], recv_sems[j], (px, py, c))
                out.append((send, recv))
        return out
    return plan


def _exchange_start(arrays, n_sems, plan, after, *, name):
    n = len(arrays)
    m = len(after)

    def body(*refs):
        send_sems, recv_sems = list(refs[n + m:n + m + n_sems]), list(refs[n + m + n_sems:n + m + 2 * n_sems])
        for send, _ in plan(refs[:n], send_sems, recv_sems):
            send.start()
        refs[-1][...] = jnp.zeros_like(refs[-1])

    outs = pl.pallas_call(
        body,
        out_shape=(*([pltpu.SemaphoreType.DMA(())] * (2 * n_sems)),
                   *[jax.ShapeDtypeStruct(a.shape, a.dtype) for a in arrays], jax.ShapeDtypeStruct((8, LANES), F32)),
        in_specs=[_ANY] * (n + m),
        out_specs=(*([_SEM] * (2 * n_sems)), *([_ANY] * n), pl.BlockSpec(memory_space=pltpu.VMEM)),
        input_output_aliases={t: 2 * n_sems + t for t in range(n)},
        compiler_params=pltpu.CompilerParams(has_side_effects=_EFFECT),
        name=name,
    )(*arrays, *after)
    return list(outs[:2 * n_sems]), list(outs[2 * n_sems:2 * n_sems + n]), outs[-1]


def _exchange_wait(sems, arrays, plan, after, *, name):
    n = len(arrays)
    n_sems = len(sems) // 2

    def body(*refs):
        send_sems, recv_sems = list(refs[n:n + n_sems]), list(refs[n + n_sems:n + 2 * n_sems])
        for send, recv in plan(refs[:n], send_sems, recv_sems):
            send.wait_send()
            recv.wait_recv()
        refs[-1][...] = jnp.zeros_like(refs[-1])

    outs = pl.pallas_call(
        body,
        out_shape=(*[jax.ShapeDtypeStruct(a.shape, a.dtype) for a in arrays], jax.ShapeDtypeStruct((8, LANES), F32)),
        in_specs=[_ANY] * n + [_SEM] * (2 * n_sems) + [_ANY] * len(after),
        out_specs=(*([_ANY] * n), pl.BlockSpec(memory_space=pltpu.VMEM)),
        input_output_aliases={t: t for t in range(n)},
        compiler_params=pltpu.CompilerParams(has_side_effects=_EFFECT),
        name=name,
    )(*arrays, *sems, *after)
    return list(outs[:n]), outs[-1]


def _add_pair(g, got, axis, core_arr, *, name):
    r, c = got.shape
    if axis == 0:
        rows, cols = r // 4, c
    else:
        rows, cols = r, c // 4
    tr = _tile(rows, 512)
    nr = rows // tr

    def body(core_ref, g_ref, got_ref, o_ref):
        o_ref[...] = (g_ref[...].astype(F32) + got_ref[...].astype(F32)).astype(o_ref.dtype)

    if axis == 0:
        g_spec = pl.BlockSpec((tr, cols), lambda k, i, core: ((2 * k + core[0]) * nr + i, 0))
        spec = pl.BlockSpec((tr, cols), lambda k, i, core: (k * nr + i, 0))
    else:
        g_spec = pl.BlockSpec((tr, cols), lambda k, i, core: (i, 2 * k + core[0]))
        spec = pl.BlockSpec((tr, cols), lambda k, i, core: (i, k))
    return pl.pallas_call(
        body,
        out_shape=jax.ShapeDtypeStruct(got.shape, got.dtype),
        grid_spec=pltpu.PrefetchScalarGridSpec(num_scalar_prefetch=1, grid=(4, nr), in_specs=[g_spec, spec], out_specs=spec),
        compiler_params=_cp("parallel", "parallel"),
        name=name,
    )(core_arr, g, got)


def _adam_math(w, g, m, v):
    m = ADAM_B1 * m + (1.0 - ADAM_B1) * g
    v = ADAM_B2 * v + (1.0 - ADAM_B2) * (g * g)
    m_hat = m / (1.0 - ADAM_B1 ** ADAM_STEP)
    v_hat = v / (1.0 - ADAM_B2 ** ADAM_STEP)
    delta = -ADAM_LR * (m_hat / (jnp.sqrt(v_hat) + ADAM_EPS) + ADAM_WD * w)
    return delta, m, v


def _adamw(w, m, v, parts, axis, chip_arr, layers, prev, after, *, name):
    n_l, r, c = w.shape
    first, count = layers
    tr = _tile(r, 128)
    nr = r // tr
    n_chip = 4
    n_in = 4 + n_chip * count

    def body(*refs):
        w_ref, m_ref, v_ref = refs[1:4]
        p_refs = refs[4:n_in]
        g_ref, d_ref, nm_ref, nv_ref = refs[-4:]
        for l in range(count):
            @pl.when(pl.program_id(0) == l)
            def _(l=l):
                g = p_refs[n_chip * l][...].astype(F32)
                for k in range(1, n_chip):
                    g = g + p_refs[n_chip * l + k][...].astype(F32)
                delta, nm, nv = _adam_math(w_ref[...], g, m_ref[...], v_ref[...])
                g_ref[...] = g
                d_ref[...] = delta
                nm_ref[...] = nm
                nv_ref[...] = nv

    def part_spec(ll, k):
        if axis == 0:
            return pl.BlockSpec((tr, c), lambda l, i, chips: (chips[k] * nr + jnp.where(l == ll, i, 0), 0))
        return pl.BlockSpec((tr, c), lambda l, i, chips: (jnp.where(l == ll, i, 0), chips[k]))

    wspec = pl.BlockSpec((None, tr, c), lambda l, i, chips: (first + l, i, 0))
    pspecs = [part_spec(ll, k) for ll in range(count) for k in range(n_chip)]
    sds = jax.ShapeDtypeStruct(w.shape, F32)
    kept = [] if prev is None else list(prev)
    return pl.pallas_call(
        body,
        out_shape=(sds, sds, sds, sds),
        grid_spec=pltpu.PrefetchScalarGridSpec(
            num_scalar_prefetch=1, grid=(count, nr),
            in_specs=[wspec, wspec, wspec] + pspecs + [_ANY] * (1 + len(kept)), out_specs=(wspec, wspec, wspec, wspec)),
        input_output_aliases={n_in + 1 + k: k for k in range(len(kept))},
        compiler_params=_cp("arbitrary", "arbitrary"),
        name=name,
    )(chip_arr, w, m, v, *[parts[first + ll][0 if k == 0 else 1] for ll in range(count) for k in range(n_chip)], after, *kept)


def _adamw_small(w, m, v, parts, *, name):
    r, c = w.shape
    tr = _tile(r, 512)

    def body(w_ref, m_ref, v_ref, p_ref, g_ref, d_ref, nm_ref, nv_ref):
        g = p_ref[0]
        for s in range(1, N_DEV):
            g = g + p_ref[s]
        delta, nm, nv = _adam_math(w_ref[...], g, m_ref[...], v_ref[...])
        g_ref[...] = g
        d_ref[...] = delta
        nm_ref[...] = nm
        nv_ref[...] = nv

    spec = pl.BlockSpec((tr, c), lambda i: (i, 0))
    sds = jax.ShapeDtypeStruct(w.shape, F32)
    return pl.pallas_call(
        body,
        out_shape=(sds, sds, sds, sds),
        grid=(r // tr,),
        in_specs=[spec, spec, spec, pl.BlockSpec((N_DEV, tr, c), lambda i: (0, i, 0))],
        out_specs=(spec, spec, spec, spec),
        compiler_params=_cp("parallel"),
        name=name,
    )(w, m, v, parts)


def _pack(arrs):
    flat = jnp.concatenate([a.reshape(-1).astype(F32) for a in arrs])
    pad = (-flat.shape[0]) % (8 * LANES)
    return jnp.pad(flat, (0, pad)).reshape(-1, LANES)


def _unpack(packed, shapes):
    flat = packed.reshape(-1)
    out, off = [], 0
    for sh in shapes:
        n = math.prod(sh)
        out.append(flat[off:off + n].reshape(sh))
        off += n
    return out


def _unpack_gathered(packed, shapes, axis):
    flat = packed.reshape(N_DEV, -1)
    out, off = [], 0
    for sh, ax in zip(shapes, axis):
        n = math.prod(sh)
        blocks = flat[:, off:off + n].reshape((N_DEV, *sh))
        out.append(jnp.concatenate([blocks[d] for d in range(N_DEV)], axis=ax))
        off += n
    return out


def _my_block(full, axis, size):
    idx = _dev_index(*_me())
    return lax.dynamic_slice_in_dim(full, idx * size, size, axis)


def kernel(x, mem, positions, norm_mix_g, norm_mem_q_g, norm_mem_kv_g, norm_ffn_g, final_norm_g, ab_w_in, a_v_norm_g, a_w_s, a_b_s, b_conv_w, ab_w_out, c_w_in, c_q_norm_g, c_kv_norm_g, c_w_uq, c_w_ukv, c_w_out, m_wq, m_wk, m_wv, m_wo, f_w1, f_w2, loss_target, m_norm_mix_g, m_norm_mem_q_g, m_norm_mem_kv_g, m_norm_ffn_g, m_final_norm_g, m_ab_w_in, m_a_v_norm_g, m_a_w_s, m_a_b_s, m_b_conv_w, m_ab_w_out, m_c_w_in, m_c_q_norm_g, m_c_kv_norm_g, m_c_w_uq, m_c_w_ukv, m_c_w_out, m_m_wq, m_m_wk, m_m_wv, m_m_wo, m_f_w1, m_f_w2, v_norm_mix_g, v_norm_mem_q_g, v_norm_mem_kv_g, v_norm_ffn_g, v_final_norm_g, v_ab_w_in, v_a_v_norm_g, v_a_w_s, v_a_b_s, v_b_conv_w, v_ab_w_out, v_c_w_in, v_c_q_norm_g, v_c_kv_norm_g, v_c_w_uq, v_c_w_ukv, v_c_w_out, v_m_wq, v_m_wk, v_m_wv, v_m_wo, v_f_w1, v_f_w2):
    bl, seq, d = x.shape
    t = bl * seq
    n_mem = mem.shape[1]
    depth = norm_mix_g.shape[0]
    n_even, n_odd = ab_w_in.shape[0], c_w_in.shape[0]
    a_width = a_v_norm_g.shape[1]
    groups = a_w_s.shape[1]
    heads = d // C_NOPE
    q_rank = c_q_norm_g.shape[1] * N_DEV
    kv_rank = c_kv_norm_g.shape[1] * N_DEV

    big = {
        "ab_w_in": (ab_w_in, m_ab_w_in, v_ab_w_in, 1), "ab_w_out": (ab_w_out, m_ab_w_out, v_ab_w_out, 0),
        "c_w_in": (c_w_in, m_c_w_in, v_c_w_in, 0), "c_w_uq": (c_w_uq, m_c_w_uq, v_c_w_uq, 1),
        "c_w_ukv": (c_w_ukv, m_c_w_ukv, v_c_w_ukv, 1), "c_w_out": (c_w_out, m_c_w_out, v_c_w_out, 0),
        "m_wq": (m_wq, m_m_wq, v_m_wq, 0), "m_wk": (m_wk, m_m_wk, v_m_wk, 0), "m_wv": (m_wv, m_m_wv, v_m_wv, 0),
        "m_wo": (m_wo, m_m_wo, v_m_wo, 0), "f_w1": (f_w1, m_f_w1, v_f_w1, 1), "f_w2": (f_w2, m_f_w2, v_f_w2, 0),
    }
    even_names = ["ab_w_in", "ab_w_out"]
    odd_names = ["c_w_in", "c_w_uq", "c_w_ukv", "c_w_out"]
    all_names = ["m_wq", "m_wk", "m_wv", "m_wo", "f_w1", "f_w2"]

    def layer_names(layer):
        return (even_names if layer % 2 == 0 else odd_names) + all_names

    def sub(name, layer):
        return layer // 2 if name in even_names or name in odd_names else layer

    split_small = [(c_q_norm_g, m_c_q_norm_g, v_c_q_norm_g, 1), (c_kv_norm_g, m_c_kv_norm_g, v_c_kv_norm_g, 1), (b_conv_w, m_b_conv_w, v_b_conv_w, 2)]
    shard_shapes = [s[0].shape for s in split_small] * 3
    shard_axes = [s[3] for s in split_small] * 3
    gathered = _all_gather_small(_pack([s[0] for s in split_small] + [s[1] for s in split_small] + [s[2] for s in split_small]))
    whole = _unpack_gathered(gathered, shard_shapes, shard_axes)
    q_norm_full, kv_norm_full, conv_full = whole[0:3]
    small_w = [norm_mix_g, norm_mem_q_g, norm_mem_kv_g, norm_ffn_g, final_norm_g, a_v_norm_g, a_w_s, a_b_s] + whole[0:3]
    small_m = [m_norm_mix_g, m_norm_mem_q_g, m_norm_mem_kv_g, m_norm_ffn_g, m_final_norm_g, m_a_v_norm_g, m_a_w_s, m_a_b_s] + whole[3:6]
    small_v = [v_norm_mix_g, v_norm_mem_q_g, v_norm_mem_kv_g, v_norm_ffn_g, v_final_norm_g, v_a_v_norm_g, v_a_w_s, v_a_b_s] + whole[6:9]

    inv = ROPE_THETA ** (-jnp.arange(0, C_ROPE, 2, dtype=F32) / C_ROPE)
    ang = positions.astype(F32).reshape(t, 1) * inv
    cos, sin, zero = jnp.cos(ang), jnp.sin(ang), jnp.zeros((t, C_ROPE // 2), F32)
    rc = jnp.concatenate([cos, cos, zero, zero], axis=1)
    rs1 = jnp.concatenate([-sin, zero, zero, zero], axis=1)
    rs2 = jnp.concatenate([zero, sin, zero, zero], axis=1)

    my_x, my_y, my_c = _me()
    my_chip = 2 * my_x + my_y
    dev_arr = jnp.stack([_dev_index(my_x, my_y, my_c)]).astype(jnp.int32)
    core_arr = jnp.stack([my_c]).astype(jnp.int32)
    chip_arr = jnp.stack([my_chip, my_chip ^ 1, my_chip ^ 2, my_chip ^ 3]).astype(jnp.int32)

    n_stage = 3 * depth
    chain = [None]
    gat = {}
    issued = [0]

    def stage_names(i):
        layer, gid = divmod(i, 3)
        return layer, [even_names if layer % 2 == 0 else odd_names, all_names[:4], all_names[4:]][gid]

    def order(after):
        return [a for a in (chain[0], after) if a is not None]

    def gather_ici_start(i, after):
        layer, names = stage_names(i)
        placed = [_cast_place(big[nm][0], sub(nm, layer), big[nm][3], dev_arr, name="cast_place") for nm in names]
        axes = [big[nm][3] for nm in names]
        sizes = [p.shape[a] // N_DEV for p, a in zip(placed, axes)]
        st = dict(names=names, ici=_plan_gather_ici(axes, sizes), d2d=_plan_gather_d2d(axes, sizes))
        st["sems"], st["arrays"], chain[0] = _exchange_start(placed, N_PEER_CHIPS, st["ici"], order(after), name=f"gather_ici_start_s{i}")
        gat[i] = st

    def gather_stage(i, after):
        while issued[0] < min(i + 3, n_stage):
            gather_ici_start(issued[0], after)
            issued[0] += 1
        ahead = [i, i + 1] if i >= 3 and i + 1 < n_stage else [i]
        for j in ahead:
            st = gat[j]
            if "landed" not in st:
                st["arrays"], chain[0] = _exchange_wait(st["sems"], st["arrays"], st["ici"], order(after), name=f"gather_ici_wait_s{j}")
                st["landed"] = True
            if "sems2" not in st:
                st["sems2"], st["arrays"], chain[0] = _exchange_start(st["arrays"], 1, st["d2d"], order(after), name=f"gather_d2d_start_s{j}")
        st = gat[i]
        st["arrays"], chain[0] = _exchange_wait(st["sems2"], st["arrays"], st["d2d"], order(after), name=f"gather_d2d_wait_s{i}")
        return dict(zip(st["names"], st["arrays"]))

    wts = []

    def group_cols(w_in):
        return w_in.reshape(d, 5, groups, LANES).transpose(0, 2, 1, 3).reshape(d, 5 * a_width)

    def ungroup_cols(g_in):
        return g_in.reshape(d, groups, 5, LANES).transpose(0, 2, 1, 3).reshape(d, 5 * a_width)

    def group_rows(w_out):
        return w_out.reshape(2, groups, LANES, d).transpose(1, 0, 2, 3).reshape(2 * a_width, d)

    def ungroup_rows(g_out):
        return g_out.reshape(groups, 2, LANES, d).transpose(1, 0, 2, 3).reshape(2 * a_width, d)

    def pad_heads(w_uq):
        w3 = w_uq.reshape(q_rank, heads, C_NOPE + C_ROPE)
        return jnp.pad(w3, ((0, 0), (0, 0), (0, LANES - C_ROPE))).reshape(q_rank, heads * 2 * LANES)

    def unpad_heads(g_uq):
        return g_uq.reshape(q_rank, heads, 2 * LANES)[:, :, :C_NOPE + C_ROPE].reshape(q_rank, heads * (C_NOPE + C_ROPE))

    x2 = x.reshape(t, d)
    mem2 = mem.reshape(bl * n_mem, d)
    saved = []

    for layer in range(depth):
        w = gather_stage(3 * layer, x2 if layer else gathered)
        wts.append(w)
        sv = {"x0": x2}
        xn = _rmsnorm_fwd(x2, norm_mix_g[layer], name="norm_mix")
        sv["xn"] = xn
        if layer % 2 == 0:
            e = layer // 2
            w_in_g = group_cols(w["ab_w_in"])
            w_out_g = group_rows(w["ab_w_out"])
            z = _mm(xn, w_in_g, out_dtypes=(F32,), extras=[(chain[0], "after")], name="ab_in")
            ws_e = a_w_s[e]
            bs_e = a_b_s[e].reshape(groups, GMLP_BLOCK, 1)
            gn_e = a_v_norm_g[e].reshape(groups, 1, LANES)
            cw_e = conv_full[e].reshape(3, groups, LANES).transpose(1, 0, 2)
            ycat = _gmlp_conv_fwd(z, ws_e, bs_e, gn_e, cw_e, seq=seq, name="gmlp_conv_fwd")
            x2 = _mm(ycat, w_out_g, out_dtypes=(F32,), extras=[(x2, "mn")], epilogue=_epi_add, name="ab_out")
            sv.update(z=z, ycat=ycat, w_in_g=w_in_g, w_out_g=w_out_g, ws=ws_e, bs=bs_e, gn=gn_e, cw=cw_e)
        else:
            o = layer // 2
            w_uq_p = pad_heads(w["c_w_uq"])
            z = _mm(xn, w["c_w_in"], out_dtypes=(F32,), extras=[(chain[0], "after")], name="c_in")
            qn = _rmsnorm_fwd(z, q_norm_full[o], width=q_rank, col=0, name="norm_cq")
            kvn = _rmsnorm_fwd(z, kv_norm_full[o], width=kv_rank, col=q_rank // kv_rank, name="norm_ckv")
            qh = _mm(qn, w_uq_p, extras=[(rc, "m"), (rs1, "m"), (rs2, "m")], epilogue=_epi_rope_heads, tn=1024, name="c_uq")
            kvh = _mm(kvn, w["c_w_ukv"], name="c_ukv")
            k_rope = jnp.pad(z[:, q_rank + kv_rank:], ((0, 0), (0, LANES - C_ROPE)))
            kr = _rope_rows(k_rope, rc, rs1, rs2, transpose=False, out_dtype=BF16, name="rope_k")
            att, lse = _mla_fwd(qh, kvh, kr, seq=seq, heads=heads, name="mla_fwd")
            x2 = _mm(att, w["c_w_out"], out_dtypes=(F32,), extras=[(x2, "mn")], epilogue=_epi_add, name="c_out")
            sv.update(z=z, qn=qn, kvn=kvn, qh=qh, kvh=kvh, kr=kr, att=att, lse=lse, w_uq_p=w_uq_p)
        sv["x1"] = x2
        w.update(gather_stage(3 * layer + 1, x2))
        xq = _rmsnorm_fwd(x2, norm_mem_q_g[layer], name="norm_mem_q")
        memn = _rmsnorm_fwd(mem2, norm_mem_kv_g[layer], name="norm_mem_kv")
        mq = _mm(xq, w["m_wq"], extras=[(chain[0], "after")], name="m_q")
        mk = _mm(memn, w["m_wk"], name="m_k")
        mv = _mm(memn, w["m_wv"], name="m_v")
        mo = _mem_fwd(mq, mk, mv, seq=seq, n_mem=n_mem, name="mem_fwd")
        x2 = _mm(mo, w["m_wo"], out_dtypes=(F32,), extras=[(x2, "mn")], epilogue=_epi_add, name="m_o")
        sv.update(xq=xq, memn=memn, mq=mq, mk=mk, mv=mv, mo=mo)
        sv["x2"] = x2
        w.update(gather_stage(3 * layer + 2, x2))
        xf = _rmsnorm_fwd(x2, norm_ffn_g[layer], name="norm_ffn")
        hpre, hact = _mm(xf, w["f_w1"], out_dtypes=(BF16, BF16), extras=[(chain[0], "after")], epilogue=_epi_relu2, name="f_1")
        x2 = _mm(hact, w["f_w2"], out_dtypes=(F32,), extras=[(x2, "mn")], epilogue=_epi_add, name="f_2")
        sv.update(xf=xf, hpre=hpre, hact=hact)
        saved.append(sv)

    loss_part, dx, dg_final = _final_loss(x2, final_norm_g, loss_target.reshape(t, d), name="final_loss")

    dg_mix, dg_mq, dg_mkv, dg_ffn = [None] * depth, [None] * depth, [None] * depth, [None] * depth
    d_av, d_ws, d_bs = [None] * n_even, [None] * n_even, [None] * n_even
    d_conv, d_qn, d_kvn = [None] * n_even, [None] * n_odd, [None] * n_odd
    parts = {nm: [None] * big[nm][0].shape[0] for nm in big}

    sca = []

    def scatter_finish_d2d(st, after):
        n = len(st["names"])
        arrays, chain[0] = _exchange_wait(st["sems"], st["arrays"], st["d2d"], order(after), name=f"scatter_d2d_wait_s{st['id']}")
        sums = [_add_pair(g, got, ax, core_arr, name="add_pair") for g, got, ax in zip(arrays[:n], arrays[n:], st["axes"])]
        lands = [pltpu.with_memory_space_constraint(lax.empty(s_.shape, s_.dtype), pltpu.HBM) for s_ in sums]
        st["sems2"], st["arrays"], chain[0] = _exchange_start(sums + lands, N_PEER_CHIPS, st["ici"], order(after), name=f"scatter_ici_start_s{st['id']}")

    def scatter_stage(layer, names, grads, after):
        if sca and "sems2" not in sca[-1]:
            scatter_finish_d2d(sca[-1], after)
        axes = [big[nm][3] for nm in names]
        arrs = [grads[nm] for nm in names]
        sizes8 = [g.shape[a] // N_DEV for g, a in zip(arrs, axes)]
        gots = []
        for g, a in zip(arrs, axes):
            sh = list(g.shape)
            sh[a] //= 2
            gots.append(pltpu.with_memory_space_constraint(lax.empty(tuple(sh), g.dtype), pltpu.HBM))
        st = dict(id=len(sca), layer=layer, names=names, axes=axes, d2d=_plan_scatter_d2d(axes, sizes8),
                  ici=_plan_scatter_ici(axes, sizes8))
        st["sems"], st["arrays"], chain[0] = _exchange_start(arrs + gots, 1, st["d2d"], order(after), name=f"scatter_d2d_start_s{st['id']}")
        sca.append(st)

    def scatter_collect(st, after):
        n = len(st["names"])
        arrays, chain[0] = _exchange_wait(st["sems2"], st["arrays"], st["ici"], order(after), name=f"scatter_ici_wait_s{st['id']}")
        for nm, s_, r in zip(st["names"], arrays[:n], arrays[n:]):
            parts[nm][sub(nm, st["layer"])] = (s_, r)

    for layer in reversed(range(depth)):
        w, sv = wts[layer], saved[layer]
        grads = {}
        dh = _mm(dx, w["f_w2"], tb=True, extras=[(sv["hpre"], "mn"), (chain[0], "after")], epilogue=_epi_drelu2, name="f_2_dx")
        grads["f_w2"] = _mm(sv["hact"], dx, ta=True, name="f_2_dw")
        grads["f_w1"] = _mm(sv["xf"], dh, ta=True, name="f_1_dw")
        dxf = _mm(dh, w["f_w1"], tb=True, name="f_1_dx")
        dx, dg_ffn[layer] = _rmsnorm_bwd(sv["x2"], norm_ffn_g[layer], dxf, dx, name="norm_ffn_bwd")
        scatter_stage(layer, all_names[4:], grads, dx)
        dmo = _mm(dx, w["m_wo"], tb=True, extras=[(chain[0], "after")], name="m_o_dx")
        grads["m_wo"] = _mm(sv["mo"], dx, ta=True, name="m_o_dw")
        dmq, dmk, dmv = _mem_bwd(sv["mq"], sv["mk"], sv["mv"], dmo, seq=seq, n_mem=n_mem, name="mem_bwd")
        grads["m_wq"] = _mm(sv["xq"], dmq, ta=True, name="m_q_dw")
        grads["m_wk"] = _mm(sv["memn"], dmk, ta=True, name="m_k_dw")
        grads["m_wv"] = _mm(sv["memn"], dmv, ta=True, name="m_v_dw")
        dxq = _mm(dmq, w["m_wq"], tb=True, name="m_q_dx")
        dmemn_k = _mm(dmk, w["m_wk"], tb=True, out_dtypes=(F32,), name="m_k_dx")
        dmemn = _mm(dmv, w["m_wv"], tb=True, out_dtypes=(F32,), extras=[(dmemn_k, "mn")], epilogue=_epi_add, name="m_v_dx")
        _, dg_mkv[layer] = _rmsnorm_bwd(mem2, norm_mem_kv_g[layer], dmemn, name="norm_mem_kv_bwd")
        dx, dg_mq[layer] = _rmsnorm_bwd(sv["x1"], norm_mem_q_g[layer], dxq, dx, name="norm_mem_q_bwd")
        scatter_stage(layer, all_names[:4], grads, dx)
        if layer % 2 == 0:
            e = layer // 2
            dycat = _mm(dx, sv["w_out_g"], tb=True, extras=[(chain[0], "after")], name="ab_out_dx")
            grads["ab_w_out"] = ungroup_rows(_mm(sv["ycat"], dx, ta=True, name="ab_out_dw"))
            dz, dws, dbs, dgn, dcw = _gmlp_conv_bwd(sv["z"], dycat, sv["ws"], sv["bs"], sv["gn"], sv["cw"], seq=seq, name="gmlp_conv_bwd")
            d_ws[e], d_bs[e], d_av[e] = dws, dbs.reshape(groups, GMLP_BLOCK), dgn.reshape(a_width)
            d_conv[e] = dcw.transpose(1, 0, 2).reshape(3, a_width)
            grads["ab_w_in"] = ungroup_cols(_mm(sv["xn"], dz, ta=True, name="ab_in_dw"))
            dxn = _mm(dz, sv["w_in_g"], tb=True, name="ab_in_dx")
        else:
            o = layer // 2
            datt = _mm(dx, w["c_w_out"], tb=True, extras=[(chain[0], "after")], name="c_out_dx")
            grads["c_w_out"] = _mm(sv["att"], dx, ta=True, name="c_out_dw")
            dq_pre, dkvh, dkr = _mla_bwd(sv["qh"], sv["kvh"], sv["kr"], datt, sv["att"], sv["lse"], rc, rs1, rs2, seq=seq, heads=heads, name="mla_bwd")
            dk_rope = _rope_rows(dkr, rc, rs1, rs2, transpose=True, out_dtype=F32, name="rope_k_bwd")
            grads["c_w_uq"] = unpad_heads(_mm(sv["qn"], dq_pre, ta=True, name="c_uq_dw"))
            grads["c_w_ukv"] = _mm(sv["kvn"], dkvh, ta=True, name="c_ukv_dw")
            dqn = _mm(dq_pre, sv["w_uq_p"], tb=True, name="c_uq_dx")
            dkvn = _mm(dkvh, w["c_w_ukv"], tb=True, name="c_ukv_dx")
            dcq, d_qn[o] = _rmsnorm_bwd(sv["z"], q_norm_full[o], dqn, width=q_rank, col=0, name="norm_cq_bwd")
            dckv, d_kvn[o] = _rmsnorm_bwd(sv["z"], kv_norm_full[o], dkvn, width=kv_rank, col=q_rank // kv_rank, name="norm_ckv_bwd")
            dz = jnp.concatenate([dcq, dckv, dk_rope[:, :C_ROPE]], axis=1)
            grads["c_w_in"] = _mm(sv["xn"], dz, ta=True, name="c_in_dw")
            dxn = _mm(dz, w["c_w_in"], tb=True, name="c_in_dx")
        dx, dg_mix[layer] = _rmsnorm_bwd(sv["x0"], norm_mix_g[layer], dxn, dx, name="norm_mix_bwd")
        scatter_stage(layer, even_names if layer % 2 == 0 else odd_names, grads, dx)

    grad_x = dx.reshape(bl, seq, d)

    scatter_finish_d2d(sca[-1], dx)
    for st in sca[:-3]:
        scatter_collect(st, dx)
    big_out = {}
    done = chain[0]
    for nm, (wv, mv_, vv, ax) in big.items():
        n_sub = wv.shape[0]
        first = 0 if nm in odd_names else 1
        if first < n_sub:
            big_out[nm] = _adamw(wv, mv_, vv, parts[nm], ax, chip_arr, (first, n_sub - first), None, done, name=f"adamw_{nm}")
            done = big_out[nm][0]
    for st in sca[-3:]:
        scatter_collect(st, done)
    for nm, (wv, mv_, vv, ax) in big.items():
        if nm not in odd_names:
            big_out[nm] = _adamw(wv, mv_, vv, parts[nm], ax, chip_arr, (0, 1), big_out.get(nm), chain[0], name=f"adamw_{nm}_first")

    small_g = [jnp.concatenate(dg_mix), jnp.concatenate(dg_mq), jnp.concatenate(dg_mkv), jnp.concatenate(dg_ffn), dg_final.reshape(d),
               jnp.stack(d_av), jnp.stack(d_ws), jnp.stack(d_bs), jnp.concatenate(d_qn), jnp.concatenate(d_kvn), jnp.stack(d_conv)]
    small_g.append(loss_part.reshape(1))
    spare = jnp.zeros((1,), F32)
    small_shapes = [a.shape for a in small_w] + [(1,)]
    g_all = _all_gather_small(_pack(small_g) + chain[0][0, 0])
    sg, sd, sm, svv = _adamw_small(_pack(small_w + [spare]), _pack(small_m + [spare]), _pack(small_v + [spare]), g_all, name="adamw_small")
    small_names = ["norm_mix_g", "norm_mem_q_g", "norm_mem_kv_g", "norm_ffn_g", "final_norm_g", "a_v_norm_g", "a_w_s", "a_b_s", "c_q_norm_g", "c_kv_norm_g", "b_conv_w"]
    small_out = {}
    unpacked = [_unpack(p, small_shapes) for p in (sg, sd, sm, svv)]
    for i, nm in enumerate(small_names):
        vals = [u[i] for u in unpacked]
        if i >= 8:
            full_w, _, _, ax = split_small[i - 8]
            vals = [_my_block(val, ax, full_w.shape[ax]) for val in vals]
        small_out[nm] = vals

    order = ["norm_mix_g", "norm_mem_q_g", "norm_mem_kv_g", "norm_ffn_g", "final_norm_g", "ab_w_in", "a_v_norm_g", "a_w_s", "a_b_s", "b_conv_w", "ab_w_out",
             "c_w_in", "c_q_norm_g", "c_kv_norm_g", "c_w_uq", "c_w_ukv", "c_w_out", "m_wq", "m_wk", "m_wv", "m_wo", "f_w1", "f_w2"]
    loss = unpacked[0][-1][0]
    res = [loss, grad_x]
    for kind in range(4):
        for nm in order:
            res.append(big_out[nm][kind] if nm in big_out else small_out[nm][kind])
    return tuple(res)
```

```python
import functools
import math

import jax
import jax.numpy as jnp
from jax import lax
from jax.experimental import pallas as pl
from jax.experimental.pallas import tpu as pltpu

F32 = jnp.float32
BF16 = jnp.bfloat16
MESH = pl.DeviceIdType.MESH
N_DEV = 8
LANES = 128

RMS_EPS = 1e-6
CHUNK = 64
GMLP_BLOCK = 128
C_NOPE = 128
C_ROPE = 64
C_V = 128
ROPE_THETA = 10000.0
MEM_HEADS = 4
ADAM_LR = 0.001
ADAM_B1 = 0.9
ADAM_B2 = 0.999
ADAM_EPS = 1e-08
ADAM_WD = 0.01
ADAM_STEP = 10
NEG_BIG = -1e30
VMEM_LIMIT = 56 * 1024 * 1024


def _cp(*sem):
    return pltpu.CompilerParams(dimension_semantics=sem, vmem_limit_bytes=VMEM_LIMIT)


def _tile(n, pref):
    t = min(n, pref)
    while n % t and t > LANES:
        t //= 2
    return n if n % t else t


def _mm(a, b, *, ta=False, tb=False, out_dtypes=(BF16,), extras=(), epilogue=None, tm=1024, tn=1024, tk=2048, name):
    m, k = (a.shape[1], a.shape[0]) if ta else a.shape
    n = b.shape[0] if tb else b.shape[1]
    assert (b.shape[1] if tb else b.shape[0]) == k, (a.shape, b.shape, ta, tb)
    tm, tn, tk = _tile(m, tm), _tile(n, tn), _tile(k, tk)
    nk = k // tk
    ne, no = len(extras), len(out_dtypes)
    dims = (((0 if ta else 1,), (1 if tb else 0,)), ((), ()))

    def body(*refs):
        a_ref, b_ref = refs[0], refs[1]
        e_refs = refs[2:2 + ne]
        o_refs = refs[2 + ne:2 + ne + no]

        def finish(acc):
            vals = [e[...] for e, (_, kind) in zip(e_refs, extras) if kind != "after"]
            outs = (acc,) if epilogue is None else epilogue(acc, *vals)
            for o, val in zip(o_refs, outs):
                o[...] = val.astype(o.dtype)

        prod = lax.dot_general(a_ref[...].astype(BF16), b_ref[...].astype(BF16), dims, preferred_element_type=F32)
        if nk == 1:
            finish(prod)
        else:
            acc_ref = refs[-1]
            kk = pl.program_id(2)

            @pl.when(kk == 0)
            def _():
                acc_ref[...] = prod

            @pl.when((kk > 0) & (kk < nk - 1))
            def _():
                acc_ref[...] += prod

            @pl.when(kk == nk - 1)
            def _():
                finish(acc_ref[...] + prod)

    a_spec = pl.BlockSpec((tk, tm), lambda i, j, kk: (kk, i)) if ta else pl.BlockSpec((tm, tk), lambda i, j, kk: (i, kk))
    b_spec = pl.BlockSpec((tn, tk), lambda i, j, kk: (j, kk)) if tb else pl.BlockSpec((tk, tn), lambda i, j, kk: (kk, j))
    e_specs = []
    for arr, kind in extras:
        if kind == "mn":
            e_specs.append(pl.BlockSpec((tm, tn), lambda i, j, kk: (i, j)))
        elif kind == "after":
            e_specs.append(pl.BlockSpec(arr.shape, lambda i, j, kk: (0, 0)))
        else:
            e_specs.append(pl.BlockSpec((tm, arr.shape[1]), lambda i, j, kk: (i, 0)))
    outs = pl.pallas_call(
        body,
        out_shape=tuple(jax.ShapeDtypeStruct((m, n), d) for d in out_dtypes),
        grid=(m // tm, n // tn, nk),
        in_specs=[a_spec, b_spec] + e_specs,
        out_specs=tuple(pl.BlockSpec((tm, tn), lambda i, j, kk: (i, j)) for _ in out_dtypes),
        scratch_shapes=[pltpu.VMEM((tm, tn), F32)] if nk > 1 else [],
        compiler_params=_cp("parallel", "parallel", "arbitrary"),
        name=name,
    )(a, b, *[arr for arr, _ in extras])
    return outs[0] if no == 1 else outs


def _epi_add(acc, r):
    return (acc + r,)


def _epi_relu2(acc):
    h = jnp.maximum(acc, 0.0)
    return (acc, h * h)


def _epi_drelu2(acc, h):
    return (acc * (2.0 * jnp.maximum(h.astype(F32), 0.0)),)


def _rope(x, c, s1, s2):
    return x * c + pltpu.roll(x, LANES - C_ROPE // 2, 1) * s1 + pltpu.roll(x, C_ROPE // 2, 1) * s2


def _rope_t(dy, c, s1, s2):
    return dy * c + pltpu.roll(dy * s1, C_ROPE // 2, 1) + pltpu.roll(dy * s2, LANES - C_ROPE // 2, 1)


def _epi_rope_heads(acc, c, s1, s2):
    parts = []
    for h0 in range(0, acc.shape[1], 2 * LANES):
        parts.append(acc[:, h0:h0 + LANES])
        parts.append(_rope(acc[:, h0 + LANES:h0 + 2 * LANES], c, s1, s2))
    return (jnp.concatenate(parts, axis=1),)


def _rmsnorm_fwd(x, g, *, width=None, col=0, name):
    t = x.shape[0]
    w = x.shape[1] if width is None else width
    tm = _tile(t, 512)

    def body(x_ref, g_ref, o_ref):
        xv = x_ref[...].astype(F32)
        r = lax.rsqrt(jnp.mean(xv * xv, axis=-1, keepdims=True) + RMS_EPS)
        o_ref[...] = (xv * r * g_ref[...]).astype(o_ref.dtype)

    return pl.pallas_call(
        body,
        out_shape=jax.ShapeDtypeStruct((t, w), BF16),
        grid=(t // tm,),
        in_specs=[pl.BlockSpec((tm, w), lambda i: (i, col)), pl.BlockSpec((1, w), lambda i: (0, 0))],
        out_specs=pl.BlockSpec((tm, w), lambda i: (i, 0)),
        compiler_params=_cp("parallel"),
        name=name,
    )(x, g.reshape(1, w))


def _rmsnorm_bwd(x, g, dy, res=None, *, width=None, col=0, name):
    t = x.shape[0]
    w = x.shape[1] if width is None else width
    tm = _tile(t, 256)
    has_res = res is not None

    def body(*refs):
        x_ref, g_ref, dy_ref = refs[:3]
        res_ref = refs[3] if has_res else None
        dx_ref, dg_ref = refs[-2], refs[-1]
        xv = x_ref[...].astype(F32)
        r = lax.rsqrt(jnp.mean(xv * xv, axis=-1, keepdims=True) + RMS_EPS)
        xh = xv * r
        dyv = dy_ref[...].astype(F32)
        dxh = dyv * g_ref[...]
        dx = r * (dxh - xh * jnp.mean(dxh * xh, axis=-1, keepdims=True))
        if has_res:
            dx = dx + res_ref[...]
        dx_ref[...] = dx

        @pl.when(pl.program_id(0) == 0)
        def _():
            dg_ref[...] = jnp.zeros_like(dg_ref)

        dg_ref[...] += jnp.sum(dyv * xh, axis=0, keepdims=True)

    in_specs = [pl.BlockSpec((tm, w), lambda i: (i, col)), pl.BlockSpec((1, w), lambda i: (0, 0)), pl.BlockSpec((tm, w), lambda i: (i, 0))]
    args = [x, g.reshape(1, w), dy]
    if has_res:
        in_specs.append(pl.BlockSpec((tm, w), lambda i: (i, 0)))
        args.append(res)
    return pl.pallas_call(
        body,
        out_shape=(jax.ShapeDtypeStruct((t, w), F32), jax.ShapeDtypeStruct((1, w), F32)),
        grid=(t // tm,),
        in_specs=in_specs,
        out_specs=(pl.BlockSpec((tm, w), lambda i: (i, 0)), pl.BlockSpec((1, w), lambda i: (0, 0))),
        compiler_params=_cp("arbitrary"),
        name=name,
    )(*args)


def _final_loss(x, g, target, *, name):
    t, d = x.shape
    tm = _tile(t, 256)

    def body(x_ref, g_ref, t_ref, loss_ref, dx_ref, dg_ref):
        xv = x_ref[...]
        r = lax.rsqrt(jnp.mean(xv * xv, axis=-1, keepdims=True) + RMS_EPS)
        xh = xv * r
        gv = g_ref[...]
        err = xh * gv - t_ref[...]
        dyv = err * (1.0 / d)
        dxh = dyv * gv
        dx_ref[...] = r * (dxh - xh * jnp.mean(dxh * xh, axis=-1, keepdims=True))

        @pl.when(pl.program_id(0) == 0)
        def _():
            dg_ref[...] = jnp.zeros_like(dg_ref)
            loss_ref[...] = jnp.zeros_like(loss_ref)

        dg_ref[...] += jnp.sum(dyv * xh, axis=0, keepdims=True)
        row = jnp.sum(err * err, axis=-1, keepdims=True) * (0.5 / d)
        loss_ref[...] += jnp.sum(row, axis=0, keepdims=True)

    return pl.pallas_call(
        body,
        out_shape=(jax.ShapeDtypeStruct((1, 1), F32), jax.ShapeDtypeStruct((t, d), F32), jax.ShapeDtypeStruct((1, d), F32)),
        grid=(t // tm,),
        in_specs=[pl.BlockSpec((tm, d), lambda i: (i, 0)), pl.BlockSpec((1, d), lambda i: (0, 0)), pl.BlockSpec((tm, d), lambda i: (i, 0))],
        out_specs=(pl.BlockSpec((1, 1), lambda i: (0, 0)), pl.BlockSpec((tm, d), lambda i: (i, 0)), pl.BlockSpec((1, d), lambda i: (0, 0))),
        compiler_params=_cp("arbitrary"),
        name=name,
    )(x, g.reshape(1, d), target)


_GELU_C = math.sqrt(2.0 / math.pi)
_GELU_A = 0.044715


def _gelu(x):
    t = jnp.tanh(_GELU_C * (x + _GELU_A * x * x * x))
    return 0.5 * x * (1.0 + t), t


def _dgelu(x, t):
    return 0.5 * (1.0 + t) + 0.5 * x * (1.0 - t * t) * (_GELU_C * (1.0 + 3.0 * _GELU_A * x * x))


def _spatial_mask():
    ci = lax.broadcasted_iota(jnp.int32, (GMLP_BLOCK, GMLP_BLOCK), 0) // CHUNK
    cj = lax.broadcasted_iota(jnp.int32, (GMLP_BLOCK, GMLP_BLOCK), 1) // CHUNK
    return (cj <= ci).astype(F32)


def _shift_down(x, k):
    rows = lax.broadcasted_iota(jnp.int32, x.shape, 0)
    return jnp.where(rows >= k, pltpu.roll(x, k, 0), 0.0)


def _shift_up(x, k):
    n = x.shape[0]
    rows = lax.broadcasted_iota(jnp.int32, x.shape, 0)
    return jnp.where(rows < n - k, pltpu.roll(x, n - k, 0), 0.0)


def _gmlp_conv_fwd(z, ws, bs, gn, cw, *, seq, name):
    t = z.shape[0]
    g_n = ws.shape[0]
    nb = seq // GMLP_BLOCK
    L = LANES

    def body(z_ref, ws_ref, bs_ref, gn_ref, cw_ref, y_ref):
        u = z_ref[:, 0:L]
        v = z_ref[:, L:2 * L]
        gu, _ = _gelu(u)
        gv, _ = _gelu(v)
        r = lax.rsqrt(jnp.mean(gv * gv, axis=-1, keepdims=True) + RMS_EPS)
        vg = (gv * r * gn_ref[0]).astype(BF16)
        wm = (ws_ref[0] * _spatial_mask()).astype(BF16)
        bias = bs_ref[0]
        for n in range(nb):
            sl = slice(n * GMLP_BLOCK, (n + 1) * GMLP_BLOCK)
            mixed = jnp.dot(wm, vg[sl], preferred_element_type=F32) + bias
            y_ref[sl, 0:L] = (gu[sl] * mixed).astype(y_ref.dtype)
        bg = z_ref[:, 2 * L:3 * L]
        zc = z_ref[:, 3 * L:4 * L] * z_ref[:, 4 * L:5 * L]
        w0, w1, w2 = cw_ref[0, 0:1, :], cw_ref[0, 1:2, :], cw_ref[0, 2:3, :]
        conv = w0 * _shift_down(zc, 2) + w1 * _shift_down(zc, 1) + w2 * zc
        y_ref[:, L:2 * L] = (bg * conv).astype(y_ref.dtype)

    return pl.pallas_call(
        body,
        out_shape=jax.ShapeDtypeStruct((t, 2 * L * g_n), BF16),
        grid=(g_n, t // seq),
        in_specs=[
            pl.BlockSpec((seq, 5 * L), lambda g, b: (b, g)),
            pl.BlockSpec((1, GMLP_BLOCK, GMLP_BLOCK), lambda g, b: (g, 0, 0)),
            pl.BlockSpec((1, GMLP_BLOCK, 1), lambda g, b: (g, 0, 0)),
            pl.BlockSpec((1, 1, L), lambda g, b: (g, 0, 0)),
            pl.BlockSpec((1, 3, L), lambda g, b: (g, 0, 0)),
        ],
        out_specs=pl.BlockSpec((seq, 2 * L), lambda g, b: (b, g)),
        compiler_params=_cp("parallel", "parallel"),
        name=name,
    )(z, ws, bs, gn, cw)


def _gmlp_conv_bwd(z, dy, ws, bs, gn, cw, *, seq, name):
    t = z.shape[0]
    g_n = ws.shape[0]
    nb = seq // GMLP_BLOCK
    L = LANES
    dims_nt = (((1,), (1,)), ((), ()))
    dims_tn = (((0,), (0,)), ((), ()))

    def body(z_ref, dy_ref, ws_ref, bs_ref, gn_ref, cw_ref, dz_ref, dws_ref, dbs_ref, dgn_ref, dcw_ref, dvg_ref):
        @pl.when(pl.program_id(1) == 0)
        def _():
            dws_ref[...] = jnp.zeros_like(dws_ref)
            dbs_ref[...] = jnp.zeros_like(dbs_ref)
            dgn_ref[...] = jnp.zeros_like(dgn_ref)
            dcw_ref[...] = jnp.zeros_like(dcw_ref)

        u = z_ref[:, 0:L]
        v = z_ref[:, L:2 * L]
        gu, tu = _gelu(u)
        gv, tv = _gelu(v)
        r = lax.rsqrt(jnp.mean(gv * gv, axis=-1, keepdims=True) + RMS_EPS)
        gvh = gv * r
        gnv = gn_ref[0]
        vg = (gvh * gnv).astype(BF16)
        mask = _spatial_mask()
        wm = (ws_ref[0] * mask).astype(BF16)
        bias = bs_ref[0]
        dya = dy_ref[:, 0:L].astype(F32)
        dws = jnp.zeros((GMLP_BLOCK, GMLP_BLOCK), F32)
        dbs = jnp.zeros((GMLP_BLOCK, 1), F32)
        for n in range(nb):
            sl = slice(n * GMLP_BLOCK, (n + 1) * GMLP_BLOCK)
            mixed = jnp.dot(wm, vg[sl], preferred_element_type=F32) + bias
            dmixed = dya[sl] * gu[sl]
            dgu = dya[sl] * mixed
            dz_ref[sl, 0:L] = (dgu * _dgelu(u[sl], tu[sl])).astype(dz_ref.dtype)
            dmb = dmixed.astype(BF16)
            dws = dws + lax.dot_general(dmb, vg[sl], dims_nt, preferred_element_type=F32)
            dbs = dbs + jnp.sum(dmixed, axis=1, keepdims=True)
            dvg_ref[sl, :] = lax.dot_general(wm, dmb, dims_tn, preferred_element_type=F32)
        dws_ref[0] += dws * mask
        dbs_ref[0] += dbs
        dvg = dvg_ref[...]
        dgn_ref[0] += jnp.sum(dvg * gvh, axis=0, keepdims=True)
        dvh = dvg * gnv
        dgv = r * (dvh - gvh * jnp.mean(dvh * gvh, axis=-1, keepdims=True))
        dz_ref[:, L:2 * L] = (dgv * _dgelu(v, tv)).astype(dz_ref.dtype)

        bg = z_ref[:, 2 * L:3 * L]
        cg = z_ref[:, 3 * L:4 * L]
        hh = z_ref[:, 4 * L:5 * L]
        zc = cg * hh
        w0, w1, w2 = cw_ref[0, 0:1, :], cw_ref[0, 1:2, :], cw_ref[0, 2:3, :]
        z1 = _shift_down(zc, 1)
        z2 = _shift_down(zc, 2)
        conv = w0 * z2 + w1 * z1 + w2 * zc
        dyb = dy_ref[:, L:2 * L].astype(F32)
        dconv = dyb * bg
        dz_ref[:, 2 * L:3 * L] = (dyb * conv).astype(dz_ref.dtype)
        dzc = w2 * dconv + w1 * _shift_up(dconv, 1) + w0 * _shift_up(dconv, 2)
        dz_ref[:, 3 * L:4 * L] = (dzc * hh).astype(dz_ref.dtype)
        dz_ref[:, 4 * L:5 * L] = (dzc * cg).astype(dz_ref.dtype)
        dcw_ref[0, 0:1, :] += jnp.sum(dconv * z2, axis=0, keepdims=True)
        dcw_ref[0, 1:2, :] += jnp.sum(dconv * z1, axis=0, keepdims=True)
        dcw_ref[0, 2:3, :] += jnp.sum(dconv * zc, axis=0, keepdims=True)

    return pl.pallas_call(
        body,
        out_shape=(
            jax.ShapeDtypeStruct((t, 5 * L * g_n), BF16),
            jax.ShapeDtypeStruct(ws.shape, F32),
            jax.ShapeDtypeStruct(bs.shape, F32),
            jax.ShapeDtypeStruct(gn.shape, F32),
            jax.ShapeDtypeStruct(cw.shape, F32),
        ),
        grid=(g_n, t // seq),
        in_specs=[
            pl.BlockSpec((seq, 5 * L), lambda g, b: (b, g)),
            pl.BlockSpec((seq, 2 * L), lambda g, b: (b, g)),
            pl.BlockSpec((1, GMLP_BLOCK, GMLP_BLOCK), lambda g, b: (g, 0, 0)),
            pl.BlockSpec((1, GMLP_BLOCK, 1), lambda g, b: (g, 0, 0)),
            pl.BlockSpec((1, 1, L), lambda g, b: (g, 0, 0)),
            pl.BlockSpec((1, 3, L), lambda g, b: (g, 0, 0)),
        ],
        out_specs=(
            pl.BlockSpec((seq, 5 * L), lambda g, b: (b, g)),
            pl.BlockSpec((1, GMLP_BLOCK, GMLP_BLOCK), lambda g, b: (g, 0, 0)),
            pl.BlockSpec((1, GMLP_BLOCK, 1), lambda g, b: (g, 0, 0)),
            pl.BlockSpec((1, 1, L), lambda g, b: (g, 0, 0)),
            pl.BlockSpec((1, 3, L), lambda g, b: (g, 0, 0)),
        ),
        scratch_shapes=[pltpu.VMEM((seq, L), F32)],
        compiler_params=_cp("parallel", "arbitrary"),
        name=name,
    )(z, dy, ws, bs, gn, cw)


def _rope_rows(x, c, s1, s2, *, transpose, out_dtype, name):
    t, w = x.shape
    tm = _tile(t, 512)
    only = w == LANES
    fn = _rope_t if transpose else _rope

    def body(x_ref, c_ref, s1_ref, s2_ref, o_ref):
        cv, s1v, s2v = c_ref[...], s1_ref[...], s2_ref[...]
        if only:
            o_ref[...] = fn(x_ref[...].astype(F32), cv, s1v, s2v).astype(o_ref.dtype)
        else:
            for h0 in range(0, w, 2 * LANES):
                o_ref[:, h0:h0 + LANES] = x_ref[:, h0:h0 + LANES].astype(o_ref.dtype)
                o_ref[:, h0 + LANES:h0 + 2 * LANES] = fn(x_ref[:, h0 + LANES:h0 + 2 * LANES].astype(F32), cv, s1v, s2v).astype(o_ref.dtype)

    tab = pl.BlockSpec((tm, LANES), lambda i: (i, 0))
    return pl.pallas_call(
        body,
        out_shape=jax.ShapeDtypeStruct((t, w), out_dtype),
        grid=(t // tm,),
        in_specs=[pl.BlockSpec((tm, w), lambda i: (i, 0)), tab, tab, tab],
        out_specs=pl.BlockSpec((tm, w), lambda i: (i, 0)),
        compiler_params=_cp("parallel"),
        name=name,
    )(x, c, s1, s2)


_NT = (((1,), (1,)), ((), ()))
_TN = (((0,), (0,)), ((), ()))


def _chunk_mask(tq):
    qc = lax.broadcasted_iota(jnp.int32, (tq, tq), 0) // CHUNK
    kc = lax.broadcasted_iota(jnp.int32, (tq, tq), 1) // CHUNK
    return kc <= qc


MLA_BLOCK = 256


def _mla_fwd(q, kv, kr, *, seq, heads, name):
    t = q.shape[0]
    tq = _tile(seq, MLA_BLOCK)
    nq = seq // tq
    scale = (C_NOPE + C_ROPE) ** -0.5

    def body(q_ref, kv_ref, kr_ref, o_ref, lse_ref, kcat_ref):
        i = pl.program_id(2)

        @pl.when(i == 0)
        def _():
            kcat_ref[:, :LANES] = kv_ref[:, :LANES]
            kcat_ref[:, LANES:] = kr_ref[...]

        qv = q_ref[...]

        def step(j, carry, diagonal):
            m_run, l_run, acc = carry
            rows = pl.ds(pl.multiple_of(j * tq, tq), tq)
            s = lax.dot_general(qv, kcat_ref[rows, :], _NT, preferred_element_type=F32) * scale
            if diagonal:
                s = jnp.where(_chunk_mask(tq), s, NEG_BIG)
            m_new = jnp.maximum(m_run, jnp.max(s, axis=-1, keepdims=True))
            alpha = jnp.exp(m_run - m_new)
            p = jnp.exp(s - m_new)
            l_new = alpha * l_run + jnp.sum(p, axis=-1, keepdims=True)
            acc = alpha * acc + jnp.dot(p.astype(BF16), kv_ref[rows, LANES:], preferred_element_type=F32)
            return m_new, l_new, acc

        init = (jnp.full((tq, 1), NEG_BIG, F32), jnp.zeros((tq, 1), F32), jnp.zeros((tq, C_V), F32))
        carry = lax.fori_loop(0, i, lambda j, c: step(j, c, False), init)
        m_run, l_run, acc = step(i, carry, True)
        o_ref[...] = (acc / l_run).astype(o_ref.dtype)
        lse_ref[...] = m_run + jnp.log(l_run)

    return pl.pallas_call(
        body,
        out_shape=(jax.ShapeDtypeStruct((t, heads * C_V), BF16), jax.ShapeDtypeStruct((heads, t, 1), F32)),
        grid=(t // seq, heads, nq),
        in_specs=[
            pl.BlockSpec((tq, 2 * LANES), lambda b, h, i: (b * nq + i, h)),
            pl.BlockSpec((seq, 2 * LANES), lambda b, h, i: (b, h)),
            pl.BlockSpec((seq, LANES), lambda b, h, i: (b, 0)),
        ],
        out_specs=(
            pl.BlockSpec((tq, C_V), lambda b, h, i: (b * nq + i, h)),
            pl.BlockSpec((None, tq, 1), lambda b, h, i: (h, b * nq + i, 0)),
        ),
        scratch_shapes=[pltpu.VMEM((seq, 2 * LANES), BF16)],
        compiler_params=_cp("parallel", "parallel", "arbitrary"),
        name=name,
    )(q, kv, kr)


def _mla_bwd(q, kv, kr, do, o, lse, rc, rs1, rs2, *, seq, heads, name):
    t = q.shape[0]
    tq = _tile(seq, MLA_BLOCK)
    nq = seq // tq
    scale = (C_NOPE + C_ROPE) ** -0.5

    def body(q_ref, kv_ref, kr_ref, do_ref, o_ref, lse_ref, rc_ref, rs1_ref, rs2_ref, dq_ref, dkv_ref, dkr_ref, kcat_ref, dk_ref, dv_ref):
        h, i = pl.program_id(1), pl.program_id(2)

        @pl.when(i == 0)
        def _():
            kcat_ref[:, :LANES] = kv_ref[:, :LANES]
            kcat_ref[:, LANES:] = kr_ref[...]
            dk_ref[...] = jnp.zeros_like(dk_ref)
            dv_ref[...] = jnp.zeros_like(dv_ref)

        @pl.when((i == 0) & (h == 0))
        def _():
            dkr_ref[...] = jnp.zeros_like(dkr_ref)

        qv = q_ref[...]
        dov = do_ref[...]
        lse_v = lse_ref[...]
        delta = jnp.sum(dov.astype(F32) * o_ref[...].astype(F32), axis=-1, keepdims=True)

        def step(j, dq, diagonal):
            rows = pl.ds(pl.multiple_of(j * tq, tq), tq)
            kj = kcat_ref[rows, :]
            s = lax.dot_general(qv, kj, _NT, preferred_element_type=F32) * scale
            if diagonal:
                s = jnp.where(_chunk_mask(tq), s, NEG_BIG)
            p = jnp.exp(s - lse_v)
            dp = lax.dot_general(dov, kv_ref[rows, LANES:], _NT, preferred_element_type=F32)
            ds = (p * (dp - delta) * scale).astype(BF16)
            dk_ref[rows, :] += lax.dot_general(ds, qv, _TN, preferred_element_type=F32)
            dv_ref[rows, :] += lax.dot_general(p.astype(BF16), dov, _TN, preferred_element_type=F32)
            return dq + jnp.dot(ds, kj, preferred_element_type=F32)

        dq = lax.fori_loop(0, i, lambda j, c: step(j, c, False), jnp.zeros((tq, 2 * LANES), F32))
        dq = step(i, dq, True)
        dq_ref[:, :LANES] = dq[:, :LANES].astype(dq_ref.dtype)
        dq_ref[:, LANES:] = _rope_t(dq[:, LANES:], rc_ref[...], rs1_ref[...], rs2_ref[...]).astype(dq_ref.dtype)

        @pl.when(i == nq - 1)
        def _():
            dkv_ref[:, :LANES] = dk_ref[:, :LANES].astype(dkv_ref.dtype)
            dkv_ref[:, LANES:] = dv_ref[...].astype(dkv_ref.dtype)
            dkr_ref[...] += dk_ref[:, LANES:]

    return pl.pallas_call(
        body,
        out_shape=(
            jax.ShapeDtypeStruct((t, heads * 2 * LANES), BF16),
            jax.ShapeDtypeStruct((t, heads * 2 * LANES), BF16),
            jax.ShapeDtypeStruct((t, LANES), F32),
        ),
        grid=(t // seq, heads, nq),
        in_specs=[
            pl.BlockSpec((tq, 2 * LANES), lambda b, h, i: (b * nq + i, h)),
            pl.BlockSpec((seq, 2 * LANES), lambda b, h, i: (b, h)),
            pl.BlockSpec((seq, LANES), lambda b, h, i: (b, 0)),
            pl.BlockSpec((tq, C_V), lambda b, h, i: (b * nq + i, h)),
            pl.BlockSpec((tq, C_V), lambda b, h, i: (b * nq + i, h)),
            pl.BlockSpec((None, tq, 1), lambda b, h, i: (h, b * nq + i, 0)),
            pl.BlockSpec((tq, LANES), lambda b, h, i: (b * nq + i, 0)),
            pl.BlockSpec((tq, LANES), lambda b, h, i: (b * nq + i, 0)),
            pl.BlockSpec((tq, LANES), lambda b, h, i: (b * nq + i, 0)),
        ],
        out_specs=(
            pl.BlockSpec((tq, 2 * LANES), lambda b, h, i: (b * nq + i, h)),
            pl.BlockSpec((seq, 2 * LANES), lambda b, h, i: (b, h)),
            pl.BlockSpec((seq, LANES), lambda b, h, i: (b, 0)),
        ),
        scratch_shapes=[pltpu.VMEM((seq, 2 * LANES), BF16), pltpu.VMEM((seq, 2 * LANES), F32), pltpu.VMEM((seq, C_V), F32)],
        compiler_params=_cp("parallel", "arbitrary", "arbitrary"),
        name=name,
    )(q, kv, kr, do, o, lse, rc, rs1, rs2)


def _softmax_rows(s):
    e = jnp.exp(s - jnp.max(s, axis=-1, keepdims=True))
    return e / jnp.sum(e, axis=-1, keepdims=True)


def _mem_fwd(q, k, v, *, seq, n_mem, name):
    t, d = q.shape
    hd = d // MEM_HEADS
    tq = _tile(seq, 512)
    nq = seq // tq
    scale = hd ** -0.5

    def body(q_ref, k_ref, v_ref, o_ref):
        s = lax.dot_general(q_ref[...], k_ref[...], _NT, preferred_element_type=F32) * scale
        p = _softmax_rows(s)
        o_ref[...] = jnp.dot(p.astype(BF16), v_ref[...], preferred_element_type=F32).astype(o_ref.dtype)

    return pl.pallas_call(
        body,
        out_shape=jax.ShapeDtypeStruct((t, d), BF16),
        grid=(t // seq, MEM_HEADS, nq),
        in_specs=[
            pl.BlockSpec((tq, hd), lambda b, h, i: (b * nq + i, h)),
            pl.BlockSpec((n_mem, hd), lambda b, h, i: (b, h)),
            pl.BlockSpec((n_mem, hd), lambda b, h, i: (b, h)),
        ],
        out_specs=pl.BlockSpec((tq, hd), lambda b, h, i: (b * nq + i, h)),
        compiler_params=_cp("parallel", "parallel", "arbitrary"),
        name=name,
    )(q, k, v)


def _mem_bwd(q, k, v, do, *, seq, n_mem, name):
    t, d = q.shape
    hd = d // MEM_HEADS
    tq = _tile(seq, 512)
    nq = seq // tq
    scale = hd ** -0.5

    def body(q_ref, k_ref, v_ref, do_ref, dq_ref, dk_ref, dv_ref):
        @pl.when(pl.program_id(2) == 0)
        def _():
            dk_ref[...] = jnp.zeros_like(dk_ref)
            dv_ref[...] = jnp.zeros_like(dv_ref)

        qv, kv_, vv, dov = q_ref[...], k_ref[...], v_ref[...], do_ref[...]
        p = _softmax_rows(lax.dot_general(qv, kv_, _NT, preferred_element_type=F32) * scale)
        dp = lax.dot_general(dov, vv, _NT, preferred_element_type=F32)
        ds = (p * (dp - jnp.sum(dp * p, axis=-1, keepdims=True)) * scale).astype(BF16)
        dq_ref[...] = jnp.dot(ds, kv_, preferred_element_type=F32).astype(dq_ref.dtype)
        dk_ref[...] += lax.dot_general(ds, qv, _TN, preferred_element_type=F32)
        dv_ref[...] += lax.dot_general(p.astype(BF16), dov, _TN, preferred_element_type=F32)

    return pl.pallas_call(
        body,
        out_shape=(
            jax.ShapeDtypeStruct((t, d), BF16),
            jax.ShapeDtypeStruct(k.shape, F32),
            jax.ShapeDtypeStruct(v.shape, F32),
        ),
        grid=(t // seq, MEM_HEADS, nq),
        in_specs=[
            pl.BlockSpec((tq, hd), lambda b, h, i: (b * nq + i, h)),
            pl.BlockSpec((n_mem, hd), lambda b, h, i: (b, h)),
            pl.BlockSpec((n_mem, hd), lambda b, h, i: (b, h)),
            pl.BlockSpec((tq, hd), lambda b, h, i: (b * nq + i, h)),
        ],
        out_specs=(
            pl.BlockSpec((tq, hd), lambda b, h, i: (b * nq + i, h)),
            pl.BlockSpec((n_mem, hd), lambda b, h, i: (b, h)),
            pl.BlockSpec((n_mem, hd), lambda b, h, i: (b, h)),
        ),
        compiler_params=_cp("parallel", "parallel", "arbitrary"),
        name=name,
    )(q, k, v, do)


def _me():
    return lax.axis_index("x"), lax.axis_index("y"), lax.axis_index("c")


def _dev_index(px, py, pc):
    return 4 * px + 2 * py + pc


def _chunk(ref, idx, axis, size):
    start = pl.multiple_of(idx * size, size)
    return ref.at[pl.ds(start, size), :] if axis == 0 else ref.at[:, pl.ds(start, size)]


def _all_gather_small(v):
    m, n = v.shape

    def body(x_ref, out_ref, send_sems, recv_sems, local_sem):
        x, y, c = _me()
        me, sibling = (x, y, c), (x, y, 1 - c)
        chips = [(1 - x, y), (x, 1 - y), (1 - x, 1 - y)]

        def rows(px, py, pc):
            return out_ref.at[_dev_index(px, py, pc)]

        def copy(k, block, to, src=None):
            return pltpu.make_async_remote_copy(
                src_ref=rows(*block) if src is None else src, dst_ref=rows(*block),
                send_sem=send_sems.at[k], recv_sem=recv_sems.at[k], device_id=to, device_id_type=MESH)

        mine = pltpu.make_async_copy(x_ref, rows(*me), local_sem)
        mine.start()
        first = [copy(0, me, sibling, src=x_ref)]
        first += [copy(1 + j, me, (*chip, c), src=x_ref) for j, chip in enumerate(chips)]
        for cp in first:
            cp.start()
        passed = [copy(4 + j, (*chip, c), sibling) for j, chip in enumerate(chips)]
        for j, chip in enumerate(chips):
            copy(1 + j, (*chip, c), me).wait_recv()
            passed[j].start()
        copy(0, sibling, me).wait_recv()
        for j, chip in enumerate(chips):
            copy(4 + j, (*chip, 1 - c), me).wait_recv()
        for cp in first + passed:
            cp.wait_send()
        mine.wait()

    return pl.pallas_call(
        body,
        out_shape=jax.ShapeDtypeStruct((N_DEV, m, n), v.dtype),
        in_specs=[pl.BlockSpec(memory_space=pltpu.VMEM)],
        out_specs=pl.BlockSpec(memory_space=pltpu.VMEM),
        scratch_shapes=[pltpu.SemaphoreType.DMA((7,)), pltpu.SemaphoreType.DMA((7,)), pltpu.SemaphoreType.DMA],
        name="all_gather_small",
    )(v)


def _cast_place(w, layer, axis, dev_arr, *, name):
    _, r, c = w.shape
    tr = _tile(r, 256)
    nr = r // tr
    full = (N_DEV * r, c) if axis == 0 else (r, N_DEV * c)

    def body(dev_ref, w_ref, o_ref):
        o_ref[...] = w_ref[...].astype(o_ref.dtype)

    if axis == 0:
        out_spec = pl.BlockSpec((tr, c), lambda i, dev: (dev[0] * nr + i, 0))
    else:
        out_spec = pl.BlockSpec((tr, c), lambda i, dev: (i, dev[0]))
    return pl.pallas_call(
        body,
        out_shape=jax.ShapeDtypeStruct(full, BF16),
        grid_spec=pltpu.PrefetchScalarGridSpec(
            num_scalar_prefetch=1, grid=(nr,), in_specs=[pl.BlockSpec((None, tr, c), lambda i, dev: (layer, i, 0))], out_specs=out_spec),
        compiler_params=_cp("parallel"),
        name=name,
    )(dev_arr, w)


_ANY = pl.BlockSpec(memory_space=pl.ANY)
_SEM = pl.BlockSpec(memory_space=pltpu.SEMAPHORE)
_EFFECT = pltpu.SideEffectType.DATAFLOW_SIDE_EFFECTING
N_PEER_CHIPS = 3


def _other_chips(x, y):
    return [((1 - x if (j >> 1) & 1 else x), (1 - y if j & 1 else y)) for j in range(1, 4)]


def _remote(src, dst, send_sem, recv_sem, to):
    return pltpu.make_async_remote_copy(src_ref=src, dst_ref=dst, send_sem=send_sem, recv_sem=recv_sem, device_id=to, device_id_type=MESH)


def _plan_gather_ici(axes, sizes):
    def plan(refs, send_sems, recv_sems):
        x, y, c = _me()
        out = []
        for t in range(len(axes)):
            for j, chip in enumerate(_other_chips(x, y)):
                mine = _chunk(refs[t], _dev_index(x, y, c), axes[t], sizes[t])
                theirs = _chunk(refs[t], _dev_index(*chip, c), axes[t], sizes[t])
                out.append((_remote(mine, mine, send_sems[j], recv_sems[j], (*chip, c)), _remote(theirs, theirs, send_sems[j], recv_sems[j], (*chip, c))))
        return out
    return plan


def _plan_gather_d2d(axes, sizes):
    def plan(refs, send_sems, recv_sems):
        x, y, c = _me()
        out = []
        for t in range(len(axes)):
            for k in range(4):
                have = _chunk(refs[t], 2 * k + c, axes[t], sizes[t])
                get = _chunk(refs[t], 2 * k + 1 - c, axes[t], sizes[t])
                out.append((_remote(have, have, send_sems[0], recv_sems[0], (x, y, 1 - c)), _remote(get, get, send_sems[0], recv_sems[0], (x, y, 1 - c))))
        return out
    return plan


def _plan_scatter_d2d(axes, sizes):
    n = len(axes)

    def plan(refs, send_sems, recv_sems):
        x, y, c = _me()
        out = []
        for t in range(n):
            for k in range(4):
                give = _chunk(refs[t], 2 * k + 1 - c, axes[t], sizes[t])
                land = _chunk(refs[n + t], k, axes[t], sizes[t])
                cp = _remote(give, land, send_sems[0], recv_sems[0], (x, y, 1 - c))
                out.append((cp, cp))
        return out
    return plan


def _plan_scatter_ici(axes, sizes):
    n = len(axes)

    def plan(refs, send_sems, recv_sems):
        x, y, c = _me()
        out = []
        for t in range(n):
            for j, (px, py) in enumerate(_other_chips(x, y)):
                send = _remote(_chunk(refs[t], 2 * px + py, axes[t], sizes[t]), _chunk(refs[n + t], 2 * x + y, axes[t], sizes[t]),
                               send_sems[j], recv_sems[j], (px, py, c))
                recv = _remote(_chunk(refs[t], 2 * x + y, axes[t], sizes[t]), _chunk(refs[n + t], 2 * px + py, axes[t], sizes[t]),
                               send_sems[j], recv_sems[j], (px, py, c))
                out.append((send, recv))
        return out
    return plan


def _exchange_start(arrays, n_sems, plan, after, *, name):
    n = len(arrays)
    m = len(after)

    def body(*refs):
        send_sems, recv_sems = list(refs[n + m:n + m + n_sems]), list(refs[n + m + n_sems:n + m + 2 * n_sems])
        for send, _ in plan(refs[:n], send_sems, recv_sems):
            send.start()
        refs[-1][...] = jnp.zeros_like(refs[-1])

    outs = pl.pallas_call(
        body,
        out_shape=(*([pltpu.SemaphoreType.DMA(())] * (2 * n_sems)),
                   *[jax.ShapeDtypeStruct(a.shape, a.dtype) for a in arrays], jax.ShapeDtypeStruct((8, LANES), F32)),
        in_specs=[_ANY] * (n + m),
        out_specs=(*([_SEM] * (2 * n_sems)), *([_ANY] * n), pl.BlockSpec(memory_space=pltpu.VMEM)),
        input_output_aliases={t: 2 * n_sems + t for t in range(n)},
        compiler_params=pltpu.CompilerParams(has_side_effects=_EFFECT),
        name=name,
    )(*arrays, *after)
    return list(outs[:2 * n_sems]), list(outs[2 * n_sems:2 * n_sems + n]), outs[-1]


def _exchange_wait(sems, arrays, plan, after, *, name):
    n = len(arrays)
    n_sems = len(sems) // 2

    def body(*refs):
        send_sems, recv_sems = list(refs[n:n + n_sems]), list(refs[n + n_sems:n + 2 * n_sems])
        for send, recv in plan(refs[:n], send_sems, recv_sems):
            send.wait_send()
            recv.wait_recv()
        refs[-1][...] = jnp.zeros_like(refs[-1])

    outs = pl.pallas_call(
        body,
        out_shape=(*[jax.ShapeDtypeStruct(a.shape, a.dtype) for a in arrays], jax.ShapeDtypeStruct((8, LANES), F32)),
        in_specs=[_ANY] * n + [_SEM] * (2 * n_sems) + [_ANY] * len(after),
        out_specs=(*([_ANY] * n), pl.BlockSpec(memory_space=pltpu.VMEM)),
        input_output_aliases={t: t for t in range(n)},
        compiler_params=pltpu.CompilerParams(has_side_effects=_EFFECT),
        name=name,
    )(*arrays, *sems, *after)
    return list(outs[:n]), outs[-1]


def _add_pair(g, got, axis, core_arr, *, name):
    r, c = got.shape
    if axis == 0:
        rows, cols = r // 4, c
    else:
        rows, cols = r, c // 4
    tr = _tile(rows, 512)
    nr = rows // tr

    def body(core_ref, g_ref, got_ref, o_ref):
        o_ref[...] = (g_ref[...].astype(F32) + got_ref[...].astype(F32)).astype(o_ref.dtype)

    if axis == 0:
        g_spec = pl.BlockSpec((tr, cols), lambda k, i, core: ((2 * k + core[0]) * nr + i, 0))
        spec = pl.BlockSpec((tr, cols), lambda k, i, core: (k * nr + i, 0))
    else:
        g_spec = pl.BlockSpec((tr, cols), lambda k, i, core: (i, 2 * k + core[0]))
        spec = pl.BlockSpec((tr, cols), lambda k, i, core: (i, k))
    return pl.pallas_call(
        body,
        out_shape=jax.ShapeDtypeStruct(got.shape, got.dtype),
        grid_spec=pltpu.PrefetchScalarGridSpec(num_scalar_prefetch=1, grid=(4, nr), in_specs=[g_spec, spec], out_specs=spec),
        compiler_params=_cp("parallel", "parallel"),
        name=name,
    )(core_arr, g, got)


def _adam_math(w, g, m, v):
    m = ADAM_B1 * m + (1.0 - ADAM_B1) * g
    v = ADAM_B2 * v + (1.0 - ADAM_B2) * (g * g)
    m_hat = m / (1.0 - ADAM_B1 ** ADAM_STEP)
    v_hat = v / (1.0 - ADAM_B2 ** ADAM_STEP)
    delta = -ADAM_LR * (m_hat / (jnp.sqrt(v_hat) + ADAM_EPS) + ADAM_WD * w)
    return delta, m, v


def _adamw(w, m, v, parts, axis, chip_arr, layers, prev, after, *, name):
    n_l, r, c = w.shape
    first, count = layers
    tr = _tile(r, 128)
    nr = r // tr
    n_chip = 4
    n_in = 4 + n_chip * count

    def body(*refs):
        w_ref, m_ref, v_ref = refs[1:4]
        p_refs = refs[4:n_in]
        g_ref, d_ref, nm_ref, nv_ref = refs[-4:]
        for l in range(count):
            @pl.when(pl.program_id(0) == l)
            def _(l=l):
                g = p_refs[n_chip * l][...].astype(F32)
                for k in range(1, n_chip):
                    g = g + p_refs[n_chip * l + k][...].astype(F32)
                delta, nm, nv = _adam_math(w_ref[...], g, m_ref[...], v_ref[...])
                g_ref[...] = g
                d_ref[...] = delta
                nm_ref[...] = nm
                nv_ref[...] = nv

    def part_spec(ll, k):
        if axis == 0:
            return pl.BlockSpec((tr, c), lambda l, i, chips: (chips[k] * nr + jnp.where(l == ll, i, 0), 0))
        return pl.BlockSpec((tr, c), lambda l, i, chips: (jnp.where(l == ll, i, 0), chips[k]))

    wspec = pl.BlockSpec((None, tr, c), lambda l, i, chips: (first + l, i, 0))
    pspecs = [part_spec(ll, k) for ll in range(count) for k in range(n_chip)]
    sds = jax.ShapeDtypeStruct(w.shape, F32)
    kept = [] if prev is None else list(prev)
    return pl.pallas_call(
        body,
        out_shape=(sds, sds, sds, sds),
        grid_spec=pltpu.PrefetchScalarGridSpec(
            num_scalar_prefetch=1, grid=(count, nr),
            in_specs=[wspec, wspec, wspec] + pspecs + [_ANY] * (1 + len(kept)), out_specs=(wspec, wspec, wspec, wspec)),
        input_output_aliases={n_in + 1 + k: k for k in range(len(kept))},
        compiler_params=_cp("arbitrary", "arbitrary"),
        name=name,
    )(chip_arr, w, m, v, *[parts[first + ll][0 if k == 0 else 1] for ll in range(count) for k in range(n_chip)], after, *kept)


def _adamw_small(w, m, v, parts, *, name):
    r, c = w.shape
    tr = _tile(r, 512)

    def body(w_ref, m_ref, v_ref, p_ref, g_ref, d_ref, nm_ref, nv_ref):
        g = p_ref[0]
        for s in range(1, N_DEV):
            g = g + p_ref[s]
        delta, nm, nv = _adam_math(w_ref[...], g, m_ref[...], v_ref[...])
        g_ref[...] = g
        d_ref[...] = delta
        nm_ref[...] = nm
        nv_ref[...] = nv

    spec = pl.BlockSpec((tr, c), lambda i: (i, 0))
    sds = jax.ShapeDtypeStruct(w.shape, F32)
    return pl.pallas_call(
        body,
        out_shape=(sds, sds, sds, sds),
        grid=(r // tr,),
        in_specs=[spec, spec, spec, pl.BlockSpec((N_DEV, tr, c), lambda i: (0, i, 0))],
        out_specs=(spec, spec, spec, spec),
        compiler_params=_cp("parallel"),
        name=name,
    )(w, m, v, parts)


def _pack(arrs):
    flat = jnp.concatenate([a.reshape(-1).astype(F32) for a in arrs])
    pad = (-flat.shape[0]) % (8 * LANES)
    return jnp.pad(flat, (0, pad)).reshape(-1, LANES)


def _unpack(packed, shapes):
    flat = packed.reshape(-1)
    out, off = [], 0
    for sh in shapes:
        n = math.prod(sh)
        out.append(flat[off:off + n].reshape(sh))
        off += n
    return out


def _unpack_gathered(packed, shapes, axis):
    flat = packed.reshape(N_DEV, -1)
    out, off = [], 0
    for sh, ax in zip(shapes, axis):
        n = math.prod(sh)
        blocks = flat[:, off:off + n].reshape((N_DEV, *sh))
        out.append(jnp.concatenate([blocks[d] for d in range(N_DEV)], axis=ax))
        off += n
    return out


def _my_block(full, axis, size):
    idx = _dev_index(*_me())
    return lax.dynamic_slice_in_dim(full, idx * size, size, axis)


def kernel(x, mem, positions, norm_mix_g, norm_mem_q_g, norm_mem_kv_g, norm_ffn_g, final_norm_g, ab_w_in, a_v_norm_g, a_w_s, a_b_s, b_conv_w, ab_w_out, c_w_in, c_q_norm_g, c_kv_norm_g, c_w_uq, c_w_ukv, c_w_out, m_wq, m_wk, m_wv, m_wo, f_w1, f_w2, loss_target, m_norm_mix_g, m_norm_mem_q_g, m_norm_mem_kv_g, m_norm_ffn_g, m_final_norm_g, m_ab_w_in, m_a_v_norm_g, m_a_w_s, m_a_b_s, m_b_conv_w, m_ab_w_out, m_c_w_in, m_c_q_norm_g, m_c_kv_norm_g, m_c_w_uq, m_c_w_ukv, m_c_w_out, m_m_wq, m_m_wk, m_m_wv, m_m_wo, m_f_w1, m_f_w2, v_norm_mix_g, v_norm_mem_q_g, v_norm_mem_kv_g, v_norm_ffn_g, v_final_norm_g, v_ab_w_in, v_a_v_norm_g, v_a_w_s, v_a_b_s, v_b_conv_w, v_ab_w_out, v_c_w_in, v_c_q_norm_g, v_c_kv_norm_g, v_c_w_uq, v_c_w_ukv, v_c_w_out, v_m_wq, v_m_wk, v_m_wv, v_m_wo, v_f_w1, v_f_w2):
    bl, seq, d = x.shape
    t = bl * seq
    n_mem = mem.shape[1]
    depth = norm_mix_g.shape[0]
    n_even, n_odd = ab_w_in.shape[0], c_w_in.shape[0]
    a_width = a_v_norm_g.shape[1]
    groups = a_w_s.shape[1]
    heads = d // C_NOPE
    q_rank = c_q_norm_g.shape[1] * N_DEV
    kv_rank = c_kv_norm_g.shape[1] * N_DEV

    big = {
        "ab_w_in": (ab_w_in, m_ab_w_in, v_ab_w_in, 1), "ab_w_out": (ab_w_out, m_ab_w_out, v_ab_w_out, 0),
        "c_w_in": (c_w_in, m_c_w_in, v_c_w_in, 0), "c_w_uq": (c_w_uq, m_c_w_uq, v_c_w_uq, 1),
        "c_w_ukv": (c_w_ukv, m_c_w_ukv, v_c_w_ukv, 1), "c_w_out": (c_w_out, m_c_w_out, v_c_w_out, 0),
        "m_wq": (m_wq, m_m_wq, v_m_wq, 0), "m_wk": (m_wk, m_m_wk, v_m_wk, 0), "m_wv": (m_wv, m_m_wv, v_m_wv, 0),
        "m_wo": (m_wo, m_m_wo, v_m_wo, 0), "f_w1": (f_w1, m_f_w1, v_f_w1, 1), "f_w2": (f_w2, m_f_w2, v_f_w2, 0),
    }
    even_names = ["ab_w_in", "ab_w_out"]
    odd_names = ["c_w_in", "c_w_uq", "c_w_ukv", "c_w_out"]
    all_names = ["m_wq", "m_wk", "m_wv", "m_wo", "f_w1", "f_w2"]

    def layer_names(layer):
        return (even_names if layer % 2 == 0 else odd_names) + all_names

    def sub(name, layer):
        return layer // 2 if name in even_names or name in odd_names else layer

    split_small = [(c_q_norm_g, m_c_q_norm_g, v_c_q_norm_g, 1), (c_kv_norm_g, m_c_kv_norm_g, v_c_kv_norm_g, 1), (b_conv_w, m_b_conv_w, v_b_conv_w, 2)]
    shard_shapes = [s[0].shape for s in split_small] * 3
    shard_axes = [s[3] for s in split_small] * 3
    gathered = _all_gather_small(_pack([s[0] for s in split_small] + [s[1] for s in split_small] + [s[2] for s in split_small]))
    whole = _unpack_gathered(gathered, shard_shapes, shard_axes)
    q_norm_full, kv_norm_full, conv_full = whole[0:3]
    small_w = [norm_mix_g, norm_mem_q_g, norm_mem_kv_g, norm_ffn_g, final_norm_g, a_v_norm_g, a_w_s, a_b_s] + whole[0:3]
    small_m = [m_norm_mix_g, m_norm_mem_q_g, m_norm_mem_kv_g, m_norm_ffn_g, m_final_norm_g, m_a_v_norm_g, m_a_w_s, m_a_b_s] + whole[3:6]
    small_v = [v_norm_mix_g, v_norm_mem_q_g, v_norm_mem_kv_g, v_norm_ffn_g, v_final_norm_g, v_a_v_norm_g, v_a_w_s, v_a_b_s] + whole[6:9]

    inv = ROPE_THETA ** (-jnp.arange(0, C_ROPE, 2, dtype=F32) / C_ROPE)
    ang = positions.astype(F32).reshape(t, 1) * inv
    cos, sin, zero = jnp.cos(ang), jnp.sin(ang), jnp.zeros((t, C_ROPE // 2), F32)
    rc = jnp.concatenate([cos, cos, zero, zero], axis=1)
    rs1 = jnp.concatenate([-sin, zero, zero, zero], axis=1)
    rs2 = jnp.concatenate([zero, sin, zero, zero], axis=1)

    my_x, my_y, my_c = _me()
    my_chip = 2 * my_x + my_y
    dev_arr = jnp.stack([_dev_index(my_x, my_y, my_c)]).astype(jnp.int32)
    core_arr = jnp.stack([my_c]).astype(jnp.int32)
    chip_arr = jnp.stack([my_chip, my_chip ^ 1, my_chip ^ 2, my_chip ^ 3]).astype(jnp.int32)

    n_stage = 3 * depth
    chain = [None]
    gat = {}
    issued = [0]

    def stage_names(i):
        layer, gid = divmod(i, 3)
        return layer, [even_names if layer % 2 == 0 else odd_names, all_names[:4], all_names[4:]][gid]

    def order(after):
        return [a for a in (chain[0], after) if a is not None]

    def gather_ici_start(i, after):
        layer, names = stage_names(i)
        placed = [_cast_place(big[nm][0], sub(nm, layer), big[nm][3], dev_arr, name="cast_place") for nm in names]
        axes = [big[nm][3] for nm in names]
        sizes = [p.shape[a] // N_DEV for p, a in zip(placed, axes)]
        st = dict(names=names, ici=_plan_gather_ici(axes, sizes), d2d=_plan_gather_d2d(axes, sizes))
        st["sems"], st["arrays"], chain[0] = _exchange_start(placed, N_PEER_CHIPS, st["ici"], order(after), name=f"gather_ici_start_s{i}")
        gat[i] = st

    def gather_stage(i, after):
        while issued[0] < min(i + 3, n_stage):
            gather_ici_start(issued[0], after)
            issued[0] += 1
        ahead = [i, i + 1] if i >= 3 and i + 1 < n_stage else [i]
        for j in ahead:
            st = gat[j]
            if "landed" not in st:
                st["arrays"], chain[0] = _exchange_wait(st["sems"], st["arrays"], st["ici"], order(after), name=f"gather_ici_wait_s{j}")
                st["landed"] = True
            if "sems2" not in st:
                st["sems2"], st["arrays"], chain[0] = _exchange_start(st["arrays"], 1, st["d2d"], order(after), name=f"gather_d2d_start_s{j}")
        st = gat[i]
        st["arrays"], chain[0] = _exchange_wait(st["sems2"], st["arrays"], st["d2d"], order(after), name=f"gather_d2d_wait_s{i}")
        return dict(zip(st["names"], st["arrays"]))

    wts = []

    def group_cols(w_in):
        return w_in.reshape(d, 5, groups, LANES).transpose(0, 2, 1, 3).reshape(d, 5 * a_width)

    def ungroup_cols(g_in):
        return g_in.reshape(d, groups, 5, LANES).transpose(0, 2, 1, 3).reshape(d, 5 * a_width)

    def group_rows(w_out):
        return w_out.reshape(2, groups, LANES, d).transpose(1, 0, 2, 3).reshape(2 * a_width, d)

    def ungroup_rows(g_out):
        return g_out.reshape(groups, 2, LANES, d).transpose(1, 0, 2, 3).reshape(2 * a_width, d)

    def pad_heads(w_uq):
        w3 = w_uq.reshape(q_rank, heads, C_NOPE + C_ROPE)
        return jnp.pad(w3, ((0, 0), (0, 0), (0, LANES - C_ROPE))).reshape(q_rank, heads * 2 * LANES)

    def unpad_heads(g_uq):
        return g_uq.reshape(q_rank, heads, 2 * LANES)[:, :, :C_NOPE + C_ROPE].reshape(q_rank, heads * (C_NOPE + C_ROPE))

    x2 = x.reshape(t, d)
    mem2 = mem.reshape(bl * n_mem, d)
    saved = []

    for layer in range(depth):
        w = gather_stage(3 * layer, x2 if layer else gathered)
        wts.append(w)
        sv = {"x0": x2}
        xn = _rmsnorm_fwd(x2, norm_mix_g[layer], name="norm_mix")
        sv["xn"] = xn
        if layer % 2 == 0:
            e = layer // 2
            w_in_g = group_cols(w["ab_w_in"])
            w_out_g = group_rows(w["ab_w_out"])
            z = _mm(xn, w_in_g, out_dtypes=(F32,), extras=[(chain[0], "after")], name="ab_in")
            ws_e = a_w_s[e]
            bs_e = a_b_s[e].reshape(groups, GMLP_BLOCK, 1)
            gn_e = a_v_norm_g[e].reshape(groups, 1, LANES)
            cw_e = conv_full[e].reshape(3, groups, LANES).transpose(1, 0, 2)
            ycat = _gmlp_conv_fwd(z, ws_e, bs_e, gn_e, cw_e, seq=seq, name="gmlp_conv_fwd")
            x2 = _mm(ycat, w_out_g, out_dtypes=(F32,), extras=[(x2, "mn")], epilogue=_epi_add, name="ab_out")
            sv.update(z=z, ycat=ycat, w_in_g=w_in_g, w_out_g=w_out_g, ws=ws_e, bs=bs_e, gn=gn_e, cw=cw_e)
        else:
            o = layer // 2
            w_uq_p = pad_heads(w["c_w_uq"])
            z = _mm(xn, w["c_w_in"], out_dtypes=(F32,), extras=[(chain[0], "after")], name="c_in")
            qn = _rmsnorm_fwd(z, q_norm_full[o], width=q_rank, col=0, name="norm_cq")
            kvn = _rmsnorm_fwd(z, kv_norm_full[o], width=kv_rank, col=q_rank // kv_rank, name="norm_ckv")
            qh = _mm(qn, w_uq_p, extras=[(rc, "m"), (rs1, "m"), (rs2, "m")], epilogue=_epi_rope_heads, tn=1024, name="c_uq")
            kvh = _mm(kvn, w["c_w_ukv"], name="c_ukv")
            k_rope = jnp.pad(z[:, q_rank + kv_rank:], ((0, 0), (0, LANES - C_ROPE)))
            kr = _rope_rows(k_rope, rc, rs1, rs2, transpose=False, out_dtype=BF16, name="rope_k")
            att, lse = _mla_fwd(qh, kvh, kr, seq=seq, heads=heads, name="mla_fwd")
            x2 = _mm(att, w["c_w_out"], out_dtypes=(F32,), extras=[(x2, "mn")], epilogue=_epi_add, name="c_out")
            sv.update(z=z, qn=qn, kvn=kvn, qh=qh, kvh=kvh, kr=kr, att=att, lse=lse, w_uq_p=w_uq_p)
        sv["x1"] = x2
        w.update(gather_stage(3 * layer + 1, x2))
        xq = _rmsnorm_fwd(x2, norm_mem_q_g[layer], name="norm_mem_q")
        memn = _rmsnorm_fwd(mem2, norm_mem_kv_g[layer], name="norm_mem_kv")
        mq = _mm(xq, w["m_wq"], extras=[(chain[0], "after")], name="m_q")
        mk = _mm(memn, w["m_wk"], name="m_k")
        mv = _mm(memn, w["m_wv"], name="m_v")
        mo = _mem_fwd(mq, mk, mv, seq=seq, n_mem=n_mem, name="mem_fwd")
        x2 = _mm(mo, w["m_wo"], out_dtypes=(F32,), extras=[(x2, "mn")], epilogue=_epi_add, name="m_o")
        sv.update(xq=xq, memn=memn, mq=mq, mk=mk, mv=mv, mo=mo)
        sv["x2"] = x2
        w.update(gather_stage(3 * layer + 2, x2))
        xf = _rmsnorm_fwd(x2, norm_ffn_g[layer], name="norm_ffn")
        hpre, hact = _mm(xf, w["f_w1"], out_dtypes=(BF16, BF16), extras=[(chain[0], "after")], epilogue=_epi_relu2, name="f_1")
        x2 = _mm(hact, w["f_w2"], out_dtypes=(F32,), extras=[(x2, "mn")], epilogue=_epi_add, name="f_2")
        sv.update(xf=xf, hpre=hpre, hact=hact)
        saved.append(sv)

    loss_part, dx, dg_final = _final_loss(x2, final_norm_g, loss_target.reshape(t, d), name="final_loss")

    dg_mix, dg_mq, dg_mkv, dg_ffn = [None] * depth, [None] * depth, [None] * depth, [None] * depth
    d_av, d_ws, d_bs = [None] * n_even, [None] * n_even, [None] * n_even
    d_conv, d_qn, d_kvn = [None] * n_even, [None] * n_odd, [None] * n_odd
    parts = {nm: [None] * big[nm][0].shape[0] for nm in big}

    sca = []

    def scatter_finish_d2d(st, after):
        n = len(st["names"])
        arrays, chain[0] = _exchange_wait(st["sems"], st["arrays"], st["d2d"], order(after), name=f"scatter_d2d_wait_s{st['id']}")
        sums = [_add_pair(g, got, ax, core_arr, name="add_pair") for g, got, ax in zip(arrays[:n], arrays[n:], st["axes"])]
        lands = [pltpu.with_memory_space_constraint(lax.empty(s_.shape, s_.dtype), pltpu.HBM) for s_ in sums]
        st["sems2"], st["arrays"], chain[0] = _exchange_start(sums + lands, N_PEER_CHIPS, st["ici"], order(after), name=f"scatter_ici_start_s{st['id']}")

    def scatter_stage(layer, names, grads, after):
        if sca and "sems2" not in sca[-1]:
            scatter_finish_d2d(sca[-1], after)
        axes = [big[nm][3] for nm in names]
        arrs = [grads[nm] for nm in names]
        sizes8 = [g.shape[a] // N_DEV for g, a in zip(arrs, axes)]
        gots = []
        for g, a in zip(arrs, axes):
            sh = list(g.shape)
            sh[a] //= 2
            gots.append(pltpu.with_memory_space_constraint(lax.empty(tuple(sh), g.dtype), pltpu.HBM))
        st = dict(id=len(sca), layer=layer, names=names, axes=axes, d2d=_plan_scatter_d2d(axes, sizes8),
                  ici=_plan_scatter_ici(axes, sizes8))
        st["sems"], st["arrays"], chain[0] = _exchange_start(arrs + gots, 1, st["d2d"], order(after), name=f"scatter_d2d_start_s{st['id']}")
        sca.append(st)

    def scatter_collect(st, after):
        n = len(st["names"])
        arrays, chain[0] = _exchange_wait(st["sems2"], st["arrays"], st["ici"], order(after), name=f"scatter_ici_wait_s{st['id']}")
        for nm, s_, r in zip(st["names"], arrays[:n], arrays[n:]):
            parts[nm][sub(nm, st["layer"])] = (s_, r)

    for layer in reversed(range(depth)):
        w, sv = wts[layer], saved[layer]
        grads = {}
        dh = _mm(dx, w["f_w2"], tb=True, extras=[(sv["hpre"], "mn"), (chain[0], "after")], epilogue=_epi_drelu2, name="f_2_dx")
        grads["f_w2"] = _mm(sv["hact"], dx, ta=True, name="f_2_dw")
        grads["f_w1"] = _mm(sv["xf"], dh, ta=True, name="f_1_dw")
        dxf = _mm(dh, w["f_w1"], tb=True, name="f_1_dx")
        dx, dg_ffn[layer] = _rmsnorm_bwd(sv["x2"], norm_ffn_g[layer], dxf, dx, name="norm_ffn_bwd")
        scatter_stage(layer, all_names[4:], grads, dx)
        dmo = _mm(dx, w["m_wo"], tb=True, extras=[(chain[0], "after")], name="m_o_dx")
        grads["m_wo"] = _mm(sv["mo"], dx, ta=True, name="m_o_dw")
        dmq, dmk, dmv = _mem_bwd(sv["mq"], sv["mk"], sv["mv"], dmo, seq=seq, n_mem=n_mem, name="mem_bwd")
        grads["m_wq"] = _mm(sv["xq"], dmq, ta=True, name="m_q_dw")
        grads["m_wk"] = _mm(sv["memn"], dmk, ta=True, name="m_k_dw")
        grads["m_wv"] = _mm(sv["memn"], dmv, ta=True, name="m_v_dw")
        dxq = _mm(dmq, w["m_wq"], tb=True, name="m_q_dx")
        dmemn_k = _mm(dmk, w["m_wk"], tb=True, out_dtypes=(F32,), name="m_k_dx")
        dmemn = _mm(dmv, w["m_wv"], tb=True, out_dtypes=(F32,), extras=[(dmemn_k, "mn")], epilogue=_epi_add, name="m_v_dx")
        _, dg_mkv[layer] = _rmsnorm_bwd(mem2, norm_mem_kv_g[layer], dmemn, name="norm_mem_kv_bwd")
        dx, dg_mq[layer] = _rmsnorm_bwd(sv["x1"], norm_mem_q_g[layer], dxq, dx, name="norm_mem_q_bwd")
        scatter_stage(layer, all_names[:4], grads, dx)
        if layer % 2 == 0:
            e = layer // 2
            dycat = _mm(dx, sv["w_out_g"], tb=True, extras=[(chain[0], "after")], name="ab_out_dx")
            grads["ab_w_out"] = ungroup_rows(_mm(sv["ycat"], dx, ta=True, name="ab_out_dw"))
            dz, dws, dbs, dgn, dcw = _gmlp_conv_bwd(sv["z"], dycat, sv["ws"], sv["bs"], sv["gn"], sv["cw"], seq=seq, name="gmlp_conv_bwd")
            d_ws[e], d_bs[e], d_av[e] = dws, dbs.reshape(groups, GMLP_BLOCK), dgn.reshape(a_width)
            d_conv[e] = dcw.transpose(1, 0, 2).reshape(3, a_width)
            grads["ab_w_in"] = ungroup_cols(_mm(sv["xn"], dz, ta=True, name="ab_in_dw"))
            dxn = _mm(dz, sv["w_in_g"], tb=True, name="ab_in_dx")
        else:
            o = layer // 2
            datt = _mm(dx, w["c_w_out"], tb=True, extras=[(chain[0], "after")], name="c_out_dx")
            grads["c_w_out"] = _mm(sv["att"], dx, ta=True, name="c_out_dw")
            dq_pre, dkvh, dkr = _mla_bwd(sv["qh"], sv["kvh"], sv["kr"], datt, sv["att"], sv["lse"], rc, rs1, rs2, seq=seq, heads=heads, name="mla_bwd")
            dk_rope = _rope_rows(dkr, rc, rs1, rs2, transpose=True, out_dtype=F32, name="rope_k_bwd")
            grads["c_w_uq"] = unpad_heads(_mm(sv["qn"], dq_pre, ta=True, name="c_uq_dw"))
            grads["c_w_ukv"] = _mm(sv["kvn"], dkvh, ta=True, name="c_ukv_dw")
            dqn = _mm(dq_pre, sv["w_uq_p"], tb=True, name="c_uq_dx")
            dkvn = _mm(dkvh, w["c_w_ukv"], tb=True, name="c_ukv_dx")
            dcq, d_qn[o] = _rmsnorm_bwd(sv["z"], q_norm_full[o], dqn, width=q_rank, col=0, name="norm_cq_bwd")
            dckv, d_kvn[o] = _rmsnorm_bwd(sv["z"], kv_norm_full[o], dkvn, width=kv_rank, col=q_rank // kv_rank, name="norm_ckv_bwd")
            dz = jnp.concatenate([dcq, dckv, dk_rope[:, :C_ROPE]], axis=1)
            grads["c_w_in"] = _mm(sv["xn"], dz, ta=True, name="c_in_dw")
            dxn = _mm(dz, w["c_w_in"], tb=True, name="c_in_dx")
        dx, dg_mix[layer] = _rmsnorm_bwd(sv["x0"], norm_mix_g[layer], dxn, dx, name="norm_mix_bwd")
        scatter_stage(layer, even_names if layer % 2 == 0 else odd_names, grads, dx)

    grad_x = dx.reshape(bl, seq, d)

    scatter_finish_d2d(sca[-1], dx)
    for st in sca[:-3]:
        scatter_collect(st, dx)
    big_out = {}
    done = chain[0]
    for nm, (wv, mv_, vv, ax) in big.items():
        n_sub = wv.shape[0]
        first = 0 if nm in odd_names else 1
        if first < n_sub:
            big_out[nm] = _adamw(wv, mv_, vv, parts[nm], ax, chip_arr, (first, n_sub - first), None, done, name=f"adamw_{nm}")
            done = big_out[nm][0]
    for st in sca[-3:]:
        scatter_collect(st, done)
    for nm, (wv, mv_, vv, ax) in big.items():
        if nm not in odd_names:
            big_out[nm] = _adamw(wv, mv_, vv, parts[nm], ax, chip_arr, (0, 1), big_out.get(nm), chain[0], name=f"adamw_{nm}_first")

    small_g = [jnp.concatenate(dg_mix), jnp.concatenate(dg_mq), jnp.concatenate(dg_mkv), jnp.concatenate(dg_ffn), dg_final.reshape(d),
               jnp.stack(d_av), jnp.stack(d_ws), jnp.stack(d_bs), jnp.concatenate(d_qn), jnp.concatenate(d_kvn), jnp.stack(d_conv)]
    small_g.append(loss_part.reshape(1))
    spare = jnp.zeros((1,), F32)
    small_shapes = [a.shape for a in small_w] + [(1,)]
    g_all = _all_gather_small(_pack(small_g) + chain[0][0, 0])
    sg, sd, sm, svv = _adamw_small(_pack(small_w + [spare]), _pack(small_m + [spare]), _pack(small_v + [spare]), g_all, name="adamw_small")
    small_names = ["norm_mix_g", "norm_mem_q_g", "norm_mem_kv_g", "norm_ffn_g", "final_norm_g", "a_v_norm_g", "a_w_s", "a_b_s", "c_q_norm_g", "c_kv_norm_g", "b_conv_w"]
    small_out = {}
    unpacked = [_unpack(p, small_shapes) for p in (sg, sd, sm, svv)]
    for i, nm in enumerate(small_names):
        vals = [u[i] for u in unpacked]
        if i >= 8:
            full_w, _, _, ax = split_small[i - 8]
            vals = [_my_block(val, ax, full_w.shape[ax]) for val in vals]
        small_out[nm] = vals

    order = ["norm_mix_g", "norm_mem_q_g", "norm_mem_kv_g", "norm_ffn_g", "final_norm_g", "ab_w_in", "a_v_norm_g", "a_w_s", "a_b_s", "b_conv_w", "ab_w_out",
             "c_w_in", "c_q_norm_g", "c_kv_norm_g", "c_w_uq", "c_w_ukv", "c_w_out", "m_wq", "m_wk", "m_wv", "m_wo", "f_w1", "f_w2"]
    loss = unpacked[0][-1][0]
    res = [loss, grad_x]
    for kind in range(4):
        for nm in order:
            res.append(big_out[nm][kind] if nm in big_out else small_out[nm][kind])
    return tuple(res)
```

```python
import functools
import math

import jax
import jax.numpy as jnp
from jax import lax
from jax.experimental import pallas as pl
from jax.experimental.pallas import tpu as pltpu

F32 = jnp.float32
BF16 = jnp.bfloat16
MESH = pl.DeviceIdType.MESH
N_DEV = 8
LANES = 128

RMS_EPS = 1e-6
CHUNK = 64
GMLP_BLOCK = 128
C_NOPE = 128
C_ROPE = 64
C_V = 128
ROPE_THETA = 10000.0
MEM_HEADS = 4
ADAM_LR = 0.001
ADAM_B1 = 0.9
ADAM_B2 = 0.999
ADAM_EPS = 1e-08
ADAM_WD = 0.01
ADAM_STEP = 10
NEG_BIG = -1e30
VMEM_LIMIT = 56 * 1024 * 1024


def _cp(*sem):
    return pltpu.CompilerParams(dimension_semantics=sem, vmem_limit_bytes=VMEM_LIMIT)


def _tile(n, pref):
    t = min(n, pref)
    while n % t and t > LANES:
        t //= 2
    return n if n % t else t


def _mm(a, b, *, ta=False, tb=False, out_dtypes=(BF16,), extras=(), epilogue=None, tm=1024, tn=1024, tk=2048, name):
    m, k = (a.shape[1], a.shape[0]) if ta else a.shape
    n = b.shape[0] if tb else b.shape[1]
    assert (b.shape[1] if tb else b.shape[0]) == k, (a.shape, b.shape, ta, tb)
    tm, tn, tk = _tile(m, tm), _tile(n, tn), _tile(k, tk)
    nk = k // tk
    ne, no = len(extras), len(out_dtypes)
    dims = (((0 if ta else 1,), (1 if tb else 0,)), ((), ()))

    def body(*refs):
        a_ref, b_ref = refs[0], refs[1]
        e_refs = refs[2:2 + ne]
        o_refs = refs[2 + ne:2 + ne + no]

        def finish(acc):
            vals = [e[...] for e, (_, kind) in zip(e_refs, extras) if kind != "after"]
            outs = (acc,) if epilogue is None else epilogue(acc, *vals)
            for o, val in zip(o_refs, outs):
                o[...] = val.astype(o.dtype)

        prod = lax.dot_general(a_ref[...].astype(BF16), b_ref[...].astype(BF16), dims, preferred_element_type=F32)
        if nk == 1:
            finish(prod)
        else:
            acc_ref = refs[-1]
            kk = pl.program_id(2)

            @pl.when(kk == 0)
            def _():
                acc_ref[...] = prod

            @pl.when((kk > 0) & (kk < nk - 1))
            def _():
                acc_ref[...] += prod

            @pl.when(kk == nk - 1)
            def _():
                finish(acc_ref[...] + prod)

    a_spec = pl.BlockSpec((tk, tm), lambda i, j, kk: (kk, i)) if ta else pl.BlockSpec((tm, tk), lambda i, j, kk: (i, kk))
    b_spec = pl.BlockSpec((tn, tk), lambda i, j, kk: (j, kk)) if tb else pl.BlockSpec((tk, tn), lambda i, j, kk: (kk, j))
    e_specs = []
    for arr, kind in extras:
        if kind == "mn":
            e_specs.append(pl.BlockSpec((tm, tn), lambda i, j, kk: (i, j)))
        elif kind == "after":
            e_specs.append(pl.BlockSpec(arr.shape, lambda i, j, kk: (0, 0)))
        else:
            e_specs.append(pl.BlockSpec((tm, arr.shape[1]), lambda i, j, kk: (i, 0)))
    outs = pl.pallas_call(
        body,
        out_shape=tuple(jax.ShapeDtypeStruct((m, n), d) for d in out_dtypes),
        grid=(m // tm, n // tn, nk),
        in_specs=[a_spec, b_spec] + e_specs,
        out_specs=tuple(pl.BlockSpec((tm, tn), lambda i, j, kk: (i, j)) for _ in out_dtypes),
        scratch_shapes=[pltpu.VMEM((tm, tn), F32)] if nk > 1 else [],
        compiler_params=_cp("parallel", "parallel", "arbitrary"),
        name=name,
    )(a, b, *[arr for arr, _ in extras])
    return outs[0] if no == 1 else outs


def _epi_add(acc, r):
    return (acc + r,)


def _epi_relu2(acc):
    h = jnp.maximum(acc, 0.0)
    return (acc, h * h)


def _epi_drelu2(acc, h):
    return (acc * (2.0 * jnp.maximum(h.astype(F32), 0.0)),)


def _rope(x, c, s1, s2):
    return x * c + pltpu.roll(x, LANES - C_ROPE // 2, 1) * s1 + pltpu.roll(x, C_ROPE // 2, 1) * s2


def _rope_t(dy, c, s1, s2):
    return dy * c + pltpu.roll(dy * s1, C_ROPE // 2, 1) + pltpu.roll(dy * s2, LANES - C_ROPE // 2, 1)


def _epi_rope_heads(acc, c, s1, s2):
    parts = []
    for h0 in range(0, acc.shape[1], 2 * LANES):
        parts.append(acc[:, h0:h0 + LANES])
        parts.append(_rope(acc[:, h0 + LANES:h0 + 2 * LANES], c, s1, s2))
    return (jnp.concatenate(parts, axis=1),)


def _rmsnorm_fwd(x, g, *, width=None, col=0, name):
    t = x.shape[0]
    w = x.shape[1] if width is None else width
    tm = _tile(t, 512)

    def body(x_ref, g_ref, o_ref):
        xv = x_ref[...].astype(F32)
        r = lax.rsqrt(jnp.mean(xv * xv, axis=-1, keepdims=True) + RMS_EPS)
        o_ref[...] = (xv * r * g_ref[...]).astype(o_ref.dtype)

    return pl.pallas_call(
        body,
        out_shape=jax.ShapeDtypeStruct((t, w), BF16),
        grid=(t // tm,),
        in_specs=[pl.BlockSpec((tm, w), lambda i: (i, col)), pl.BlockSpec((1, w), lambda i: (0, 0))],
        out_specs=pl.BlockSpec((tm, w), lambda i: (i, 0)),
        compiler_params=_cp("parallel"),
        name=name,
    )(x, g.reshape(1, w))


def _rmsnorm_bwd(x, g, dy, res=None, *, width=None, col=0, name):
    t = x.shape[0]
    w = x.shape[1] if width is None else width
    tm = _tile(t, 256)
    has_res = res is not None

    def body(*refs):
        x_ref, g_ref, dy_ref = refs[:3]
        res_ref = refs[3] if has_res else None
        dx_ref, dg_ref = refs[-2], refs[-1]
        xv = x_ref[...].astype(F32)
        r = lax.rsqrt(jnp.mean(xv * xv, axis=-1, keepdims=True) + RMS_EPS)
        xh = xv * r
        dyv = dy_ref[...].astype(F32)
        dxh = dyv * g_ref[...]
        dx = r * (dxh - xh * jnp.mean(dxh * xh, axis=-1, keepdims=True))
        if has_res:
            dx = dx + res_ref[...]
        dx_ref[...] = dx

        @pl.when(pl.program_id(0) == 0)
        def _():
            dg_ref[...] = jnp.zeros_like(dg_ref)

        dg_ref[...] += jnp.sum(dyv * xh, axis=0, keepdims=True)

    in_specs = [pl.BlockSpec((tm, w), lambda i: (i, col)), pl.BlockSpec((1, w), lambda i: (0, 0)), pl.BlockSpec((tm, w), lambda i: (i, 0))]
    args = [x, g.reshape(1, w), dy]
    if has_res:
        in_specs.append(pl.BlockSpec((tm, w), lambda i: (i, 0)))
        args.append(res)
    return pl.pallas_call(
        body,
        out_shape=(jax.ShapeDtypeStruct((t, w), F32), jax.ShapeDtypeStruct((1, w), F32)),
        grid=(t // tm,),
        in_specs=in_specs,
        out_specs=(pl.BlockSpec((tm, w), lambda i: (i, 0)), pl.BlockSpec((1, w), lambda i: (0, 0))),
        compiler_params=_cp("arbitrary"),
        name=name,
    )(*args)


def _final_loss(x, g, target, *, name):
    t, d = x.shape
    tm = _tile(t, 256)

    def body(x_ref, g_ref, t_ref, loss_ref, dx_ref, dg_ref):
        xv = x_ref[...]
        r = lax.rsqrt(jnp.mean(xv * xv, axis=-1, keepdims=True) + RMS_EPS)
        xh = xv * r
        gv = g_ref[...]
        err = xh * gv - t_ref[...]
        dyv = err * (1.0 / d)
        dxh = dyv * gv
        dx_ref[...] = r * (dxh - xh * jnp.mean(dxh * xh, axis=-1, keepdims=True))

        @pl.when(pl.program_id(0) == 0)
        def _():
            dg_ref[...] = jnp.zeros_like(dg_ref)
            loss_ref[...] = jnp.zeros_like(loss_ref)

        dg_ref[...] += jnp.sum(dyv * xh, axis=0, keepdims=True)
        row = jnp.sum(err * err, axis=-1, keepdims=True) * (0.5 / d)
        loss_ref[...] += jnp.sum(row, axis=0, keepdims=True)

    return pl.pallas_call(
        body,
        out_shape=(jax.ShapeDtypeStruct((1, 1), F32), jax.ShapeDtypeStruct((t, d), F32), jax.ShapeDtypeStruct((1, d), F32)),
        grid=(t // tm,),
        in_specs=[pl.BlockSpec((tm, d), lambda i: (i, 0)), pl.BlockSpec((1, d), lambda i: (0, 0)), pl.BlockSpec((tm, d), lambda i: (i, 0))],
        out_specs=(pl.BlockSpec((1, 1), lambda i: (0, 0)), pl.BlockSpec((tm, d), lambda i: (i, 0)), pl.BlockSpec((1, d), lambda i: (0, 0))),
        compiler_params=_cp("arbitrary"),
        name=name,
    )(x, g.reshape(1, d), target)


_GELU_C = math.sqrt(2.0 / math.pi)
_GELU_A = 0.044715


def _gelu(x):
    t = jnp.tanh(_GELU_C * (x + _GELU_A * x * x * x))
    return 0.5 * x * (1.0 + t), t


def _dgelu(x, t):
    return 0.5 * (1.0 + t) + 0.5 * x * (1.0 - t * t) * (_GELU_C * (1.0 + 3.0 * _GELU_A * x * x))


def _spatial_mask():
    ci = lax.broadcasted_iota(jnp.int32, (GMLP_BLOCK, GMLP_BLOCK), 0) // CHUNK
    cj = lax.broadcasted_iota(jnp.int32, (GMLP_BLOCK, GMLP_BLOCK), 1) // CHUNK
    return (cj <= ci).astype(F32)


def _shift_down(x, k):
    rows = lax.broadcasted_iota(jnp.int32, x.shape, 0)
    return jnp.where(rows >= k, pltpu.roll(x, k, 0), 0.0)


def _shift_up(x, k):
    n = x.shape[0]
    rows = lax.broadcasted_iota(jnp.int32, x.shape, 0)
    return jnp.where(rows < n - k, pltpu.roll(x, n - k, 0), 0.0)


def _gmlp_conv_fwd(z, ws, bs, gn, cw, *, seq, name):
    t = z.shape[0]
    g_n = ws.shape[0]
    nb = seq // GMLP_BLOCK
    L = LANES

    def body(z_ref, ws_ref, bs_ref, gn_ref, cw_ref, y_ref):
        u = z_ref[:, 0:L]
        v = z_ref[:, L:2 * L]
        gu, _ = _gelu(u)
        gv, _ = _gelu(v)
        r = lax.rsqrt(jnp.mean(gv * gv, axis=-1, keepdims=True) + RMS_EPS)
        vg = (gv * r * gn_ref[0]).astype(BF16)
        wm = (ws_ref[0] * _spatial_mask()).astype(BF16)
        bias = bs_ref[0]
        for n in range(nb):
            sl = slice(n * GMLP_BLOCK, (n + 1) * GMLP_BLOCK)
            mixed = jnp.dot(wm, vg[sl], preferred_element_type=F32) + bias
            y_ref[sl, 0:L] = (gu[sl] * mixed).astype(y_ref.dtype)
        bg = z_ref[:, 2 * L:3 * L]
        zc = z_ref[:, 3 * L:4 * L] * z_ref[:, 4 * L:5 * L]
        w0, w1, w2 = cw_ref[0, 0:1, :], cw_ref[0, 1:2, :], cw_ref[0, 2:3, :]
        conv = w0 * _shift_down(zc, 2) + w1 * _shift_down(zc, 1) + w2 * zc
        y_ref[:, L:2 * L] = (bg * conv).astype(y_ref.dtype)

    return pl.pallas_call(
        body,
        out_shape=jax.ShapeDtypeStruct((t, 2 * L * g_n), BF16),
        grid=(g_n, t // seq),
        in_specs=[
            pl.BlockSpec((seq, 5 * L), lambda g, b: (b, g)),
            pl.BlockSpec((1, GMLP_BLOCK, GMLP_BLOCK), lambda g, b: (g, 0, 0)),
            pl.BlockSpec((1, GMLP_BLOCK, 1), lambda g, b: (g, 0, 0)),
            pl.BlockSpec((1, 1, L), lambda g, b: (g, 0, 0)),
            pl.BlockSpec((1, 3, L), lambda g, b: (g, 0, 0)),
        ],
        out_specs=pl.BlockSpec((seq, 2 * L), lambda g, b: (b, g)),
        compiler_params=_cp("parallel", "parallel"),
        name=name,
    )(z, ws, bs, gn, cw)


def _gmlp_conv_bwd(z, dy, ws, bs, gn, cw, *, seq, name):
    t = z.shape[0]
    g_n = ws.shape[0]
    nb = seq // GMLP_BLOCK
    L = LANES
    dims_nt = (((1,), (1,)), ((), ()))
    dims_tn = (((0,), (0,)), ((), ()))

    def body(z_ref, dy_ref, ws_ref, bs_ref, gn_ref, cw_ref, dz_ref, dws_ref, dbs_ref, dgn_ref, dcw_ref, dvg_ref):
        @pl.when(pl.program_id(1) == 0)
        def _():
            dws_ref[...] = jnp.zeros_like(dws_ref)
            dbs_ref[...] = jnp.zeros_like(dbs_ref)
            dgn_ref[...] = jnp.zeros_like(dgn_ref)
            dcw_ref[...] = jnp.zeros_like(dcw_ref)

        u = z_ref[:, 0:L]
        v = z_ref[:, L:2 * L]
        gu, tu = _gelu(u)
        gv, tv = _gelu(v)
        r = lax.rsqrt(jnp.mean(gv * gv, axis=-1, keepdims=True) + RMS_EPS)
        gvh = gv * r
        gnv = gn_ref[0]
        vg = (gvh * gnv).astype(BF16)
        mask = _spatial_mask()
        wm = (ws_ref[0] * mask).astype(BF16)
        bias = bs_ref[0]
        dya = dy_ref[:, 0:L].astype(F32)
        dws = jnp.zeros((GMLP_BLOCK, GMLP_BLOCK), F32)
        dbs = jnp.zeros((GMLP_BLOCK, 1), F32)
        for n in range(nb):
            sl = slice(n * GMLP_BLOCK, (n + 1) * GMLP_BLOCK)
            mixed = jnp.dot(wm, vg[sl], preferred_element_type=F32) + bias
            dmixed = dya[sl] * gu[sl]
            dgu = dya[sl] * mixed
            dz_ref[sl, 0:L] = (dgu * _dgelu(u[sl], tu[sl])).astype(dz_ref.dtype)
            dmb = dmixed.astype(BF16)
            dws = dws + lax.dot_general(dmb, vg[sl], dims_nt, preferred_element_type=F32)
            dbs = dbs + jnp.sum(dmixed, axis=1, keepdims=True)
            dvg_ref[sl, :] = lax.dot_general(wm, dmb, dims_tn, preferred_element_type=F32)
        dws_ref[0] += dws * mask
        dbs_ref[0] += dbs
        dvg = dvg_ref[...]
        dgn_ref[0] += jnp.sum(dvg * gvh, axis=0, keepdims=True)
        dvh = dvg * gnv
        dgv = r * (dvh - gvh * jnp.mean(dvh * gvh, axis=-1, keepdims=True))
        dz_ref[:, L:2 * L] = (dgv * _dgelu(v, tv)).astype(dz_ref.dtype)

        bg = z_ref[:, 2 * L:3 * L]
        cg = z_ref[:, 3 * L:4 * L]
        hh = z_ref[:, 4 * L:5 * L]
        zc = cg * hh
        w0, w1, w2 = cw_ref[0, 0:1, :], cw_ref[0, 1:2, :], cw_ref[0, 2:3, :]
        z1 = _shift_down(zc, 1)
        z2 = _shift_down(zc, 2)
        conv = w0 * z2 + w1 * z1 + w2 * zc
        dyb = dy_ref[:, L:2 * L].astype(F32)
        dconv = dyb * bg
        dz_ref[:, 2 * L:3 * L] = (dyb * conv).astype(dz_ref.dtype)
        dzc = w2 * dconv + w1 * _shift_up(dconv, 1) + w0 * _shift_up(dconv, 2)
        dz_ref[:, 3 * L:4 * L] = (dzc * hh).astype(dz_ref.dtype)
        dz_ref[:, 4 * L:5 * L] = (dzc * cg).astype(dz_ref.dtype)
        dcw_ref[0, 0:1, :] += jnp.sum(dconv * z2, axis=0, keepdims=True)
        dcw_ref[0, 1:2, :] += jnp.sum(dconv * z1, axis=0, keepdims=True)
        dcw_ref[0, 2:3, :] += jnp.sum(dconv * zc, axis=0, keepdims=True)

    return pl.pallas_call(
        body,
        out_shape=(
            jax.ShapeDtypeStruct((t, 5 * L * g_n), BF16),
            jax.ShapeDtypeStruct(ws.shape, F32),
            jax.ShapeDtypeStruct(bs.shape, F32),
            jax.ShapeDtypeStruct(gn.shape, F32),
            jax.ShapeDtypeStruct(cw.shape, F32),
        ),
        grid=(g_n, t // seq),
        in_specs=[
            pl.BlockSpec((seq, 5 * L), lambda g, b: (b, g)),
            pl.BlockSpec((seq, 2 * L), lambda g, b: (b, g)),
            pl.BlockSpec((1, GMLP_BLOCK, GMLP_BLOCK), lambda g, b: (g, 0, 0)),
            pl.BlockSpec((1, GMLP_BLOCK, 1), lambda g, b: (g, 0, 0)),
            pl.BlockSpec((1, 1, L), lambda g, b: (g, 0, 0)),
            pl.BlockSpec((1, 3, L), lambda g, b: (g, 0, 0)),
        ],
        out_specs=(
            pl.BlockSpec((seq, 5 * L), lambda g, b: (b, g)),
            pl.BlockSpec((1, GMLP_BLOCK, GMLP_BLOCK), lambda g, b: (g, 0, 0)),
            pl.BlockSpec((1, GMLP_BLOCK, 1), lambda g, b: (g, 0, 0)),
            pl.BlockSpec((1, 1, L), lambda g, b: (g, 0, 0)),
            pl.BlockSpec((1, 3, L), lambda g, b: (g, 0, 0)),
        ),
        scratch_shapes=[pltpu.VMEM((seq, L), F32)],
        compiler_params=_cp("parallel", "arbitrary"),
        name=name,
    )(z, dy, ws, bs, gn, cw)


def _rope_rows(x, c, s1, s2, *, transpose, out_dtype, name):
    t, w = x.shape
    tm = _tile(t, 512)
    only = w == LANES
    fn = _rope_t if transpose else _rope

    def body(x_ref, c_ref, s1_ref, s2_ref, o_ref):
        cv, s1v, s2v = c_ref[...], s1_ref[...], s2_ref[...]
        if only:
            o_ref[...] = fn(x_ref[...].astype(F32), cv, s1v, s2v).astype(o_ref.dtype)
        else:
            for h0 in range(0, w, 2 * LANES):
                o_ref[:, h0:h0 + LANES] = x_ref[:, h0:h0 + LANES].astype(o_ref.dtype)
                o_ref[:, h0 + LANES:h0 + 2 * LANES] = fn(x_ref[:, h0 + LANES:h0 + 2 * LANES].astype(F32), cv, s1v, s2v).astype(o_ref.dtype)

    tab = pl.BlockSpec((tm, LANES), lambda i: (i, 0))
    return pl.pallas_call(
        body,
        out_shape=jax.ShapeDtypeStruct((t, w), out_dtype),
        grid=(t // tm,),
        in_specs=[pl.BlockSpec((tm, w), lambda i: (i, 0)), tab, tab, tab],
        out_specs=pl.BlockSpec((tm, w), lambda i: (i, 0)),
        compiler_params=_cp("parallel"),
        name=name,
    )(x, c, s1, s2)


_NT = (((1,), (1,)), ((), ()))
_TN = (((0,), (0,)), ((), ()))


def _chunk_mask(tq):
    qc = lax.broadcasted_iota(jnp.int32, (tq, tq), 0) // CHUNK
    kc = lax.broadcasted_iota(jnp.int32, (tq, tq), 1) // CHUNK
    return kc <= qc


MLA_BLOCK = 1024


def _mla_fwd(q, kv, kr, *, seq, heads, name):
    t = q.shape[0]
    tq = _tile(seq, MLA_BLOCK)
    nq = seq // tq
    scale = (C_NOPE + C_ROPE) ** -0.5

    def body(q_ref, kv_ref, kr_ref, o_ref, lse_ref, kcat_ref):
        i = pl.program_id(2)

        @pl.when(i == 0)
        def _():
            kcat_ref[:, :LANES] = kv_ref[:, :LANES]
            kcat_ref[:, LANES:] = kr_ref[...]

        qv = q_ref[...]

        def step(j, carry, diagonal):
            m_run, l_run, acc = carry
            rows = pl.ds(pl.multiple_of(j * tq, tq), tq)
            s = lax.dot_general(qv, kcat_ref[rows, :], _NT, preferred_element_type=F32) * scale
            if diagonal:
                s = jnp.where(_chunk_mask(tq), s, NEG_BIG)
            m_new = jnp.maximum(m_run, jnp.max(s, axis=-1, keepdims=True))
            alpha = jnp.exp(m_run - m_new)
            p = jnp.exp(s - m_new)
            l_new = alpha * l_run + jnp.sum(p, axis=-1, keepdims=True)
            acc = alpha * acc + jnp.dot(p.astype(BF16), kv_ref[rows, LANES:], preferred_element_type=F32)
            return m_new, l_new, acc

        init = (jnp.full((tq, 1), NEG_BIG, F32), jnp.zeros((tq, 1), F32), jnp.zeros((tq, C_V), F32))
        carry = lax.fori_loop(0, i, lambda j, c: step(j, c, False), init)
        m_run, l_run, acc = step(i, carry, True)
        o_ref[...] = (acc / l_run).astype(o_ref.dtype)
        lse_ref[...] = m_run + jnp.log(l_run)

    return pl.pallas_call(
        body,
        out_shape=(jax.ShapeDtypeStruct((t, heads * C_V), BF16), jax.ShapeDtypeStruct((heads, t, 1), F32)),
        grid=(t // seq, heads, nq),
        in_specs=[
            pl.BlockSpec((tq, 2 * LANES), lambda b, h, i: (b * nq + i, h)),
            pl.BlockSpec((seq, 2 * LANES), lambda b, h, i: (b, h)),
            pl.BlockSpec((seq, LANES), lambda b, h, i: (b, 0)),
        ],
        out_specs=(
            pl.BlockSpec((tq, C_V), lambda b, h, i: (b * nq + i, h)),
            pl.BlockSpec((None, tq, 1), lambda b, h, i: (h, b * nq + i, 0)),
        ),
        scratch_shapes=[pltpu.VMEM((seq, 2 * LANES), BF16)],
        compiler_params=_cp("parallel", "parallel", "arbitrary"),
        name=name,
    )(q, kv, kr)


def _mla_bwd(q, kv, kr, do, o, lse, rc, rs1, rs2, *, seq, heads, name):
    t = q.shape[0]
    tq = _tile(seq, MLA_BLOCK)
    nq = seq // tq
    scale = (C_NOPE + C_ROPE) ** -0.5

    def body(q_ref, kv_ref, kr_ref, do_ref, o_ref, lse_ref, rc_ref, rs1_ref, rs2_ref, dq_ref, dkv_ref, dkr_ref, kcat_ref, dk_ref, dv_ref):
        h, i = pl.program_id(1), pl.program_id(2)

        @pl.when(i == 0)
        def _():
            kcat_ref[:, :LANES] = kv_ref[:, :LANES]
            kcat_ref[:, LANES:] = kr_ref[...]
            dk_ref[...] = jnp.zeros_like(dk_ref)
            dv_ref[...] = jnp.zeros_like(dv_ref)

        @pl.when((i == 0) & (h == 0))
        def _():
            dkr_ref[...] = jnp.zeros_like(dkr_ref)

        qv = q_ref[...]
        dov = do_ref[...]
        lse_v = lse_ref[...]
        delta = jnp.sum(dov.astype(F32) * o_ref[...].astype(F32), axis=-1, keepdims=True)

        def step(j, dq, diagonal):
            rows = pl.ds(pl.multiple_of(j * tq, tq), tq)
            kj = kcat_ref[rows, :]
            s = lax.dot_general(qv, kj, _NT, preferred_element_type=F32) * scale
            if diagonal:
                s = jnp.where(_chunk_mask(tq), s, NEG_BIG)
            p = jnp.exp(s - lse_v)
            dp = lax.dot_general(dov, kv_ref[rows, LANES:], _NT, preferred_element_type=F32)
            ds = (p * (dp - delta) * scale).astype(BF16)
            dk_ref[rows, :] += lax.dot_general(ds, qv, _TN, preferred_element_type=F32)
            dv_ref[rows, :] += lax.dot_general(p.astype(BF16), dov, _TN, preferred_element_type=F32)
            return dq + jnp.dot(ds, kj, preferred_element_type=F32)

        dq = lax.fori_loop(0, i, lambda j, c: step(j, c, False), jnp.zeros((tq, 2 * LANES), F32))
        dq = step(i, dq, True)
        dq_ref[:, :LANES] = dq[:, :LANES].astype(dq_ref.dtype)
        dq_ref[:, LANES:] = _rope_t(dq[:, LANES:], rc_ref[...], rs1_ref[...], rs2_ref[...]).astype(dq_ref.dtype)

        @pl.when(i == nq - 1)
        def _():
            dkv_ref[:, :LANES] = dk_ref[:, :LANES].astype(dkv_ref.dtype)
            dkv_ref[:, LANES:] = dv_ref[...].astype(dkv_ref.dtype)
            dkr_ref[...] += dk_ref[:, LANES:]

    return pl.pallas_call(
        body,
        out_shape=(
            jax.ShapeDtypeStruct((t, heads * 2 * LANES), BF16),
            jax.ShapeDtypeStruct((t, heads * 2 * LANES), BF16),
            jax.ShapeDtypeStruct((t, LANES), F32),
        ),
        grid=(t // seq, heads, nq),
        in_specs=[
            pl.BlockSpec((tq, 2 * LANES), lambda b, h, i: (b * nq + i, h)),
            pl.BlockSpec((seq, 2 * LANES), lambda b, h, i: (b, h)),
            pl.BlockSpec((seq, LANES), lambda b, h, i: (b, 0)),
            pl.BlockSpec((tq, C_V), lambda b, h, i: (b * nq + i, h)),
            pl.BlockSpec((tq, C_V), lambda b, h, i: (b * nq + i, h)),
            pl.BlockSpec((None, tq, 1), lambda b, h, i: (h, b * nq + i, 0)),
            pl.BlockSpec((tq, LANES), lambda b, h, i: (b * nq + i, 0)),
            pl.BlockSpec((tq, LANES), lambda b, h, i: (b * nq + i, 0)),
            pl.BlockSpec((tq, LANES), lambda b, h, i: (b * nq + i, 0)),
        ],
        out_specs=(
            pl.BlockSpec((tq, 2 * LANES), lambda b, h, i: (b * nq + i, h)),
            pl.BlockSpec((seq, 2 * LANES), lambda b, h, i: (b, h)),
            pl.BlockSpec((seq, LANES), lambda b, h, i: (b, 0)),
        ),
        scratch_shapes=[pltpu.VMEM((seq, 2 * LANES), BF16), pltpu.VMEM((seq, 2 * LANES), F32), pltpu.VMEM((seq, C_V), F32)],
        compiler_params=_cp("parallel", "arbitrary", "arbitrary"),
        name=name,
    )(q, kv, kr, do, o, lse, rc, rs1, rs2)


def _softmax_rows(s):
    e = jnp.exp(s - jnp.max(s, axis=-1, keepdims=True))
    return e / jnp.sum(e, axis=-1, keepdims=True)


def _mem_fwd(q, k, v, *, seq, n_mem, name):
    t, d = q.shape
    hd = d // MEM_HEADS
    tq = _tile(seq, 512)
    nq = seq // tq
    scale = hd ** -0.5

    def body(q_ref, k_ref, v_ref, o_ref):
        s = lax.dot_general(q_ref[...], k_ref[...], _NT, preferred_element_type=F32) * scale
        p = _softmax_rows(s)
        o_ref[...] = jnp.dot(p.astype(BF16), v_ref[...], preferred_element_type=F32).astype(o_ref.dtype)

    return pl.pallas_call(
        body,
        out_shape=jax.ShapeDtypeStruct((t, d), BF16),
        grid=(t // seq, MEM_HEADS, nq),
        in_specs=[
            pl.BlockSpec((tq, hd), lambda b, h, i: (b * nq + i, h)),
            pl.BlockSpec((n_mem, hd), lambda b, h, i: (b, h)),
            pl.BlockSpec((n_mem, hd), lambda b, h, i: (b, h)),
        ],
        out_specs=pl.BlockSpec((tq, hd), lambda b, h, i: (b * nq + i, h)),
        compiler_params=_cp("parallel", "parallel", "arbitrary"),
        name=name,
    )(q, k, v)


def _mem_bwd(q, k, v, do, *, seq, n_mem, name):
    t, d = q.shape
    hd = d // MEM_HEADS
    tq = _tile(seq, 512)
    nq = seq // tq
    scale = hd ** -0.5

    def body(q_ref, k_ref, v_ref, do_ref, dq_ref, dk_ref, dv_ref):
        @pl.when(pl.program_id(2) == 0)
        def _():
            dk_ref[...] = jnp.zeros_like(dk_ref)
            dv_ref[...] = jnp.zeros_like(dv_ref)

        qv, kv_, vv, dov = q_ref[...], k_ref[...], v_ref[...], do_ref[...]
        p = _softmax_rows(lax.dot_general(qv, kv_, _NT, preferred_element_type=F32) * scale)
        dp = lax.dot_general(dov, vv, _NT, preferred_element_type=F32)
        ds = (p * (dp - jnp.sum(dp * p, axis=-1, keepdims=True)) * scale).astype(BF16)
        dq_ref[...] = jnp.dot(ds, kv_, preferred_element_type=F32).astype(dq_ref.dtype)
        dk_ref[...] += lax.dot_general(ds, qv, _TN, preferred_element_type=F32)
        dv_ref[...] += lax.dot_general(p.astype(BF16), dov, _TN, preferred_element_type=F32)

    return pl.pallas_call(
        body,
        out_shape=(
            jax.ShapeDtypeStruct((t, d), BF16),
            jax.ShapeDtypeStruct(k.shape, F32),
            jax.ShapeDtypeStruct(v.shape, F32),
        ),
        grid=(t // seq, MEM_HEADS, nq),
        in_specs=[
            pl.BlockSpec((tq, hd), lambda b, h, i: (b * nq + i, h)),
            pl.BlockSpec((n_mem, hd), lambda b, h, i: (b, h)),
            pl.BlockSpec((n_mem, hd), lambda b, h, i: (b, h)),
            pl.BlockSpec((tq, hd), lambda b, h, i: (b * nq + i, h)),
        ],
        out_specs=(
            pl.BlockSpec((tq, hd), lambda b, h, i: (b * nq + i, h)),
            pl.BlockSpec((n_mem, hd), lambda b, h, i: (b, h)),
            pl.BlockSpec((n_mem, hd), lambda b, h, i: (b, h)),
        ),
        compiler_params=_cp("parallel", "parallel", "arbitrary"),
        name=name,
    )(q, k, v, do)


def _me():
    return lax.axis_index("x"), lax.axis_index("y"), lax.axis_index("c")


def _dev_index(px, py, pc):
    return 4 * px + 2 * py + pc


def _chunk(ref, idx, axis, size):
    start = pl.multiple_of(idx * size, size)
    return ref.at[pl.ds(start, size), :] if axis == 0 else ref.at[:, pl.ds(start, size)]


def _all_gather_small(v):
    m, n = v.shape

    def body(x_ref, out_ref, send_sems, recv_sems, local_sem):
        x, y, c = _me()
        me, sibling = (x, y, c), (x, y, 1 - c)
        chips = [(1 - x, y), (x, 1 - y), (1 - x, 1 - y)]

        def rows(px, py, pc):
            return out_ref.at[_dev_index(px, py, pc)]

        def copy(k, block, to, src=None):
            return pltpu.make_async_remote_copy(
                src_ref=rows(*block) if src is None else src, dst_ref=rows(*block),
                send_sem=send_sems.at[k], recv_sem=recv_sems.at[k], device_id=to, device_id_type=MESH)

        mine = pltpu.make_async_copy(x_ref, rows(*me), local_sem)
        mine.start()
        first = [copy(0, me, sibling, src=x_ref)]
        first += [copy(1 + j, me, (*chip, c), src=x_ref) for j, chip in enumerate(chips)]
        for cp in first:
            cp.start()
        passed = [copy(4 + j, (*chip, c), sibling) for j, chip in enumerate(chips)]
        for j, chip in enumerate(chips):
            copy(1 + j, (*chip, c), me).wait_recv()
            passed[j].start()
        copy(0, sibling, me).wait_recv()
        for j, chip in enumerate(chips):
            copy(4 + j, (*chip, 1 - c), me).wait_recv()
        for cp in first + passed:
            cp.wait_send()
        mine.wait()

    return pl.pallas_call(
        body,
        out_shape=jax.ShapeDtypeStruct((N_DEV, m, n), v.dtype),
        in_specs=[pl.BlockSpec(memory_space=pltpu.VMEM)],
        out_specs=pl.BlockSpec(memory_space=pltpu.VMEM),
        scratch_shapes=[pltpu.SemaphoreType.DMA((7,)), pltpu.SemaphoreType.DMA((7,)), pltpu.SemaphoreType.DMA],
        name="all_gather_small",
    )(v)


def _cast_place(w, layer, axis, dev_arr, *, name):
    _, r, c = w.shape
    tr = _tile(r, 256)
    nr = r // tr
    full = (N_DEV * r, c) if axis == 0 else (r, N_DEV * c)

    def body(dev_ref, w_ref, o_ref):
        o_ref[...] = w_ref[...].astype(o_ref.dtype)

    if axis == 0:
        out_spec = pl.BlockSpec((tr, c), lambda i, dev: (dev[0] * nr + i, 0))
    else:
        out_spec = pl.BlockSpec((tr, c), lambda i, dev: (i, dev[0]))
    return pl.pallas_call(
        body,
        out_shape=jax.ShapeDtypeStruct(full, BF16),
        grid_spec=pltpu.PrefetchScalarGridSpec(
            num_scalar_prefetch=1, grid=(nr,), in_specs=[pl.BlockSpec((None, tr, c), lambda i, dev: (layer, i, 0))], out_specs=out_spec),
        compiler_params=_cp("parallel"),
        name=name,
    )(dev_arr, w)


_ANY = pl.BlockSpec(memory_space=pl.ANY)
_SEM = pl.BlockSpec(memory_space=pltpu.SEMAPHORE)
_EFFECT = pltpu.SideEffectType.DATAFLOW_SIDE_EFFECTING
N_PEER_CHIPS = 3


def _other_chips(x, y):
    return [((1 - x if (j >> 1) & 1 else x), (1 - y if j & 1 else y)) for j in range(1, 4)]


def _remote(src, dst, send_sem, recv_sem, to):
    return pltpu.make_async_remote_copy(src_ref=src, dst_ref=dst, send_sem=send_sem, recv_sem=recv_sem, device_id=to, device_id_type=MESH)


def _plan_gather_ici(axes, sizes):
    def plan(refs, send_sems, recv_sems):
        x, y, c = _me()
        out = []
        for t in range(len(axes)):
            for j, chip in enumerate(_other_chips(x, y)):
                mine = _chunk(refs[t], _dev_index(x, y, c), axes[t], sizes[t])
                theirs = _chunk(refs[t], _dev_index(*chip, c), axes[t], sizes[t])
                out.append((_remote(mine, mine, send_sems[j], recv_sems[j], (*chip, c)), _remote(theirs, theirs, send_sems[j], recv_sems[j], (*chip, c))))
        return out
    return plan


def _plan_gather_d2d(axes, sizes):
    def plan(refs, send_sems, recv_sems):
        x, y, c = _me()
        out = []
        for t in range(len(axes)):
            for k in range(4):
                have = _chunk(refs[t], 2 * k + c, axes[t], sizes[t])
                get = _chunk(refs[t], 2 * k + 1 - c, axes[t], sizes[t])
                out.append((_remote(have, have, send_sems[0], recv_sems[0], (x, y, 1 - c)), _remote(get, get, send_sems[0], recv_sems[0], (x, y, 1 - c))))
        return out
    return plan


def _plan_scatter_d2d(axes, sizes):
    n = len(axes)

    def plan(refs, send_sems, recv_sems):
        x, y, c = _me()
        out = []
        for t in range(n):
            for k in range(4):
                give = _chunk(refs[t], 2 * k + 1 - c, axes[t], sizes[t])
                land = _chunk(refs[n + t], k, axes[t], sizes[t])
                cp = _remote(give, land, send_sems[0], recv_sems[0], (x, y, 1 - c))
                out.append((cp, cp))
        return out
    return plan


def _plan_scatter_ici(axes, sizes):
    n = len(axes)

    def plan(refs, send_sems, recv_sems):
        x, y, c = _me()
        out = []
        for t in range(n):
            for j, (px, py) in enumerate(_other_chips(x, y)):
                send = _remote(_chunk(refs[t], 2 * px + py, axes[t], sizes[t]), _chunk(refs[n + t], 2 * x + y, axes[t], sizes[t]),
                               send_sems[j], recv_sems[j], (px, py, c))
                recv = _remote(_chunk(refs[t], 2 * x + y, axes[t], sizes[t]), _chunk(refs[n + t], 2 * px + py, axes[t], sizes[t]),
                               send_sems[j], recv_sems[j], (px, py, c))
                out.append((send, recv))
        return out
    return plan


def _exchange_start(arrays, n_sems, plan, after, *, name):
    n = len(arrays)
    m = len(after)

    def body(*refs):
        send_sems, recv_sems = list(refs[n + m:n + m + n_sems]), list(refs[n + m + n_sems:n + m + 2 * n_sems])
        for send, _ in plan(refs[:n], send_sems, recv_sems):
            send.start()
        refs[-1][...] = jnp.zeros_like(refs[-1])

    outs = pl.pallas_call(
        body,
        out_shape=(*([pltpu.SemaphoreType.DMA(())] * (2 * n_sems)),
                   *[jax.ShapeDtypeStruct(a.shape, a.dtype) for a in arrays], jax.ShapeDtypeStruct((8, LANES), F32)),
        in_specs=[_ANY] * (n + m),
        out_specs=(*([_SEM] * (2 * n_sems)), *([_ANY] * n), pl.BlockSpec(memory_space=pltpu.VMEM)),
        input_output_aliases={t: 2 * n_sems + t for t in range(n)},
        compiler_params=pltpu.CompilerParams(has_side_effects=_EFFECT),
        name=name,
    )(*arrays, *after)
    return list(outs[:2 * n_sems]), list(outs[2 * n_sems:2 * n_sems + n]), outs[-1]


def _exchange_wait(sems, arrays, plan, after, *, name):
    n = len(arrays)
    n_sems = len(sems) // 2

    def body(*refs):
        send_sems, recv_sems = list(refs[n:n + n_sems]), list(refs[n + n_sems:n + 2 * n_sems])
        for send, recv in plan(refs[:n], send_sems, recv_sems):
            send.wait_send()
            recv.wait_recv()
        refs[-1][...] = jnp.zeros_like(refs[-1])

    outs = pl.pallas_call(
        body,
        out_shape=(*[jax.ShapeDtypeStruct(a.shape, a.dtype) for a in arrays], jax.ShapeDtypeStruct((8, LANES), F32)),
        in_specs=[_ANY] * n + [_SEM] * (2 * n_sems) + [_ANY] * len(after),
        out_specs=(*([_ANY] * n), pl.BlockSpec(memory_space=pltpu.VMEM)),
        input_output_aliases={t: t for t in range(n)},
        compiler_params=pltpu.CompilerParams(has_side_effects=_EFFECT),
        name=name,
    )(*arrays, *sems, *after)
    return list(outs[:n]), outs[-1]


def _add_pair(g, got, axis, core_arr, *, name):
    r, c = got.shape
    if axis == 0:
        rows, cols = r // 4, c
    else:
        rows, cols = r, c // 4
    tr = _tile(rows, 512)
    nr = rows // tr

    def body(core_ref, g_ref, got_ref, o_ref):
        o_ref[...] = (g_ref[...].astype(F32) + got_ref[...].astype(F32)).astype(o_ref.dtype)

    if axis == 0:
        g_spec = pl.BlockSpec((tr, cols), lambda k, i, core: ((2 * k + core[0]) * nr + i, 0))
        spec = pl.BlockSpec((tr, cols), lambda k, i, core: (k * nr + i, 0))
    else:
        g_spec = pl.BlockSpec((tr, cols), lambda k, i, core: (i, 2 * k + core[0]))
        spec = pl.BlockSpec((tr, cols), lambda k, i, core: (i, k))
    return pl.pallas_call(
        body,
        out_shape=jax.ShapeDtypeStruct(got.shape, got.dtype),
        grid_spec=pltpu.PrefetchScalarGridSpec(num_scalar_prefetch=1, grid=(4, nr), in_specs=[g_spec, spec], out_specs=spec),
        compiler_params=_cp("parallel", "parallel"),
        name=name,
    )(core_arr, g, got)


def _adam_math(w, g, m, v):
    m = ADAM_B1 * m + (1.0 - ADAM_B1) * g
    v = ADAM_B2 * v + (1.0 - ADAM_B2) * (g * g)
    m_hat = m / (1.0 - ADAM_B1 ** ADAM_STEP)
    v_hat = v / (1.0 - ADAM_B2 ** ADAM_STEP)
    delta = -ADAM_LR * (m_hat / (jnp.sqrt(v_hat) + ADAM_EPS) + ADAM_WD * w)
    return delta, m, v


def _adamw(w, m, v, parts, axis, chip_arr, layers, prev, after, *, name):
    n_l, r, c = w.shape
    first, count = layers
    tr = _tile(r, 128)
    nr = r // tr
    n_chip = 4
    n_in = 4 + n_chip * count

    def body(*refs):
        w_ref, m_ref, v_ref = refs[1:4]
        p_refs = refs[4:n_in]
        g_ref, d_ref, nm_ref, nv_ref = refs[-4:]
        for l in range(count):
            @pl.when(pl.program_id(0) == l)
            def _(l=l):
                g = p_refs[n_chip * l][...].astype(F32)
                for k in range(1, n_chip):
                    g = g + p_refs[n_chip * l + k][...].astype(F32)
                delta, nm, nv = _adam_math(w_ref[...], g, m_ref[...], v_ref[...])
                g_ref[...] = g
                d_ref[...] = delta
                nm_ref[...] = nm
                nv_ref[...] = nv

    def part_spec(ll, k):
        if axis == 0:
            return pl.BlockSpec((tr, c), lambda l, i, chips: (chips[k] * nr + jnp.where(l == ll, i, 0), 0))
        return pl.BlockSpec((tr, c), lambda l, i, chips: (jnp.where(l == ll, i, 0), chips[k]))

    wspec = pl.BlockSpec((None, tr, c), lambda l, i, chips: (first + l, i, 0))
    pspecs = [part_spec(ll, k) for ll in range(count) for k in range(n_chip)]
    sds = jax.ShapeDtypeStruct(w.shape, F32)
    kept = [] if prev is None else list(prev)
    return pl.pallas_call(
        body,
        out_shape=(sds, sds, sds, sds),
        grid_spec=pltpu.PrefetchScalarGridSpec(
            num_scalar_prefetch=1, grid=(count, nr),
            in_specs=[wspec, wspec, wspec] + pspecs + [_ANY] * (1 + len(kept)), out_specs=(wspec, wspec, wspec, wspec)),
        input_output_aliases={n_in + 1 + k: k for k in range(len(kept))},
        compiler_params=_cp("arbitrary", "arbitrary"),
        name=name,
    )(chip_arr, w, m, v, *[parts[first + ll][0 if k == 0 else 1] for ll in range(count) for k in range(n_chip)], after, *kept)


def _adamw_small(w, m, v, parts, *, name):
    r, c = w.shape
    tr = _tile(r, 512)

    def body(w_ref, m_ref, v_ref, p_ref, g_ref, d_ref, nm_ref, nv_ref):
        g = p_ref[0]
        for s in range(1, N_DEV):
            g = g + p_ref[s]
        delta, nm, nv = _adam_math(w_ref[...], g, m_ref[...], v_ref[...])
        g_ref[...] = g
        d_ref[...] = delta
        nm_ref[...] = nm
        nv_ref[...] = nv

    spec = pl.BlockSpec((tr, c), lambda i: (i, 0))
    sds = jax.ShapeDtypeStruct(w.shape, F32)
    return pl.pallas_call(
        body,
        out_shape=(sds, sds, sds, sds),
        grid=(r // tr,),
        in_specs=[spec, spec, spec, pl.BlockSpec((N_DEV, tr, c), lambda i: (0, i, 0))],
        out_specs=(spec, spec, spec, spec),
        compiler_params=_cp("parallel"),
        name=name,
    )(w, m, v, parts)


def _pack(arrs):
    flat = jnp.concatenate([a.reshape(-1).astype(F32) for a in arrs])
    pad = (-flat.shape[0]) % (8 * LANES)
    return jnp.pad(flat, (0, pad)).reshape(-1, LANES)


def _unpack(packed, shapes):
    flat = packed.reshape(-1)
    out, off = [], 0
    for sh in shapes:
        n = math.prod(sh)
        out.append(flat[off:off + n].reshape(sh))
        off += n
    return out


def _unpack_gathered(packed, shapes, axis):
    flat = packed.reshape(N_DEV, -1)
    out, off = [], 0
    for sh, ax in zip(shapes, axis):
        n = math.prod(sh)
        blocks = flat[:, off:off + n].reshape((N_DEV, *sh))
        out.append(jnp.concatenate([blocks[d] for d in range(N_DEV)], axis=ax))
        off += n
    return out


def _my_block(full, axis, size):
    idx = _dev_index(*_me())
    return lax.dynamic_slice_in_dim(full, idx * size, size, axis)


def kernel(x, mem, positions, norm_mix_g, norm_mem_q_g, norm_mem_kv_g, norm_ffn_g, final_norm_g, ab_w_in, a_v_norm_g, a_w_s, a_b_s, b_conv_w, ab_w_out, c_w_in, c_q_norm_g, c_kv_norm_g, c_w_uq, c_w_ukv, c_w_out, m_wq, m_wk, m_wv, m_wo, f_w1, f_w2, loss_target, m_norm_mix_g, m_norm_mem_q_g, m_norm_mem_kv_g, m_norm_ffn_g, m_final_norm_g, m_ab_w_in, m_a_v_norm_g, m_a_w_s, m_a_b_s, m_b_conv_w, m_ab_w_out, m_c_w_in, m_c_q_norm_g, m_c_kv_norm_g, m_c_w_uq, m_c_w_ukv, m_c_w_out, m_m_wq, m_m_wk, m_m_wv, m_m_wo, m_f_w1, m_f_w2, v_norm_mix_g, v_norm_mem_q_g, v_norm_mem_kv_g, v_norm_ffn_g, v_final_norm_g, v_ab_w_in, v_a_v_norm_g, v_a_w_s, v_a_b_s, v_b_conv_w, v_ab_w_out, v_c_w_in, v_c_q_norm_g, v_c_kv_norm_g, v_c_w_uq, v_c_w_ukv, v_c_w_out, v_m_wq, v_m_wk, v_m_wv, v_m_wo, v_f_w1, v_f_w2):
    bl, seq, d = x.shape
    t = bl * seq
    n_mem = mem.shape[1]
    depth = norm_mix_g.shape[0]
    n_even, n_odd = ab_w_in.shape[0], c_w_in.shape[0]
    a_width = a_v_norm_g.shape[1]
    groups = a_w_s.shape[1]
    heads = d // C_NOPE
    q_rank = c_q_norm_g.shape[1] * N_DEV
    kv_rank = c_kv_norm_g.shape[1] * N_DEV

    big = {
        "ab_w_in": (ab_w_in, m_ab_w_in, v_ab_w_in, 1), "ab_w_out": (ab_w_out, m_ab_w_out, v_ab_w_out, 0),
        "c_w_in": (c_w_in, m_c_w_in, v_c_w_in, 0), "c_w_uq": (c_w_uq, m_c_w_uq, v_c_w_uq, 1),
        "c_w_ukv": (c_w_ukv, m_c_w_ukv, v_c_w_ukv, 1), "c_w_out": (c_w_out, m_c_w_out, v_c_w_out, 0),
        "m_wq": (m_wq, m_m_wq, v_m_wq, 0), "m_wk": (m_wk, m_m_wk, v_m_wk, 0), "m_wv": (m_wv, m_m_wv, v_m_wv, 0),
        "m_wo": (m_wo, m_m_wo, v_m_wo, 0), "f_w1": (f_w1, m_f_w1, v_f_w1, 1), "f_w2": (f_w2, m_f_w2, v_f_w2, 0),
    }
    even_names = ["ab_w_in", "ab_w_out"]
    odd_names = ["c_w_in", "c_w_uq", "c_w_ukv", "c_w_out"]
    all_names = ["m_wq", "m_wk", "m_wv", "m_wo", "f_w1", "f_w2"]

    def layer_names(layer):
        return (even_names if layer % 2 == 0 else odd_names) + all_names

    def sub(name, layer):
        return layer // 2 if name in even_names or name in odd_names else layer

    split_small = [(c_q_norm_g, m_c_q_norm_g, v_c_q_norm_g, 1), (c_kv_norm_g, m_c_kv_norm_g, v_c_kv_norm_g, 1), (b_conv_w, m_b_conv_w, v_b_conv_w, 2)]
    shard_shapes = [s[0].shape for s in split_small] * 3
    shard_axes = [s[3] for s in split_small] * 3
    gathered = _all_gather_small(_pack([s[0] for s in split_small] + [s[1] for s in split_small] + [s[2] for s in split_small]))
    whole = _unpack_gathered(gathered, shard_shapes, shard_axes)
    q_norm_full, kv_norm_full, conv_full = whole[0:3]
    small_w = [norm_mix_g, norm_mem_q_g, norm_mem_kv_g, norm_ffn_g, final_norm_g, a_v_norm_g, a_w_s, a_b_s] + whole[0:3]
    small_m = [m_norm_mix_g, m_norm_mem_q_g, m_norm_mem_kv_g, m_norm_ffn_g, m_final_norm_g, m_a_v_norm_g, m_a_w_s, m_a_b_s] + whole[3:6]
    small_v = [v_norm_mix_g, v_norm_mem_q_g, v_norm_mem_kv_g, v_norm_ffn_g, v_final_norm_g, v_a_v_norm_g, v_a_w_s, v_a_b_s] + whole[6:9]

    inv = ROPE_THETA ** (-jnp.arange(0, C_ROPE, 2, dtype=F32) / C_ROPE)
    ang = positions.astype(F32).reshape(t, 1) * inv
    cos, sin, zero = jnp.cos(ang), jnp.sin(ang), jnp.zeros((t, C_ROPE // 2), F32)
    rc = jnp.concatenate([cos, cos, zero, zero], axis=1)
    rs1 = jnp.concatenate([-sin, zero, zero, zero], axis=1)
    rs2 = jnp.concatenate([zero, sin, zero, zero], axis=1)

    my_x, my_y, my_c = _me()
    my_chip = 2 * my_x + my_y
    dev_arr = jnp.stack([_dev_index(my_x, my_y, my_c)]).astype(jnp.int32)
    core_arr = jnp.stack([my_c]).astype(jnp.int32)
    chip_arr = jnp.stack([my_chip, my_chip ^ 1, my_chip ^ 2, my_chip ^ 3]).astype(jnp.int32)

    n_stage = 3 * depth
    chain = [None]
    gat = {}
    issued = [0]

    def stage_names(i):
        layer, gid = divmod(i, 3)
        return layer, [even_names if layer % 2 == 0 else odd_names, all_names[:4], all_names[4:]][gid]

    def order(after):
        return [a for a in (chain[0], after) if a is not None]

    def gather_ici_start(i, after):
        layer, names = stage_names(i)
        placed = [_cast_place(big[nm][0], sub(nm, layer), big[nm][3], dev_arr, name="cast_place") for nm in names]
        axes = [big[nm][3] for nm in names]
        sizes = [p.shape[a] // N_DEV for p, a in zip(placed, axes)]
        st = dict(names=names, ici=_plan_gather_ici(axes, sizes), d2d=_plan_gather_d2d(axes, sizes))
        st["sems"], st["arrays"], chain[0] = _exchange_start(placed, N_PEER_CHIPS, st["ici"], order(after), name=f"gather_ici_start_s{i}")
        gat[i] = st

    def gather_stage(i, after):
        while issued[0] < min(i + 3, n_stage):
            gather_ici_start(issued[0], after)
            issued[0] += 1
        ahead = [i, i + 1] if i >= 3 and i + 1 < n_stage else [i]
        for j in ahead:
            st = gat[j]
            if "landed" not in st:
                st["arrays"], chain[0] = _exchange_wait(st["sems"], st["arrays"], st["ici"], order(after), name=f"gather_ici_wait_s{j}")
                st["landed"] = True
            if "sems2" not in st:
                st["sems2"], st["arrays"], chain[0] = _exchange_start(st["arrays"], 1, st["d2d"], order(after), name=f"gather_d2d_start_s{j}")
        st = gat[i]
        st["arrays"], chain[0] = _exchange_wait(st["sems2"], st["arrays"], st["d2d"], order(after), name=f"gather_d2d_wait_s{i}")
        return dict(zip(st["names"], st["arrays"]))

    wts = []

    def group_cols(w_in):
        return w_in.reshape(d, 5, groups, LANES).transpose(0, 2, 1, 3).reshape(d, 5 * a_width)

    def ungroup_cols(g_in):
        return g_in.reshape(d, groups, 5, LANES).transpose(0, 2, 1, 3).reshape(d, 5 * a_width)

    def group_rows(w_out):
        return w_out.reshape(2, groups, LANES, d).transpose(1, 0, 2, 3).reshape(2 * a_width, d)

    def ungroup_rows(g_out):
        return g_out.reshape(groups, 2, LANES, d).transpose(1, 0, 2, 3).reshape(2 * a_width, d)

    def pad_heads(w_uq):
        w3 = w_uq.reshape(q_rank, heads, C_NOPE + C_ROPE)
        return jnp.pad(w3, ((0, 0), (0, 0), (0, LANES - C_ROPE))).reshape(q_rank, heads * 2 * LANES)

    def unpad_heads(g_uq):
        return g_uq.reshape(q_rank, heads, 2 * LANES)[:, :, :C_NOPE + C_ROPE].reshape(q_rank, heads * (C_NOPE + C_ROPE))

    x2 = x.reshape(t, d)
    mem2 = mem.reshape(bl * n_mem, d)
    saved = []

    for layer in range(depth):
        w = gather_stage(3 * layer, x2 if layer else gathered)
        wts.append(w)
        sv = {"x0": x2}
        xn = _rmsnorm_fwd(x2, norm_mix_g[layer], name="norm_mix")
        sv["xn"] = xn
        if layer % 2 == 0:
            e = layer // 2
            w_in_g = group_cols(w["ab_w_in"])
            w_out_g = group_rows(w["ab_w_out"])
            z = _mm(xn, w_in_g, out_dtypes=(F32,), extras=[(chain[0], "after")], name="ab_in")
            ws_e = a_w_s[e]
            bs_e = a_b_s[e].reshape(groups, GMLP_BLOCK, 1)
            gn_e = a_v_norm_g[e].reshape(groups, 1, LANES)
            cw_e = conv_full[e].reshape(3, groups, LANES).transpose(1, 0, 2)
            ycat = _gmlp_conv_fwd(z, ws_e, bs_e, gn_e, cw_e, seq=seq, name="gmlp_conv_fwd")
            x2 = _mm(ycat, w_out_g, out_dtypes=(F32,), extras=[(x2, "mn")], epilogue=_epi_add, name="ab_out")
            sv.update(z=z, ycat=ycat, w_in_g=w_in_g, w_out_g=w_out_g, ws=ws_e, bs=bs_e, gn=gn_e, cw=cw_e)
        else:
            o = layer // 2
            w_uq_p = pad_heads(w["c_w_uq"])
            z = _mm(xn, w["c_w_in"], out_dtypes=(F32,), extras=[(chain[0], "after")], name="c_in")
            qn = _rmsnorm_fwd(z, q_norm_full[o], width=q_rank, col=0, name="norm_cq")
            kvn = _rmsnorm_fwd(z, kv_norm_full[o], width=kv_rank, col=q_rank // kv_rank, name="norm_ckv")
            qh = _mm(qn, w_uq_p, extras=[(rc, "m"), (rs1, "m"), (rs2, "m")], epilogue=_epi_rope_heads, tn=1024, name="c_uq")
            kvh = _mm(kvn, w["c_w_ukv"], name="c_ukv")
            k_rope = jnp.pad(z[:, q_rank + kv_rank:], ((0, 0), (0, LANES - C_ROPE)))
            kr = _rope_rows(k_rope, rc, rs1, rs2, transpose=False, out_dtype=BF16, name="rope_k")
            att, lse = _mla_fwd(qh, kvh, kr, seq=seq, heads=heads, name="mla_fwd")
            x2 = _mm(att, w["c_w_out"], out_dtypes=(F32,), extras=[(x2, "mn")], epilogue=_epi_add, name="c_out")
            sv.update(z=z, qn=qn, kvn=kvn, qh=qh, kvh=kvh, kr=kr, att=att, lse=lse, w_uq_p=w_uq_p)
        sv["x1"] = x2
        w.update(gather_stage(3 * layer + 1, x2))
        xq = _rmsnorm_fwd(x2, norm_mem_q_g[layer], name="norm_mem_q")
        memn = _rmsnorm_fwd(mem2, norm_mem_kv_g[layer], name="norm_mem_kv")
        mq = _mm(xq, w["m_wq"], extras=[(chain[0], "after")], name="m_q")
        mk = _mm(memn, w["m_wk"], name="m_k")
        mv = _mm(memn, w["m_wv"], name="m_v")
        mo = _mem_fwd(mq, mk, mv, seq=seq, n_mem=n_mem, name="mem_fwd")
        x2 = _mm(mo, w["m_wo"], out_dtypes=(F32,), extras=[(x2, "mn")], epilogue=_epi_add, name="m_o")
        sv.update(xq=xq, memn=memn, mq=mq, mk=mk, mv=mv, mo=mo)
        sv["x2"] = x2
        w.update(gather_stage(3 * layer + 2, x2))
        xf = _rmsnorm_fwd(x2, norm_ffn_g[layer], name="norm_ffn")
        hpre, hact = _mm(xf, w["f_w1"], out_dtypes=(BF16, BF16), extras=[(chain[0], "after")], epilogue=_epi_relu2, name="f_1")
        x2 = _mm(hact, w["f_w2"], out_dtypes=(F32,), extras=[(x2, "mn")], epilogue=_epi_add, name="f_2")
        sv.update(xf=xf, hpre=hpre, hact=hact)
        saved.append(sv)

    loss_part, dx, dg_final = _final_loss(x2, final_norm_g, loss_target.reshape(t, d), name="final_loss")

    dg_mix, dg_mq, dg_mkv, dg_ffn = [None] * depth, [None] * depth, [None] * depth, [None] * depth
    d_av, d_ws, d_bs = [None] * n_even, [None] * n_even, [None] * n_even
    d_conv, d_qn, d_kvn = [None] * n_even, [None] * n_odd, [None] * n_odd
    parts = {nm: [None] * big[nm][0].shape[0] for nm in big}

    sca = []

    def scatter_finish_d2d(st, after):
        n = len(st["names"])
        arrays, chain[0] = _exchange_wait(st["sems"], st["arrays"], st["d2d"], order(after), name=f"scatter_d2d_wait_s{st['id']}")
        sums = [_add_pair(g, got, ax, core_arr, name="add_pair") for g, got, ax in zip(arrays[:n], arrays[n:], st["axes"])]
        lands = [pltpu.with_memory_space_constraint(lax.empty(s_.shape, s_.dtype), pltpu.HBM) for s_ in sums]
        st["sems2"], st["arrays"], chain[0] = _exchange_start(sums + lands, N_PEER_CHIPS, st["ici"], order(after), name=f"scatter_ici_start_s{st['id']}")

    def scatter_stage(layer, names, grads, after):
        if sca and "sems2" not in sca[-1]:
            scatter_finish_d2d(sca[-1], after)
        axes = [big[nm][3] for nm in names]
        arrs = [grads[nm] for nm in names]
        sizes8 = [g.shape[a] // N_DEV for g, a in zip(arrs, axes)]
        gots = []
        for g, a in zip(arrs, axes):
            sh = list(g.shape)
            sh[a] //= 2
            gots.append(pltpu.with_memory_space_constraint(lax.empty(tuple(sh), g.dtype), pltpu.HBM))
        st = dict(id=len(sca), layer=layer, names=names, axes=axes, d2d=_plan_scatter_d2d(axes, sizes8),
                  ici=_plan_scatter_ici(axes, sizes8))
        st["sems"], st["arrays"], chain[0] = _exchange_start(arrs + gots, 1, st["d2d"], order(after), name=f"scatter_d2d_start_s{st['id']}")
        sca.append(st)

    def scatter_collect(st, after):
        n = len(st["names"])
        arrays, chain[0] = _exchange_wait(st["sems2"], st["arrays"], st["ici"], order(after), name=f"scatter_ici_wait_s{st['id']}")
        for nm, s_, r in zip(st["names"], arrays[:n], arrays[n:]):
            parts[nm][sub(nm, st["layer"])] = (s_, r)

    for layer in reversed(range(depth)):
        w, sv = wts[layer], saved[layer]
        grads = {}
        dh = _mm(dx, w["f_w2"], tb=True, extras=[(sv["hpre"], "mn"), (chain[0], "after")], epilogue=_epi_drelu2, name="f_2_dx")
        grads["f_w2"] = _mm(sv["hact"], dx, ta=True, name="f_2_dw")
        grads["f_w1"] = _mm(sv["xf"], dh, ta=True, name="f_1_dw")
        dxf = _mm(dh, w["f_w1"], tb=True, name="f_1_dx")
        dx, dg_ffn[layer] = _rmsnorm_bwd(sv["x2"], norm_ffn_g[layer], dxf, dx, name="norm_ffn_bwd")
        scatter_stage(layer, all_names[4:], grads, dx)
        dmo = _mm(dx, w["m_wo"], tb=True, extras=[(chain[0], "after")], name="m_o_dx")
        grads["m_wo"] = _mm(sv["mo"], dx, ta=True, name="m_o_dw")
        dmq, dmk, dmv = _mem_bwd(sv["mq"], sv["mk"], sv["mv"], dmo, seq=seq, n_mem=n_mem, name="mem_bwd")
        grads["m_wq"] = _mm(sv["xq"], dmq, ta=True, name="m_q_dw")
        grads["m_wk"] = _mm(sv["memn"], dmk, ta=True, name="m_k_dw")
        grads["m_wv"] = _mm(sv["memn"], dmv, ta=True, name="m_v_dw")
        dxq = _mm(dmq, w["m_wq"], tb=True, name="m_q_dx")
        dmemn_k = _mm(dmk, w["m_wk"], tb=True, out_dtypes=(F32,), name="m_k_dx")
        dmemn = _mm(dmv, w["m_wv"], tb=True, out_dtypes=(F32,), extras=[(dmemn_k, "mn")], epilogue=_epi_add, name="m_v_dx")
        _, dg_mkv[layer] = _rmsnorm_bwd(mem2, norm_mem_kv_g[layer], dmemn, name="norm_mem_kv_bwd")
        dx, dg_mq[layer] = _rmsnorm_bwd(sv["x1"], norm_mem_q_g[layer], dxq, dx, name="norm_mem_q_bwd")
        scatter_stage(layer, all_names[:4], grads, dx)
        if layer % 2 == 0:
            e = layer // 2
            dycat = _mm(dx, sv["w_out_g"], tb=True, extras=[(chain[0], "after")], name="ab_out_dx")
            grads["ab_w_out"] = ungroup_rows(_mm(sv["ycat"], dx, ta=True, name="ab_out_dw"))
            dz, dws, dbs, dgn, dcw = _gmlp_conv_bwd(sv["z"], dycat, sv["ws"], sv["bs"], sv["gn"], sv["cw"], seq=seq, name="gmlp_conv_bwd")
            d_ws[e], d_bs[e], d_av[e] = dws, dbs.reshape(groups, GMLP_BLOCK), dgn.reshape(a_width)
            d_conv[e] = dcw.transpose(1, 0, 2).reshape(3, a_width)
            grads["ab_w_in"] = ungroup_cols(_mm(sv["xn"], dz, ta=True, name="ab_in_dw"))
            dxn = _mm(dz, sv["w_in_g"], tb=True, name="ab_in_dx")
        else:
            o = layer // 2
            datt = _mm(dx, w["c_w_out"], tb=True, extras=[(chain[0], "after")], name="c_out_dx")
            grads["c_w_out"] = _mm(sv["att"], dx, ta=True, name="c_out_dw")
            dq_pre, dkvh, dkr = _mla_bwd(sv["qh"], sv["kvh"], sv["kr"], datt, sv["att"], sv["lse"], rc, rs1, rs2, seq=seq, heads=heads, name="mla_bwd")
            dk_rope = _rope_rows(dkr, rc, rs1, rs2, transpose=True, out_dtype=F32, name="rope_k_bwd")
            grads["c_w_uq"] = unpad_heads(_mm(sv["qn"], dq_pre, ta=True, name="c_uq_dw"))
            grads["c_w_ukv"] = _mm(sv["kvn"], dkvh, ta=True, name="c_ukv_dw")
            dqn = _mm(dq_pre, sv["w_uq_p"], tb=True, name="c_uq_dx")
            dkvn = _mm(dkvh, w["c_w_ukv"], tb=True, name="c_ukv_dx")
            dcq, d_qn[o] = _rmsnorm_bwd(sv["z"], q_norm_full[o], dqn, width=q_rank, col=0, name="norm_cq_bwd")
            dckv, d_kvn[o] = _rmsnorm_bwd(sv["z"], kv_norm_full[o], dkvn, width=kv_rank, col=q_rank // kv_rank, name="norm_ckv_bwd")
            dz = jnp.concatenate([dcq, dckv, dk_rope[:, :C_ROPE]], axis=1)
            grads["c_w_in"] = _mm(sv["xn"], dz, ta=True, name="c_in_dw")
            dxn = _mm(dz, w["c_w_in"], tb=True, name="c_in_dx")
        dx, dg_mix[layer] = _rmsnorm_bwd(sv["x0"], norm_mix_g[layer], dxn, dx, name="norm_mix_bwd")
        scatter_stage(layer, even_names if layer % 2 == 0 else odd_names, grads, dx)

    grad_x = dx.reshape(bl, seq, d)

    scatter_finish_d2d(sca[-1], dx)
    for st in sca[:-3]:
        scatter_collect(st, dx)
    big_out = {}
    done = chain[0]
    for nm, (wv, mv_, vv, ax) in big.items():
        n_sub = wv.shape[0]
        first = 0 if nm in odd_names else 1
        if first < n_sub:
            big_out[nm] = _adamw(wv, mv_, vv, parts[nm], ax, chip_arr, (first, n_sub - first), None, done, name=f"adamw_{nm}")
            done = big_out[nm][0]
    for st in sca[-3:]:
        scatter_collect(st, done)
    for nm, (wv, mv_, vv, ax) in big.items():
        if nm not in odd_names:
            big_out[nm] = _adamw(wv, mv_, vv, parts[nm], ax, chip_arr, (0, 1), big_out.get(nm), chain[0], name=f"adamw_{nm}_first")

    small_g = [jnp.concatenate(dg_mix), jnp.concatenate(dg_mq), jnp.concatenate(dg_mkv), jnp.concatenate(dg_ffn), dg_final.reshape(d),
               jnp.stack(d_av), jnp.stack(d_ws), jnp.stack(d_bs), jnp.concatenate(d_qn), jnp.concatenate(d_kvn), jnp.stack(d_conv)]
    small_g.append(loss_part.reshape(1))
    spare = jnp.zeros((1,), F32)
    small_shapes = [a.shape for a in small_w] + [(1,)]
    g_all = _all_gather_small(_pack(small_g) + chain[0][0, 0])
    sg, sd, sm, svv = _adamw_small(_pack(small_w + [spare]), _pack(small_m + [spare]), _pack(small_v + [spare]), g_all, name="adamw_small")
    small_names = ["norm_mix_g", "norm_mem_q_g", "norm_mem_kv_g", "norm_ffn_g", "final_norm_g", "a_v_norm_g", "a_w_s", "a_b_s", "c_q_norm_g", "c_kv_norm_g", "b_conv_w"]
    small_out = {}
    unpacked = [_unpack(p, small_shapes) for p in (sg, sd, sm, svv)]
    for i, nm in enumerate(small_names):
        vals = [u[i] for u in unpacked]
        if i >= 8:
            full_w, _, _, ax = split_small[i - 8]
            vals = [_my_block(val, ax, full_w.shape[ax]) for val in vals]
        small_out[nm] = vals

    order = ["norm_mix_g", "norm_mem_q_g", "norm_mem_kv_g", "norm_ffn_g", "final_norm_g", "ab_w_in", "a_v_norm_g", "a_w_s", "a_b_s", "b_conv_w", "ab_w_out",
             "c_w_in", "c_q_norm_g", "c_kv_norm_g", "c_w_uq", "c_w_ukv", "c_w_out", "m_wq", "m_wk", "m_wv", "m_wo", "f_w1", "f_w2"]
    loss = unpacked[0][-1][0]
    res = [loss, grad_x]
    for kind in range(4):
        for nm in order:
            res.append(big_out[nm][kind] if nm in big_out else small_out[nm][kind])
    return tuple(res)
```

```python
import functools
import math

import jax
import jax.numpy as jnp
from jax import lax
from jax.experimental import pallas as pl
from jax.experimental.pallas import tpu as pltpu

F32 = jnp.float32
BF16 = jnp.bfloat16
MESH = pl.DeviceIdType.MESH
N_DEV = 8
LANES = 128

RMS_EPS = 1e-6
CHUNK = 64
GMLP_BLOCK = 128
C_NOPE = 128
C_ROPE = 64
C_V = 128
ROPE_THETA = 10000.0
MEM_HEADS = 4
ADAM_LR = 0.001
ADAM_B1 = 0.9
ADAM_B2 = 0.999
ADAM_EPS = 1e-08
ADAM_WD = 0.01
ADAM_STEP = 10
NEG_BIG = -1e30
VMEM_LIMIT = 56 * 1024 * 1024


def _cp(*sem):
    return pltpu.CompilerParams(dimension_semantics=sem, vmem_limit_bytes=VMEM_LIMIT)


def _tile(n, pref):
    t = min(n, pref)
    while n % t and t > LANES:
        t //= 2
    return n if n % t else t


def _mm(a, b, *, ta=False, tb=False, out_dtypes=(BF16,), extras=(), epilogue=None, tm=1024, tn=1024, tk=2048, name):
    m, k = (a.shape[1], a.shape[0]) if ta else a.shape
    n = b.shape[0] if tb else b.shape[1]
    assert (b.shape[1] if tb else b.shape[0]) == k, (a.shape, b.shape, ta, tb)
    tm, tn, tk = _tile(m, tm), _tile(n, tn), _tile(k, tk)
    nk = k // tk
    ne, no = len(extras), len(out_dtypes)
    dims = (((0 if ta else 1,), (1 if tb else 0,)), ((), ()))

    def body(*refs):
        a_ref, b_ref = refs[0], refs[1]
        e_refs = refs[2:2 + ne]
        o_refs = refs[2 + ne:2 + ne + no]

        def finish(acc):
            vals = [e[...] for e, (_, kind) in zip(e_refs, extras) if kind != "after"]
            outs = (acc,) if epilogue is None else epilogue(acc, *vals)
            for o, val in zip(o_refs, outs):
                o[...] = val.astype(o.dtype)

        prod = lax.dot_general(a_ref[...].astype(BF16), b_ref[...].astype(BF16), dims, preferred_element_type=F32)
        if nk == 1:
            finish(prod)
        else:
            acc_ref = refs[-1]
            kk = pl.program_id(2)

            @pl.when(kk == 0)
            def _():
                acc_ref[...] = prod

            @pl.when((kk > 0) & (kk < nk - 1))
            def _():
                acc_ref[...] += prod

            @pl.when(kk == nk - 1)
            def _():
                finish(acc_ref[...] + prod)

    a_spec = pl.BlockSpec((tk, tm), lambda i, j, kk: (kk, i)) if ta else pl.BlockSpec((tm, tk), lambda i, j, kk: (i, kk))
    b_spec = pl.BlockSpec((tn, tk), lambda i, j, kk: (j, kk)) if tb else pl.BlockSpec((tk, tn), lambda i, j, kk: (kk, j))
    e_specs = []
    for arr, kind in extras:
        if kind == "mn":
            e_specs.append(pl.BlockSpec((tm, tn), lambda i, j, kk: (i, j)))
        elif kind == "after":
            e_specs.append(pl.BlockSpec(arr.shape, lambda i, j, kk: (0, 0)))
        else:
            e_specs.append(pl.BlockSpec((tm, arr.shape[1]), lambda i, j, kk: (i, 0)))
    outs = pl.pallas_call(
        body,
        out_shape=tuple(jax.ShapeDtypeStruct((m, n), d) for d in out_dtypes),
        grid=(m // tm, n // tn, nk),
        in_specs=[a_spec, b_spec] + e_specs,
        out_specs=tuple(pl.BlockSpec((tm, tn), lambda i, j, kk: (i, j)) for _ in out_dtypes),
        scratch_shapes=[pltpu.VMEM((tm, tn), F32)] if nk > 1 else [],
        compiler_params=_cp("parallel", "parallel", "arbitrary"),
        name=name,
    )(a, b, *[arr for arr, _ in extras])
    return outs[0] if no == 1 else outs


def _epi_add(acc, r):
    return (acc + r,)


def _epi_relu2(acc):
    h = jnp.maximum(acc, 0.0)
    return (acc, h * h)


def _epi_drelu2(acc, h):
    return (acc * (2.0 * jnp.maximum(h.astype(F32), 0.0)),)


def _rope(x, c, s1, s2):
    return x * c + pltpu.roll(x, LANES - C_ROPE // 2, 1) * s1 + pltpu.roll(x, C_ROPE // 2, 1) * s2


def _rope_t(dy, c, s1, s2):
    return dy * c + pltpu.roll(dy * s1, C_ROPE // 2, 1) + pltpu.roll(dy * s2, LANES - C_ROPE // 2, 1)


def _epi_rope_heads(acc, c, s1, s2):
    parts = []
    for h0 in range(0, acc.shape[1], 2 * LANES):
        parts.append(acc[:, h0:h0 + LANES])
        parts.append(_rope(acc[:, h0 + LANES:h0 + 2 * LANES], c, s1, s2))
    return (jnp.concatenate(parts, axis=1),)


def _rmsnorm_fwd(x, g, *, width=None, col=0, name):
    t = x.shape[0]
    w = x.shape[1] if width is None else width
    tm = _tile(t, 512)

    def body(x_ref, g_ref, o_ref):
        xv = x_ref[...].astype(F32)
        r = lax.rsqrt(jnp.mean(xv * xv, axis=-1, keepdims=True) + RMS_EPS)
        o_ref[...] = (xv * r * g_ref[...]).astype(o_ref.dtype)

    return pl.pallas_call(
        body,
        out_shape=jax.ShapeDtypeStruct((t, w), BF16),
        grid=(t // tm,),
        in_specs=[pl.BlockSpec((tm, w), lambda i: (i, col)), pl.BlockSpec((1, w), lambda i: (0, 0))],
        out_specs=pl.BlockSpec((tm, w), lambda i: (i, 0)),
        compiler_params=_cp("parallel"),
        name=name,
    )(x, g.reshape(1, w))


def _rmsnorm_bwd(x, g, dy, res=None, *, width=None, col=0, name):
    t = x.shape[0]
    w = x.shape[1] if width is None else width
    tm = _tile(t, 512)
    has_res = res is not None

    def body(*refs):
        x_ref, g_ref, dy_ref = refs[:3]
        res_ref = refs[3] if has_res else None
        dx_ref, dg_ref = refs[-2], refs[-1]
        xv = x_ref[...].astype(F32)
        r = lax.rsqrt(jnp.mean(xv * xv, axis=-1, keepdims=True) + RMS_EPS)
        xh = xv * r
        dyv = dy_ref[...].astype(F32)
        dxh = dyv * g_ref[...]
        dx = r * (dxh - xh * jnp.mean(dxh * xh, axis=-1, keepdims=True))
        if has_res:
            dx = dx + res_ref[...]
        dx_ref[...] = dx

        @pl.when(pl.program_id(0) == 0)
        def _():
            dg_ref[...] = jnp.zeros_like(dg_ref)

        dg_ref[...] += jnp.sum(dyv * xh, axis=0, keepdims=True)

    in_specs = [pl.BlockSpec((tm, w), lambda i: (i, col)), pl.BlockSpec((1, w), lambda i: (0, 0)), pl.BlockSpec((tm, w), lambda i: (i, 0))]
    args = [x, g.reshape(1, w), dy]
    if has_res:
        in_specs.append(pl.BlockSpec((tm, w), lambda i: (i, 0)))
        args.append(res)
    return pl.pallas_call(
        body,
        out_shape=(jax.ShapeDtypeStruct((t, w), F32), jax.ShapeDtypeStruct((1, w), F32)),
        grid=(t // tm,),
        in_specs=in_specs,
        out_specs=(pl.BlockSpec((tm, w), lambda i: (i, 0)), pl.BlockSpec((1, w), lambda i: (0, 0))),
        compiler_params=_cp("arbitrary"),
        name=name,
    )(*args)


def _final_loss(x, g, target, *, name):
    t, d = x.shape
    tm = _tile(t, 256)

    def body(x_ref, g_ref, t_ref, loss_ref, dx_ref, dg_ref):
        xv = x_ref[...]
        r = lax.rsqrt(jnp.mean(xv * xv, axis=-1, keepdims=True) + RMS_EPS)
        xh = xv * r
        gv = g_ref[...]
        err = xh * gv - t_ref[...]
        dyv = err * (1.0 / d)
        dxh = dyv * gv
        dx_ref[...] = r * (dxh - xh * jnp.mean(dxh * xh, axis=-1, keepdims=True))

        @pl.when(pl.program_id(0) == 0)
        def _():
            dg_ref[...] = jnp.zeros_like(dg_ref)
            loss_ref[...] = jnp.zeros_like(loss_ref)

        dg_ref[...] += jnp.sum(dyv * xh, axis=0, keepdims=True)
        row = jnp.sum(err * err, axis=-1, keepdims=True) * (0.5 / d)
        loss_ref[...] += jnp.sum(row, axis=0, keepdims=True)

    return pl.pallas_call(
        body,
        out_shape=(jax.ShapeDtypeStruct((1, 1), F32), jax.ShapeDtypeStruct((t, d), F32), jax.ShapeDtypeStruct((1, d), F32)),
        grid=(t // tm,),
        in_specs=[pl.BlockSpec((tm, d), lambda i: (i, 0)), pl.BlockSpec((1, d), lambda i: (0, 0)), pl.BlockSpec((tm, d), lambda i: (i, 0))],
        out_specs=(pl.BlockSpec((1, 1), lambda i: (0, 0)), pl.BlockSpec((tm, d), lambda i: (i, 0)), pl.BlockSpec((1, d), lambda i: (0, 0))),
        compiler_params=_cp("arbitrary"),
        name=name,
    )(x, g.reshape(1, d), target)


_GELU_C = math.sqrt(2.0 / math.pi)
_GELU_A = 0.044715


def _gelu(x):
    t = jnp.tanh(_GELU_C * (x + _GELU_A * x * x * x))
    return 0.5 * x * (1.0 + t), t


def _dgelu(x, t):
    return 0.5 * (1.0 + t) + 0.5 * x * (1.0 - t * t) * (_GELU_C * (1.0 + 3.0 * _GELU_A * x * x))


def _spatial_mask():
    ci = lax.broadcasted_iota(jnp.int32, (GMLP_BLOCK, GMLP_BLOCK), 0) // CHUNK
    cj = lax.broadcasted_iota(jnp.int32, (GMLP_BLOCK, GMLP_BLOCK), 1) // CHUNK
    return (cj <= ci).astype(F32)


def _shift_down(x, k):
    rows = lax.broadcasted_iota(jnp.int32, x.shape, 0)
    return jnp.where(rows >= k, pltpu.roll(x, k, 0), 0.0)


def _shift_up(x, k):
    n = x.shape[0]
    rows = lax.broadcasted_iota(jnp.int32, x.shape, 0)
    return jnp.where(rows < n - k, pltpu.roll(x, n - k, 0), 0.0)


def _gmlp_conv_fwd(z, ws, bs, gn, cw, *, seq, name):
    t = z.shape[0]
    g_n = ws.shape[0]
    nb = seq // GMLP_BLOCK
    L = LANES

    def body(z_ref, ws_ref, bs_ref, gn_ref, cw_ref, y_ref):
        u = z_ref[:, 0:L]
        v = z_ref[:, L:2 * L]
        gu, _ = _gelu(u)
        gv, _ = _gelu(v)
        r = lax.rsqrt(jnp.mean(gv * gv, axis=-1, keepdims=True) + RMS_EPS)
        vg = (gv * r * gn_ref[0]).astype(BF16)
        wm = (ws_ref[0] * _spatial_mask()).astype(BF16)
        bias = bs_ref[0]
        for n in range(nb):
            sl = slice(n * GMLP_BLOCK, (n + 1) * GMLP_BLOCK)
            mixed = jnp.dot(wm, vg[sl], preferred_element_type=F32) + bias
            y_ref[sl, 0:L] = (gu[sl] * mixed).astype(y_ref.dtype)
        bg = z_ref[:, 2 * L:3 * L]
        zc = z_ref[:, 3 * L:4 * L] * z_ref[:, 4 * L:5 * L]
        w0, w1, w2 = cw_ref[0, 0:1, :], cw_ref[0, 1:2, :], cw_ref[0, 2:3, :]
        conv = w0 * _shift_down(zc, 2) + w1 * _shift_down(zc, 1) + w2 * zc
        y_ref[:, L:2 * L] = (bg * conv).astype(y_ref.dtype)

    return pl.pallas_call(
        body,
        out_shape=jax.ShapeDtypeStruct((t, 2 * L * g_n), BF16),
        grid=(g_n, t // seq),
        in_specs=[
            pl.BlockSpec((seq, 5 * L), lambda g, b: (b, g)),
            pl.BlockSpec((1, GMLP_BLOCK, GMLP_BLOCK), lambda g, b: (g, 0, 0)),
            pl.BlockSpec((1, GMLP_BLOCK, 1), lambda g, b: (g, 0, 0)),
            pl.BlockSpec((1, 1, L), lambda g, b: (g, 0, 0)),
            pl.BlockSpec((1, 3, L), lambda g, b: (g, 0, 0)),
        ],
        out_specs=pl.BlockSpec((seq, 2 * L), lambda g, b: (b, g)),
        compiler_params=_cp("parallel", "parallel"),
        name=name,
    )(z, ws, bs, gn, cw)


def _gmlp_conv_bwd(z, dy, ws, bs, gn, cw, *, seq, name):
    t = z.shape[0]
    g_n = ws.shape[0]
    nb = seq // GMLP_BLOCK
    L = LANES
    dims_nt = (((1,), (1,)), ((), ()))
    dims_tn = (((0,), (0,)), ((), ()))

    def body(z_ref, dy_ref, ws_ref, bs_ref, gn_ref, cw_ref, dz_ref, dws_ref, dbs_ref, dgn_ref, dcw_ref, dvg_ref):
        @pl.when(pl.program_id(1) == 0)
        def _():
            dws_ref[...] = jnp.zeros_like(dws_ref)
            dbs_ref[...] = jnp.zeros_like(dbs_ref)
            dgn_ref[...] = jnp.zeros_like(dgn_ref)
            dcw_ref[...] = jnp.zeros_like(dcw_ref)

        u = z_ref[:, 0:L]
        v = z_ref[:, L:2 * L]
        gu, tu = _gelu(u)
        gv, tv = _gelu(v)
        r = lax.rsqrt(jnp.mean(gv * gv, axis=-1, keepdims=True) + RMS_EPS)
        gvh = gv * r
        gnv = gn_ref[0]
        vg = (gvh * gnv).astype(BF16)
        mask = _spatial_mask()
        wm = (ws_ref[0] * mask).astype(BF16)
        bias = bs_ref[0]
        dya = dy_ref[:, 0:L].astype(F32)
        dws = jnp.zeros((GMLP_BLOCK, GMLP_BLOCK), F32)
        dbs = jnp.zeros((GMLP_BLOCK, 1), F32)
        for n in range(nb):
            sl = slice(n * GMLP_BLOCK, (n + 1) * GMLP_BLOCK)
            mixed = jnp.dot(wm, vg[sl], preferred_element_type=F32) + bias
            dmixed = dya[sl] * gu[sl]
            dgu = dya[sl] * mixed
            dz_ref[sl, 0:L] = (dgu * _dgelu(u[sl], tu[sl])).astype(dz_ref.dtype)
            dmb = dmixed.astype(BF16)
            dws = dws + lax.dot_general(dmb, vg[sl], dims_nt, preferred_element_type=F32)
            dbs = dbs + jnp.sum(dmixed, axis=1, keepdims=True)
            dvg_ref[sl, :] = lax.dot_general(wm, dmb, dims_tn, preferred_element_type=F32)
        dws_ref[0] += dws * mask
        dbs_ref[0] += dbs
        dvg = dvg_ref[...]
        dgn_ref[0] += jnp.sum(dvg * gvh, axis=0, keepdims=True)
        dvh = dvg * gnv
        dgv = r * (dvh - gvh * jnp.mean(dvh * gvh, axis=-1, keepdims=True))
        dz_ref[:, L:2 * L] = (dgv * _dgelu(v, tv)).astype(dz_ref.dtype)

        bg = z_ref[:, 2 * L:3 * L]
        cg = z_ref[:, 3 * L:4 * L]
        hh = z_ref[:, 4 * L:5 * L]
        zc = cg * hh
        w0, w1, w2 = cw_ref[0, 0:1, :], cw_ref[0, 1:2, :], cw_ref[0, 2:3, :]
        z1 = _shift_down(zc, 1)
        z2 = _shift_down(zc, 2)
        conv = w0 * z2 + w1 * z1 + w2 * zc
        dyb = dy_ref[:, L:2 * L].astype(F32)
        dconv = dyb * bg
        dz_ref[:, 2 * L:3 * L] = (dyb * conv).astype(dz_ref.dtype)
        dzc = w2 * dconv + w1 * _shift_up(dconv, 1) + w0 * _shift_up(dconv, 2)
        dz_ref[:, 3 * L:4 * L] = (dzc * hh).astype(dz_ref.dtype)
        dz_ref[:, 4 * L:5 * L] = (dzc * cg).astype(dz_ref.dtype)
        dcw_ref[0, 0:1, :] += jnp.sum(dconv * z2, axis=0, keepdims=True)
        dcw_ref[0, 1:2, :] += jnp.sum(dconv * z1, axis=0, keepdims=True)
        dcw_ref[0, 2:3, :] += jnp.sum(dconv * zc, axis=0, keepdims=True)

    return pl.pallas_call(
        body,
        out_shape=(
            jax.ShapeDtypeStruct((t, 5 * L * g_n), BF16),
            jax.ShapeDtypeStruct(ws.shape, F32),
            jax.ShapeDtypeStruct(bs.shape, F32),
            jax.ShapeDtypeStruct(gn.shape, F32),
            jax.ShapeDtypeStruct(cw.shape, F32),
        ),
        grid=(g_n, t // seq),
        in_specs=[
            pl.BlockSpec((seq, 5 * L), lambda g, b: (b, g)),
            pl.BlockSpec((seq, 2 * L), lambda g, b: (b, g)),
            pl.BlockSpec((1, GMLP_BLOCK, GMLP_BLOCK), lambda g, b: (g, 0, 0)),
            pl.BlockSpec((1, GMLP_BLOCK, 1), lambda g, b: (g, 0, 0)),
            pl.BlockSpec((1, 1, L), lambda g, b: (g, 0, 0)),
            pl.BlockSpec((1, 3, L), lambda g, b: (g, 0, 0)),
        ],
        out_specs=(
            pl.BlockSpec((seq, 5 * L), lambda g, b: (b, g)),
            pl.BlockSpec((1, GMLP_BLOCK, GMLP_BLOCK), lambda g, b: (g, 0, 0)),
            pl.BlockSpec((1, GMLP_BLOCK, 1), lambda g, b: (g, 0, 0)),
            pl.BlockSpec((1, 1, L), lambda g, b: (g, 0, 0)),
            pl.BlockSpec((1, 3, L), lambda g, b: (g, 0, 0)),
        ),
        scratch_shapes=[pltpu.VMEM((seq, L), F32)],
        compiler_params=_cp("parallel", "arbitrary"),
        name=name,
    )(z, dy, ws, bs, gn, cw)


def _rope_rows(x, c, s1, s2, *, transpose, out_dtype, name):
    t, w = x.shape
    tm = _tile(t, 512)
    only = w == LANES
    fn = _rope_t if transpose else _rope

    def body(x_ref, c_ref, s1_ref, s2_ref, o_ref):
        cv, s1v, s2v = c_ref[...], s1_ref[...], s2_ref[...]
        if only:
            o_ref[...] = fn(x_ref[...].astype(F32), cv, s1v, s2v).astype(o_ref.dtype)
        else:
            for h0 in range(0, w, 2 * LANES):
                o_ref[:, h0:h0 + LANES] = x_ref[:, h0:h0 + LANES].astype(o_ref.dtype)
                o_ref[:, h0 + LANES:h0 + 2 * LANES] = fn(x_ref[:, h0 + LANES:h0 + 2 * LANES].astype(F32), cv, s1v, s2v).astype(o_ref.dtype)

    tab = pl.BlockSpec((tm, LANES), lambda i: (i, 0))
    return pl.pallas_call(
        body,
        out_shape=jax.ShapeDtypeStruct((t, w), out_dtype),
        grid=(t // tm,),
        in_specs=[pl.BlockSpec((tm, w), lambda i: (i, 0)), tab, tab, tab],
        out_specs=pl.BlockSpec((tm, w), lambda i: (i, 0)),
        compiler_params=_cp("parallel"),
        name=name,
    )(x, c, s1, s2)


_NT = (((1,), (1,)), ((), ()))
_TN = (((0,), (0,)), ((), ()))


def _chunk_mask(tq):
    qc = lax.broadcasted_iota(jnp.int32, (tq, tq), 0) // CHUNK
    kc = lax.broadcasted_iota(jnp.int32, (tq, tq), 1) // CHUNK
    return kc <= qc


MLA_BLOCK = 1024


def _mla_fwd(q, kv, kr, *, seq, heads, name):
    t = q.shape[0]
    tq = _tile(seq, MLA_BLOCK)
    nq = seq // tq
    scale = (C_NOPE + C_ROPE) ** -0.5

    def body(q_ref, kv_ref, kr_ref, o_ref, lse_ref, kcat_ref):
        i = pl.program_id(2)

        @pl.when(i == 0)
        def _():
            kcat_ref[:, :LANES] = kv_ref[:, :LANES]
            kcat_ref[:, LANES:] = kr_ref[...]

        qv = q_ref[...]

        def step(j, carry, diagonal):
            m_run, l_run, acc = carry
            rows = pl.ds(pl.multiple_of(j * tq, tq), tq)
            s = lax.dot_general(qv, kcat_ref[rows, :], _NT, preferred_element_type=F32) * scale
            if diagonal:
                s = jnp.where(_chunk_mask(tq), s, NEG_BIG)
            m_new = jnp.maximum(m_run, jnp.max(s, axis=-1, keepdims=True))
            alpha = jnp.exp(m_run - m_new)
            p = jnp.exp(s - m_new)
            l_new = alpha * l_run + jnp.sum(p, axis=-1, keepdims=True)
            acc = alpha * acc + jnp.dot(p.astype(BF16), kv_ref[rows, LANES:], preferred_element_type=F32)
            return m_new, l_new, acc

        init = (jnp.full((tq, 1), NEG_BIG, F32), jnp.zeros((tq, 1), F32), jnp.zeros((tq, C_V), F32))
        carry = lax.fori_loop(0, i, lambda j, c: step(j, c, False), init)
        m_run, l_run, acc = step(i, carry, True)
        o_ref[...] = (acc / l_run).astype(o_ref.dtype)
        lse_ref[...] = m_run + jnp.log(l_run)

    return pl.pallas_call(
        body,
        out_shape=(jax.ShapeDtypeStruct((t, heads * C_V), BF16), jax.ShapeDtypeStruct((heads, t, 1), F32)),
        grid=(t // seq, heads, nq),
        in_specs=[
            pl.BlockSpec((tq, 2 * LANES), lambda b, h, i: (b * nq + i, h)),
            pl.BlockSpec((seq, 2 * LANES), lambda b, h, i: (b, h)),
            pl.BlockSpec((seq, LANES), lambda b, h, i: (b, 0)),
        ],
        out_specs=(
            pl.BlockSpec((tq, C_V), lambda b, h, i: (b * nq + i, h)),
            pl.BlockSpec((None, tq, 1), lambda b, h, i: (h, b * nq + i, 0)),
        ),
        scratch_shapes=[pltpu.VMEM((seq, 2 * LANES), BF16)],
        compiler_params=_cp("parallel", "parallel", "arbitrary"),
        name=name,
    )(q, kv, kr)


def _mla_bwd(q, kv, kr, do, o, lse, rc, rs1, rs2, *, seq, heads, name):
    t = q.shape[0]
    tq = _tile(seq, MLA_BLOCK)
    nq = seq // tq
    scale = (C_NOPE + C_ROPE) ** -0.5

    def body(q_ref, kv_ref, kr_ref, do_ref, o_ref, lse_ref, rc_ref, rs1_ref, rs2_ref, dq_ref, dkv_ref, dkr_ref, kcat_ref, dk_ref, dv_ref):
        h, i = pl.program_id(1), pl.program_id(2)

        @pl.when(i == 0)
        def _():
            kcat_ref[:, :LANES] = kv_ref[:, :LANES]
            kcat_ref[:, LANES:] = kr_ref[...]
            dk_ref[...] = jnp.zeros_like(dk_ref)
            dv_ref[...] = jnp.zeros_like(dv_ref)

        @pl.when((i == 0) & (h == 0))
        def _():
            dkr_ref[...] = jnp.zeros_like(dkr_ref)

        qv = q_ref[...]
        dov = do_ref[...]
        lse_v = lse_ref[...]
        delta = jnp.sum(dov.astype(F32) * o_ref[...].astype(F32), axis=-1, keepdims=True)

        def step(j, dq, diagonal):
            rows = pl.ds(pl.multiple_of(j * tq, tq), tq)
            kj = kcat_ref[rows, :]
            s = lax.dot_general(qv, kj, _NT, preferred_element_type=F32) * scale
            if diagonal:
                s = jnp.where(_chunk_mask(tq), s, NEG_BIG)
            p = jnp.exp(s - lse_v)
            dp = lax.dot_general(dov, kv_ref[rows, LANES:], _NT, preferred_element_type=F32)
            ds = (p * (dp - delta) * scale).astype(BF16)
            dk_ref[rows, :] += lax.dot_general(ds, qv, _TN, preferred_element_type=F32)
            dv_ref[rows, :] += lax.dot_general(p.astype(BF16), dov, _TN, preferred_element_type=F32)
            return dq + jnp.dot(ds, kj, preferred_element_type=F32)

        dq = lax.fori_loop(0, i, lambda j, c: step(j, c, False), jnp.zeros((tq, 2 * LANES), F32))
        dq = step(i, dq, True)
        dq_ref[:, :LANES] = dq[:, :LANES].astype(dq_ref.dtype)
        dq_ref[:, LANES:] = _rope_t(dq[:, LANES:], rc_ref[...], rs1_ref[...], rs2_ref[...]).astype(dq_ref.dtype)

        @pl.when(i == nq - 1)
        def _():
            dkv_ref[:, :LANES] = dk_ref[:, :LANES].astype(dkv_ref.dtype)
            dkv_ref[:, LANES:] = dv_ref[...].astype(dkv_ref.dtype)
            dkr_ref[...] += dk_ref[:, LANES:]

    return pl.pallas_call(
        body,
        out_shape=(
            jax.ShapeDtypeStruct((t, heads * 2 * LANES), BF16),
            jax.ShapeDtypeStruct((t, heads * 2 * LANES), BF16),
            jax.ShapeDtypeStruct((t, LANES), F32),
        ),
        grid=(t // seq, heads, nq),
        in_specs=[
            pl.BlockSpec((tq, 2 * LANES), lambda b, h, i: (b * nq + i, h)),
            pl.BlockSpec((seq, 2 * LANES), lambda b, h, i: (b, h)),
            pl.BlockSpec((seq, LANES), lambda b, h, i: (b, 0)),
            pl.BlockSpec((tq, C_V), lambda b, h, i: (b * nq + i, h)),
            pl.BlockSpec((tq, C_V), lambda b, h, i: (b * nq + i, h)),
            pl.BlockSpec((None, tq, 1), lambda b, h, i: (h, b * nq + i, 0)),
            pl.BlockSpec((tq, LANES), lambda b, h, i: (b * nq + i, 0)),
            pl.BlockSpec((tq, LANES), lambda b, h, i: (b * nq + i, 0)),
            pl.BlockSpec((tq, LANES), lambda b, h, i: (b * nq + i, 0)),
        ],
        out_specs=(
            pl.BlockSpec((tq, 2 * LANES), lambda b, h, i: (b * nq + i, h)),
            pl.BlockSpec((seq, 2 * LANES), lambda b, h, i: (b, h)),
            pl.BlockSpec((seq, LANES), lambda b, h, i: (b, 0)),
        ),
        scratch_shapes=[pltpu.VMEM((seq, 2 * LANES), BF16), pltpu.VMEM((seq, 2 * LANES), F32), pltpu.VMEM((seq, C_V), F32)],
        compiler_params=_cp("parallel", "arbitrary", "arbitrary"),
        name=name,
    )(q, kv, kr, do, o, lse, rc, rs1, rs2)


def _softmax_rows(s):
    e = jnp.exp(s - jnp.max(s, axis=-1, keepdims=True))
    return e / jnp.sum(e, axis=-1, keepdims=True)


def _mem_fwd(q, k, v, *, seq, n_mem, name):
    t, d = q.shape
    hd = d // MEM_HEADS
    tq = _tile(seq, 1024)
    nq = seq // tq
    scale = hd ** -0.5

    def body(q_ref, k_ref, v_ref, o_ref):
        s = lax.dot_general(q_ref[...], k_ref[...], _NT, preferred_element_type=F32) * scale
        p = _softmax_rows(s)
        o_ref[...] = jnp.dot(p.astype(BF16), v_ref[...], preferred_element_type=F32).astype(o_ref.dtype)

    return pl.pallas_call(
        body,
        out_shape=jax.ShapeDtypeStruct((t, d), BF16),
        grid=(t // seq, MEM_HEADS, nq),
        in_specs=[
            pl.BlockSpec((tq, hd), lambda b, h, i: (b * nq + i, h)),
            pl.BlockSpec((n_mem, hd), lambda b, h, i: (b, h)),
            pl.BlockSpec((n_mem, hd), lambda b, h, i: (b, h)),
        ],
        out_specs=pl.BlockSpec((tq, hd), lambda b, h, i: (b * nq + i, h)),
        compiler_params=_cp("parallel", "parallel", "arbitrary"),
        name=name,
    )(q, k, v)


def _mem_bwd(q, k, v, do, *, seq, n_mem, name):
    t, d = q.shape
    hd = d // MEM_HEADS
    tq = _tile(seq, 1024)
    nq = seq // tq
    scale = hd ** -0.5

    def body(q_ref, k_ref, v_ref, do_ref, dq_ref, dk_ref, dv_ref):
        @pl.when(pl.program_id(2) == 0)
        def _():
            dk_ref[...] = jnp.zeros_like(dk_ref)
            dv_ref[...] = jnp.zeros_like(dv_ref)

        qv, kv_, vv, dov = q_ref[...], k_ref[...], v_ref[...], do_ref[...]
        p = _softmax_rows(lax.dot_general(qv, kv_, _NT, preferred_element_type=F32) * scale)
        dp = lax.dot_general(dov, vv, _NT, preferred_element_type=F32)
        ds = (p * (dp - jnp.sum(dp * p, axis=-1, keepdims=True)) * scale).astype(BF16)
        dq_ref[...] = jnp.dot(ds, kv_, preferred_element_type=F32).astype(dq_ref.dtype)
        dk_ref[...] += lax.dot_general(ds, qv, _TN, preferred_element_type=F32)
        dv_ref[...] += lax.dot_general(p.astype(BF16), dov, _TN, preferred_element_type=F32)

    return pl.pallas_call(
        body,
        out_shape=(
            jax.ShapeDtypeStruct((t, d), BF16),
            jax.ShapeDtypeStruct(k.shape, F32),
            jax.ShapeDtypeStruct(v.shape, F32),
        ),
        grid=(t // seq, MEM_HEADS, nq),
        in_specs=[
            pl.BlockSpec((tq, hd), lambda b, h, i: (b * nq + i, h)),
            pl.BlockSpec((n_mem, hd), lambda b, h, i: (b, h)),
            pl.BlockSpec((n_mem, hd), lambda b, h, i: (b, h)),
            pl.BlockSpec((tq, hd), lambda b, h, i: (b * nq + i, h)),
        ],
        out_specs=(
            pl.BlockSpec((tq, hd), lambda b, h, i: (b * nq + i, h)),
            pl.BlockSpec((n_mem, hd), lambda b, h, i: (b, h)),
            pl.BlockSpec((n_mem, hd), lambda b, h, i: (b, h)),
        ),
        compiler_params=_cp("parallel", "parallel", "arbitrary"),
        name=name,
    )(q, k, v, do)


def _me():
    return lax.axis_index("x"), lax.axis_index("y"), lax.axis_index("c")


def _dev_index(px, py, pc):
    return 4 * px + 2 * py + pc


def _chunk(ref, idx, axis, size):
    start = pl.multiple_of(idx * size, size)
    return ref.at[pl.ds(start, size), :] if axis == 0 else ref.at[:, pl.ds(start, size)]


def _all_gather_small(v):
    m, n = v.shape

    def body(x_ref, out_ref, send_sems, recv_sems, local_sem):
        x, y, c = _me()
        me, sibling = (x, y, c), (x, y, 1 - c)
        chips = [(1 - x, y), (x, 1 - y), (1 - x, 1 - y)]

        def rows(px, py, pc):
            return out_ref.at[_dev_index(px, py, pc)]

        def copy(k, block, to, src=None):
            return pltpu.make_async_remote_copy(
                src_ref=rows(*block) if src is None else src, dst_ref=rows(*block),
                send_sem=send_sems.at[k], recv_sem=recv_sems.at[k], device_id=to, device_id_type=MESH)

        mine = pltpu.make_async_copy(x_ref, rows(*me), local_sem)
        mine.start()
        first = [copy(0, me, sibling, src=x_ref)]
        first += [copy(1 + j, me, (*chip, c), src=x_ref) for j, chip in enumerate(chips)]
        for cp in first:
            cp.start()
        passed = [copy(4 + j, (*chip, c), sibling) for j, chip in enumerate(chips)]
        for j, chip in enumerate(chips):
            copy(1 + j, (*chip, c), me).wait_recv()
            passed[j].start()
        copy(0, sibling, me).wait_recv()
        for j, chip in enumerate(chips):
            copy(4 + j, (*chip, 1 - c), me).wait_recv()
        for cp in first + passed:
            cp.wait_send()
        mine.wait()

    return pl.pallas_call(
        body,
        out_shape=jax.ShapeDtypeStruct((N_DEV, m, n), v.dtype),
        in_specs=[pl.BlockSpec(memory_space=pltpu.VMEM)],
        out_specs=pl.BlockSpec(memory_space=pltpu.VMEM),
        scratch_shapes=[pltpu.SemaphoreType.DMA((7,)), pltpu.SemaphoreType.DMA((7,)), pltpu.SemaphoreType.DMA],
        name="all_gather_small",
    )(v)


def _cast_place(w, layer, axis, dev_arr, *, name):
    _, r, c = w.shape
    tr = _tile(r, 256)
    nr = r // tr
    full = (N_DEV * r, c) if axis == 0 else (r, N_DEV * c)

    def body(dev_ref, w_ref, o_ref):
        o_ref[...] = w_ref[...].astype(o_ref.dtype)

    if axis == 0:
        out_spec = pl.BlockSpec((tr, c), lambda i, dev: (dev[0] * nr + i, 0))
    else:
        out_spec = pl.BlockSpec((tr, c), lambda i, dev: (i, dev[0]))
    return pl.pallas_call(
        body,
        out_shape=jax.ShapeDtypeStruct(full, BF16),
        grid_spec=pltpu.PrefetchScalarGridSpec(
            num_scalar_prefetch=1, grid=(nr,), in_specs=[pl.BlockSpec((None, tr, c), lambda i, dev: (layer, i, 0))], out_specs=out_spec),
        compiler_params=_cp("parallel"),
        name=name,
    )(dev_arr, w)


_ANY = pl.BlockSpec(memory_space=pl.ANY)
_SEM = pl.BlockSpec(memory_space=pltpu.SEMAPHORE)
_EFFECT = pltpu.SideEffectType.DATAFLOW_SIDE_EFFECTING
N_PEER_CHIPS = 3


def _other_chips(x, y):
    return [((1 - x if (j >> 1) & 1 else x), (1 - y if j & 1 else y)) for j in range(1, 4)]


def _remote(src, dst, send_sem, recv_sem, to):
    return pltpu.make_async_remote_copy(src_ref=src, dst_ref=dst, send_sem=send_sem, recv_sem=recv_sem, device_id=to, device_id_type=MESH)


def _plan_gather_ici(axes, sizes):
    def plan(refs, send_sems, recv_sems):
        x, y, c = _me()
        out = []
        for t in range(len(axes)):
            for j, chip in enumerate(_other_chips(x, y)):
                mine = _chunk(refs[t], _dev_index(x, y, c), axes[t], sizes[t])
                theirs = _chunk(refs[t], _dev_index(*chip, c), axes[t], sizes[t])
                out.append((_remote(mine, mine, send_sems[j], recv_sems[j], (*chip, c)), _remote(theirs, theirs, send_sems[j], recv_sems[j], (*chip, c))))
        return out
    return plan


def _plan_gather_d2d(axes, sizes):
    def plan(refs, send_sems, recv_sems):
        x, y, c = _me()
        out = []
        for t in range(len(axes)):
            for k in range(4):
                have = _chunk(refs[t], 2 * k + c, axes[t], sizes[t])
                get = _chunk(refs[t], 2 * k + 1 - c, axes[t], sizes[t])
                out.append((_remote(have, have, send_sems[0], recv_sems[0], (x, y, 1 - c)), _remote(get, get, send_sems[0], recv_sems[0], (x, y, 1 - c))))
        return out
    return plan


def _plan_scatter_d2d(axes, sizes):
    n = len(axes)

    def plan(refs, send_sems, recv_sems):
        x, y, c = _me()
        out = []
        for t in range(n):
            for k in range(4):
                give = _chunk(refs[t], 2 * k + 1 - c, axes[t], sizes[t])
                land = _chunk(refs[n + t], k, axes[t], sizes[t])
                cp = _remote(give, land, send_sems[0], recv_sems[0], (x, y, 1 - c))
                out.append((cp, cp))
        return out
    return plan


def _plan_scatter_ici(axes, sizes):
    n = len(axes)

    def plan(refs, send_sems, recv_sems):
        x, y, c = _me()
        out = []
        for t in range(n):
            for j, (px, py) in enumerate(_other_chips(x, y)):
                send = _remote(_chunk(refs[t], 2 * px + py, axes[t], sizes[t]), _chunk(refs[n + t], 2 * x + y, axes[t], sizes[t]),
                               send_sems[j], recv_sems[j], (px, py, c))
                recv = _remote(_chunk(refs[t], 2 * x + y, axes[t], sizes[t]), _chunk(refs[n + t], 2 * px + py, axes[t], sizes[t]),
                               send_sems[j], recv_sems[j], (px, py, c))
                out.append((send, recv))
        return out
    return plan


def _exchange_start(arrays, n_sems, plan, after, *, name):
    n = len(arrays)
    m = len(after)

    def body(*refs):
        send_sems, recv_sems = list(refs[n + m:n + m + n_sems]), list(refs[n + m + n_sems:n + m + 2 * n_sems])
        for send, _ in plan(refs[:n], send_sems, recv_sems):
            send.start()
        refs[-1][...] = jnp.zeros_like(refs[-1])

    outs = pl.pallas_call(
        body,
        out_shape=(*([pltpu.SemaphoreType.DMA(())] * (2 * n_sems)),
                   *[jax.ShapeDtypeStruct(a.shape, a.dtype) for a in arrays], jax.ShapeDtypeStruct((8, LANES), F32)),
        in_specs=[_ANY] * (n + m),
        out_specs=(*([_SEM] * (2 * n_sems)), *([_ANY] * n), pl.BlockSpec(memory_space=pltpu.VMEM)),
        input_output_aliases={t: 2 * n_sems + t for t in range(n)},
        compiler_params=pltpu.CompilerParams(has_side_effects=_EFFECT),
        name=name,
    )(*arrays, *after)
    return list(outs[:2 * n_sems]), list(outs[2 * n_sems:2 * n_sems + n]), outs[-1]


def _exchange_wait(sems, arrays, plan, after, *, name):
    n = len(arrays)
    n_sems = len(sems) // 2

    def body(*refs):
        send_sems, recv_sems = list(refs[n:n + n_sems]), list(refs[n + n_sems:n + 2 * n_sems])
        for send, recv in plan(refs[:n], send_sems, recv_sems):
            send.wait_send()
            recv.wait_recv()
        refs[-1][...] = jnp.zeros_like(refs[-1])

    outs = pl.pallas_call(
        body,
        out_shape=(*[jax.ShapeDtypeStruct(a.shape, a.dtype) for a in arrays], jax.ShapeDtypeStruct((8, LANES), F32)),
        in_specs=[_ANY] * n + [_SEM] * (2 * n_sems) + [_ANY] * len(after),
        out_specs=(*([_ANY] * n), pl.BlockSpec(memory_space=pltpu.VMEM)),
        input_output_aliases={t: t for t in range(n)},
        compiler_params=pltpu.CompilerParams(has_side_effects=_EFFECT),
        name=name,
    )(*arrays, *sems, *after)
    return list(outs[:n]), outs[-1]


def _add_pair(g, got, axis, core_arr, *, name):
    r, c = got.shape
    if axis == 0:
        rows, cols = r // 4, c
    else:
        rows, cols = r, c // 4
    tr = _tile(rows, 512)
    nr = rows // tr

    def body(core_ref, g_ref, got_ref, o_ref):
        o_ref[...] = (g_ref[...].astype(F32) + got_ref[...].astype(F32)).astype(o_ref.dtype)

    if axis == 0:
        g_spec = pl.BlockSpec((tr, cols), lambda k, i, core: ((2 * k + core[0]) * nr + i, 0))
        spec = pl.BlockSpec((tr, cols), lambda k, i, core: (k * nr + i, 0))
    else:
        g_spec = pl.BlockSpec((tr, cols), lambda k, i, core: (i, 2 * k + core[0]))
        spec = pl.BlockSpec((tr, cols), lambda k, i, core: (i, k))
    return pl.pallas_call(
        body,
        out_shape=jax.ShapeDtypeStruct(got.shape, got.dtype),
        grid_spec=pltpu.PrefetchScalarGridSpec(num_scalar_prefetch=1, grid=(4, nr), in_specs=[g_spec, spec], out_specs=spec),
        compiler_params=_cp("parallel", "parallel"),
        name=name,
    )(core_arr, g, got)


def _adam_math(w, g, m, v):
    m = ADAM_B1 * m + (1.0 - ADAM_B1) * g
    v = ADAM_B2 * v + (1.0 - ADAM_B2) * (g * g)
    m_hat = m / (1.0 - ADAM_B1 ** ADAM_STEP)
    v_hat = v / (1.0 - ADAM_B2 ** ADAM_STEP)
    delta = -ADAM_LR * (m_hat / (jnp.sqrt(v_hat) + ADAM_EPS) + ADAM_WD * w)
    return delta, m, v


def _adamw(w, m, v, parts, axis, chip_arr, layers, prev, after, *, name):
    n_l, r, c = w.shape
    first, count = layers
    tr = _tile(r, 256)
    nr = r // tr
    n_chip = 4
    n_in = 4 + n_chip * count

    def body(*refs):
        w_ref, m_ref, v_ref = refs[1:4]
        p_refs = refs[4:n_in]
        g_ref, d_ref, nm_ref, nv_ref = refs[-4:]
        for l in range(count):
            @pl.when(pl.program_id(0) == l)
            def _(l=l):
                g = p_refs[n_chip * l][...].astype(F32)
                for k in range(1, n_chip):
                    g = g + p_refs[n_chip * l + k][...].astype(F32)
                delta, nm, nv = _adam_math(w_ref[...], g, m_ref[...], v_ref[...])
                g_ref[...] = g
                d_ref[...] = delta
                nm_ref[...] = nm
                nv_ref[...] = nv

    def part_spec(ll, k):
        if axis == 0:
            return pl.BlockSpec((tr, c), lambda l, i, chips: (chips[k] * nr + jnp.where(l == ll, i, 0), 0))
        return pl.BlockSpec((tr, c), lambda l, i, chips: (jnp.where(l == ll, i, 0), chips[k]))

    wspec = pl.BlockSpec((None, tr, c), lambda l, i, chips: (first + l, i, 0))
    pspecs = [part_spec(ll, k) for ll in range(count) for k in range(n_chip)]
    sds = jax.ShapeDtypeStruct(w.shape, F32)
    kept = [] if prev is None else list(prev)
    return pl.pallas_call(
        body,
        out_shape=(sds, sds, sds, sds),
        grid_spec=pltpu.PrefetchScalarGridSpec(
            num_scalar_prefetch=1, grid=(count, nr),
            in_specs=[wspec, wspec, wspec] + pspecs + [_ANY] * (1 + len(kept)), out_specs=(wspec, wspec, wspec, wspec)),
        input_output_aliases={n_in + 1 + k: k for k in range(len(kept))},
        compiler_params=_cp("arbitrary", "arbitrary"),
        name=name,
    )(chip_arr, w, m, v, *[parts[first + ll][0 if k == 0 else 1] for ll in range(count) for k in range(n_chip)], after, *kept)


def _adamw_small(w, m, v, parts, *, name):
    r, c = w.shape
    tr = _tile(r, 512)

    def body(w_ref, m_ref, v_ref, p_ref, g_ref, d_ref, nm_ref, nv_ref):
        g = p_ref[0]
        for s in range(1, N_DEV):
            g = g + p_ref[s]
        delta, nm, nv = _adam_math(w_ref[...], g, m_ref[...], v_ref[...])
        g_ref[...] = g
        d_ref[...] = delta
        nm_ref[...] = nm
        nv_ref[...] = nv

    spec = pl.BlockSpec((tr, c), lambda i: (i, 0))
    sds = jax.ShapeDtypeStruct(w.shape, F32)
    return pl.pallas_call(
        body,
        out_shape=(sds, sds, sds, sds),
        grid=(r // tr,),
        in_specs=[spec, spec, spec, pl.BlockSpec((N_DEV, tr, c), lambda i: (0, i, 0))],
        out_specs=(spec, spec, spec, spec),
        compiler_params=_cp("parallel"),
        name=name,
    )(w, m, v, parts)


def _pack(arrs):
    flat = jnp.concatenate([a.reshape(-1).astype(F32) for a in arrs])
    pad = (-flat.shape[0]) % (8 * LANES)
    return jnp.pad(flat, (0, pad)).reshape(-1, LANES)


def _unpack(packed, shapes):
    flat = packed.reshape(-1)
    out, off = [], 0
    for sh in shapes:
        n = math.prod(sh)
        out.append(flat[off:off + n].reshape(sh))
        off += n
    return out


def _unpack_gathered(packed, shapes, axis):
    flat = packed.reshape(N_DEV, -1)
    out, off = [], 0
    for sh, ax in zip(shapes, axis):
        n = math.prod(sh)
        blocks = flat[:, off:off + n].reshape((N_DEV, *sh))
        out.append(jnp.concatenate([blocks[d] for d in range(N_DEV)], axis=ax))
        off += n
    return out


def _my_block(full, axis, size):
    idx = _dev_index(*_me())
    return lax.dynamic_slice_in_dim(full, idx * size, size, axis)


def kernel(x, mem, positions, norm_mix_g, norm_mem_q_g, norm_mem_kv_g, norm_ffn_g, final_norm_g, ab_w_in, a_v_norm_g, a_w_s, a_b_s, b_conv_w, ab_w_out, c_w_in, c_q_norm_g, c_kv_norm_g, c_w_uq, c_w_ukv, c_w_out, m_wq, m_wk, m_wv, m_wo, f_w1, f_w2, loss_target, m_norm_mix_g, m_norm_mem_q_g, m_norm_mem_kv_g, m_norm_ffn_g, m_final_norm_g, m_ab_w_in, m_a_v_norm_g, m_a_w_s, m_a_b_s, m_b_conv_w, m_ab_w_out, m_c_w_in, m_c_q_norm_g, m_c_kv_norm_g, m_c_w_uq, m_c_w_ukv, m_c_w_out, m_m_wq, m_m_wk, m_m_wv, m_m_wo, m_f_w1, m_f_w2, v_norm_mix_g, v_norm_mem_q_g, v_norm_mem_kv_g, v_norm_ffn_g, v_final_norm_g, v_ab_w_in, v_a_v_norm_g, v_a_w_s, v_a_b_s, v_b_conv_w, v_ab_w_out, v_c_w_in, v_c_q_norm_g, v_c_kv_norm_g, v_c_w_uq, v_c_w_ukv, v_c_w_out, v_m_wq, v_m_wk, v_m_wv, v_m_wo, v_f_w1, v_f_w2):
    bl, seq, d = x.shape
    t = bl * seq
    n_mem = mem.shape[1]
    depth = norm_mix_g.shape[0]
    n_even, n_odd = ab_w_in.shape[0], c_w_in.shape[0]
    a_width = a_v_norm_g.shape[1]
    groups = a_w_s.shape[1]
    heads = d // C_NOPE
    q_rank = c_q_norm_g.shape[1] * N_DEV
    kv_rank = c_kv_norm_g.shape[1] * N_DEV

    big = {
        "ab_w_in": (ab_w_in, m_ab_w_in, v_ab_w_in, 1), "ab_w_out": (ab_w_out, m_ab_w_out, v_ab_w_out, 0),
        "c_w_in": (c_w_in, m_c_w_in, v_c_w_in, 0), "c_w_uq": (c_w_uq, m_c_w_uq, v_c_w_uq, 1),
        "c_w_ukv": (c_w_ukv, m_c_w_ukv, v_c_w_ukv, 1), "c_w_out": (c_w_out, m_c_w_out, v_c_w_out, 0),
        "m_wq": (m_wq, m_m_wq, v_m_wq, 0), "m_wk": (m_wk, m_m_wk, v_m_wk, 0), "m_wv": (m_wv, m_m_wv, v_m_wv, 0),
        "m_wo": (m_wo, m_m_wo, v_m_wo, 0), "f_w1": (f_w1, m_f_w1, v_f_w1, 1), "f_w2": (f_w2, m_f_w2, v_f_w2, 0),
    }
    even_names = ["ab_w_in", "ab_w_out"]
    odd_names = ["c_w_in", "c_w_uq", "c_w_ukv", "c_w_out"]
    all_names = ["m_wq", "m_wk", "m_wv", "m_wo", "f_w1", "f_w2"]

    def layer_names(layer):
        return (even_names if layer % 2 == 0 else odd_names) + all_names

    def sub(name, layer):
        return layer // 2 if name in even_names or name in odd_names else layer

    split_small = [(c_q_norm_g, m_c_q_norm_g, v_c_q_norm_g, 1), (c_kv_norm_g, m_c_kv_norm_g, v_c_kv_norm_g, 1), (b_conv_w, m_b_conv_w, v_b_conv_w, 2)]
    shard_shapes = [s[0].shape for s in split_small] * 3
    shard_axes = [s[3] for s in split_small] * 3
    gathered = _all_gather_small(_pack([s[0] for s in split_small] + [s[1] for s in split_small] + [s[2] for s in split_small]))
    whole = _unpack_gathered(gathered, shard_shapes, shard_axes)
    q_norm_full, kv_norm_full, conv_full = whole[0:3]
    small_w = [norm_mix_g, norm_mem_q_g, norm_mem_kv_g, norm_ffn_g, final_norm_g, a_v_norm_g, a_w_s, a_b_s] + whole[0:3]
    small_m = [m_norm_mix_g, m_norm_mem_q_g, m_norm_mem_kv_g, m_norm_ffn_g, m_final_norm_g, m_a_v_norm_g, m_a_w_s, m_a_b_s] + whole[3:6]
    small_v = [v_norm_mix_g, v_norm_mem_q_g, v_norm_mem_kv_g, v_norm_ffn_g, v_final_norm_g, v_a_v_norm_g, v_a_w_s, v_a_b_s] + whole[6:9]

    inv = ROPE_THETA ** (-jnp.arange(0, C_ROPE, 2, dtype=F32) / C_ROPE)
    ang = positions.astype(F32).reshape(t, 1) * inv
    cos, sin, zero = jnp.cos(ang), jnp.sin(ang), jnp.zeros((t, C_ROPE // 2), F32)
    rc = jnp.concatenate([cos, cos, zero, zero], axis=1)
    rs1 = jnp.concatenate([-sin, zero, zero, zero], axis=1)
    rs2 = jnp.concatenate([zero, sin, zero, zero], axis=1)

    my_x, my_y, my_c = _me()
    my_chip = 2 * my_x + my_y
    dev_arr = jnp.stack([_dev_index(my_x, my_y, my_c)]).astype(jnp.int32)
    core_arr = jnp.stack([my_c]).astype(jnp.int32)
    chip_arr = jnp.stack([my_chip, my_chip ^ 1, my_chip ^ 2, my_chip ^ 3]).astype(jnp.int32)

    n_stage = 3 * depth
    chain = [None]
    gat = {}
    issued = [0]

    def stage_names(i):
        layer, gid = divmod(i, 3)
        return layer, [even_names if layer % 2 == 0 else odd_names, all_names[:4], all_names[4:]][gid]

    def order(after):
        return [a for a in (chain[0], after) if a is not None]

    def gather_ici_start(i, after):
        layer, names = stage_names(i)
        placed = [_cast_place(big[nm][0], sub(nm, layer), big[nm][3], dev_arr, name="cast_place") for nm in names]
        axes = [big[nm][3] for nm in names]
        sizes = [p.shape[a] // N_DEV for p, a in zip(placed, axes)]
        st = dict(names=names, ici=_plan_gather_ici(axes, sizes), d2d=_plan_gather_d2d(axes, sizes))
        st["sems"], st["arrays"], chain[0] = _exchange_start(placed, N_PEER_CHIPS, st["ici"], order(after), name=f"gather_ici_start_s{i}")
        gat[i] = st

    def gather_stage(i, after):
        while issued[0] < min(i + 3, n_stage):
            gather_ici_start(issued[0], after)
            issued[0] += 1
        ahead = [i, i + 1] if i >= 3 and i + 1 < n_stage else [i]
        for j in ahead:
            st = gat[j]
            if "landed" not in st:
                st["arrays"], chain[0] = _exchange_wait(st["sems"], st["arrays"], st["ici"], order(after), name=f"gather_ici_wait_s{j}")
                st["landed"] = True
            if "sems2" not in st:
                st["sems2"], st["arrays"], chain[0] = _exchange_start(st["arrays"], 1, st["d2d"], order(after), name=f"gather_d2d_start_s{j}")
        st = gat[i]
        st["arrays"], chain[0] = _exchange_wait(st["sems2"], st["arrays"], st["d2d"], order(after), name=f"gather_d2d_wait_s{i}")
        return dict(zip(st["names"], st["arrays"]))

    wts = []

    def group_cols(w_in):
        return w_in.reshape(d, 5, groups, LANES).transpose(0, 2, 1, 3).reshape(d, 5 * a_width)

    def ungroup_cols(g_in):
        return g_in.reshape(d, groups, 5, LANES).transpose(0, 2, 1, 3).reshape(d, 5 * a_width)

    def group_rows(w_out):
        return w_out.reshape(2, groups, LANES, d).transpose(1, 0, 2, 3).reshape(2 * a_width, d)

    def ungroup_rows(g_out):
        return g_out.reshape(groups, 2, LANES, d).transpose(1, 0, 2, 3).reshape(2 * a_width, d)

    def pad_heads(w_uq):
        w3 = w_uq.reshape(q_rank, heads, C_NOPE + C_ROPE)
        return jnp.pad(w3, ((0, 0), (0, 0), (0, LANES - C_ROPE))).reshape(q_rank, heads * 2 * LANES)

    def unpad_heads(g_uq):
        return g_uq.reshape(q_rank, heads, 2 * LANES)[:, :, :C_NOPE + C_ROPE].reshape(q_rank, heads * (C_NOPE + C_ROPE))

    x2 = x.reshape(t, d)
    mem2 = mem.reshape(bl * n_mem, d)
    saved = []

    for layer in range(depth):
        w = gather_stage(3 * layer, x2 if layer else gathered)
        wts.append(w)
        sv = {"x0": x2}
        xn = _rmsnorm_fwd(x2, norm_mix_g[layer], name="norm_mix")
        sv["xn"] = xn
        if layer % 2 == 0:
            e = layer // 2
            w_in_g = group_cols(w["ab_w_in"])
            w_out_g = group_rows(w["ab_w_out"])
            z = _mm(xn, w_in_g, out_dtypes=(F32,), extras=[(chain[0], "after")], name="ab_in")
            ws_e = a_w_s[e]
            bs_e = a_b_s[e].reshape(groups, GMLP_BLOCK, 1)
            gn_e = a_v_norm_g[e].reshape(groups, 1, LANES)
            cw_e = conv_full[e].reshape(3, groups, LANES).transpose(1, 0, 2)
            ycat = _gmlp_conv_fwd(z, ws_e, bs_e, gn_e, cw_e, seq=seq, name="gmlp_conv_fwd")
            x2 = _mm(ycat, w_out_g, out_dtypes=(F32,), extras=[(x2, "mn")], epilogue=_epi_add, name="ab_out")
            sv.update(z=z, ycat=ycat, w_in_g=w_in_g, w_out_g=w_out_g, ws=ws_e, bs=bs_e, gn=gn_e, cw=cw_e)
        else:
            o = layer // 2
            w_uq_p = pad_heads(w["c_w_uq"])
            z = _mm(xn, w["c_w_in"], out_dtypes=(F32,), extras=[(chain[0], "after")], name="c_in")
            qn = _rmsnorm_fwd(z, q_norm_full[o], width=q_rank, col=0, name="norm_cq")
            kvn = _rmsnorm_fwd(z, kv_norm_full[o], width=kv_rank, col=q_rank // kv_rank, name="norm_ckv")
            qh = _mm(qn, w_uq_p, extras=[(rc, "m"), (rs1, "m"), (rs2, "m")], epilogue=_epi_rope_heads, tn=1024, name="c_uq")
            kvh = _mm(kvn, w["c_w_ukv"], name="c_ukv")
            k_rope = jnp.pad(z[:, q_rank + kv_rank:], ((0, 0), (0, LANES - C_ROPE)))
            kr = _rope_rows(k_rope, rc, rs1, rs2, transpose=False, out_dtype=BF16, name="rope_k")
            att, lse = _mla_fwd(qh, kvh, kr, seq=seq, heads=heads, name="mla_fwd")
            x2 = _mm(att, w["c_w_out"], out_dtypes=(F32,), extras=[(x2, "mn")], epilogue=_epi_add, name="c_out")
            sv.update(z=z, qn=qn, kvn=kvn, qh=qh, kvh=kvh, kr=kr, att=att, lse=lse, w_uq_p=w_uq_p)
        sv["x1"] = x2
        w.update(gather_stage(3 * layer + 1, x2))
        xq = _rmsnorm_fwd(x2, norm_mem_q_g[layer], name="norm_mem_q")
        memn = _rmsnorm_fwd(mem2, norm_mem_kv_g[layer], name="norm_mem_kv")
        mq = _mm(xq, w["m_wq"], extras=[(chain[0], "after")], name="m_q")
        mk = _mm(memn, w["m_wk"], name="m_k")
        mv = _mm(memn, w["m_wv"], name="m_v")
        mo = _mem_fwd(mq, mk, mv, seq=seq, n_mem=n_mem, name="mem_fwd")
        x2 = _mm(mo, w["m_wo"], out_dtypes=(F32,), extras=[(x2, "mn")], epilogue=_epi_add, name="m_o")
        sv.update(xq=xq, memn=memn, mq=mq, mk=mk, mv=mv, mo=mo)
        sv["x2"] = x2
        w.update(gather_stage(3 * layer + 2, x2))
        xf = _rmsnorm_fwd(x2, norm_ffn_g[layer], name="norm_ffn")
        hpre, hact = _mm(xf, w["f_w1"], out_dtypes=(BF16, BF16), extras=[(chain[0], "after")], epilogue=_epi_relu2, name="f_1")
        x2 = _mm(hact, w["f_w2"], out_dtypes=(F32,), extras=[(x2, "mn")], epilogue=_epi_add, name="f_2")
        sv.update(xf=xf, hpre=hpre, hact=hact)
        saved.append(sv)

    loss_part, dx, dg_final = _final_loss(x2, final_norm_g, loss_target.reshape(t, d), name="final_loss")

    dg_mix, dg_mq, dg_mkv, dg_ffn = [None] * depth, [None] * depth, [None] * depth, [None] * depth
    d_av, d_ws, d_bs = [None] * n_even, [None] * n_even, [None] * n_even
    d_conv, d_qn, d_kvn = [None] * n_even, [None] * n_odd, [None] * n_odd
    parts = {nm: [None] * big[nm][0].shape[0] for nm in big}

    sca = []

    def scatter_finish_d2d(st, after):
        n = len(st["names"])
        arrays, chain[0] = _exchange_wait(st["sems"], st["arrays"], st["d2d"], order(after), name=f"scatter_d2d_wait_s{st['id']}")
        sums = [_add_pair(g, got, ax, core_arr, name="add_pair") for g, got, ax in zip(arrays[:n], arrays[n:], st["axes"])]
        lands = [pltpu.with_memory_space_constraint(lax.empty(s_.shape, s_.dtype), pltpu.HBM) for s_ in sums]
        st["sems2"], st["arrays"], chain[0] = _exchange_start(sums + lands, N_PEER_CHIPS, st["ici"], order(after), name=f"scatter_ici_start_s{st['id']}")

    def scatter_stage(layer, names, grads, after):
        if sca and "sems2" not in sca[-1]:
            scatter_finish_d2d(sca[-1], after)
        axes = [big[nm][3] for nm in names]
        arrs = [grads[nm] for nm in names]
        sizes8 = [g.shape[a] // N_DEV for g, a in zip(arrs, axes)]
        gots = []
        for g, a in zip(arrs, axes):
            sh = list(g.shape)
            sh[a] //= 2
            gots.append(pltpu.with_memory_space_constraint(lax.empty(tuple(sh), g.dtype), pltpu.HBM))
        st = dict(id=len(sca), layer=layer, names=names, axes=axes, d2d=_plan_scatter_d2d(axes, sizes8),
                  ici=_plan_scatter_ici(axes, sizes8))
        st["sems"], st["arrays"], chain[0] = _exchange_start(arrs + gots, 1, st["d2d"], order(after), name=f"scatter_d2d_start_s{st['id']}")
        sca.append(st)

    def scatter_collect(st, after):
        n = len(st["names"])
        arrays, chain[0] = _exchange_wait(st["sems2"], st["arrays"], st["ici"], order(after), name=f"scatter_ici_wait_s{st['id']}")
        for nm, s_, r in zip(st["names"], arrays[:n], arrays[n:]):
            parts[nm][sub(nm, st["layer"])] = (s_, r)

    for layer in reversed(range(depth)):
        w, sv = wts[layer], saved[layer]
        grads = {}
        dh = _mm(dx, w["f_w2"], tb=True, extras=[(sv["hpre"], "mn"), (chain[0], "after")], epilogue=_epi_drelu2, name="f_2_dx")
        grads["f_w2"] = _mm(sv["hact"], dx, ta=True, name="f_2_dw")
        grads["f_w1"] = _mm(sv["xf"], dh, ta=True, name="f_1_dw")
        dxf = _mm(dh, w["f_w1"], tb=True, name="f_1_dx")
        dx, dg_ffn[layer] = _rmsnorm_bwd(sv["x2"], norm_ffn_g[layer], dxf, dx, name="norm_ffn_bwd")
        scatter_stage(layer, all_names[4:], grads, dx)
        dmo = _mm(dx, w["m_wo"], tb=True, extras=[(chain[0], "after")], name="m_o_dx")
        grads["m_wo"] = _mm(sv["mo"], dx, ta=True, name="m_o_dw")
        dmq, dmk, dmv = _mem_bwd(sv["mq"], sv["mk"], sv["mv"], dmo, seq=seq, n_mem=n_mem, name="mem_bwd")
        grads["m_wq"] = _mm(sv["xq"], dmq, ta=True, name="m_q_dw")
        grads["m_wk"] = _mm(sv["memn"], dmk, ta=True, name="m_k_dw")
        grads["m_wv"] = _mm(sv["memn"], dmv, ta=True, name="m_v_dw")
        dxq = _mm(dmq, w["m_wq"], tb=True, name="m_q_dx")
        dmemn_k = _mm(dmk, w["m_wk"], tb=True, out_dtypes=(F32,), name="m_k_dx")
        dmemn = _mm(dmv, w["m_wv"], tb=True, out_dtypes=(F32,), extras=[(dmemn_k, "mn")], epilogue=_epi_add, name="m_v_dx")
        _, dg_mkv[layer] = _rmsnorm_bwd(mem2, norm_mem_kv_g[layer], dmemn, name="norm_mem_kv_bwd")
        dx, dg_mq[layer] = _rmsnorm_bwd(sv["x1"], norm_mem_q_g[layer], dxq, dx, name="norm_mem_q_bwd")
        scatter_stage(layer, all_names[:4], grads, dx)
        if layer % 2 == 0:
            e = layer // 2
            dycat = _mm(dx, sv["w_out_g"], tb=True, extras=[(chain[0], "after")], name="ab_out_dx")
            grads["ab_w_out"] = ungroup_rows(_mm(sv["ycat"], dx, ta=True, name="ab_out_dw"))
            dz, dws, dbs, dgn, dcw = _gmlp_conv_bwd(sv["z"], dycat, sv["ws"], sv["bs"], sv["gn"], sv["cw"], seq=seq, name="gmlp_conv_bwd")
            d_ws[e], d_bs[e], d_av[e] = dws, dbs.reshape(groups, GMLP_BLOCK), dgn.reshape(a_width)
            d_conv[e] = dcw.transpose(1, 0, 2).reshape(3, a_width)
            grads["ab_w_in"] = ungroup_cols(_mm(sv["xn"], dz, ta=True, name="ab_in_dw"))
            dxn = _mm(dz, sv["w_in_g"], tb=True, name="ab_in_dx")
        else:
            o = layer // 2
            datt = _mm(dx, w["c_w_out"], tb=True, extras=[(chain[0], "after")], name="c_out_dx")
            grads["c_w_out"] = _mm(sv["att"], dx, ta=True, name="c_out_dw")
            dq_pre, dkvh, dkr = _mla_bwd(sv["qh"], sv["kvh"], sv["kr"], datt, sv["att"], sv["lse"], rc, rs1, rs2, seq=seq, heads=heads, name="mla_bwd")
            dk_rope = _rope_rows(dkr, rc, rs1, rs2, transpose=True, out_dtype=F32, name="rope_k_bwd")
            grads["c_w_uq"] = unpad_heads(_mm(sv["qn"], dq_pre, ta=True, name="c_uq_dw"))
            grads["c_w_ukv"] = _mm(sv["kvn"], dkvh, ta=True, name="c_ukv_dw")
            dqn = _mm(dq_pre, sv["w_uq_p"], tb=True, name="c_uq_dx")
            dkvn = _mm(dkvh, w["c_w_ukv"], tb=True, name="c_ukv_dx")
            dcq, d_qn[o] = _rmsnorm_bwd(sv["z"], q_norm_full[o], dqn, width=q_rank, col=0, name="norm_cq_bwd")
            dckv, d_kvn[o] = _rmsnorm_bwd(sv["z"], kv_norm_full[o], dkvn, width=kv_rank, col=q_rank // kv_rank, name="norm_ckv_bwd")
            dz = jnp.concatenate([dcq, dckv, dk_rope[:, :C_ROPE]], axis=1)
            grads["c_w_in"] = _mm(sv["xn"], dz, ta=True, name="c_in_dw")
            dxn = _mm(dz, w["c_w_in"], tb=True, name="c_in_dx")
        dx, dg_mix[layer] = _rmsnorm_bwd(sv["x0"], norm_mix_g[layer], dxn, dx, name="norm_mix_bwd")
        scatter_stage(layer, even_names if layer % 2 == 0 else odd_names, grads, dx)

    grad_x = dx.reshape(bl, seq, d)

    scatter_finish_d2d(sca[-1], dx)
    for st in sca[:-3]:
        scatter_collect(st, dx)
    big_out = {}
    done = chain[0]
    for nm, (wv, mv_, vv, ax) in big.items():
        n_sub = wv.shape[0]
        first = 0 if nm in odd_names else 1
        if first < n_sub:
            big_out[nm] = _adamw(wv, mv_, vv, parts[nm], ax, chip_arr, (first, n_sub - first), None, done, name=f"adamw_{nm}")
            done = big_out[nm][0]
    for st in sca[-3:]:
        scatter_collect(st, done)
    for nm, (wv, mv_, vv, ax) in big.items():
        if nm not in odd_names:
            big_out[nm] = _adamw(wv, mv_, vv, parts[nm], ax, chip_arr, (0, 1), big_out.get(nm), chain[0], name=f"adamw_{nm}_first")

    small_g = [jnp.concatenate(dg_mix), jnp.concatenate(dg_mq), jnp.concatenate(dg_mkv), jnp.concatenate(dg_ffn), dg_final.reshape(d),
               jnp.stack(d_av), jnp.stack(d_ws), jnp.stack(d_bs), jnp.concatenate(d_qn), jnp.concatenate(d_kvn), jnp.stack(d_conv)]
    small_g.append(loss_part.reshape(1))
    spare = jnp.zeros((1,), F32)
    small_shapes = [a.shape for a in small_w] + [(1,)]
    g_all = _all_gather_small(_pack(small_g) + chain[0][0, 0])
    sg, sd, sm, svv = _adamw_small(_pack(small_w + [spare]), _pack(small_m + [spare]), _pack(small_v + [spare]), g_all, name="adamw_small")
    small_names = ["norm_mix_g", "norm_mem_q_g", "norm_mem_kv_g", "norm_ffn_g", "final_norm_g", "a_v_norm_g", "a_w_s", "a_b_s", "c_q_norm_g", "c_kv_norm_g", "b_conv_w"]
    small_out = {}
    unpacked = [_unpack(p, small_shapes) for p in (sg, sd, sm, svv)]
    for i, nm in enumerate(small_names):
        vals = [u[i] for u in unpacked]
        if i >= 8:
            full_w, _, _, ax = split_small[i - 8]
            vals = [_my_block(val, ax, full_w.shape[ax]) for val in vals]
        small_out[nm] = vals

    order = ["norm_mix_g", "norm_mem_q_g", "norm_mem_kv_g", "norm_ffn_g", "final_norm_g", "ab_w_in", "a_v_norm_g", "a_w_s", "a_b_s", "b_conv_w", "ab_w_out",
             "c_w_in", "c_q_norm_g", "c_kv_norm_g", "c_w_uq", "c_w_ukv", "c_w_out", "m_wq", "m_wk", "m_wv", "m_wo", "f_w1", "f_w2"]
    loss = unpacked[0][-1][0]
    res = [loss, grad_x]
    for kind in range(4):
        for nm in order:
            res.append(big_out[nm][kind] if nm in big_out else small_out[nm][kind])
    return tuple(res)
```
